```python
import math
import jax, jax.numpy as jnp
from jax import lax
import numpy as np

D_MODEL = 1024
BATCH = 2
SEQ = 16384
DEPTH = 1

ATTN_HEADS = 8
ATTN_HEAD_DIM = D_MODEL // 16
ATTN_WIDTH = ATTN_HEADS * ATTN_HEAD_DIM
RET_HEADS = 4
RET_HEAD_DIM = D_MODEL // 8
RET_WIDTH = RET_HEADS * RET_HEAD_DIM
MIX_WIDTH = ATTN_WIDTH + RET_WIDTH
N_IN_SLICES = 7
MOBA_BLOCK = 256
MOBA_TOPK = 3
Q_BLOCK = 128
RET_CHUNK = 256
REL_BUCKETS = 32
REL_MAX_DIST = 128
ROPE_BASE = 10000.0
N_GROUPS = 4
EXPERTS_PER_GROUP = 8
N_EXPERTS = N_GROUPS * EXPERTS_PER_GROUP
D_EXPERT = D_MODEL // 2
TOP_K_INNER = 2
ALPHA = (2.0 * DEPTH) ** 0.25
BETA = (8.0 * DEPTH) ** -0.25
LN_EPS = 1e-5

kernel_name = "hymba_moba_retnet_hmoe_deepnorm"


def layer_norm(x, g, b):
    xf = x.astype(jnp.float32)
    mu = jnp.mean(xf, -1, keepdims=True)
    var = jnp.mean(jnp.square(xf - mu), -1, keepdims=True)
    return ((xf - mu) * lax.rsqrt(var + LN_EPS) * g + b).astype(x.dtype)


def rms_norm(x, g):
    xf = x.astype(jnp.float32)
    return (xf * lax.rsqrt(jnp.mean(jnp.square(xf), -1, keepdims=True) + LN_EPS) * g).astype(x.dtype)


def t5_bucket(dist):
    n = jnp.maximum(dist, 0)
    max_exact = REL_BUCKETS // 2
    nf = jnp.maximum(n, 1).astype(jnp.float32)
    large = max_exact + (jnp.log(nf / max_exact) / math.log(REL_MAX_DIST / max_exact)
                         * (REL_BUCKETS - max_exact)).astype(jnp.int32)
    large = jnp.minimum(large, REL_BUCKETS - 1)
    return jnp.where(n < max_exact, n, large)


def rope(t, pos):
    half = t.shape[-1] // 2
    inv = ROPE_BASE ** (-jnp.arange(half, dtype=jnp.float32) / half)
    ang = pos[:, None].astype(jnp.float32) * inv[None, :]
    cos, sin = jnp.cos(ang), jnp.sin(ang)
    tf = t.astype(jnp.float32)
    t1, t2 = tf[..., :half], tf[..., half:]
    return jnp.concatenate([t1 * cos - t2 * sin, t1 * sin + t2 * cos], -1).astype(t.dtype)


def moba_attention(q, k, v, rel_table):
    B, H, S, Dh = q.shape
    L = MOBA_BLOCK
    nb = S // L
    topk = min(MOBA_TOPK, nb)
    scale = Dh ** -0.5
    k_blk = k.reshape(B, H, nb, L, Dh)
    v_blk = v.reshape(B, H, nb, L, Dh)
    k_mean = jnp.mean(k_blk.astype(jnp.float32), axis=3)
    n_chunks = S // Q_BLOCK
    q_chunks = q.reshape(B, H, n_chunks, Q_BLOCK, Dh).transpose(2, 0, 1, 3, 4)
    table_t = rel_table.astype(jnp.float32).T
    head_idx = jnp.arange(H)[None, :, None, None, None]
    gather = jax.vmap(jax.vmap(lambda kb, idx: kb[idx]))

    def one_chunk(args):
        qc, c = args
        q_pos = c * Q_BLOCK + jnp.arange(Q_BLOCK)
        own = (c * Q_BLOCK) // L
        gate = jnp.einsum('bhqd,bhnd->bhqn', qc.astype(jnp.float32), k_mean)
        gate = jnp.where(jnp.arange(nb) < own, gate, -jnp.inf)
        _, sel = lax.top_k(gate, topk)
        sel_valid = sel < own
        k_sel = gather(k_blk, sel)
        v_sel = gather(v_blk, sel)
        s_sel = jnp.einsum('bhqd,bhqtld->bhqtl', qc, k_sel).astype(jnp.float32) * scale
        k_pos_sel = sel[..., None] * L + jnp.arange(L)
        bucket_sel = t5_bucket(q_pos[None, None, :, None, None] - k_pos_sel)
        s_sel = s_sel + table_t[head_idx, bucket_sel]
        s_sel = jnp.where(sel_valid[..., None], s_sel, -jnp.inf)
        start = own * L
        k_own = lax.dynamic_slice_in_dim(k, start, L, axis=2)
        v_own = lax.dynamic_slice_in_dim(v, start, L, axis=2)
        s_own = jnp.einsum('bhqd,bhld->bhql', qc, k_own).astype(jnp.float32) * scale
        dist_own = q_pos[:, None] - (start + jnp.arange(L))[None, :]
        s_own = s_own + table_t[:, t5_bucket(dist_own)][None]
        s_own = jnp.where(dist_own[None, None] >= 0, s_own, -jnp.inf)
        logits = jnp.concatenate([s_sel.reshape(B, H, Q_BLOCK, topk * L), s_own], -1)
        p = jax.nn.softmax(logits, axis=-1)
        p_sel = p[..., :topk * L].reshape(B, H, Q_BLOCK, topk, L).astype(v.dtype)
        p_own = p[..., topk * L:].astype(v.dtype)
        return (jnp.einsum('bhqtl,bhqtld->bhqd', p_sel, v_sel)
                + jnp.einsum('bhql,bhld->bhqd', p_own, v_own))

    out = lax.map(one_chunk, (q_chunks, jnp.arange(n_chunks)))
    return out.transpose(1, 2, 0, 3, 4).reshape(B, H, S, Dh)


def retention_chunkwise(q, k, v):
    B, H, S, Dk = q.shape
    Dv = v.shape[-1]
    C = RET_CHUNK
    n = S // C
    gammas = 1.0 - jnp.exp(jnp.linspace(math.log(1.0 / 32), math.log(1.0 / 512), H, dtype=jnp.float32))
    log_g = jnp.log(gammas)
    idx = jnp.arange(C, dtype=jnp.float32)
    diff = idx[:, None] - idx[None, :]
    decay_inner = jnp.where(diff[None] >= 0, jnp.exp(jnp.maximum(diff, 0.0)[None] * log_g[:, None, None]), 0.0)
    q_decay = jnp.exp((idx[None, :] + 1.0) * log_g[:, None])[None, :, :, None]
    k_decay = jnp.exp((C - 1.0 - idx[None, :]) * log_g[:, None])[None, :, :, None]
    chunk_decay = jnp.exp(C * log_g)[None, :, None, None]

    def to_chunks(t):
        return t.astype(jnp.float32).reshape(B, H, n, C, t.shape[-1]).transpose(2, 0, 1, 3, 4)

    def step(R, xs):
        qi, ki, vi = xs
        s = jnp.einsum('bhqd,bhkd->bhqk', qi, ki) * decay_inner
        inner = jnp.einsum('bhqk,bhkv->bhqv', s, vi)
        cross = jnp.einsum('bhqd,bhdv->bhqv', qi * q_decay, R)
        R = R * chunk_decay + jnp.einsum('bhkd,bhkv->bhdv', ki * k_decay, vi)
        return R, inner + cross

    R0 = jnp.zeros((B, H, Dk, Dv), jnp.float32)
    _, out = lax.scan(step, R0, (to_chunks(q), to_chunks(k), to_chunks(v)))
    return out.transpose(1, 2, 0, 3, 4).reshape(B, H, S, Dv)


def hybrid_mixer(h, w_in, attn_out_gain, rel_table, w_out):
    B, S, _ = h.shape
    proj = jnp.einsum('bsd,de->bse', h, w_in)
    pad = (-S) % MOBA_BLOCK
    proj = jnp.pad(proj, ((0, 0), (0, pad), (0, 0)))
    Sp = S + pad
    aq, ak, av, rq, rk, rv, rg = jnp.split(proj, N_IN_SLICES, axis=-1)

    def heads(t, nh):
        return t.reshape(B, Sp, nh, -1).transpose(0, 2, 1, 3)

    attn = moba_attention(heads(aq, ATTN_HEADS), heads(ak, ATTN_HEADS), heads(av, ATTN_HEADS), rel_table)
    attn = rms_norm(attn.transpose(0, 2, 1, 3).reshape(B, Sp, ATTN_WIDTH), attn_out_gain)
    pos = jnp.arange(Sp)
    rqh = rope(heads(rq, RET_HEADS), pos)
    rkh = rope(heads(rk, RET_HEADS), pos) * (RET_HEAD_DIM ** -0.5)
    ret = retention_chunkwise(rqh, rkh, heads(rv, RET_HEADS))
    mu = jnp.mean(ret, -1, keepdims=True)
    var = jnp.mean(jnp.square(ret - mu), -1, keepdims=True)
    ret = (ret - mu) * lax.rsqrt(var + LN_EPS)
    ret = ret.transpose(0, 2, 1, 3).reshape(B, Sp, RET_WIDTH)
    ret = (jax.nn.silu(rg.astype(jnp.float32)) * ret).astype(h.dtype)
    cat = jnp.concatenate([attn, ret], -1)[:, :S]
    return jnp.einsum('bse,ed->bsd', cat, w_out)


def hierarchical_moe(x2d, w_rg, b_rg, w_re, b_re, w_gu, w_dn):
    T, D = x2d.shape
    g_prob = jax.nn.softmax((x2d @ w_rg).astype(jnp.float32) + b_rg.astype(jnp.float32), axis=-1)
    g_p, g_idx = lax.top_k(g_prob, 1)
    e_logits = ((x2d @ w_re).astype(jnp.float32) + b_re.astype(jnp.float32)).reshape(T, N_GROUPS, EXPERTS_PER_GROUP)
    e_logits = jnp.take_along_axis(e_logits, jnp.broadcast_to(g_idx[:, :, None], (T, 1, EXPERTS_PER_GROUP)), axis=1)[:, 0]
    e_prob = jax.nn.softmax(e_logits, axis=-1)
    e_p, e_idx = lax.top_k(e_prob, TOP_K_INNER)
    e_p = e_p / jnp.sum(e_p, -1, keepdims=True)
    weights = g_p * e_p
    flat_e = (g_idx * EXPERTS_PER_GROUP + e_idx).reshape(-1)
    order = jnp.argsort(flat_e)
    xs = x2d[order // TOP_K_INNER]
    sizes = jnp.bincount(flat_e, length=N_EXPERTS).astype(jnp.int32)
    gu = lax.ragged_dot(xs, w_gu, sizes)
    hid = jax.nn.silu(gu[:, :D_EXPERT]) * gu[:, D_EXPERT:]
    y = lax.ragged_dot(hid, w_dn, sizes)
    y = jnp.zeros_like(y).at[order].set(y).reshape(T, TOP_K_INNER, D)
    return jnp.sum(y * weights[..., None].astype(y.dtype), axis=1)


def setup_inputs(seed: int = 0) -> dict:
    key = jax.random.key(seed)
    ks = jax.random.split(key, 16)
    f32 = jnp.float32
    x = jax.random.normal(ks[0], (BATCH, SEQ, D_MODEL), f32)
    w_in = jax.random.normal(ks[1], (DEPTH, D_MODEL, N_IN_SLICES * ATTN_WIDTH), f32) * D_MODEL ** -0.5
    col = jnp.arange(N_IN_SLICES * ATTN_WIDTH) // ATTN_WIDTH
    v_cols = (col == 2) | (col == 5)
    w_in = w_in * jnp.where(v_cols, BETA, 1.0).astype(f32)
    attn_out_gain = 1.0 + 0.02 * jax.random.normal(ks[2], (DEPTH, ATTN_WIDTH), f32)
    rel_bias_table = 0.5 * jax.random.normal(ks[3], (REL_BUCKETS, ATTN_HEADS), f32)
    w_out = jax.random.normal(ks[4], (DEPTH, MIX_WIDTH, D_MODEL), f32) * MIX_WIDTH ** -0.5 * BETA
    ln1_g = 1.0 + 0.02 * jax.random.normal(ks[5], (DEPTH, D_MODEL), f32)
    ln1_b = 0.02 * jax.random.normal(ks[6], (DEPTH, D_MODEL), f32)
    w_router_group = jax.random.normal(ks[7], (DEPTH, D_MODEL, N_GROUPS), f32) * D_MODEL ** -0.5
    b_router_group = 0.01 * jax.random.normal(ks[8], (DEPTH, N_GROUPS), f32)
    w_router_expert = jax.random.normal(ks[9], (DEPTH, D_MODEL, N_EXPERTS), f32) * D_MODEL ** -0.5
    b_router_expert = 0.01 * jax.random.normal(ks[10], (DEPTH, N_EXPERTS), f32)
    w_gate_up = jax.random.normal(ks[11], (DEPTH, N_EXPERTS, D_MODEL, 2 * D_EXPERT), f32) * D_MODEL ** -0.5 * BETA
    w_down = jax.random.normal(ks[12], (DEPTH, N_EXPERTS, D_EXPERT, D_MODEL), f32) * D_EXPERT ** -0.5 * BETA
    ln2_g = 1.0 + 0.02 * jax.random.normal(ks[13], (DEPTH, D_MODEL), f32)
    ln2_b = 0.02 * jax.random.normal(ks[14], (DEPTH, D_MODEL), f32)
    return {"x": x, "w_in": w_in, "attn_out_gain": attn_out_gain, "rel_bias_table": rel_bias_table,
            "w_out": w_out, "ln1_g": ln1_g, "ln1_b": ln1_b, "w_router_group": w_router_group,
            "b_router_group": b_router_group, "w_router_expert": w_router_expert,
            "b_router_expert": b_router_expert, "w_gate_up": w_gate_up, "w_down": w_down,
            "ln2_g": ln2_g, "ln2_b": ln2_b}


def reference(x, w_in, attn_out_gain, rel_bias_table, w_out, ln1_g, ln1_b, w_router_group,
              b_router_group, w_router_expert, b_router_expert, w_gate_up, w_down, ln2_g, ln2_b):
    B, S, D = x.shape
    h = x
    for l in range(DEPTH):
        mix = hybrid_mixer(h, w_in[l], attn_out_gain[l], rel_bias_table, w_out[l])
        h = layer_norm(ALPHA * h + mix, ln1_g[l], ln1_b[l])
        ffn = hierarchical_moe(h.reshape(B * S, D), w_router_group[l], b_router_group[l],
                               w_router_expert[l], b_router_expert[l], w_gate_up[l], w_down[l]).reshape(B, S, D)
        h = layer_norm(ALPHA * h + ffn, ln2_g[l], ln2_b[l])
    return h
```

```python
import functools
import math

import numpy as np
import jax
import jax.numpy as jnp
from jax import lax
from jax.experimental import pallas as pl
from jax.experimental.pallas import tpu as pltpu

F32 = jnp.float32
BF16 = jnp.bfloat16

D_MODEL = 1024
ATTN_HEADS = 8
ATTN_HEAD_DIM = 64
ATTN_WIDTH = ATTN_HEADS * ATTN_HEAD_DIM
RET_HEADS = 4
RET_HEAD_DIM = 128
RET_WIDTH = RET_HEADS * RET_HEAD_DIM
N_IN_SLICES = 7
PROJ_WIDTH = N_IN_SLICES * ATTN_WIDTH
MOBA_BLOCK = 256
MOBA_TOPK = 3
RET_CHUNK = 256
REL_BUCKETS = 32
REL_MAX_DIST = 128
ROPE_BASE = 10000.0
N_GROUPS = 4
EXPERTS_PER_GROUP = 8
N_EXPERTS = N_GROUPS * EXPERTS_PER_GROUP
D_EXPERT = D_MODEL // 2
DEPTH = 1
ALPHA = (2.0 * DEPTH) ** 0.25
LN_EPS = 1e-5

MASKED = -1e30

INPROJ_TOKENS = 512
INPROJ_ROWS = 512
OUTPROJ_TOKENS = 256
MOE_ROWS = 256
ROUTE_TOKENS = 256
ROUTER_EXPERT_ROW0 = 8
ROUTER_ROWS = ROUTER_EXPERT_ROW0 + N_EXPERTS

VMEM_LIMIT = 48 * 1024 * 1024

_TN = (((0,), (0,)), ((), ()))
_NT = (((1,), (1,)), ((), ()))


def _params(*sem):
    return pltpu.CompilerParams(dimension_semantics=sem, vmem_limit_bytes=VMEM_LIMIT)


def _inproj_kernel(x_ref, w_ref, o_ref):
    xb = x_ref[0].astype(BF16)
    for c in range(PROJ_WIDTH // INPROJ_ROWS):
        rows = slice(c * INPROJ_ROWS, (c + 1) * INPROJ_ROWS)
        o_ref[0, rows, :] = lax.dot_general(
            w_ref[rows, :], xb, _NT, preferred_element_type=F32).astype(BF16)


def _inproj(x, w_in_t):
    B, S, D = x.shape
    return pl.pallas_call(
        _inproj_kernel,
        grid=(B, S // INPROJ_TOKENS),
        in_specs=[
            pl.BlockSpec((1, INPROJ_TOKENS, D), lambda b, i: (b, i, 0)),
            pl.BlockSpec((PROJ_WIDTH, D), lambda b, i: (0, 0)),
        ],
        out_specs=pl.BlockSpec((1, PROJ_WIDTH, INPROJ_TOKENS), lambda b, i: (b, 0, i)),
        out_shape=jax.ShapeDtypeStruct((B, PROJ_WIDTH, S), BF16),
        compiler_params=_params("arbitrary", "arbitrary"),
        name="inproj",
    )(x, w_in_t)


def _moba_kernel(far_ref, q_ref, k_ref, v_ref, own_ref, prev_ref, o_ref,
                 kmean_ref, sel_ref, m_ref, l_ref, acc_ref, *, n_blocks):
    L = MOBA_BLOCK
    h = pl.program_id(1)
    qi = pl.program_id(2)

    @pl.when(qi == 0)
    def _():
        chunk_blocks = min(8, n_blocks)
        chunk = chunk_blocks * L
        acc = jnp.zeros((n_blocks, ATTN_HEAD_DIM), F32)
        for c in range(n_blocks // chunk_blocks):
            blk = lax.broadcasted_iota(jnp.int32, (n_blocks, chunk), 0)
            key = lax.broadcasted_iota(jnp.int32, (n_blocks, chunk), 1)
            ind = jnp.where(blk == c * chunk_blocks + key // L, 1.0 / L, 0.0).astype(BF16)
            acc = acc + lax.dot_general(ind, k_ref[0, :, c * chunk:(c + 1) * chunk], _NT,
                                        preferred_element_type=F32)
        kmean_ref[...] = acc

    q = q_ref[0]
    gate = jnp.dot(kmean_ref[...], q.astype(F32), precision=lax.Precision.HIGHEST,
                   preferred_element_type=F32)
    blk = lax.broadcasted_iota(jnp.int32, gate.shape, 0)
    past = blk < qi
    gate = jnp.where(past, gate, -jnp.inf)
    chosen = jnp.zeros(gate.shape, jnp.bool_)
    for _ in range(MOBA_TOPK):
        top = jnp.max(gate, axis=0, keepdims=True)
        first = jnp.min(jnp.where(gate == top, blk, n_blocks), axis=0, keepdims=True)
        hit = blk == first
        chosen = jnp.logical_or(chosen, hit)
        gate = jnp.where(hit, -jnp.inf, gate)
    sel_ref[...] = jnp.where(jnp.logical_and(chosen, past), 0.0, MASKED)

    qs = q * (ATTN_HEAD_DIM ** -0.5)

    def scores(j):
        start = pl.multiple_of(j * L, L)
        kt = k_ref[0, :, pl.ds(start, L)]
        return lax.dot_general(kt, qs, _TN, preferred_element_type=F32)

    def accumulate(j, s):
        start = pl.multiple_of(j * L, L)
        vt = v_ref[0, :, pl.ds(start, L)]
        m_prev = m_ref[...]
        m_new = jnp.maximum(m_prev, jnp.max(s, axis=0, keepdims=True))
        alpha = jnp.exp(m_prev - m_new)
        p = jnp.exp(s - m_new)
        l_ref[...] = alpha * l_ref[...] + jnp.sum(p, axis=0, keepdims=True)
        acc_ref[...] = alpha * acc_ref[...] + jnp.dot(vt, p.astype(BF16), preferred_element_type=F32)
        m_ref[...] = m_new

    m_ref[...] = jnp.full(m_ref.shape, MASKED, F32)
    l_ref[...] = jnp.zeros(l_ref.shape, F32)
    acc_ref[...] = jnp.zeros(acc_ref.shape, F32)
    accumulate(qi, scores(qi) + own_ref[0])

    @pl.when(qi >= 1)
    def _():
        j = qi - 1
        accumulate(j, scores(j) + prev_ref[0] + sel_ref[pl.ds(j, 1), :])

    far_bias = far_ref[h]

    def far_block(j, carry):
        accumulate(j, scores(j) + (sel_ref[pl.ds(j, 1), :] + far_bias))
        return carry

    lax.fori_loop(0, jnp.maximum(qi - 1, 0), far_block, 0)
    o_ref[0] = acc_ref[...] / l_ref[...]


def _t5_bucket_np(dist):
    n = np.maximum(dist, 0)
    max_exact = REL_BUCKETS // 2
    nf = np.maximum(n, 1).astype(np.float32)
    large = max_exact + (np.log(nf / max_exact) / math.log(REL_MAX_DIST / max_exact)
                         * (REL_BUCKETS - max_exact)).astype(np.int32)
    large = np.minimum(large, REL_BUCKETS - 1)
    return np.where(n < max_exact, n, large)


def _moba(proj_t, rel_table):
    B, _, S = proj_t.shape
    L = MOBA_BLOCK
    n_blocks = S // L
    H = ATTN_HEADS
    table_t = rel_table.astype(F32).T
    key = np.arange(L)[:, None]
    qry = np.arange(L)[None, :]
    d_own = qry - key
    own = jnp.where(d_own >= 0, table_t[:, _t5_bucket_np(d_own)], MASKED)
    prev = table_t[:, _t5_bucket_np(d_own + L)]
    assert int(_t5_bucket_np(np.array(L + 1))) == REL_BUCKETS - 1
    far = table_t[:, REL_BUCKETS - 1]

    hb = ATTN_WIDTH // ATTN_HEAD_DIM
    grid_spec = pltpu.PrefetchScalarGridSpec(
        num_scalar_prefetch=1,
        grid=(B, H, n_blocks),
        in_specs=[
            pl.BlockSpec((1, ATTN_HEAD_DIM, L), lambda b, h, i, far: (b, h, i)),
            pl.BlockSpec((1, ATTN_HEAD_DIM, S), lambda b, h, i, far: (b, hb + h, 0)),
            pl.BlockSpec((1, ATTN_HEAD_DIM, S), lambda b, h, i, far: (b, 2 * hb + h, 0)),
            pl.BlockSpec((1, L, L), lambda b, h, i, far: (h, 0, 0)),
            pl.BlockSpec((1, L, L), lambda b, h, i, far: (h, 0, 0)),
        ],
        out_specs=pl.BlockSpec((1, ATTN_HEAD_DIM, L), lambda b, h, i, far: (b, h, i)),
        scratch_shapes=[
            pltpu.VMEM((n_blocks, ATTN_HEAD_DIM), F32),
            pltpu.VMEM((n_blocks, L), F32),
            pltpu.VMEM((1, L), F32),
            pltpu.VMEM((1, L), F32),
            pltpu.VMEM((ATTN_HEAD_DIM, L), F32),
        ],
    )
    return pl.pallas_call(
        functools.partial(_moba_kernel, n_blocks=n_blocks),
        grid_spec=grid_spec,
        out_shape=jax.ShapeDtypeStruct((B, ATTN_WIDTH, S), F32),
        compiler_params=_params("arbitrary", "arbitrary", "arbitrary"),
        name="moba",
    )(far, proj_t, proj_t, proj_t, own, prev)


def _retention_kernel(cdec_ref, q_ref, k_ref, v_ref, g_ref, cos_ref, sin_ref,
                      dec_ref, qdec_ref, kdec_ref, o_ref, state_ref):
    c = pl.program_id(1)

    @pl.when(c == 0)
    def _():
        state_ref[...] = jnp.zeros(state_ref.shape, F32)

    cos = cos_ref[...]
    sin = sin_ref[...]
    half = RET_HEAD_DIM // 2

    def rope(t):
        t1, t2 = t[:half], t[half:]
        return jnp.concatenate([t1 * cos - t2 * sin, t1 * sin + t2 * cos], axis=0)

    for hh in range(RET_HEADS):
        rows = slice(hh * RET_HEAD_DIM, (hh + 1) * RET_HEAD_DIM)
        q = rope(q_ref[0, rows, :].astype(F32))
        k = rope(k_ref[0, rows, :].astype(F32)) * (RET_HEAD_DIM ** -0.5)
        v = v_ref[0, rows, :]
        qb = q.astype(BF16)
        s = lax.dot_general(k.astype(BF16), qb, _TN, preferred_element_type=F32)
        s = (s * dec_ref[hh]).astype(BF16)
        inner = jnp.dot(v, s, preferred_element_type=F32)
        state = state_ref[hh]
        cross = jnp.dot(state.astype(BF16), (q * qdec_ref[hh:hh + 1, :]).astype(BF16),
                        preferred_element_type=F32)
        kd = (k * kdec_ref[hh:hh + 1, :]).astype(BF16)
        state_ref[hh] = state * cdec_ref[hh] + lax.dot_general(v, kd, _NT, preferred_element_type=F32)
        out = inner + cross
        mu = jnp.mean(out, axis=0, keepdims=True)
        var = jnp.mean(jnp.square(out - mu), axis=0, keepdims=True)
        out = (out - mu) * lax.rsqrt(var + LN_EPS)
        g = g_ref[0, rows, :].astype(F32)
        o_ref[0, rows, :] = (g * jax.nn.sigmoid(g) * out).astype(BF16)


def _retention(proj_t):
    B, _, S = proj_t.shape
    C = RET_CHUNK
    H = RET_HEADS
    half = RET_HEAD_DIM // 2
    inv = ROPE_BASE ** (-jnp.arange(half, dtype=F32) / half)
    ang = inv[:, None] * jnp.arange(S).astype(F32)[None, :]
    cos_t, sin_t = jnp.cos(ang), jnp.sin(ang)
    gammas = 1.0 - jnp.exp(jnp.linspace(math.log(1.0 / 32), math.log(1.0 / 512), H, dtype=F32))
    log_g = jnp.log(gammas)
    idx = jnp.arange(C, dtype=F32)
    diff = idx[None, :] - idx[:, None]
    dec_t = jnp.where(diff[None] >= 0, jnp.exp(jnp.maximum(diff, 0.0)[None] * log_g[:, None, None]), 0.0)
    q_dec = jnp.exp((idx[None, :] + 1.0) * log_g[:, None])
    k_dec = jnp.exp((C - 1.0 - idx[None, :]) * log_g[:, None])
    chunk_dec = jnp.exp(C * log_g)

    rb = RET_WIDTH // RET_WIDTH
    grid_spec = pltpu.PrefetchScalarGridSpec(
        num_scalar_prefetch=1,
        grid=(B, S // C),
        in_specs=[
            pl.BlockSpec((1, RET_WIDTH, C), lambda b, c, cd: (b, 3 * rb, c)),
            pl.BlockSpec((1, RET_WIDTH, C), lambda b, c, cd: (b, 4 * rb, c)),
            pl.BlockSpec((1, RET_WIDTH, C), lambda b, c, cd: (b, 5 * rb, c)),
            pl.BlockSpec((1, RET_WIDTH, C), lambda b, c, cd: (b, 6 * rb, c)),
            pl.BlockSpec((half, C), lambda b, c, cd: (0, c)),
            pl.BlockSpec((half, C), lambda b, c, cd: (0, c)),
            pl.BlockSpec((H, C, C), lambda b, c, cd: (0, 0, 0)),
            pl.BlockSpec((H, C), lambda b, c, cd: (0, 0)),
            pl.BlockSpec((H, C), lambda b, c, cd: (0, 0)),
        ],
        out_specs=pl.BlockSpec((1, RET_WIDTH, C), lambda b, c, cd: (b, 0, c)),
        scratch_shapes=[pltpu.VMEM((H, RET_HEAD_DIM, RET_HEAD_DIM), F32)],
    )
    return pl.pallas_call(
        _retention_kernel,
        grid_spec=grid_spec,
        out_shape=jax.ShapeDtypeStruct((B, RET_WIDTH, S), BF16),
        compiler_params=_params("arbitrary", "arbitrary"),
        name="retention",
    )(chunk_dec, proj_t, proj_t, proj_t, proj_t, cos_t, sin_t, dec_t, q_dec, k_dec)


def _layer_norm_rows(y, g, b):
    mu = jnp.mean(y, axis=-1, keepdims=True)
    var = jnp.mean(jnp.square(y - mu), axis=-1, keepdims=True)
    return (y - mu) * lax.rsqrt(var + LN_EPS) * g + b


def _outproj_kernel(attn_ref, ret_ref, x_ref, gain_ref, wa_ref, wr_ref, g1_ref, b1_ref,
                    wrt_ref, brt_ref,
                    h_ref, eid_ref, rank_ref, cw_ref, cnt_ref, carry_ref):
    first = jnp.logical_and(pl.program_id(0) == 0, pl.program_id(1) == 0)

    @pl.when(first)
    def _():
        carry_ref[...] = jnp.zeros(carry_ref.shape, F32)

    a = attn_ref[0]
    a = a * lax.rsqrt(jnp.mean(jnp.square(a), axis=0, keepdims=True) + LN_EPS) * gain_ref[...]
    mix = (lax.dot_general(a.astype(BF16), wa_ref[...], _TN, preferred_element_type=F32)
           + lax.dot_general(ret_ref[0], wr_ref[...], _TN, preferred_element_type=F32))
    h = _layer_norm_rows(ALPHA * x_ref[0] + mix, g1_ref[...], b1_ref[...])
    h_ref[...] = h

    logits = lax.dot_general(wrt_ref[...], h, _NT, precision=lax.Precision.HIGHEST,
                             preferred_element_type=F32) + brt_ref[...]
    T = logits.shape[1]
    gl = logits[0:N_GROUPS]
    gmax = jnp.max(gl, axis=0, keepdims=True)
    grow = lax.broadcasted_iota(jnp.int32, gl.shape, 0)
    gidx = jnp.min(jnp.where(gl == gmax, grow, N_GROUPS), axis=0, keepdims=True)
    g_p = 1.0 / jnp.sum(jnp.exp(gl - gmax), axis=0, keepdims=True)
    el = jnp.zeros((EXPERTS_PER_GROUP, T), F32)
    for g in range(N_GROUPS):
        r0 = ROUTER_EXPERT_ROW0 + g * EXPERTS_PER_GROUP
        el = jnp.where(gidx == g, logits[r0:r0 + EXPERTS_PER_GROUP], el)
    erow = lax.broadcasted_iota(jnp.int32, el.shape, 0)
    e1 = jnp.max(el, axis=0, keepdims=True)
    i1 = jnp.min(jnp.where(el == e1, erow, EXPERTS_PER_GROUP), axis=0, keepdims=True)
    el2 = jnp.where(erow == i1, -jnp.inf, el)
    e2 = jnp.max(el2, axis=0, keepdims=True)
    i2 = jnp.min(jnp.where(el2 == e2, erow, EXPERTS_PER_GROUP), axis=0, keepdims=True)
    r = jnp.exp(e2 - e1)
    w1 = g_p / (1.0 + r)
    w2 = g_p * r / (1.0 + r)
    id1 = gidx * EXPERTS_PER_GROUP + i1
    id2 = gidx * EXPERTS_PER_GROUP + i2
    eid_ref[0:1, :] = id1
    eid_ref[1:2, :] = id2
    cw_ref[0:1, :] = w1
    cw_ref[1:2, :] = w2

    xrow = lax.broadcasted_iota(jnp.int32, (N_EXPERTS, T), 0)
    oh1 = (xrow == id1).astype(F32)
    oh2 = (xrow == id2).astype(F32)
    both = oh1 + oh2
    before = (lax.broadcasted_iota(jnp.int32, (T, T), 0)
              < lax.broadcasted_iota(jnp.int32, (T, T), 1)).astype(BF16)
    seen = jnp.dot(both.astype(BF16), before, preferred_element_type=F32) + carry_ref[...]
    rank_ref[0:1, :] = jnp.sum(oh1 * seen, axis=0, keepdims=True).astype(jnp.int32)
    rank_ref[1:2, :] = jnp.sum(oh2 * seen, axis=0, keepdims=True).astype(jnp.int32)
    carry = carry_ref[...] + jnp.sum(both, axis=1, keepdims=True)
    carry_ref[...] = carry
    cnt_ref[...] = carry.astype(jnp.int32)


def _outproj(attn_t, ret_t, x, gain, w_out, ln_g, ln_b, w_rg, b_rg, w_re, b_re):
    B, S, D = x.shape
    Tt = OUTPROJ_TOKENS
    n_t = S // Tt
    T = B * S
    wa = w_out[:ATTN_WIDTH].astype(BF16)
    wr = w_out[ATTN_WIDTH:].astype(BF16)
    wrt = jnp.zeros((ROUTER_ROWS, D), F32)
    wrt = wrt.at[:N_GROUPS].set(w_rg.T).at[ROUTER_EXPERT_ROW0:].set(w_re.T)
    brt = jnp.zeros((ROUTER_ROWS, 1), F32)
    brt = brt.at[:N_GROUPS, 0].set(b_rg.astype(F32)).at[ROUTER_EXPERT_ROW0:, 0].set(b_re.astype(F32))
    const = lambda b, i: (0, 0)
    tok = lambda b, i: (0, b * n_t + i)
    return pl.pallas_call(
        _outproj_kernel,
        grid=(B, n_t),
        in_specs=[
            pl.BlockSpec((1, ATTN_WIDTH, Tt), lambda b, i: (b, 0, i)),
            pl.BlockSpec((1, RET_WIDTH, Tt), lambda b, i: (b, 0, i)),
            pl.BlockSpec((1, Tt, D), lambda b, i: (b, i, 0)),
            pl.BlockSpec((ATTN_WIDTH, 1), const),
            pl.BlockSpec((ATTN_WIDTH, D), const),
            pl.BlockSpec((RET_WIDTH, D), const),
            pl.BlockSpec((1, D), const),
            pl.BlockSpec((1, D), const),
            pl.BlockSpec((ROUTER_ROWS, D), const),
            pl.BlockSpec((ROUTER_ROWS, 1), const),
        ],
        out_specs=[
            pl.BlockSpec((Tt, D), lambda b, i: (b * n_t + i, 0)),
            pl.BlockSpec((2, Tt), tok),
            pl.BlockSpec((2, Tt), tok),
            pl.BlockSpec((2, Tt), tok),
            pl.BlockSpec((N_EXPERTS, 1), const),
        ],
        out_shape=[
            jax.ShapeDtypeStruct((T, D), F32),
            jax.ShapeDtypeStruct((2, T), jnp.int32),
            jax.ShapeDtypeStruct((2, T), jnp.int32),
            jax.ShapeDtypeStruct((2, T), F32),
            jax.ShapeDtypeStruct((N_EXPERTS, 1), jnp.int32),
        ],
        scratch_shapes=[pltpu.VMEM((N_EXPERTS, 1), F32)],
        compiler_params=_params("arbitrary", "arbitrary"),
        name="outproj_ln_router",
    )(attn_t, ret_t, x, gain.reshape(ATTN_WIDTH, 1), wa, wr, ln_g.reshape(1, D), ln_b.reshape(1, D),
      wrt, brt)


def _dispatch_kernel(start_ref, eid_ref, rank_ref, h_hbm, xs_in_hbm, xs_hbm, sem):
    del xs_in_hbm
    G = ROUTE_TOKENS
    base = pl.program_id(0) * G

    def row_copy(t, slot):
        dst = start_ref[eid_ref[slot, t]] + rank_ref[slot, t]
        return pltpu.make_async_copy(h_hbm.at[pl.ds(base + t, 1)], xs_hbm.at[pl.ds(dst, 1)], sem)

    def issue(t, carry):
        row_copy(t, 0).start()
        row_copy(t, 1).start()
        return carry

    lax.fori_loop(0, G, issue, 0)

    def drain(t, carry):
        row_copy(t, 0).wait()
        row_copy(t, 1).wait()
        return carry

    lax.fori_loop(0, G, drain, 0)


def _dispatch(h, eid, rank, starts, n_rows):
    T, D = h.shape
    G = ROUTE_TOKENS
    smem_blk = pl.BlockSpec((2, G), lambda i, st: (0, i), memory_space=pltpu.SMEM)
    grid_spec = pltpu.PrefetchScalarGridSpec(
        num_scalar_prefetch=1,
        grid=(T // G,),
        in_specs=[smem_blk, smem_blk,
                  pl.BlockSpec(memory_space=pl.ANY), pl.BlockSpec(memory_space=pl.ANY)],
        out_specs=pl.BlockSpec(memory_space=pl.ANY),
        scratch_shapes=[pltpu.SemaphoreType.DMA(())],
    )
    return pl.pallas_call(
        _dispatch_kernel,
        grid_spec=grid_spec,
        out_shape=jax.ShapeDtypeStruct((n_rows, D), F32),
        input_output_aliases={4: 0},
        compiler_params=_params("arbitrary"),
        name="moe_dispatch",
    )(starts, eid, rank, h, jnp.zeros((n_rows, D), F32))


def _experts_kernel(te_ref, nv_ref, xs_ref, wgu_ref, wdn_ref, y_ref):
    i = pl.program_id(0)

    @pl.when(i < nv_ref[0])
    def _():
        gu = jnp.dot(xs_ref[...].astype(BF16), wgu_ref[0], preferred_element_type=F32)
        gate, up = gu[:, :D_EXPERT], gu[:, D_EXPERT:]
        hid = gate * jax.nn.sigmoid(gate) * up
        y_ref[...] = jnp.dot(hid.astype(BF16), wdn_ref[0], preferred_element_type=F32)

    @pl.when(i >= nv_ref[0])
    def _():
        y_ref[...] = jnp.zeros(y_ref.shape, F32)


def _experts(xs, w_gu, w_dn, tile_expert, n_valid):
    n_rows, D = xs.shape
    Tm = MOE_ROWS
    grid_spec = pltpu.PrefetchScalarGridSpec(
        num_scalar_prefetch=2,
        grid=(n_rows // Tm,),
        in_specs=[
            pl.BlockSpec((Tm, D), lambda i, te, nv: (i, 0)),
            pl.BlockSpec((1, D, 2 * D_EXPERT), lambda i, te, nv: (te[i], 0, 0)),
            pl.BlockSpec((1, D_EXPERT, D), lambda i, te, nv: (te[i], 0, 0)),
        ],
        out_specs=pl.BlockSpec((Tm, D), lambda i, te, nv: (i, 0)),
    )
    return pl.pallas_call(
        _experts_kernel,
        grid_spec=grid_spec,
        out_shape=jax.ShapeDtypeStruct((n_rows, D), F32),
        compiler_params=_params("arbitrary"),
        name="moe_experts",
    )(tile_expert, n_valid, xs, w_gu, w_dn)


def _combine_kernel(start_ref, eid_ref, rank_ref, y_hbm, h_ref, cw_ref, g_ref, b_ref, o_ref,
                    ybuf, sem):
    G = ROUTE_TOKENS

    def row_copy(t, slot):
        src = start_ref[eid_ref[slot, t]] + rank_ref[slot, t]
        return pltpu.make_async_copy(y_hbm.at[pl.ds(src, 1)], ybuf.at[slot, pl.ds(t, 1)], sem)

    def issue(t, carry):
        row_copy(t, 0).start()
        row_copy(t, 1).start()
        return carry

    lax.fori_loop(0, G, issue, 0)

    def drain(t, carry):
        row_copy(t, 0).wait()
        row_copy(t, 1).wait()
        return carry

    lax.fori_loop(0, G, drain, 0)

    cw = cw_ref[...]
    ffn = ybuf[0] * cw[:, 0:1] + ybuf[1] * cw[:, 1:2]
    o_ref[...] = _layer_norm_rows(ALPHA * h_ref[...] + ffn, g_ref[...], b_ref[...])


def _combine(y, h, eid, rank, starts, cw_rows, ln_g, ln_b):
    T, D = h.shape
    G = ROUTE_TOKENS
    smem_blk = pl.BlockSpec((2, G), lambda i, st: (0, i), memory_space=pltpu.SMEM)
    grid_spec = pltpu.PrefetchScalarGridSpec(
        num_scalar_prefetch=1,
        grid=(T // G,),
        in_specs=[
            smem_blk, smem_blk,
            pl.BlockSpec(memory_space=pl.ANY),
            pl.BlockSpec((G, D), lambda i, st: (i, 0)),
            pl.BlockSpec((G, 2), lambda i, st: (i, 0)),
            pl.BlockSpec((1, D), lambda i, st: (0, 0)),
            pl.BlockSpec((1, D), lambda i, st: (0, 0)),
        ],
        out_specs=pl.BlockSpec((G, D), lambda i, st: (i, 0)),
        scratch_shapes=[pltpu.VMEM((2, G, D), F32), pltpu.SemaphoreType.DMA(())],
    )
    return pl.pallas_call(
        _combine_kernel,
        grid_spec=grid_spec,
        out_shape=jax.ShapeDtypeStruct((T, D), F32),
        compiler_params=_params("arbitrary"),
        name="moe_combine_ln",
    )(starts, eid, rank, y, h, cw_rows, ln_g.reshape(1, D), ln_b.reshape(1, D))


def kernel(x, w_in, attn_out_gain, rel_bias_table, w_out, ln1_g, ln1_b, w_router_group,
           b_router_group, w_router_expert, b_router_expert, w_gate_up, w_down, ln2_g, ln2_b):
    B, S, D = x.shape
    assert D == D_MODEL and S % MOBA_BLOCK == 0 and S % INPROJ_TOKENS == 0
    assert w_in.shape[0] == DEPTH
    T = B * S
    h = x
    for l in range(DEPTH):
        proj_t = _inproj(h, w_in[l].T.astype(BF16))
        attn_t = _moba(proj_t, rel_bias_table)
        ret_t = _retention(proj_t)
        h1, eid, rank, cw, counts = _outproj(
            attn_t, ret_t, h, attn_out_gain[l], w_out[l], ln1_g[l], ln1_b[l],
            w_router_group[l], b_router_group[l], w_router_expert[l], b_router_expert[l])

        Tm = MOE_ROWS
        n_tiles = (2 * T) // Tm + N_EXPERTS
        padded = ((counts[:, 0] + Tm - 1) // Tm) * Tm
        ends = jnp.cumsum(padded)
        starts = (ends - padded).astype(jnp.int32)
        tile_row0 = jnp.arange(n_tiles, dtype=jnp.int32) * Tm
        tile_expert = jnp.minimum(
            jnp.sum((ends[None, :] <= tile_row0[:, None]).astype(jnp.int32), axis=1), N_EXPERTS - 1)
        n_valid = (ends[-1:] // Tm).astype(jnp.int32)

        xs = _dispatch(h1, eid, rank, starts, n_tiles * Tm)
        y = _experts(xs, w_gate_up[l].astype(BF16), w_down[l].astype(BF16), tile_expert, n_valid)
        h2 = _combine(y, h1, eid, rank, starts, cw.T, ln2_g[l], ln2_b[l])
        h = h2.reshape(B, S, D)
    return h
```

```python
import functools
import math

import numpy as np
import jax
import jax.numpy as jnp
from jax import lax
from jax.experimental import pallas as pl
from jax.experimental.pallas import tpu as pltpu

F32 = jnp.float32
BF16 = jnp.bfloat16

D_MODEL = 1024
ATTN_HEADS = 8
ATTN_HEAD_DIM = 64
ATTN_WIDTH = ATTN_HEADS * ATTN_HEAD_DIM
RET_HEADS = 4
RET_HEAD_DIM = 128
RET_WIDTH = RET_HEADS * RET_HEAD_DIM
N_IN_SLICES = 7
PROJT_SLICES = N_IN_SLICES - 1
PROJT_WIDTH = PROJT_SLICES * ATTN_WIDTH
HEAD_PAIR = 2 * ATTN_HEAD_DIM
MOBA_BLOCK = 256
MOBA_TOPK = 3
RET_CHUNK = 256
REL_BUCKETS = 32
REL_MAX_DIST = 128
ROPE_BASE = 10000.0
N_GROUPS = 4
EXPERTS_PER_GROUP = 8
N_EXPERTS = N_GROUPS * EXPERTS_PER_GROUP
D_EXPERT = D_MODEL // 2
DEPTH = 1
ALPHA = (2.0 * DEPTH) ** 0.25
LN_EPS = 1e-5

MASKED = -1e30

INPROJ_TOKENS = 512
INPROJ_ROWS = 512
OUTPROJ_TOKENS = 256
MOE_ROWS = 256
ROUTE_TOKENS = 256
ROUTER_EXPERT_ROW0 = 8
ROUTER_ROWS = ROUTER_EXPERT_ROW0 + N_EXPERTS

VMEM_LIMIT = 48 * 1024 * 1024

_TN = (((0,), (0,)), ((), ()))
_NT = (((1,), (1,)), ((), ()))


def _params(*sem):
    return pltpu.CompilerParams(dimension_semantics=sem, vmem_limit_bytes=VMEM_LIMIT)


def _inproj_kernel(x_ref, wt_ref, wk_ref, o_ref, k_ref):
    xb = x_ref[0].astype(BF16)
    for c in range(PROJT_WIDTH // INPROJ_ROWS):
        rows = slice(c * INPROJ_ROWS, (c + 1) * INPROJ_ROWS)
        o_ref[0, rows, :] = lax.dot_general(
            wt_ref[rows, :], xb, _NT, preferred_element_type=F32).astype(BF16)
    k_ref[0] = jnp.dot(xb, wk_ref[...], preferred_element_type=F32).astype(BF16)


def _inproj(x, w_in):
    B, S, D = x.shape
    W = ATTN_WIDTH
    w_t = jnp.concatenate([w_in[:, :W], w_in[:, 2 * W:]], axis=1).T.astype(BF16)
    w_k = w_in[:, W:2 * W].astype(BF16)
    return pl.pallas_call(
        _inproj_kernel,
        grid=(B, S // INPROJ_TOKENS),
        in_specs=[
            pl.BlockSpec((1, INPROJ_TOKENS, D), lambda b, i: (b, i, 0)),
            pl.BlockSpec((PROJT_WIDTH, D), lambda b, i: (0, 0)),
            pl.BlockSpec((D, W), lambda b, i: (0, 0)),
        ],
        out_specs=[
            pl.BlockSpec((1, PROJT_WIDTH, INPROJ_TOKENS), lambda b, i: (b, 0, i)),
            pl.BlockSpec((1, INPROJ_TOKENS, W), lambda b, i: (b, i, 0)),
        ],
        out_shape=[
            jax.ShapeDtypeStruct((B, PROJT_WIDTH, S), BF16),
            jax.ShapeDtypeStruct((B, S, W), BF16),
        ],
        compiler_params=_params("arbitrary", "arbitrary"),
        name="inproj",
    )(x, w_t, w_k)


def _moba_kernel(far_ref, q_ref, k_ref, v_ref, bias_ref, o_ref,
                 kmean_ref, sel_ref, farsel_ref, s0_ref, s1_ref, m_ref, l_ref, acc_ref, *, n_blocks):
    L = MOBA_BLOCK
    Dh = ATTN_HEAD_DIM
    pair = pl.program_id(1)
    qi = pl.program_id(2)
    last_blk = n_blocks - 1

    @pl.when(qi == 0)
    def _():
        chunk_blocks = min(8, n_blocks)
        chunk = chunk_blocks * L
        acc = jnp.zeros((n_blocks, HEAD_PAIR), F32)
        for c in range(n_blocks // chunk_blocks):
            blk = lax.broadcasted_iota(jnp.int32, (n_blocks, chunk), 0)
            key = lax.broadcasted_iota(jnp.int32, (n_blocks, chunk), 1)
            ind = jnp.where(blk == c * chunk_blocks + key // L, 1.0 / L, 0.0).astype(BF16)
            acc = acc + jnp.dot(ind, k_ref[0, c * chunk:(c + 1) * chunk, :], preferred_element_type=F32)
        kmean_ref[...] = acc

    q = q_ref[0]
    feat = lax.broadcasted_iota(jnp.int32, q.shape, 0)
    qz = [jnp.where((feat >= hd * Dh) & (feat < (hd + 1) * Dh), q, jnp.zeros_like(q)) for hd in range(2)]
    kmean = kmean_ref[...]
    blk = lax.broadcasted_iota(jnp.int32, (n_blocks, L), 0)
    past = blk < qi
    for hd in range(2):
        gate = jnp.dot(kmean, qz[hd].astype(F32), precision=lax.Precision.HIGHEST,
                       preferred_element_type=F32)
        gate = jnp.where(past, gate, -jnp.inf)
        chosen = jnp.zeros(gate.shape, jnp.bool_)
        for _ in range(MOBA_TOPK):
            top = jnp.max(gate, axis=0, keepdims=True)
            first = jnp.min(jnp.where(gate == top, blk, n_blocks), axis=0, keepdims=True)
            hit = blk == first
            chosen = jnp.logical_or(chosen, hit)
            gate = jnp.where(hit, -jnp.inf, gate)
        chosen = jnp.logical_and(chosen, past)
        sel_ref[hd] = jnp.where(chosen, 0.0, MASKED)
        farsel_ref[hd] = jnp.where(jnp.logical_and(chosen, blk < qi - 1), far_ref[2 * pair + hd], MASKED)

    qs = [z * (Dh ** -0.5) for z in qz]

    def k_block(j):
        return k_ref[0, pl.ds(pl.multiple_of(j * L, L), L), :]

    def v_block(hd, j):
        return v_ref[0, hd * Dh:(hd + 1) * Dh, pl.ds(pl.multiple_of(j * L, L), L)]

    def far_blocks(g):
        b0 = 2 * g - 2
        return jnp.minimum(b0, last_blk), jnp.minimum(b0 + 1, last_blk)

    def far_scores(g, s_ref):
        for u, j in enumerate(far_blocks(g)):
            kb = k_block(j)
            for hd in range(2):
                s = jnp.dot(kb, qs[hd], preferred_element_type=F32)
                s_ref[hd, u * L:(u + 1) * L, :] = s + farsel_ref[hd, pl.ds(j, 1), :]

    def softmax_step(s_ref, j0, j1):
        for hd in range(2):
            s = s_ref[hd]
            m_prev = m_ref[hd]
            m_new = jnp.maximum(m_prev, jnp.max(s, axis=0, keepdims=True))
            alpha = jnp.exp(m_prev - m_new)
            p = jnp.exp(s - m_new)
            l_ref[hd] = alpha * l_ref[hd] + jnp.sum(p, axis=0, keepdims=True)
            pb = p.astype(BF16)
            pv = (jnp.dot(v_block(hd, j0), pb[:L], preferred_element_type=F32)
                  + jnp.dot(v_block(hd, j1), pb[L:], preferred_element_type=F32))
            acc_ref[hd] = alpha * acc_ref[hd] + pv
            m_ref[hd] = m_new

    m_ref[...] = jnp.full(m_ref.shape, MASKED, F32)
    l_ref[...] = jnp.zeros(l_ref.shape, F32)
    acc_ref[...] = jnp.zeros(acc_ref.shape, F32)

    jp = jnp.maximum(qi - 1, 0)
    kp, ko = k_block(jp), k_block(qi)
    for hd in range(2):
        s_prev = jnp.dot(kp, qs[hd], preferred_element_type=F32)
        s0_ref[hd, 0:L, :] = s_prev + bias_ref[1, hd] + sel_ref[hd, pl.ds(jp, 1), :]
        s_own = jnp.dot(ko, qs[hd], preferred_element_type=F32)
        s0_ref[hd, L:2 * L, :] = s_own + bias_ref[0, hd]

    n_far = jnp.maximum(qi - 1, 0)
    n_groups = 1 + (n_far + 1) // 2

    def two_groups(k, carry):
        g = 2 * k
        far_scores(g + 1, s1_ref)
        f0, f1 = far_blocks(g)
        first = k == 0
        softmax_step(s0_ref, jnp.where(first, jp, f0), jnp.where(first, qi, f1))
        far_scores(g + 2, s0_ref)
        softmax_step(s1_ref, *far_blocks(g + 1))
        return carry

    lax.fori_loop(0, (n_groups + 1) // 2, two_groups, 0)
    for hd in range(2):
        o_ref[0, hd * Dh:(hd + 1) * Dh, :] = acc_ref[hd] / l_ref[hd]


def _t5_bucket_np(dist):
    n = np.maximum(dist, 0)
    max_exact = REL_BUCKETS // 2
    nf = np.maximum(n, 1).astype(np.float32)
    large = max_exact + (np.log(nf / max_exact) / math.log(REL_MAX_DIST / max_exact)
                         * (REL_BUCKETS - max_exact)).astype(np.int32)
    large = np.minimum(large, REL_BUCKETS - 1)
    return np.where(n < max_exact, n, large)


def _moba(proj_t, k_nat, rel_table):
    B, _, S = proj_t.shape
    L = MOBA_BLOCK
    n_blocks = S // L
    assert n_blocks % 2 == 0
    table_t = rel_table.astype(F32).T
    key = np.arange(L)[:, None]
    qry = np.arange(L)[None, :]
    d_own = qry - key
    bucket = np.stack([_t5_bucket_np(d_own), _t5_bucket_np(d_own + L)])
    onehot = jnp.asarray(bucket)[:, None, None] == jnp.arange(REL_BUCKETS)[None, None, :, None, None]
    bias = jnp.sum(jnp.where(onehot, table_t[None, :, :, None, None], 0.0), axis=2)
    causal = jnp.asarray(np.stack([d_own >= 0, np.ones_like(d_own, bool)]))[:, None]
    bias = jnp.where(causal, bias, MASKED)
    assert int(_t5_bucket_np(np.array(L + 1))) == REL_BUCKETS - 1
    far = table_t[:, REL_BUCKETS - 1]

    pairs = ATTN_WIDTH // HEAD_PAIR
    grid_spec = pltpu.PrefetchScalarGridSpec(
        num_scalar_prefetch=1,
        grid=(B, pairs, n_blocks),
        in_specs=[
            pl.BlockSpec((1, HEAD_PAIR, L), lambda b, p, i, far: (b, p, i)),
            pl.BlockSpec((1, S, HEAD_PAIR), lambda b, p, i, far: (b, 0, p)),
            pl.BlockSpec((1, HEAD_PAIR, S), lambda b, p, i, far: (b, pairs + p, 0)),
            pl.BlockSpec((2, 2, L, L), lambda b, p, i, far: (0, p, 0, 0)),
        ],
        out_specs=pl.BlockSpec((1, HEAD_PAIR, L), lambda b, p, i, far: (b, p, i)),
        scratch_shapes=[
            pltpu.VMEM((n_blocks, HEAD_PAIR), F32),
            pltpu.VMEM((2, n_blocks, L), F32),
            pltpu.VMEM((2, n_blocks, L), F32),
            pltpu.VMEM((2, 2 * L, L), F32),
            pltpu.VMEM((2, 2 * L, L), F32),
            pltpu.VMEM((2, 1, L), F32),
            pltpu.VMEM((2, 1, L), F32),
            pltpu.VMEM((2, ATTN_HEAD_DIM, L), F32),
        ],
    )
    return pl.pallas_call(
        functools.partial(_moba_kernel, n_blocks=n_blocks),
        grid_spec=grid_spec,
        out_shape=jax.ShapeDtypeStruct((B, ATTN_WIDTH, S), F32),
        compiler_params=_params("arbitrary", "arbitrary", "arbitrary"),
        name="moba",
    )(far, proj_t, k_nat, proj_t, bias)


def _retention_kernel(cdec_ref, q_ref, k_ref, v_ref, g_ref, cos_ref, sin_ref,
                      dec_ref, qdec_ref, kdec_ref, o_ref, state_ref):
    c = pl.program_id(1)

    @pl.when(c == 0)
    def _():
        state_ref[...] = jnp.zeros(state_ref.shape, F32)

    cos = cos_ref[...]
    sin = sin_ref[...]
    half = RET_HEAD_DIM // 2

    def rope(t):
        t1, t2 = t[:half], t[half:]
        return jnp.concatenate([t1 * cos - t2 * sin, t1 * sin + t2 * cos], axis=0)

    for hh in range(RET_HEADS):
        rows = slice(hh * RET_HEAD_DIM, (hh + 1) * RET_HEAD_DIM)
        q = rope(q_ref[0, rows, :].astype(F32))
        k = rope(k_ref[0, rows, :].astype(F32)) * (RET_HEAD_DIM ** -0.5)
        v = v_ref[0, rows, :]
        qb = q.astype(BF16)
        s = lax.dot_general(k.astype(BF16), qb, _TN, preferred_element_type=F32)
        s = (s * dec_ref[hh]).astype(BF16)
        inner = jnp.dot(v, s, preferred_element_type=F32)
        state = state_ref[hh]
        cross = jnp.dot(state.astype(BF16), (q * qdec_ref[hh:hh + 1, :]).astype(BF16),
                        preferred_element_type=F32)
        kd = (k * kdec_ref[hh:hh + 1, :]).astype(BF16)
        state_ref[hh] = state * cdec_ref[hh] + lax.dot_general(v, kd, _NT, preferred_element_type=F32)
        out = inner + cross
        mu = jnp.mean(out, axis=0, keepdims=True)
        var = jnp.mean(jnp.square(out - mu), axis=0, keepdims=True)
        out = (out - mu) * lax.rsqrt(var + LN_EPS)
        g = g_ref[0, rows, :].astype(F32)
        o_ref[0, rows, :] = (g * jax.nn.sigmoid(g) * out).astype(BF16)


def _retention(proj_t):
    B, _, S = proj_t.shape
    C = RET_CHUNK
    H = RET_HEADS
    half = RET_HEAD_DIM // 2
    inv = ROPE_BASE ** (-jnp.arange(half, dtype=F32) / half)
    ang = inv[:, None] * jnp.arange(S).astype(F32)[None, :]
    cos_t, sin_t = jnp.cos(ang), jnp.sin(ang)
    gammas = 1.0 - jnp.exp(jnp.linspace(math.log(1.0 / 32), math.log(1.0 / 512), H, dtype=F32))
    log_g = jnp.log(gammas)
    idx = jnp.arange(C, dtype=F32)
    diff = idx[None, :] - idx[:, None]
    dec_t = jnp.where(diff[None] >= 0, jnp.exp(jnp.maximum(diff, 0.0)[None] * log_g[:, None, None]), 0.0)
    q_dec = jnp.exp((idx[None, :] + 1.0) * log_g[:, None])
    k_dec = jnp.exp((C - 1.0 - idx[None, :]) * log_g[:, None])
    chunk_dec = jnp.exp(C * log_g)

    grid_spec = pltpu.PrefetchScalarGridSpec(
        num_scalar_prefetch=1,
        grid=(B, S // C),
        in_specs=[
            pl.BlockSpec((1, RET_WIDTH, C), lambda b, c, cd: (b, 2, c)),
            pl.BlockSpec((1, RET_WIDTH, C), lambda b, c, cd: (b, 3, c)),
            pl.BlockSpec((1, RET_WIDTH, C), lambda b, c, cd: (b, 4, c)),
            pl.BlockSpec((1, RET_WIDTH, C), lambda b, c, cd: (b, 5, c)),
            pl.BlockSpec((half, C), lambda b, c, cd: (0, c)),
            pl.BlockSpec((half, C), lambda b, c, cd: (0, c)),
            pl.BlockSpec((H, C, C), lambda b, c, cd: (0, 0, 0)),
            pl.BlockSpec((H, C), lambda b, c, cd: (0, 0)),
            pl.BlockSpec((H, C), lambda b, c, cd: (0, 0)),
        ],
        out_specs=pl.BlockSpec((1, RET_WIDTH, C), lambda b, c, cd: (b, 0, c)),
        scratch_shapes=[pltpu.VMEM((H, RET_HEAD_DIM, RET_HEAD_DIM), F32)],
    )
    return pl.pallas_call(
        _retention_kernel,
        grid_spec=grid_spec,
        out_shape=jax.ShapeDtypeStruct((B, RET_WIDTH, S), BF16),
        compiler_params=_params("arbitrary", "arbitrary"),
        name="retention",
    )(chunk_dec, proj_t, proj_t, proj_t, proj_t, cos_t, sin_t, dec_t, q_dec, k_dec)


def _layer_norm_rows(y, g, b):
    mu = jnp.mean(y, axis=-1, keepdims=True)
    var = jnp.mean(jnp.square(y - mu), axis=-1, keepdims=True)
    return (y - mu) * lax.rsqrt(var + LN_EPS) * g + b


def _outproj_kernel(attn_ref, ret_ref, x_ref, gain_ref, wa_ref, wr_ref, g1_ref, b1_ref,
                    wrt_ref, brt_ref,
                    h_ref, eid_ref, rank_ref, cw_ref, cnt_ref, carry_ref):
    first = jnp.logical_and(pl.program_id(0) == 0, pl.program_id(1) == 0)

    @pl.when(first)
    def _():
        carry_ref[...] = jnp.zeros(carry_ref.shape, F32)

    a = attn_ref[0]
    a = a * lax.rsqrt(jnp.mean(jnp.square(a), axis=0, keepdims=True) + LN_EPS) * gain_ref[...]
    mix = (lax.dot_general(a.astype(BF16), wa_ref[...], _TN, preferred_element_type=F32)
           + lax.dot_general(ret_ref[0], wr_ref[...], _TN, preferred_element_type=F32))
    h = _layer_norm_rows(ALPHA * x_ref[0] + mix, g1_ref[...], b1_ref[...])
    h_ref[...] = h

    logits = lax.dot_general(wrt_ref[...], h, _NT, precision=lax.Precision.HIGHEST,
                             preferred_element_type=F32) + brt_ref[...]
    T = logits.shape[1]
    gl = logits[0:N_GROUPS]
    gmax = jnp.max(gl, axis=0, keepdims=True)
    grow = lax.broadcasted_iota(jnp.int32, gl.shape, 0)
    gidx = jnp.min(jnp.where(gl == gmax, grow, N_GROUPS), axis=0, keepdims=True)
    g_p = 1.0 / jnp.sum(jnp.exp(gl - gmax), axis=0, keepdims=True)
    el = jnp.zeros((EXPERTS_PER_GROUP, T), F32)
    for g in range(N_GROUPS):
        r0 = ROUTER_EXPERT_ROW0 + g * EXPERTS_PER_GROUP
        el = jnp.where(gidx == g, logits[r0:r0 + EXPERTS_PER_GROUP], el)
    erow = lax.broadcasted_iota(jnp.int32, el.shape, 0)
    e1 = jnp.max(el, axis=0, keepdims=True)
    i1 = jnp.min(jnp.where(el == e1, erow, EXPERTS_PER_GROUP), axis=0, keepdims=True)
    el2 = jnp.where(erow == i1, -jnp.inf, el)
    e2 = jnp.max(el2, axis=0, keepdims=True)
    i2 = jnp.min(jnp.where(el2 == e2, erow, EXPERTS_PER_GROUP), axis=0, keepdims=True)
    r = jnp.exp(e2 - e1)
    w1 = g_p / (1.0 + r)
    w2 = g_p * r / (1.0 + r)
    id1 = gidx * EXPERTS_PER_GROUP + i1
    id2 = gidx * EXPERTS_PER_GROUP + i2
    eid_ref[0:1, :] = id1
    eid_ref[1:2, :] = id2
    cw_ref[0:1, :] = w1
    cw_ref[1:2, :] = w2

    xrow = lax.broadcasted_iota(jnp.int32, (N_EXPERTS, T), 0)
    oh1 = (xrow == id1).astype(F32)
    oh2 = (xrow == id2).astype(F32)
    both = oh1 + oh2
    before = (lax.broadcasted_iota(jnp.int32, (T, T), 0)
              < lax.broadcasted_iota(jnp.int32, (T, T), 1)).astype(BF16)
    seen = jnp.dot(both.astype(BF16), before, preferred_element_type=F32) + carry_ref[...]
    rank_ref[0:1, :] = jnp.sum(oh1 * seen, axis=0, keepdims=True).astype(jnp.int32)
    rank_ref[1:2, :] = jnp.sum(oh2 * seen, axis=0, keepdims=True).astype(jnp.int32)
    carry = carry_ref[...] + jnp.sum(both, axis=1, keepdims=True)
    carry_ref[...] = carry
    cnt_ref[...] = carry.astype(jnp.int32)


def _outproj(attn_t, ret_t, x, gain, w_out, ln_g, ln_b, w_rg, b_rg, w_re, b_re):
    B, S, D = x.shape
    Tt = OUTPROJ_TOKENS
    n_t = S // Tt
    T = B * S
    wa = w_out[:ATTN_WIDTH].astype(BF16)
    wr = w_out[ATTN_WIDTH:].astype(BF16)
    wrt = jnp.zeros((ROUTER_ROWS, D), F32)
    wrt = wrt.at[:N_GROUPS].set(w_rg.T).at[ROUTER_EXPERT_ROW0:].set(w_re.T)
    brt = jnp.zeros((ROUTER_ROWS, 1), F32)
    brt = brt.at[:N_GROUPS, 0].set(b_rg.astype(F32)).at[ROUTER_EXPERT_ROW0:, 0].set(b_re.astype(F32))
    const = lambda b, i: (0, 0)
    tok = lambda b, i: (0, b * n_t + i)
    return pl.pallas_call(
        _outproj_kernel,
        grid=(B, n_t),
        in_specs=[
            pl.BlockSpec((1, ATTN_WIDTH, Tt), lambda b, i: (b, 0, i)),
            pl.BlockSpec((1, RET_WIDTH, Tt), lambda b, i: (b, 0, i)),
            pl.BlockSpec((1, Tt, D), lambda b, i: (b, i, 0)),
            pl.BlockSpec((ATTN_WIDTH, 1), const),
            pl.BlockSpec((ATTN_WIDTH, D), const),
            pl.BlockSpec((RET_WIDTH, D), const),
            pl.BlockSpec((1, D), const),
            pl.BlockSpec((1, D), const),
            pl.BlockSpec((ROUTER_ROWS, D), const),
            pl.BlockSpec((ROUTER_ROWS, 1), const),
        ],
        out_specs=[
            pl.BlockSpec((Tt, D), lambda b, i: (b * n_t + i, 0)),
            pl.BlockSpec((2, Tt), tok),
            pl.BlockSpec((2, Tt), tok),
            pl.BlockSpec((2, Tt), tok),
            pl.BlockSpec((N_EXPERTS, 1), const),
        ],
        out_shape=[
            jax.ShapeDtypeStruct((T, D), F32),
            jax.ShapeDtypeStruct((2, T), jnp.int32),
            jax.ShapeDtypeStruct((2, T), jnp.int32),
            jax.ShapeDtypeStruct((2, T), F32),
            jax.ShapeDtypeStruct((N_EXPERTS, 1), jnp.int32),
        ],
        scratch_shapes=[pltpu.VMEM((N_EXPERTS, 1), F32)],
        compiler_params=_params("arbitrary", "arbitrary"),
        name="outproj_ln_router",
    )(attn_t, ret_t, x, gain.reshape(ATTN_WIDTH, 1), wa, wr, ln_g.reshape(1, D), ln_b.reshape(1, D),
      wrt, brt)


def _dispatch_kernel(start_ref, eid_ref, rank_ref, h_ref, xs_in_hbm, xs_hbm, sem):
    del xs_in_hbm
    G = ROUTE_TOKENS

    def issue(t, carry):
        for slot in range(2):
            dst = start_ref[eid_ref[slot, t]] + rank_ref[slot, t]
            pltpu.make_async_copy(h_ref.at[pl.ds(t, 1)], xs_hbm.at[pl.ds(dst, 1)], sem).start()
        return carry

    lax.fori_loop(0, G, issue, 0, unroll=8)
    for _ in range(2):
        pltpu.make_async_copy(h_ref, xs_hbm.at[pl.ds(0, G)], sem).wait()


def _dispatch(h, eid, rank, starts, n_rows):
    T, D = h.shape
    G = ROUTE_TOKENS
    smem_blk = pl.BlockSpec((2, G), lambda i, st: (0, i), memory_space=pltpu.SMEM)
    grid_spec = pltpu.PrefetchScalarGridSpec(
        num_scalar_prefetch=1,
        grid=(T // G,),
        in_specs=[smem_blk, smem_blk,
                  pl.BlockSpec((G, D), lambda i, st: (i, 0)), pl.BlockSpec(memory_space=pl.ANY)],
        out_specs=pl.BlockSpec(memory_space=pl.ANY),
        scratch_shapes=[pltpu.SemaphoreType.DMA(())],
    )
    return pl.pallas_call(
        _dispatch_kernel,
        grid_spec=grid_spec,
        out_shape=jax.ShapeDtypeStruct((n_rows, D), F32),
        input_output_aliases={4: 0},
        compiler_params=_params("arbitrary"),
        name="moe_dispatch",
    )(starts, eid, rank, h, jnp.zeros((n_rows, D), F32))


def _experts_kernel(te_ref, nv_ref, xs_ref, wgu_ref, wdn_ref, y_ref):
    i = pl.program_id(0)

    @pl.when(i < nv_ref[0])
    def _():
        gu = jnp.dot(xs_ref[...].astype(BF16), wgu_ref[0], preferred_element_type=F32)
        gate, up = gu[:, :D_EXPERT], gu[:, D_EXPERT:]
        hid = gate * jax.nn.sigmoid(gate) * up
        y_ref[...] = jnp.dot(hid.astype(BF16), wdn_ref[0], preferred_element_type=F32)

    @pl.when(i >= nv_ref[0])
    def _():
        y_ref[...] = jnp.zeros(y_ref.shape, F32)


def _experts(xs, w_gu, w_dn, tile_expert, n_valid):
    n_rows, D = xs.shape
    Tm = MOE_ROWS
    grid_spec = pltpu.PrefetchScalarGridSpec(
        num_scalar_prefetch=2,
        grid=(n_rows // Tm,),
        in_specs=[
            pl.BlockSpec((Tm, D), lambda i, te, nv: (i, 0)),
            pl.BlockSpec((1, D, 2 * D_EXPERT), lambda i, te, nv: (te[i], 0, 0)),
            pl.BlockSpec((1, D_EXPERT, D), lambda i, te, nv: (te[i], 0, 0)),
        ],
        out_specs=pl.BlockSpec((Tm, D), lambda i, te, nv: (i, 0)),
    )
    return pl.pallas_call(
        _experts_kernel,
        grid_spec=grid_spec,
        out_shape=jax.ShapeDtypeStruct((n_rows, D), F32),
        compiler_params=_params("arbitrary"),
        name="moe_experts",
    )(tile_expert, n_valid, xs, w_gu, w_dn)


def _combine_kernel(start_ref, eid_ref, rank_ref, y_hbm, h_ref, cw_ref, g_ref, b_ref, o_ref,
                    ybuf, sem):
    G = ROUTE_TOKENS

    def issue(t, carry):
        for slot in range(2):
            src = start_ref[eid_ref[slot, t]] + rank_ref[slot, t]
            pltpu.make_async_copy(y_hbm.at[pl.ds(src, 1)], ybuf.at[slot, pl.ds(t, 1)], sem).start()
        return carry

    lax.fori_loop(0, G, issue, 0, unroll=8)
    for slot in range(2):
        pltpu.make_async_copy(y_hbm.at[pl.ds(0, G)], ybuf.at[slot], sem).wait()

    cw = cw_ref[...]
    ffn = ybuf[0] * cw[:, 0:1] + ybuf[1] * cw[:, 1:2]
    o_ref[...] = _layer_norm_rows(ALPHA * h_ref[...] + ffn, g_ref[...], b_ref[...])


def _combine(y, h, eid, rank, starts, cw_rows, ln_g, ln_b):
    T, D = h.shape
    G = ROUTE_TOKENS
    smem_blk = pl.BlockSpec((2, G), lambda i, st: (0, i), memory_space=pltpu.SMEM)
    grid_spec = pltpu.PrefetchScalarGridSpec(
        num_scalar_prefetch=1,
        grid=(T // G,),
        in_specs=[
            smem_blk, smem_blk,
            pl.BlockSpec(memory_space=pl.ANY),
            pl.BlockSpec((G, D), lambda i, st: (i, 0)),
            pl.BlockSpec((G, 2), lambda i, st: (i, 0)),
            pl.BlockSpec((1, D), lambda i, st: (0, 0)),
            pl.BlockSpec((1, D), lambda i, st: (0, 0)),
        ],
        out_specs=pl.BlockSpec((G, D), lambda i, st: (i, 0)),
        scratch_shapes=[pltpu.VMEM((2, G, D), F32), pltpu.SemaphoreType.DMA(())],
    )
    return pl.pallas_call(
        _combine_kernel,
        grid_spec=grid_spec,
        out_shape=jax.ShapeDtypeStruct((T, D), F32),
        compiler_params=_params("arbitrary"),
        name="moe_combine_ln",
    )(starts, eid, rank, y, h, cw_rows, ln_g.reshape(1, D), ln_b.reshape(1, D))


def kernel(x, w_in, attn_out_gain, rel_bias_table, w_out, ln1_g, ln1_b, w_router_group,
           b_router_group, w_router_expert, b_router_expert, w_gate_up, w_down, ln2_g, ln2_b):
    B, S, D = x.shape
    assert D == D_MODEL and S % MOBA_BLOCK == 0 and S % INPROJ_TOKENS == 0
    assert w_in.shape[0] == DEPTH
    T = B * S
    h = x
    for l in range(DEPTH):
        proj_t, k_nat = _inproj(h, w_in[l])
        attn_t = _moba(proj_t, k_nat, rel_bias_table)
        ret_t = _retention(proj_t)
        h1, eid, rank, cw, counts = _outproj(
            attn_t, ret_t, h, attn_out_gain[l], w_out[l], ln1_g[l], ln1_b[l],
            w_router_group[l], b_router_group[l], w_router_expert[l], b_router_expert[l])

        Tm = MOE_ROWS
        n_tiles = (2 * T) // Tm + N_EXPERTS
        padded = ((counts[:, 0] + Tm - 1) // Tm) * Tm
        ends = jnp.cumsum(padded)
        starts = (ends - padded).astype(jnp.int32)
        tile_row0 = jnp.arange(n_tiles, dtype=jnp.int32) * Tm
        tile_expert = jnp.minimum(
            jnp.sum((ends[None, :] <= tile_row0[:, None]).astype(jnp.int32), axis=1), N_EXPERTS - 1)
        n_valid = (ends[-1:] // Tm).astype(jnp.int32)

        xs = _dispatch(h1, eid, rank, starts, n_tiles * Tm)
        y = _experts(xs, w_gate_up[l].astype(BF16), w_down[l].astype(BF16), tile_expert, n_valid)
        h2 = _combine(y, h1, eid, rank, starts, cw.T, ln2_g[l], ln2_b[l])
        h = h2.reshape(B, S, D)
    return h
```

```python
import functools
import math

import numpy as np
import jax
import jax.numpy as jnp
from jax import lax
from jax.experimental import pallas as pl
from jax.experimental.pallas import tpu as pltpu

F32 = jnp.float32
BF16 = jnp.bfloat16

D_MODEL = 1024
ATTN_HEADS = 8
ATTN_HEAD_DIM = 64
ATTN_WIDTH = ATTN_HEADS * ATTN_HEAD_DIM
RET_HEADS = 4
RET_HEAD_DIM = 128
RET_WIDTH = RET_HEADS * RET_HEAD_DIM
N_IN_SLICES = 7
PROJT_SLICES = N_IN_SLICES - 1
PROJT_WIDTH = PROJT_SLICES * ATTN_WIDTH
HEAD_PAIR = 2 * ATTN_HEAD_DIM
MOBA_BLOCK = 256
MOBA_TOPK = 3
RET_CHUNK = 256
REL_BUCKETS = 32
REL_MAX_DIST = 128
ROPE_BASE = 10000.0
N_GROUPS = 4
EXPERTS_PER_GROUP = 8
N_EXPERTS = N_GROUPS * EXPERTS_PER_GROUP
D_EXPERT = D_MODEL // 2
DEPTH = 1
ALPHA = (2.0 * DEPTH) ** 0.25
LN_EPS = 1e-5

LOG2E = math.log2(math.e)
Q_SCALE = ATTN_HEAD_DIM ** -0.5 * LOG2E
BF16_SUBLANES = 16
V_ROWS = ATTN_HEAD_DIM + BF16_SUBLANES

MASKED = -1e30

INPROJ_TOKENS = 512
INPROJ_ROWS = ATTN_WIDTH
OUTPROJ_TOKENS = 256
MOE_ROWS = 256
ROUTE_TOKENS = 256
ROUTER_EXPERT_ROW0 = 8
ROUTER_ROWS = ROUTER_EXPERT_ROW0 + N_EXPERTS

VMEM_LIMIT = 48 * 1024 * 1024

_TN = (((0,), (0,)), ((), ()))
_NT = (((1,), (1,)), ((), ()))


def _params(*sem):
    return pltpu.CompilerParams(dimension_semantics=sem, vmem_limit_bytes=VMEM_LIMIT)


def _inproj_kernel(x_ref, wt_ref, wk_ref, o_ref, k_ref):
    xb = x_ref[0].astype(BF16)
    for c in range(PROJT_WIDTH // INPROJ_ROWS):
        rows = slice(c * INPROJ_ROWS, (c + 1) * INPROJ_ROWS)
        acc = lax.dot_general(wt_ref[rows, :], xb, _NT, preferred_element_type=F32)
        if c == 0:
            acc = acc * Q_SCALE
        o_ref[0, rows, :] = acc.astype(BF16)
    k_ref[0] = jnp.dot(xb, wk_ref[...], preferred_element_type=F32).astype(BF16)


def _inproj(x, w_in):
    B, S, D = x.shape
    W = ATTN_WIDTH
    w_t = jnp.concatenate([w_in[:, :W], w_in[:, 2 * W:]], axis=1).T.astype(BF16)
    w_k = w_in[:, W:2 * W].astype(BF16)
    return pl.pallas_call(
        _inproj_kernel,
        grid=(B, S // INPROJ_TOKENS),
        in_specs=[
            pl.BlockSpec((1, INPROJ_TOKENS, D), lambda b, i: (b, i, 0)),
            pl.BlockSpec((PROJT_WIDTH, D), lambda b, i: (0, 0)),
            pl.BlockSpec((D, W), lambda b, i: (0, 0)),
        ],
        out_specs=[
            pl.BlockSpec((1, PROJT_WIDTH, INPROJ_TOKENS), lambda b, i: (b, 0, i)),
            pl.BlockSpec((1, INPROJ_TOKENS, W), lambda b, i: (b, i, 0)),
        ],
        out_shape=[
            jax.ShapeDtypeStruct((B, PROJT_WIDTH, S), BF16),
            jax.ShapeDtypeStruct((B, S, W), BF16),
        ],
        compiler_params=_params("arbitrary", "arbitrary"),
        name="inproj",
    )(x, w_t, w_k)


def _moba_kernel(far_ref, q_ref, k_ref, v_ref, bias_ref, o_ref,
                 kmean_ref, vext_ref, sel_ref, farsel_ref, s0_ref, s1_ref, m_ref, acc_ref, *, n_blocks):
    L = MOBA_BLOCK
    Dh = ATTN_HEAD_DIM
    pair = pl.program_id(1)
    qi = pl.program_id(2)
    last_blk = n_blocks - 1

    @pl.when(qi == 0)
    def _():
        chunk_blocks = min(8, n_blocks)
        chunk = chunk_blocks * L
        acc = jnp.zeros((n_blocks, HEAD_PAIR), F32)
        ones_row = (lax.broadcasted_iota(jnp.int32, (BF16_SUBLANES, chunk), 0) == 0).astype(BF16)
        for c in range(n_blocks // chunk_blocks):
            cols = slice(c * chunk, (c + 1) * chunk)
            blk = lax.broadcasted_iota(jnp.int32, (n_blocks, chunk), 0)
            key = lax.broadcasted_iota(jnp.int32, (n_blocks, chunk), 1)
            ind = jnp.where(blk == c * chunk_blocks + key // L, 1.0 / L, 0.0).astype(BF16)
            acc = acc + jnp.dot(ind, k_ref[0, cols, :], preferred_element_type=F32)
            for hd in range(2):
                vext_ref[hd, 0:Dh, cols] = v_ref[0, hd * Dh:(hd + 1) * Dh, cols]
                vext_ref[hd, Dh:V_ROWS, cols] = ones_row
        kmean_ref[...] = acc

    q = q_ref[0]
    feat = lax.broadcasted_iota(jnp.int32, q.shape, 0)
    qz = [jnp.where((feat >= hd * Dh) & (feat < (hd + 1) * Dh), q, jnp.zeros_like(q)) for hd in range(2)]
    kmean = kmean_ref[...]
    blk = lax.broadcasted_iota(jnp.int32, (n_blocks, L), 0)
    past = blk < qi
    for hd in range(2):
        gate = jnp.dot(kmean, qz[hd].astype(F32), precision=lax.Precision.HIGHEST,
                       preferred_element_type=F32)
        gate = jnp.where(past, gate, -jnp.inf)
        chosen = jnp.zeros(gate.shape, jnp.bool_)
        for _ in range(MOBA_TOPK):
            top = jnp.max(gate, axis=0, keepdims=True)
            first = jnp.min(jnp.where(gate == top, blk, n_blocks), axis=0, keepdims=True)
            hit = blk == first
            chosen = jnp.logical_or(chosen, hit)
            gate = jnp.where(hit, -jnp.inf, gate)
        chosen = jnp.logical_and(chosen, past)
        sel_ref[hd] = jnp.where(chosen, 0.0, MASKED)
        farsel_ref[hd] = jnp.where(jnp.logical_and(chosen, blk < qi - 1), far_ref[2 * pair + hd], MASKED)

    qs = qz

    def k_block(j):
        return k_ref[0, pl.ds(pl.multiple_of(j * L, L), L), :]

    def v_block(hd, j):
        return vext_ref[hd, :, pl.ds(pl.multiple_of(j * L, L), L)]

    def far_blocks(g):
        b0 = 2 * g - 2
        return jnp.minimum(b0, last_blk), jnp.minimum(b0 + 1, last_blk)

    def far_scores(g, s_ref):
        for u, j in enumerate(far_blocks(g)):
            kb = k_block(j)
            for hd in range(2):
                s = jnp.dot(kb, qs[hd], preferred_element_type=F32)
                s_ref[hd, u * L:(u + 1) * L, :] = s + farsel_ref[hd, pl.ds(j, 1), :]

    def softmax_step(s_ref, j0, j1):
        for hd in range(2):
            s = s_ref[hd]
            m_prev = m_ref[hd]
            m_new = jnp.maximum(m_prev, jnp.max(s, axis=0, keepdims=True))
            alpha = jnp.exp2(m_prev - m_new)
            pb = jnp.exp2(s - m_new).astype(BF16)
            pv = (jnp.dot(v_block(hd, j0), pb[:L], preferred_element_type=F32)
                  + jnp.dot(v_block(hd, j1), pb[L:], preferred_element_type=F32))
            acc_ref[hd] = alpha * acc_ref[hd] + pv
            m_ref[hd] = m_new

    m_ref[...] = jnp.full(m_ref.shape, MASKED, F32)
    acc_ref[...] = jnp.zeros(acc_ref.shape, F32)

    jp = jnp.maximum(qi - 1, 0)
    kp, ko = k_block(jp), k_block(qi)
    for hd in range(2):
        s_prev = jnp.dot(kp, qs[hd], preferred_element_type=F32)
        s0_ref[hd, 0:L, :] = s_prev + bias_ref[1, hd] + sel_ref[hd, pl.ds(jp, 1), :]
        s_own = jnp.dot(ko, qs[hd], preferred_element_type=F32)
        s0_ref[hd, L:2 * L, :] = s_own + bias_ref[0, hd]

    n_far = jnp.maximum(qi - 1, 0)
    n_groups = 1 + (n_far + 1) // 2

    def two_groups(k, carry):
        g = 2 * k
        far_scores(g + 1, s1_ref)
        f0, f1 = far_blocks(g)
        first = k == 0
        softmax_step(s0_ref, jnp.where(first, jp, f0), jnp.where(first, qi, f1))
        far_scores(g + 2, s0_ref)
        softmax_step(s1_ref, *far_blocks(g + 1))
        return carry

    lax.fori_loop(0, (n_groups + 1) // 2, two_groups, 0)
    for hd in range(2):
        o_ref[0, hd * Dh:(hd + 1) * Dh, :] = acc_ref[hd, 0:Dh, :] / acc_ref[hd, Dh:Dh + 1, :]


def _t5_bucket_np(dist):
    n = np.maximum(dist, 0)
    max_exact = REL_BUCKETS // 2
    nf = np.maximum(n, 1).astype(np.float32)
    large = max_exact + (np.log(nf / max_exact) / math.log(REL_MAX_DIST / max_exact)
                         * (REL_BUCKETS - max_exact)).astype(np.int32)
    large = np.minimum(large, REL_BUCKETS - 1)
    return np.where(n < max_exact, n, large)


def _moba(proj_t, k_nat, rel_table):
    B, _, S = proj_t.shape
    L = MOBA_BLOCK
    n_blocks = S // L
    assert n_blocks % 2 == 0
    table_t = rel_table.astype(F32).T
    key = np.arange(L)[:, None]
    qry = np.arange(L)[None, :]
    d_own = qry - key
    bucket = np.stack([_t5_bucket_np(d_own), _t5_bucket_np(d_own + L)])
    onehot = jnp.asarray(bucket)[:, None, None] == jnp.arange(REL_BUCKETS)[None, None, :, None, None]
    bias = jnp.sum(jnp.where(onehot, table_t[None, :, :, None, None], 0.0), axis=2)
    causal = jnp.asarray(np.stack([d_own >= 0, np.ones_like(d_own, bool)]))[:, None]
    bias = jnp.where(causal, bias * LOG2E, MASKED)
    assert int(_t5_bucket_np(np.array(L + 1))) == REL_BUCKETS - 1
    far = table_t[:, REL_BUCKETS - 1] * LOG2E

    pairs = ATTN_WIDTH // HEAD_PAIR
    grid_spec = pltpu.PrefetchScalarGridSpec(
        num_scalar_prefetch=1,
        grid=(B, pairs, n_blocks),
        in_specs=[
            pl.BlockSpec((1, HEAD_PAIR, L), lambda b, p, i, far: (b, p, i)),
            pl.BlockSpec((1, S, HEAD_PAIR), lambda b, p, i, far: (b, 0, p)),
            pl.BlockSpec((1, HEAD_PAIR, S), lambda b, p, i, far: (b, pairs + p, 0)),
            pl.BlockSpec((2, 2, L, L), lambda b, p, i, far: (0, p, 0, 0)),
        ],
        out_specs=pl.BlockSpec((1, HEAD_PAIR, L), lambda b, p, i, far: (b, p, i)),
        scratch_shapes=[
            pltpu.VMEM((n_blocks, HEAD_PAIR), F32),
            pltpu.VMEM((2, V_ROWS, S), BF16),
            pltpu.VMEM((2, n_blocks, L), F32),
            pltpu.VMEM((2, n_blocks, L), F32),
            pltpu.VMEM((2, 2 * L, L), F32),
            pltpu.VMEM((2, 2 * L, L), F32),
            pltpu.VMEM((2, 1, L), F32),
            pltpu.VMEM((2, V_ROWS, L), F32),
        ],
    )
    return pl.pallas_call(
        functools.partial(_moba_kernel, n_blocks=n_blocks),
        grid_spec=grid_spec,
        out_shape=jax.ShapeDtypeStruct((B, ATTN_WIDTH, S), F32),
        compiler_params=_params("arbitrary", "arbitrary", "arbitrary"),
        name="moba",
    )(far, proj_t, k_nat, proj_t, bias)


def _retention_kernel(cdec_ref, q_ref, k_ref, v_ref, g_ref, cos_ref, sin_ref,
                      dec_ref, qdec_ref, kdec_ref, o_ref, state_ref):
    c = pl.program_id(1)

    @pl.when(c == 0)
    def _():
        state_ref[...] = jnp.zeros(state_ref.shape, F32)

    cos = cos_ref[...]
    sin = sin_ref[...]
    half = RET_HEAD_DIM // 2

    def rope(t):
        t1, t2 = t[:half], t[half:]
        return jnp.concatenate([t1 * cos - t2 * sin, t1 * sin + t2 * cos], axis=0)

    for hh in range(RET_HEADS):
        rows = slice(hh * RET_HEAD_DIM, (hh + 1) * RET_HEAD_DIM)
        q = rope(q_ref[0, rows, :].astype(F32))
        k = rope(k_ref[0, rows, :].astype(F32)) * (RET_HEAD_DIM ** -0.5)
        v = v_ref[0, rows, :]
        qb = q.astype(BF16)
        s = lax.dot_general(k.astype(BF16), qb, _TN, preferred_element_type=F32)
        s = (s * dec_ref[hh]).astype(BF16)
        inner = jnp.dot(v, s, preferred_element_type=F32)
        state = state_ref[hh]
        cross = jnp.dot(state.astype(BF16), (q * qdec_ref[hh:hh + 1, :]).astype(BF16),
                        preferred_element_type=F32)
        kd = (k * kdec_ref[hh:hh + 1, :]).astype(BF16)
        state_ref[hh] = state * cdec_ref[hh] + lax.dot_general(v, kd, _NT, preferred_element_type=F32)
        out = inner + cross
        mu = jnp.mean(out, axis=0, keepdims=True)
        var = jnp.mean(jnp.square(out - mu), axis=0, keepdims=True)
        out = (out - mu) * lax.rsqrt(var + LN_EPS)
        g = g_ref[0, rows, :].astype(F32)
        o_ref[0, rows, :] = (g * jax.nn.sigmoid(g) * out).astype(BF16)


def _retention(proj_t):
    B, _, S = proj_t.shape
    C = RET_CHUNK
    H = RET_HEADS
    half = RET_HEAD_DIM // 2
    inv = ROPE_BASE ** (-jnp.arange(half, dtype=F32) / half)
    ang = inv[:, None] * jnp.arange(S).astype(F32)[None, :]
    cos_t, sin_t = jnp.cos(ang), jnp.sin(ang)
    gammas = 1.0 - jnp.exp(jnp.linspace(math.log(1.0 / 32), math.log(1.0 / 512), H, dtype=F32))
    log_g = jnp.log(gammas)
    idx = jnp.arange(C, dtype=F32)
    diff = idx[None, :] - idx[:, None]
    dec_t = jnp.where(diff[None] >= 0, jnp.exp(jnp.maximum(diff, 0.0)[None] * log_g[:, None, None]), 0.0)
    q_dec = jnp.exp((idx[None, :] + 1.0) * log_g[:, None])
    k_dec = jnp.exp((C - 1.0 - idx[None, :]) * log_g[:, None])
    chunk_dec = jnp.exp(C * log_g)

    grid_spec = pltpu.PrefetchScalarGridSpec(
        num_scalar_prefetch=1,
        grid=(B, S // C),
        in_specs=[
            pl.BlockSpec((1, RET_WIDTH, C), lambda b, c, cd: (b, 2, c)),
            pl.BlockSpec((1, RET_WIDTH, C), lambda b, c, cd: (b, 3, c)),
            pl.BlockSpec((1, RET_WIDTH, C), lambda b, c, cd: (b, 4, c)),
            pl.BlockSpec((1, RET_WIDTH, C), lambda b, c, cd: (b, 5, c)),
            pl.BlockSpec((half, C), lambda b, c, cd: (0, c)),
            pl.BlockSpec((half, C), lambda b, c, cd: (0, c)),
            pl.BlockSpec((H, C, C), lambda b, c, cd: (0, 0, 0)),
            pl.BlockSpec((H, C), lambda b, c, cd: (0, 0)),
            pl.BlockSpec((H, C), lambda b, c, cd: (0, 0)),
        ],
        out_specs=pl.BlockSpec((1, RET_WIDTH, C), lambda b, c, cd: (b, 0, c)),
        scratch_shapes=[pltpu.VMEM((H, RET_HEAD_DIM, RET_HEAD_DIM), F32)],
    )
    return pl.pallas_call(
        _retention_kernel,
        grid_spec=grid_spec,
        out_shape=jax.ShapeDtypeStruct((B, RET_WIDTH, S), BF16),
        compiler_params=_params("arbitrary", "arbitrary"),
        name="retention",
    )(chunk_dec, proj_t, proj_t, proj_t, proj_t, cos_t, sin_t, dec_t, q_dec, k_dec)


def _layer_norm_rows(y, g, b):
    mu = jnp.mean(y, axis=-1, keepdims=True)
    var = jnp.mean(jnp.square(y - mu), axis=-1, keepdims=True)
    return (y - mu) * lax.rsqrt(var + LN_EPS) * g + b


def _outproj_kernel(attn_ref, ret_ref, x_ref, gain_ref, wa_ref, wr_ref, g1_ref, b1_ref,
                    wrt_ref, brt_ref,
                    h_ref, eid_ref, rank_ref, cw_ref, cnt_ref, carry_ref):
    first = jnp.logical_and(pl.program_id(0) == 0, pl.program_id(1) == 0)

    @pl.when(first)
    def _():
        carry_ref[...] = jnp.zeros(carry_ref.shape, F32)

    a = attn_ref[0]
    a = a * lax.rsqrt(jnp.mean(jnp.square(a), axis=0, keepdims=True) + LN_EPS) * gain_ref[...]
    mix = (lax.dot_general(a.astype(BF16), wa_ref[...], _TN, preferred_element_type=F32)
           + lax.dot_general(ret_ref[0], wr_ref[...], _TN, preferred_element_type=F32))
    h = _layer_norm_rows(ALPHA * x_ref[0] + mix, g1_ref[...], b1_ref[...])
    h_ref[...] = h

    logits = lax.dot_general(wrt_ref[...], h, _NT, precision=lax.Precision.HIGHEST,
                             preferred_element_type=F32) + brt_ref[...]
    T = logits.shape[1]
    gl = logits[0:N_GROUPS]
    gmax = jnp.max(gl, axis=0, keepdims=True)
    grow = lax.broadcasted_iota(jnp.int32, gl.shape, 0)
    gidx = jnp.min(jnp.where(gl == gmax, grow, N_GROUPS), axis=0, keepdims=True)
    g_p = 1.0 / jnp.sum(jnp.exp(gl - gmax), axis=0, keepdims=True)
    el = jnp.zeros((EXPERTS_PER_GROUP, T), F32)
    for g in range(N_GROUPS):
        r0 = ROUTER_EXPERT_ROW0 + g * EXPERTS_PER_GROUP
        el = jnp.where(gidx == g, logits[r0:r0 + EXPERTS_PER_GROUP], el)
    erow = lax.broadcasted_iota(jnp.int32, el.shape, 0)
    e1 = jnp.max(el, axis=0, keepdims=True)
    i1 = jnp.min(jnp.where(el == e1, erow, EXPERTS_PER_GROUP), axis=0, keepdims=True)
    el2 = jnp.where(erow == i1, -jnp.inf, el)
    e2 = jnp.max(el2, axis=0, keepdims=True)
    i2 = jnp.min(jnp.where(el2 == e2, erow, EXPERTS_PER_GROUP), axis=0, keepdims=True)
    r = jnp.exp(e2 - e1)
    w1 = g_p / (1.0 + r)
    w2 = g_p * r / (1.0 + r)
    id1 = gidx * EXPERTS_PER_GROUP + i1
    id2 = gidx * EXPERTS_PER_GROUP + i2
    eid_ref[0:1, :] = id1
    eid_ref[1:2, :] = id2
    cw_ref[0:1, :] = w1
    cw_ref[1:2, :] = w2

    xrow = lax.broadcasted_iota(jnp.int32, (N_EXPERTS, T), 0)
    oh1 = (xrow == id1).astype(F32)
    oh2 = (xrow == id2).astype(F32)
    both = oh1 + oh2
    before = (lax.broadcasted_iota(jnp.int32, (T, T), 0)
              < lax.broadcasted_iota(jnp.int32, (T, T), 1)).astype(BF16)
    seen = jnp.dot(both.astype(BF16), before, preferred_element_type=F32) + carry_ref[...]
    rank_ref[0:1, :] = jnp.sum(oh1 * seen, axis=0, keepdims=True).astype(jnp.int32)
    rank_ref[1:2, :] = jnp.sum(oh2 * seen, axis=0, keepdims=True).astype(jnp.int32)
    carry = carry_ref[...] + jnp.sum(both, axis=1, keepdims=True)
    carry_ref[...] = carry
    cnt_ref[...] = carry.astype(jnp.int32)


def _outproj(attn_t, ret_t, x, gain, w_out, ln_g, ln_b, w_rg, b_rg, w_re, b_re):
    B, S, D = x.shape
    Tt = OUTPROJ_TOKENS
    n_t = S // Tt
    T = B * S
    wa = w_out[:ATTN_WIDTH].astype(BF16)
    wr = w_out[ATTN_WIDTH:].astype(BF16)
    wrt = jnp.zeros((ROUTER_ROWS, D), F32)
    wrt = wrt.at[:N_GROUPS].set(w_rg.T).at[ROUTER_EXPERT_ROW0:].set(w_re.T)
    brt = jnp.zeros((ROUTER_ROWS, 1), F32)
    brt = brt.at[:N_GROUPS, 0].set(b_rg.astype(F32)).at[ROUTER_EXPERT_ROW0:, 0].set(b_re.astype(F32))
    const = lambda b, i: (0, 0)
    tok = lambda b, i: (0, b * n_t + i)
    return pl.pallas_call(
        _outproj_kernel,
        grid=(B, n_t),
        in_specs=[
            pl.BlockSpec((1, ATTN_WIDTH, Tt), lambda b, i: (b, 0, i)),
            pl.BlockSpec((1, RET_WIDTH, Tt), lambda b, i: (b, 0, i)),
            pl.BlockSpec((1, Tt, D), lambda b, i: (b, i, 0)),
            pl.BlockSpec((ATTN_WIDTH, 1), const),
            pl.BlockSpec((ATTN_WIDTH, D), const),
            pl.BlockSpec((RET_WIDTH, D), const),
            pl.BlockSpec((1, D), const),
            pl.BlockSpec((1, D), const),
            pl.BlockSpec((ROUTER_ROWS, D), const),
            pl.BlockSpec((ROUTER_ROWS, 1), const),
        ],
        out_specs=[
            pl.BlockSpec((Tt, D), lambda b, i: (b * n_t + i, 0)),
            pl.BlockSpec((2, Tt), tok),
            pl.BlockSpec((2, Tt), tok),
            pl.BlockSpec((2, Tt), tok),
            pl.BlockSpec((N_EXPERTS, 1), const),
        ],
        out_shape=[
            jax.ShapeDtypeStruct((T, D), F32),
            jax.ShapeDtypeStruct((2, T), jnp.int32),
            jax.ShapeDtypeStruct((2, T), jnp.int32),
            jax.ShapeDtypeStruct((2, T), F32),
            jax.ShapeDtypeStruct((N_EXPERTS, 1), jnp.int32),
        ],
        scratch_shapes=[pltpu.VMEM((N_EXPERTS, 1), F32)],
        compiler_params=_params("arbitrary", "arbitrary"),
        name="outproj_ln_router",
    )(attn_t, ret_t, x, gain.reshape(ATTN_WIDTH, 1), wa, wr, ln_g.reshape(1, D), ln_b.reshape(1, D),
      wrt, brt)


def _dispatch_kernel(start_ref, end_ref, eid_ref, rank_ref, h_ref, xs_hbm, zero_ref, sem, zsem):
    G = ROUTE_TOKENS
    Tm = MOE_ROWS

    @pl.when(pl.program_id(0) == 0)
    def _():
        zero_ref[...] = jnp.zeros(zero_ref.shape, F32)

        def tile_clear(row0):
            row0 = pl.multiple_of(row0, Tm)
            return pltpu.make_async_copy(zero_ref, xs_hbm.at[pl.ds(row0, Tm)], zsem)

        used = end_ref[N_EXPERTS - 1]
        n_rows = xs_hbm.shape[0]
        clears = [(end_ref[e] - Tm, end_ref[e] > start_ref[e]) for e in range(N_EXPERTS)]
        clears += [(jnp.minimum(used + k * Tm, n_rows - Tm), used + k * Tm < n_rows) for k in range(N_EXPERTS)]
        for row0, cond in clears:
            @pl.when(cond)
            def _():
                tile_clear(row0).start()
        for row0, cond in clears:
            @pl.when(cond)
            def _():
                tile_clear(row0).wait()

    def issue(t, carry):
        for slot in range(2):
            dst = start_ref[eid_ref[slot, t]] + rank_ref[slot, t]
            pltpu.make_async_copy(h_ref.at[pl.ds(t, 1)], xs_hbm.at[pl.ds(dst, 1)], sem).start(priority=slot)
        return carry

    lax.fori_loop(0, G, issue, 0, unroll=8)
    for _ in range(2):
        pltpu.make_async_copy(h_ref, xs_hbm.at[pl.ds(0, G)], sem).wait()


def _dispatch(h, eid, rank, starts, ends, n_rows):
    T, D = h.shape
    G = ROUTE_TOKENS
    smem_blk = pl.BlockSpec((2, G), lambda i, st, en: (0, i), memory_space=pltpu.SMEM)
    grid_spec = pltpu.PrefetchScalarGridSpec(
        num_scalar_prefetch=2,
        grid=(T // G,),
        in_specs=[smem_blk, smem_blk, pl.BlockSpec((G, D), lambda i, st, en: (i, 0))],
        out_specs=pl.BlockSpec(memory_space=pl.ANY),
        scratch_shapes=[pltpu.VMEM((MOE_ROWS, D), F32), pltpu.SemaphoreType.DMA(()),
                        pltpu.SemaphoreType.DMA(())],
    )
    return pl.pallas_call(
        _dispatch_kernel,
        grid_spec=grid_spec,
        out_shape=jax.ShapeDtypeStruct((n_rows, D), F32),
        compiler_params=_params("arbitrary"),
        name="moe_dispatch",
    )(starts, ends, eid, rank, h)


def _experts_kernel(te_ref, nv_ref, xs_ref, wgu_ref, wdn_ref, y_ref, wgu_b, wdn_b):
    i = pl.program_id(0)
    live = i < nv_ref[0]
    new_expert = jnp.logical_or(i == 0, te_ref[i] != te_ref[jnp.maximum(i - 1, 0)])

    @pl.when(jnp.logical_and(live, new_expert))
    def _():
        wgu_b[...] = wgu_ref[0].astype(BF16)
        wdn_b[...] = wdn_ref[0].astype(BF16)

    @pl.when(live)
    def _():
        gu = jnp.dot(xs_ref[...].astype(BF16), wgu_b[...], preferred_element_type=F32)
        gate, up = gu[:, :D_EXPERT], gu[:, D_EXPERT:]
        hid = gate * jax.nn.sigmoid(gate) * up
        y_ref[...] = jnp.dot(hid.astype(BF16), wdn_b[...], preferred_element_type=F32)

    @pl.when(jnp.logical_not(live))
    def _():
        y_ref[...] = jnp.zeros(y_ref.shape, F32)


def _experts(xs, w_gu, w_dn, tile_expert, n_valid):
    n_rows, D = xs.shape
    Tm = MOE_ROWS
    grid_spec = pltpu.PrefetchScalarGridSpec(
        num_scalar_prefetch=2,
        grid=(n_rows // Tm,),
        in_specs=[
            pl.BlockSpec((Tm, D), lambda i, te, nv: (jnp.minimum(i, nv[0] - 1), 0)),
            pl.BlockSpec((1, D, 2 * D_EXPERT), lambda i, te, nv: (te[i], 0, 0)),
            pl.BlockSpec((1, D_EXPERT, D), lambda i, te, nv: (te[i], 0, 0)),
        ],
        out_specs=pl.BlockSpec((Tm, D), lambda i, te, nv: (i, 0)),
        scratch_shapes=[pltpu.VMEM((D, 2 * D_EXPERT), BF16), pltpu.VMEM((D_EXPERT, D), BF16)],
    )
    return pl.pallas_call(
        _experts_kernel,
        grid_spec=grid_spec,
        out_shape=jax.ShapeDtypeStruct((n_rows, D), F32),
        compiler_params=_params("arbitrary"),
        name="moe_experts",
    )(tile_expert, n_valid, xs, w_gu, w_dn)


def _combine_kernel(start_ref, eid_ref, rank_ref, y_hbm, h_ref, cw_ref, g_ref, b_ref, o_ref,
                    ybuf, sem):
    G = ROUTE_TOKENS

    def issue(t, carry):
        for slot in range(2):
            src = start_ref[eid_ref[slot, t]] + rank_ref[slot, t]
            pltpu.make_async_copy(y_hbm.at[pl.ds(src, 1)], ybuf.at[slot, pl.ds(t, 1)], sem).start(priority=slot)
        return carry

    lax.fori_loop(0, G, issue, 0, unroll=8)
    for slot in range(2):
        pltpu.make_async_copy(y_hbm.at[pl.ds(0, G)], ybuf.at[slot], sem).wait()

    cw = cw_ref[...]
    ffn = ybuf[0] * cw[:, 0:1] + ybuf[1] * cw[:, 1:2]
    o_ref[...] = _layer_norm_rows(ALPHA * h_ref[...] + ffn, g_ref[...], b_ref[...])


def _combine(y, h, eid, rank, starts, cw_rows, ln_g, ln_b):
    T, D = h.shape
    G = ROUTE_TOKENS
    smem_blk = pl.BlockSpec((2, G), lambda i, st: (0, i), memory_space=pltpu.SMEM)
    grid_spec = pltpu.PrefetchScalarGridSpec(
        num_scalar_prefetch=1,
        grid=(T // G,),
        in_specs=[
            smem_blk, smem_blk,
            pl.BlockSpec(memory_space=pl.ANY),
            pl.BlockSpec((G, D), lambda i, st: (i, 0)),
            pl.BlockSpec((G, 2), lambda i, st: (i, 0)),
            pl.BlockSpec((1, D), lambda i, st: (0, 0)),
            pl.BlockSpec((1, D), lambda i, st: (0, 0)),
        ],
        out_specs=pl.BlockSpec((G, D), lambda i, st: (i, 0)),
        scratch_shapes=[pltpu.VMEM((2, G, D), F32), pltpu.SemaphoreType.DMA(())],
    )
    return pl.pallas_call(
        _combine_kernel,
        grid_spec=grid_spec,
        out_shape=jax.ShapeDtypeStruct((T, D), F32),
        compiler_params=_params("arbitrary"),
        name="moe_combine_ln",
    )(starts, eid, rank, y, h, cw_rows, ln_g.reshape(1, D), ln_b.reshape(1, D))


def kernel(x, w_in, attn_out_gain, rel_bias_table, w_out, ln1_g, ln1_b, w_router_group,
           b_router_group, w_router_expert, b_router_expert, w_gate_up, w_down, ln2_g, ln2_b):
    B, S, D = x.shape
    assert D == D_MODEL and S % MOBA_BLOCK == 0 and S % INPROJ_TOKENS == 0
    assert w_in.shape[0] == DEPTH
    T = B * S
    h = x
    for l in range(DEPTH):
        proj_t, k_nat = _inproj(h, w_in[l])
        attn_t = _moba(proj_t, k_nat, rel_bias_table)
        ret_t = _retention(proj_t)
        h1, eid, rank, cw, counts = _outproj(
            attn_t, ret_t, h, attn_out_gain[l], w_out[l], ln1_g[l], ln1_b[l],
            w_router_group[l], b_router_group[l], w_router_expert[l], b_router_expert[l])

        Tm = MOE_ROWS
        n_tiles = (2 * T) // Tm + N_EXPERTS
        padded = ((counts[:, 0] + Tm - 1) // Tm) * Tm
        ends = jnp.cumsum(padded)
        starts = (ends - padded).astype(jnp.int32)
        tile_row0 = jnp.arange(n_tiles, dtype=jnp.int32) * Tm
        tile_expert = jnp.sum((ends[None, :] <= jnp.minimum(tile_row0, ends[-1] - Tm)[:, None]).astype(jnp.int32),
                              axis=1)
        n_valid = (ends[-1:] // Tm).astype(jnp.int32)

        xs = _dispatch(h1, eid, rank, starts, ends.astype(jnp.int32), n_tiles * Tm)
        y = _experts(xs, w_gate_up[l], w_down[l], tile_expert, n_valid)
        h2 = _combine(y, h1, eid, rank, starts, cw.T, ln2_g[l], ln2_b[l])
        h = h2.reshape(B, S, D)
    return h
```

```python
import functools
import math

import numpy as np
import jax
import jax.numpy as jnp
from jax import lax
from jax.experimental import pallas as pl
from jax.experimental.pallas import tpu as pltpu

F32 = jnp.float32
BF16 = jnp.bfloat16

D_MODEL = 1024
ATTN_HEADS = 8
ATTN_HEAD_DIM = 64
ATTN_WIDTH = ATTN_HEADS * ATTN_HEAD_DIM
RET_HEADS = 4
RET_HEAD_DIM = 128
RET_WIDTH = RET_HEADS * RET_HEAD_DIM
N_IN_SLICES = 7
PROJT_SLICES = N_IN_SLICES - 1
HEAD_PAIR = 2 * ATTN_HEAD_DIM
BF16_SUBLANES = 16
F32_SUBLANES = 8
V_ROWS = ATTN_HEAD_DIM + BF16_SUBLANES
PROJT_V_ROW0 = (PROJT_SLICES - 1) * ATTN_WIDTH
PROJT_WIDTH = PROJT_V_ROW0 + ATTN_HEADS * V_ROWS
MOBA_BLOCK = 256
MOBA_TOPK = 3
RET_CHUNK = 256
REL_BUCKETS = 32
REL_MAX_DIST = 128
ROPE_BASE = 10000.0
N_GROUPS = 4
EXPERTS_PER_GROUP = 8
N_EXPERTS = N_GROUPS * EXPERTS_PER_GROUP
D_EXPERT = D_MODEL // 2
DEPTH = 1
ALPHA = (2.0 * DEPTH) ** 0.25
LN_EPS = 1e-5

LOG2E = math.log2(math.e)
Q_SCALE = ATTN_HEAD_DIM ** -0.5 * LOG2E

MASKED = -1e30

INPROJ_TOKENS = 512
INPROJ_ROWS = ATTN_WIDTH
OUTPROJ_TOKENS = 512
MOBA_UNROLL = 2
MOE_ROWS = 256
ROUTE_TOKENS = 256
ROUTER_EXPERT_ROW0 = 8
ROUTER_ROWS = ROUTER_EXPERT_ROW0 + N_EXPERTS

VMEM_LIMIT = 48 * 1024 * 1024
MOBA_VMEM_LIMIT = 58 * 1024 * 1024

_TN = (((0,), (0,)), ((), ()))
_NT = (((1,), (1,)), ((), ()))


def _params(*sem):
    return pltpu.CompilerParams(dimension_semantics=sem, vmem_limit_bytes=VMEM_LIMIT)


def _inproj_kernel(x_ref, wt_ref, wk_ref, o_ref, k_ref):
    xb = x_ref[0].astype(BF16)
    n_tok = xb.shape[0]
    ones_row = (lax.broadcasted_iota(jnp.int32, (BF16_SUBLANES, n_tok), 0) == 0).astype(BF16)
    for c in range(PROJT_SLICES):
        rows = slice(c * INPROJ_ROWS, (c + 1) * INPROJ_ROWS)
        acc = lax.dot_general(wt_ref[rows, :], xb, _NT, preferred_element_type=F32)
        if c == 0:
            acc = acc * Q_SCALE
        if c < PROJT_SLICES - 1:
            o_ref[0, rows, :] = acc.astype(BF16)
        else:
            for h in range(ATTN_HEADS):
                r0 = PROJT_V_ROW0 + h * V_ROWS
                o_ref[0, r0:r0 + ATTN_HEAD_DIM, :] = acc[h * ATTN_HEAD_DIM:(h + 1) * ATTN_HEAD_DIM].astype(BF16)
                o_ref[0, r0 + ATTN_HEAD_DIM:r0 + V_ROWS, :] = ones_row
    k_ref[0] = jnp.dot(xb, wk_ref[...], preferred_element_type=F32).astype(BF16)


def _inproj(x, w_in):
    B, S, D = x.shape
    W = ATTN_WIDTH
    w_t = jnp.concatenate([w_in[:, :W], w_in[:, 3 * W:], w_in[:, 2 * W:3 * W]], axis=1).T.astype(BF16)
    w_k = w_in[:, W:2 * W].astype(BF16)
    return pl.pallas_call(
        _inproj_kernel,
        grid=(B, S // INPROJ_TOKENS),
        in_specs=[
            pl.BlockSpec((1, INPROJ_TOKENS, D), lambda b, i: (b, i, 0)),
            pl.BlockSpec((PROJT_SLICES * W, D), lambda b, i: (0, 0)),
            pl.BlockSpec((D, W), lambda b, i: (0, 0)),
        ],
        out_specs=[
            pl.BlockSpec((1, PROJT_WIDTH, INPROJ_TOKENS), lambda b, i: (b, 0, i)),
            pl.BlockSpec((1, INPROJ_TOKENS, W), lambda b, i: (b, i, 0)),
        ],
        out_shape=[
            jax.ShapeDtypeStruct((B, PROJT_WIDTH, S), BF16),
            jax.ShapeDtypeStruct((B, S, W), BF16),
        ],
        compiler_params=_params("arbitrary", "arbitrary"),
        name="inproj",
    )(x, w_t, w_k)


def _moba_kernel(far_ref, q_ref, k_ref, v_ref, bias_ref, o_ref,
                 kmean_ref, qz_ref, sel_ref, farsel_ref,
                 s0_ref, smax0_ref, p0_ref, alpha0_ref, s1_ref, smax1_ref, p1_ref, alpha1_ref,
                 m_ref, acc_ref, *, n_blocks):
    L = MOBA_BLOCK
    Dh = ATTN_HEAD_DIM
    H = ATTN_HEADS
    qi = pl.program_id(1)
    last_blk = n_blocks - 1
    s_ref, smax_ref = (s0_ref, s1_ref), (smax0_ref, smax1_ref)
    p_ref, alpha_ref = (p0_ref, p1_ref), (alpha0_ref, alpha1_ref)

    @pl.when(qi == 0)
    def _():
        chunk_blocks = min(8, n_blocks)
        chunk = chunk_blocks * L
        acc = jnp.zeros((n_blocks, ATTN_WIDTH), F32)
        for c in range(n_blocks // chunk_blocks):
            blk = lax.broadcasted_iota(jnp.int32, (n_blocks, chunk), 0)
            key = lax.broadcasted_iota(jnp.int32, (n_blocks, chunk), 1)
            ind = jnp.where(blk == c * chunk_blocks + key // L, 1.0 / L, 0.0).astype(BF16)
            acc = acc + jnp.dot(ind, k_ref[0, c * chunk:(c + 1) * chunk, :], preferred_element_type=F32)
        kmean_ref[...] = acc

    feat = lax.broadcasted_iota(jnp.int32, (HEAD_PAIR, L), 0)
    blk = lax.broadcasted_iota(jnp.int32, (n_blocks, L), 0)
    past = blk < qi
    for h in range(H):
        pair, hd = divmod(h, 2)
        qp = q_ref[0, pair * HEAD_PAIR:(pair + 1) * HEAD_PAIR, :]
        qz = jnp.where((feat >= hd * Dh) & (feat < (hd + 1) * Dh), qp, jnp.zeros_like(qp))
        qz_ref[h] = qz
        gate = jnp.dot(kmean_ref[:, pair * HEAD_PAIR:(pair + 1) * HEAD_PAIR], qz.astype(F32),
                       precision=lax.Precision.HIGHEST, preferred_element_type=F32)
        gate = jnp.where(past, gate, -jnp.inf)
        chosen = jnp.zeros(gate.shape, jnp.bool_)
        for _ in range(MOBA_TOPK):
            top = jnp.max(gate, axis=0, keepdims=True)
            first = jnp.min(jnp.where(gate == top, blk, n_blocks), axis=0, keepdims=True)
            hit = blk == first
            chosen = jnp.logical_or(chosen, hit)
            gate = jnp.where(hit, -jnp.inf, gate)
        chosen = jnp.logical_and(chosen, past)
        sel_ref[h] = jnp.where(chosen, 0.0, MASKED)
        farsel_ref[h, 0:n_blocks, :] = jnp.where(jnp.logical_and(chosen, blk < qi - 1), far_ref[h], MASKED)
        farsel_ref[h, n_blocks:n_blocks + F32_SUBLANES, :] = jnp.zeros((F32_SUBLANES, L), F32)

    def stage_scores(h, j, tile, slot):
        pair = h // 2
        kb = k_ref[0, pl.ds(pl.multiple_of(j * L, L), L), pair * HEAD_PAIR:(pair + 1) * HEAD_PAIR]
        s = jnp.dot(kb, qz_ref[h], preferred_element_type=F32)
        if tile is not None:
            s = s + tile
        s_ref[slot][h] = s
        smax_ref[slot][h] = jnp.max(s, axis=0, keepdims=True)

    def stage_softmax(h, row, slot):
        m_prev = m_ref[h]
        m_new = jnp.maximum(m_prev, smax_ref[slot][h] + row)
        alpha_ref[slot][h] = jnp.exp2(m_prev - m_new)
        p_ref[slot][h] = jnp.exp2(s_ref[slot][h] - (m_new - row)).astype(BF16)
        m_ref[h] = m_new

    def stage_values(h, j, slot):
        vb = v_ref[0, h * V_ROWS:(h + 1) * V_ROWS, pl.ds(pl.multiple_of(j * L, L), L)]
        pv = jnp.dot(vb, p_ref[slot][h], preferred_element_type=F32)
        acc_ref[h] = alpha_ref[slot][h] * acc_ref[h] + pv

    m_ref[...] = jnp.full(m_ref.shape, MASKED, F32)
    acc_ref[...] = jnp.zeros(acc_ref.shape, F32)

    jp = jnp.maximum(qi - 1, 0)
    zero_row = jnp.zeros((1, L), F32)
    for h in range(H):
        stage_scores(h, qi, bias_ref[0, h], 0)
        stage_scores(h, jp, bias_ref[1, h] + sel_ref[h, pl.ds(jp, 1), :], 1)
    for h in range(H):
        stage_softmax(h, zero_row, 0)
    n_steps = 2 + jnp.maximum(qi - 1, 0)

    def step(t, slot):
        value_block = jnp.where(t == 0, qi, jnp.where(t == 1, jp, jnp.minimum(t - 2, last_blk)))
        j = jnp.minimum(t, last_blk)
        jr = jnp.where(t == 0, n_blocks, jnp.minimum(t - 1, last_blk))
        for h in range(H):
            stage_scores(h, j, None, slot)
            stage_softmax(h, farsel_ref[h, pl.ds(jr, 1), :], 1 - slot)
            stage_values(h, value_block, slot)

    def unrolled_steps(k, carry):
        for u in range(MOBA_UNROLL):
            step(MOBA_UNROLL * k + u, u % 2)
        return carry

    lax.fori_loop(0, (n_steps + MOBA_UNROLL - 1) // MOBA_UNROLL, unrolled_steps, 0)
    for h in range(H):
        o_ref[0, h * Dh:(h + 1) * Dh, :] = acc_ref[h, 0:Dh, :] / acc_ref[h, Dh:Dh + 1, :]


def _t5_bucket_np(dist):
    n = np.maximum(dist, 0)
    max_exact = REL_BUCKETS // 2
    nf = np.maximum(n, 1).astype(np.float32)
    large = max_exact + (np.log(nf / max_exact) / math.log(REL_MAX_DIST / max_exact)
                         * (REL_BUCKETS - max_exact)).astype(np.int32)
    large = np.minimum(large, REL_BUCKETS - 1)
    return np.where(n < max_exact, n, large)


def _moba(proj_t, k_nat, rel_table):
    B, _, S = proj_t.shape
    L = MOBA_BLOCK
    n_blocks = S // L
    table_t = rel_table.astype(F32).T
    key = np.arange(L)[:, None]
    qry = np.arange(L)[None, :]
    d_own = qry - key
    bucket = np.stack([_t5_bucket_np(d_own), _t5_bucket_np(d_own + L)])
    onehot = jnp.asarray(bucket)[:, None, None] == jnp.arange(REL_BUCKETS)[None, None, :, None, None]
    bias = jnp.sum(jnp.where(onehot, table_t[None, :, :, None, None], 0.0), axis=2)
    causal = jnp.asarray(np.stack([d_own >= 0, np.ones_like(d_own, bool)]))[:, None]
    bias = jnp.where(causal, bias * LOG2E, MASKED)
    assert int(_t5_bucket_np(np.array(L + 1))) == REL_BUCKETS - 1
    far = table_t[:, REL_BUCKETS - 1] * LOG2E

    H = ATTN_HEADS
    v_rows = H * V_ROWS
    assert PROJT_V_ROW0 % v_rows == 0
    once = dict(pipeline_mode=pl.Buffered(1))
    grid_spec = pltpu.PrefetchScalarGridSpec(
        num_scalar_prefetch=1,
        grid=(B, n_blocks),
        in_specs=[
            pl.BlockSpec((1, ATTN_WIDTH, L), lambda b, i, far: (b, 0, i)),
            pl.BlockSpec((1, S, ATTN_WIDTH), lambda b, i, far: (b, 0, 0), **once),
            pl.BlockSpec((1, v_rows, S), lambda b, i, far: (b, PROJT_V_ROW0 // v_rows, 0), **once),
            pl.BlockSpec((2, H, L, L), lambda b, i, far: (0, 0, 0, 0), **once),
        ],
        out_specs=pl.BlockSpec((1, ATTN_WIDTH, L), lambda b, i, far: (b, 0, i)),
        scratch_shapes=[
            pltpu.VMEM((n_blocks, ATTN_WIDTH), F32),
            pltpu.VMEM((H, HEAD_PAIR, L), BF16),
            pltpu.VMEM((H, n_blocks, L), F32),
            pltpu.VMEM((H, n_blocks + F32_SUBLANES, L), F32),
            *([pltpu.VMEM((H, L, L), F32), pltpu.VMEM((H, 1, L), F32),
               pltpu.VMEM((H, L, L), BF16), pltpu.VMEM((H, 1, L), F32)] * 2),
            pltpu.VMEM((H, 1, L), F32),
            pltpu.VMEM((H, V_ROWS, L), F32),
        ],
    )
    return pl.pallas_call(
        functools.partial(_moba_kernel, n_blocks=n_blocks),
        grid_spec=grid_spec,
        out_shape=jax.ShapeDtypeStruct((B, ATTN_WIDTH, S), F32),
        compiler_params=pltpu.CompilerParams(dimension_semantics=("arbitrary", "arbitrary"),
                                             vmem_limit_bytes=MOBA_VMEM_LIMIT),
        name="moba",
    )(far, proj_t, k_nat, proj_t, bias)


def _retention_kernel(cdec_ref, q_ref, k_ref, v_ref, g_ref, cos_ref, sin_ref,
                      dec_ref, qdec_ref, kdec_ref, o_ref, state_ref):
    c = pl.program_id(1)

    @pl.when(c == 0)
    def _():
        state_ref[...] = jnp.zeros(state_ref.shape, F32)

    cos = cos_ref[...]
    sin = sin_ref[...]
    half = RET_HEAD_DIM // 2

    def rope(t):
        t1, t2 = t[:half], t[half:]
        return jnp.concatenate([t1 * cos - t2 * sin, t1 * sin + t2 * cos], axis=0)

    for hh in range(RET_HEADS):
        rows = slice(hh * RET_HEAD_DIM, (hh + 1) * RET_HEAD_DIM)
        q = rope(q_ref[0, rows, :].astype(F32))
        k = rope(k_ref[0, rows, :].astype(F32)) * (RET_HEAD_DIM ** -0.5)
        v = v_ref[0, rows, :]
        qb = q.astype(BF16)
        s = lax.dot_general(k.astype(BF16), qb, _TN, preferred_element_type=F32)
        s = (s * dec_ref[hh]).astype(BF16)
        inner = jnp.dot(v, s, preferred_element_type=F32)
        state = state_ref[hh]
        cross = jnp.dot(state.astype(BF16), (q * qdec_ref[hh:hh + 1, :]).astype(BF16),
                        preferred_element_type=F32)
        kd = (k * kdec_ref[hh:hh + 1, :]).astype(BF16)
        state_ref[hh] = state * cdec_ref[hh] + lax.dot_general(v, kd, _NT, preferred_element_type=F32)
        out = inner + cross
        mu = jnp.mean(out, axis=0, keepdims=True)
        var = jnp.mean(jnp.square(out - mu), axis=0, keepdims=True)
        out = (out - mu) * lax.rsqrt(var + LN_EPS)
        g = g_ref[0, rows, :].astype(F32)
        o_ref[0, rows, :] = (g * jax.nn.sigmoid(g) * out).astype(BF16)


def _retention(proj_t):
    B, _, S = proj_t.shape
    C = RET_CHUNK
    H = RET_HEADS
    half = RET_HEAD_DIM // 2
    inv = ROPE_BASE ** (-jnp.arange(half, dtype=F32) / half)
    ang = inv[:, None] * jnp.arange(S).astype(F32)[None, :]
    cos_t, sin_t = jnp.cos(ang), jnp.sin(ang)
    gammas = 1.0 - jnp.exp(jnp.linspace(math.log(1.0 / 32), math.log(1.0 / 512), H, dtype=F32))
    log_g = jnp.log(gammas)
    idx = jnp.arange(C, dtype=F32)
    diff = idx[None, :] - idx[:, None]
    dec_t = jnp.where(diff[None] >= 0, jnp.exp(jnp.maximum(diff, 0.0)[None] * log_g[:, None, None]), 0.0)
    q_dec = jnp.exp((idx[None, :] + 1.0) * log_g[:, None])
    k_dec = jnp.exp((C - 1.0 - idx[None, :]) * log_g[:, None])
    chunk_dec = jnp.exp(C * log_g)

    grid_spec = pltpu.PrefetchScalarGridSpec(
        num_scalar_prefetch=1,
        grid=(B, S // C),
        in_specs=[
            pl.BlockSpec((1, RET_WIDTH, C), lambda b, c, cd: (b, 1, c)),
            pl.BlockSpec((1, RET_WIDTH, C), lambda b, c, cd: (b, 2, c)),
            pl.BlockSpec((1, RET_WIDTH, C), lambda b, c, cd: (b, 3, c)),
            pl.BlockSpec((1, RET_WIDTH, C), lambda b, c, cd: (b, 4, c)),
            pl.BlockSpec((half, C), lambda b, c, cd: (0, c)),
            pl.BlockSpec((half, C), lambda b, c, cd: (0, c)),
            pl.BlockSpec((H, C, C), lambda b, c, cd: (0, 0, 0)),
            pl.BlockSpec((H, C), lambda b, c, cd: (0, 0)),
            pl.BlockSpec((H, C), lambda b, c, cd: (0, 0)),
        ],
        out_specs=pl.BlockSpec((1, RET_WIDTH, C), lambda b, c, cd: (b, 0, c)),
        scratch_shapes=[pltpu.VMEM((H, RET_HEAD_DIM, RET_HEAD_DIM), F32)],
    )
    return pl.pallas_call(
        _retention_kernel,
        grid_spec=grid_spec,
        out_shape=jax.ShapeDtypeStruct((B, RET_WIDTH, S), BF16),
        compiler_params=_params("arbitrary", "arbitrary"),
        name="retention",
    )(chunk_dec, proj_t, proj_t, proj_t, proj_t, cos_t, sin_t, dec_t, q_dec, k_dec)


def _layer_norm_rows(y, g, b):
    mu = jnp.mean(y, axis=-1, keepdims=True)
    var = jnp.mean(jnp.square(y - mu), axis=-1, keepdims=True)
    return (y - mu) * lax.rsqrt(var + LN_EPS) * g + b


def _outproj_kernel(attn_ref, ret_ref, x_ref, gain_ref, wa_ref, wr_ref, g1_ref, b1_ref,
                    wrt_ref, brt_ref,
                    h_ref, eid_ref, rank_ref, cw_ref, cnt_ref, carry_ref):
    first = jnp.logical_and(pl.program_id(0) == 0, pl.program_id(1) == 0)

    @pl.when(first)
    def _():
        carry_ref[...] = jnp.zeros(carry_ref.shape, F32)

    a = attn_ref[0]
    a = a * lax.rsqrt(jnp.mean(jnp.square(a), axis=0, keepdims=True) + LN_EPS) * gain_ref[...]
    mix = (lax.dot_general(a.astype(BF16), wa_ref[...], _TN, preferred_element_type=F32)
           + lax.dot_general(ret_ref[0], wr_ref[...], _TN, preferred_element_type=F32))
    h = _layer_norm_rows(ALPHA * x_ref[0] + mix, g1_ref[...], b1_ref[...])
    h_ref[...] = h

    logits = lax.dot_general(wrt_ref[...], h, _NT, precision=lax.Precision.HIGHEST,
                             preferred_element_type=F32) + brt_ref[...]
    T = logits.shape[1]
    gl = logits[0:N_GROUPS]
    gmax = jnp.max(gl, axis=0, keepdims=True)
    grow = lax.broadcasted_iota(jnp.int32, gl.shape, 0)
    gidx = jnp.min(jnp.where(gl == gmax, grow, N_GROUPS), axis=0, keepdims=True)
    g_p = 1.0 / jnp.sum(jnp.exp(gl - gmax), axis=0, keepdims=True)
    el = jnp.zeros((EXPERTS_PER_GROUP, T), F32)
    for g in range(N_GROUPS):
        r0 = ROUTER_EXPERT_ROW0 + g * EXPERTS_PER_GROUP
        el = jnp.where(gidx == g, logits[r0:r0 + EXPERTS_PER_GROUP], el)
    erow = lax.broadcasted_iota(jnp.int32, el.shape, 0)
    e1 = jnp.max(el, axis=0, keepdims=True)
    i1 = jnp.min(jnp.where(el == e1, erow, EXPERTS_PER_GROUP), axis=0, keepdims=True)
    el2 = jnp.where(erow == i1, -jnp.inf, el)
    e2 = jnp.max(el2, axis=0, keepdims=True)
    i2 = jnp.min(jnp.where(el2 == e2, erow, EXPERTS_PER_GROUP), axis=0, keepdims=True)
    r = jnp.exp(e2 - e1)
    w1 = g_p / (1.0 + r)
    w2 = g_p * r / (1.0 + r)
    id1 = gidx * EXPERTS_PER_GROUP + i1
    id2 = gidx * EXPERTS_PER_GROUP + i2
    eid_ref[0:1, :] = id1
    eid_ref[1:2, :] = id2
    cw_ref[0:1, :] = w1
    cw_ref[1:2, :] = w2

    xrow = lax.broadcasted_iota(jnp.int32, (N_EXPERTS, T), 0)
    oh1 = (xrow == id1).astype(F32)
    oh2 = (xrow == id2).astype(F32)
    both = oh1 + oh2
    before = (lax.broadcasted_iota(jnp.int32, (T, T), 0)
              < lax.broadcasted_iota(jnp.int32, (T, T), 1)).astype(BF16)
    seen = jnp.dot(both.astype(BF16), before, preferred_element_type=F32) + carry_ref[...]
    rank_ref[0:1, :] = jnp.sum(oh1 * seen, axis=0, keepdims=True).astype(jnp.int32)
    rank_ref[1:2, :] = jnp.sum(oh2 * seen, axis=0, keepdims=True).astype(jnp.int32)
    carry = carry_ref[...] + jnp.sum(both, axis=1, keepdims=True)
    carry_ref[...] = carry
    cnt_ref[...] = carry.astype(jnp.int32)


def _outproj(attn_t, ret_t, x, gain, w_out, ln_g, ln_b, w_rg, b_rg, w_re, b_re):
    B, S, D = x.shape
    Tt = OUTPROJ_TOKENS
    n_t = S // Tt
    T = B * S
    wa = w_out[:ATTN_WIDTH].astype(BF16)
    wr = w_out[ATTN_WIDTH:].astype(BF16)
    wrt = jnp.zeros((ROUTER_ROWS, D), F32)
    wrt = wrt.at[:N_GROUPS].set(w_rg.T).at[ROUTER_EXPERT_ROW0:].set(w_re.T)
    brt = jnp.zeros((ROUTER_ROWS, 1), F32)
    brt = brt.at[:N_GROUPS, 0].set(b_rg.astype(F32)).at[ROUTER_EXPERT_ROW0:, 0].set(b_re.astype(F32))
    const = lambda b, i: (0, 0)
    tok = lambda b, i: (0, b * n_t + i)
    return pl.pallas_call(
        _outproj_kernel,
        grid=(B, n_t),
        in_specs=[
            pl.BlockSpec((1, ATTN_WIDTH, Tt), lambda b, i: (b, 0, i)),
            pl.BlockSpec((1, RET_WIDTH, Tt), lambda b, i: (b, 0, i)),
            pl.BlockSpec((1, Tt, D), lambda b, i: (b, i, 0)),
            pl.BlockSpec((ATTN_WIDTH, 1), const),
            pl.BlockSpec((ATTN_WIDTH, D), const),
            pl.BlockSpec((RET_WIDTH, D), const),
            pl.BlockSpec((1, D), const),
            pl.BlockSpec((1, D), const),
            pl.BlockSpec((ROUTER_ROWS, D), const),
            pl.BlockSpec((ROUTER_ROWS, 1), const),
        ],
        out_specs=[
            pl.BlockSpec((Tt, D), lambda b, i: (b * n_t + i, 0)),
            pl.BlockSpec((2, Tt), tok),
            pl.BlockSpec((2, Tt), tok),
            pl.BlockSpec((2, Tt), tok),
            pl.BlockSpec((N_EXPERTS, 1), const),
        ],
        out_shape=[
            jax.ShapeDtypeStruct((T, D), F32),
            jax.ShapeDtypeStruct((2, T), jnp.int32),
            jax.ShapeDtypeStruct((2, T), jnp.int32),
            jax.ShapeDtypeStruct((2, T), F32),
            jax.ShapeDtypeStruct((N_EXPERTS, 1), jnp.int32),
        ],
        scratch_shapes=[pltpu.VMEM((N_EXPERTS, 1), F32)],
        compiler_params=_params("arbitrary", "arbitrary"),
        name="outproj_ln_router",
    )(attn_t, ret_t, x, gain.reshape(ATTN_WIDTH, 1), wa, wr, ln_g.reshape(1, D), ln_b.reshape(1, D),
      wrt, brt)


def _dispatch_kernel(start_ref, end_ref, eid_ref, rank_ref, h_ref, xs_hbm, zero_ref, sem, zsem):
    G = ROUTE_TOKENS
    Tm = MOE_ROWS

    @pl.when(pl.program_id(0) == 0)
    def _():
        zero_ref[...] = jnp.zeros(zero_ref.shape, F32)

        def tile_clear(row0):
            row0 = pl.multiple_of(row0, Tm)
            return pltpu.make_async_copy(zero_ref, xs_hbm.at[pl.ds(row0, Tm)], zsem)

        used = end_ref[N_EXPERTS - 1]
        n_rows = xs_hbm.shape[0]
        clears = [(end_ref[e] - Tm, end_ref[e] > start_ref[e]) for e in range(N_EXPERTS)]
        clears += [(jnp.minimum(used + k * Tm, n_rows - Tm), used + k * Tm < n_rows) for k in range(N_EXPERTS)]
        for row0, cond in clears:
            @pl.when(cond)
            def _():
                tile_clear(row0).start()
        for row0, cond in clears:
            @pl.when(cond)
            def _():
                tile_clear(row0).wait()

    def issue(t, carry):
        for slot in range(2):
            dst = start_ref[eid_ref[slot, t]] + rank_ref[slot, t]
            pltpu.make_async_copy(h_ref.at[pl.ds(t, 1)], xs_hbm.at[pl.ds(dst, 1)], sem).start(priority=slot)
        return carry

    lax.fori_loop(0, G, issue, 0, unroll=8)
    for _ in range(2):
        pltpu.make_async_copy(h_ref, xs_hbm.at[pl.ds(0, G)], sem).wait()


def _dispatch(h, eid, rank, starts, ends, n_rows):
    T, D = h.shape
    G = ROUTE_TOKENS
    smem_blk = pl.BlockSpec((2, G), lambda i, st, en: (0, i), memory_space=pltpu.SMEM)
    grid_spec = pltpu.PrefetchScalarGridSpec(
        num_scalar_prefetch=2,
        grid=(T // G,),
        in_specs=[smem_blk, smem_blk, pl.BlockSpec((G, D), lambda i, st, en: (i, 0))],
        out_specs=pl.BlockSpec(memory_space=pl.ANY),
        scratch_shapes=[pltpu.VMEM((MOE_ROWS, D), F32), pltpu.SemaphoreType.DMA(()),
                        pltpu.SemaphoreType.DMA(())],
    )
    return pl.pallas_call(
        _dispatch_kernel,
        grid_spec=grid_spec,
        out_shape=jax.ShapeDtypeStruct((n_rows, D), F32),
        compiler_params=_params("arbitrary"),
        name="moe_dispatch",
    )(starts, ends, eid, rank, h)


def _experts_kernel(te_ref, nv_ref, xs_ref, wgu_ref, wdn_ref, y_ref, wgu_b, wdn_b):
    i = pl.program_id(0)
    live = i < nv_ref[0]
    new_expert = jnp.logical_or(i == 0, te_ref[i] != te_ref[jnp.maximum(i - 1, 0)])

    @pl.when(jnp.logical_and(live, new_expert))
    def _():
        wgu_b[...] = wgu_ref[0].astype(BF16)
        wdn_b[...] = wdn_ref[0].astype(BF16)

    @pl.when(live)
    def _():
        gu = jnp.dot(xs_ref[...].astype(BF16), wgu_b[...], preferred_element_type=F32)
        gate, up = gu[:, :D_EXPERT], gu[:, D_EXPERT:]
        hid = gate * jax.nn.sigmoid(gate) * up
        y_ref[...] = jnp.dot(hid.astype(BF16), wdn_b[...], preferred_element_type=F32)

    @pl.when(jnp.logical_not(live))
    def _():
        y_ref[...] = jnp.zeros(y_ref.shape, F32)


def _experts(xs, w_gu, w_dn, tile_expert, n_valid):
    n_rows, D = xs.shape
    Tm = MOE_ROWS
    grid_spec = pltpu.PrefetchScalarGridSpec(
        num_scalar_prefetch=2,
        grid=(n_rows // Tm,),
        in_specs=[
            pl.BlockSpec((Tm, D), lambda i, te, nv: (jnp.minimum(i, jnp.maximum(nv[0] - 1, 0)), 0)),
            pl.BlockSpec((1, D, 2 * D_EXPERT), lambda i, te, nv: (te[i], 0, 0)),
            pl.BlockSpec((1, D_EXPERT, D), lambda i, te, nv: (te[i], 0, 0)),
        ],
        out_specs=pl.BlockSpec((Tm, D), lambda i, te, nv: (i, 0)),
        scratch_shapes=[pltpu.VMEM((D, 2 * D_EXPERT), BF16), pltpu.VMEM((D_EXPERT, D), BF16)],
    )
    return pl.pallas_call(
        _experts_kernel,
        grid_spec=grid_spec,
        out_shape=jax.ShapeDtypeStruct((n_rows, D), F32),
        compiler_params=_params("arbitrary"),
        name="moe_experts",
    )(tile_expert, n_valid, xs, w_gu, w_dn)


def _combine_kernel(start_ref, eid_ref, rank_ref, y_hbm, h_ref, cw_ref, g_ref, b_ref, o_ref,
                    ybuf, sem):
    G = ROUTE_TOKENS

    def issue(t, carry):
        for slot in range(2):
            src = start_ref[eid_ref[slot, t]] + rank_ref[slot, t]
            pltpu.make_async_copy(y_hbm.at[pl.ds(src, 1)], ybuf.at[slot, pl.ds(t, 1)], sem).start(priority=slot)
        return carry

    lax.fori_loop(0, G, issue, 0, unroll=8)
    for slot in range(2):
        pltpu.make_async_copy(y_hbm.at[pl.ds(0, G)], ybuf.at[slot], sem).wait()

    cw = cw_ref[...]
    ffn = ybuf[0] * cw[:, 0:1] + ybuf[1] * cw[:, 1:2]
    o_ref[...] = _layer_norm_rows(ALPHA * h_ref[...] + ffn, g_ref[...], b_ref[...])


def _combine(y, h, eid, rank, starts, cw_rows, ln_g, ln_b):
    T, D = h.shape
    G = ROUTE_TOKENS
    smem_blk = pl.BlockSpec((2, G), lambda i, st: (0, i), memory_space=pltpu.SMEM)
    grid_spec = pltpu.PrefetchScalarGridSpec(
        num_scalar_prefetch=1,
        grid=(T // G,),
        in_specs=[
            smem_blk, smem_blk,
            pl.BlockSpec(memory_space=pl.ANY),
            pl.BlockSpec((G, D), lambda i, st: (i, 0)),
            pl.BlockSpec((G, 2), lambda i, st: (i, 0)),
            pl.BlockSpec((1, D), lambda i, st: (0, 0)),
            pl.BlockSpec((1, D), lambda i, st: (0, 0)),
        ],
        out_specs=pl.BlockSpec((G, D), lambda i, st: (i, 0)),
        scratch_shapes=[pltpu.VMEM((2, G, D), F32), pltpu.SemaphoreType.DMA(())],
    )
    return pl.pallas_call(
        _combine_kernel,
        grid_spec=grid_spec,
        out_shape=jax.ShapeDtypeStruct((T, D), F32),
        compiler_params=_params("arbitrary"),
        name="moe_combine_ln",
    )(starts, eid, rank, y, h, cw_rows, ln_g.reshape(1, D), ln_b.reshape(1, D))


def kernel(x, w_in, attn_out_gain, rel_bias_table, w_out, ln1_g, ln1_b, w_router_group,
           b_router_group, w_router_expert, b_router_expert, w_gate_up, w_down, ln2_g, ln2_b):
    B, S, D = x.shape
    assert D == D_MODEL and S % MOBA_BLOCK == 0 and S % INPROJ_TOKENS == 0
    assert w_in.shape[0] == DEPTH
    T = B * S
    h = x
    for l in range(DEPTH):
        proj_t, k_nat = _inproj(h, w_in[l])
        attn_t = _moba(proj_t, k_nat, rel_bias_table)
        ret_t = _retention(proj_t)
        h1, eid, rank, cw, counts = _outproj(
            attn_t, ret_t, h, attn_out_gain[l], w_out[l], ln1_g[l], ln1_b[l],
            w_router_group[l], b_router_group[l], w_router_expert[l], b_router_expert[l])

        Tm = MOE_ROWS
        n_tiles = (2 * T) // Tm + N_EXPERTS
        padded = ((counts[:, 0] + Tm - 1) // Tm) * Tm
        ends = jnp.cumsum(padded)
        starts = (ends - padded).astype(jnp.int32)
        tile_row0 = jnp.arange(n_tiles, dtype=jnp.int32) * Tm
        tile_expert = jnp.sum((ends[None, :] <= jnp.minimum(tile_row0, ends[-1] - Tm)[:, None]).astype(jnp.int32),
                              axis=1)
        n_valid = (ends[-1:] // Tm).astype(jnp.int32)

        xs = _dispatch(h1, eid, rank, starts, ends.astype(jnp.int32), n_tiles * Tm)
        y = _experts(xs, w_gate_up[l], w_down[l], tile_expert, n_valid)
        h2 = _combine(y, h1, eid, rank, starts, cw.T, ln2_g[l], ln2_b[l])
        h = h2.reshape(B, S, D)
    return h
```

```python
import functools
import math

import numpy as np
import jax
import jax.numpy as jnp
from jax import lax
from jax.experimental import pallas as pl
from jax.experimental.pallas import tpu as pltpu

F32 = jnp.float32
BF16 = jnp.bfloat16

D_MODEL = 1024
ATTN_HEADS = 8
ATTN_HEAD_DIM = 64
ATTN_WIDTH = ATTN_HEADS * ATTN_HEAD_DIM
RET_HEADS = 4
RET_HEAD_DIM = 128
RET_WIDTH = RET_HEADS * RET_HEAD_DIM
N_IN_SLICES = 7
PROJT_SLICES = N_IN_SLICES - 1
HEAD_PAIR = 2 * ATTN_HEAD_DIM
BF16_SUBLANES = 16
F32_SUBLANES = 8
V_ROWS = ATTN_HEAD_DIM + BF16_SUBLANES
PROJT_V_ROW0 = (PROJT_SLICES - 1) * ATTN_WIDTH
PROJT_WIDTH = PROJT_V_ROW0 + ATTN_HEADS * V_ROWS
MOBA_BLOCK = 256
MOBA_TOPK = 3
RET_CHUNK = 256
REL_BUCKETS = 32
REL_MAX_DIST = 128
ROPE_BASE = 10000.0
N_GROUPS = 4
EXPERTS_PER_GROUP = 8
N_EXPERTS = N_GROUPS * EXPERTS_PER_GROUP
D_EXPERT = D_MODEL // 2
DEPTH = 1
ALPHA = (2.0 * DEPTH) ** 0.25
LN_EPS = 1e-5

LOG2E = math.log2(math.e)
Q_SCALE = ATTN_HEAD_DIM ** -0.5 * LOG2E

MASKED = -1e30

INPROJ_TOKENS = 512
INPROJ_ROWS = ATTN_WIDTH
OUTPROJ_TOKENS = 512
MOBA_UNROLL = 2
MOE_ROWS = 512
ROUTE_TOKENS = 1024
COMBINE_CHUNK = 256
ROUTER_EXPERT_ROW0 = 8
ROUTER_ROWS = ROUTER_EXPERT_ROW0 + N_EXPERTS

VMEM_LIMIT = 48 * 1024 * 1024
MOBA_VMEM_LIMIT = 58 * 1024 * 1024

_TN = (((0,), (0,)), ((), ()))
_NT = (((1,), (1,)), ((), ()))


def _params(*sem):
    return pltpu.CompilerParams(dimension_semantics=sem, vmem_limit_bytes=VMEM_LIMIT)


def _inproj_kernel(x_ref, wt_ref, wk_ref, o_ref, k_ref):
    xb = x_ref[0].astype(BF16)
    n_tok = xb.shape[0]
    ones_row = (lax.broadcasted_iota(jnp.int32, (BF16_SUBLANES, n_tok), 0) == 0).astype(BF16)
    for c in range(PROJT_SLICES):
        rows = slice(c * INPROJ_ROWS, (c + 1) * INPROJ_ROWS)
        acc = lax.dot_general(wt_ref[rows, :], xb, _NT, preferred_element_type=F32)
        if c == 0:
            acc = acc * Q_SCALE
        if c < PROJT_SLICES - 1:
            o_ref[0, rows, :] = acc.astype(BF16)
        else:
            for h in range(ATTN_HEADS):
                r0 = PROJT_V_ROW0 + h * V_ROWS
                o_ref[0, r0:r0 + ATTN_HEAD_DIM, :] = acc[h * ATTN_HEAD_DIM:(h + 1) * ATTN_HEAD_DIM].astype(BF16)
                o_ref[0, r0 + ATTN_HEAD_DIM:r0 + V_ROWS, :] = ones_row
    k_ref[0] = jnp.dot(xb, wk_ref[...], preferred_element_type=F32).astype(BF16)


def _inproj(x, w_in):
    B, S, D = x.shape
    W = ATTN_WIDTH
    w_t = jnp.concatenate([w_in[:, :W], w_in[:, 3 * W:], w_in[:, 2 * W:3 * W]], axis=1).T.astype(BF16)
    w_k = w_in[:, W:2 * W].astype(BF16)
    return pl.pallas_call(
        _inproj_kernel,
        grid=(B, S // INPROJ_TOKENS),
        in_specs=[
            pl.BlockSpec((1, INPROJ_TOKENS, D), lambda b, i: (b, i, 0)),
            pl.BlockSpec((PROJT_SLICES * W, D), lambda b, i: (0, 0)),
            pl.BlockSpec((D, W), lambda b, i: (0, 0)),
        ],
        out_specs=[
            pl.BlockSpec((1, PROJT_WIDTH, INPROJ_TOKENS), lambda b, i: (b, 0, i)),
            pl.BlockSpec((1, INPROJ_TOKENS, W), lambda b, i: (b, i, 0)),
        ],
        out_shape=[
            jax.ShapeDtypeStruct((B, PROJT_WIDTH, S), BF16),
            jax.ShapeDtypeStruct((B, S, W), BF16),
        ],
        compiler_params=_params("arbitrary", "arbitrary"),
        name="inproj",
    )(x, w_t, w_k)


def _moba_kernel(far_ref, q_ref, k_ref, v_ref, bias_ref, o_ref,
                 kmean_ref, qz_ref, sel_ref, farsel_ref,
                 s0_ref, smax0_ref, p0_ref, alpha0_ref, s1_ref, smax1_ref, p1_ref, alpha1_ref,
                 m_ref, acc_ref, *, n_blocks):
    L = MOBA_BLOCK
    Dh = ATTN_HEAD_DIM
    H = ATTN_HEADS
    qi = pl.program_id(1)
    last_blk = n_blocks - 1
    s_ref, smax_ref = (s0_ref, s1_ref), (smax0_ref, smax1_ref)
    p_ref, alpha_ref = (p0_ref, p1_ref), (alpha0_ref, alpha1_ref)

    @pl.when(qi == 0)
    def _():
        chunk_blocks = min(8, n_blocks)
        chunk = chunk_blocks * L
        acc = jnp.zeros((n_blocks, ATTN_WIDTH), F32)
        for c in range(n_blocks // chunk_blocks):
            blk = lax.broadcasted_iota(jnp.int32, (n_blocks, chunk), 0)
            key = lax.broadcasted_iota(jnp.int32, (n_blocks, chunk), 1)
            ind = jnp.where(blk == c * chunk_blocks + key // L, 1.0 / L, 0.0).astype(BF16)
            acc = acc + jnp.dot(ind, k_ref[0, c * chunk:(c + 1) * chunk, :], preferred_element_type=F32)
        kmean_ref[...] = acc

    feat = lax.broadcasted_iota(jnp.int32, (HEAD_PAIR, L), 0)
    blk = lax.broadcasted_iota(jnp.int32, (n_blocks, L), 0)
    past = blk < qi
    for h in range(H):
        pair, hd = divmod(h, 2)
        qp = q_ref[0, pair * HEAD_PAIR:(pair + 1) * HEAD_PAIR, :]
        qz = jnp.where((feat >= hd * Dh) & (feat < (hd + 1) * Dh), qp, jnp.zeros_like(qp))
        qz_ref[h] = qz
        gate = jnp.dot(kmean_ref[:, pair * HEAD_PAIR:(pair + 1) * HEAD_PAIR], qz.astype(F32),
                       precision=lax.Precision.HIGHEST, preferred_element_type=F32)
        gate = jnp.where(past, gate, -jnp.inf)
        chosen = jnp.zeros(gate.shape, jnp.bool_)
        for _ in range(MOBA_TOPK):
            top = jnp.max(gate, axis=0, keepdims=True)
            first = jnp.min(jnp.where(gate == top, blk, n_blocks), axis=0, keepdims=True)
            hit = blk == first
            chosen = jnp.logical_or(chosen, hit)
            gate = jnp.where(hit, -jnp.inf, gate)
        chosen = jnp.logical_and(chosen, past)
        sel_ref[h] = jnp.where(chosen, 0.0, MASKED)
        farsel_ref[h, 0:n_blocks, :] = jnp.where(jnp.logical_and(chosen, blk < qi - 1), far_ref[h], MASKED)
        farsel_ref[h, n_blocks:n_blocks + F32_SUBLANES, :] = jnp.zeros((F32_SUBLANES, L), F32)

    def stage_scores(h, j, tile, slot):
        pair = h // 2
        kb = k_ref[0, pl.ds(pl.multiple_of(j * L, L), L), pair * HEAD_PAIR:(pair + 1) * HEAD_PAIR]
        s = jnp.dot(kb, qz_ref[h], preferred_element_type=F32)
        if tile is not None:
            s = s + tile
        s_ref[slot][h] = s
        smax_ref[slot][h] = jnp.max(s, axis=0, keepdims=True)

    def stage_softmax(h, row, slot):
        m_prev = m_ref[h]
        m_new = jnp.maximum(m_prev, smax_ref[slot][h] + row)
        alpha_ref[slot][h] = jnp.exp2(m_prev - m_new)
        p_ref[slot][h] = jnp.exp2(s_ref[slot][h] - (m_new - row)).astype(BF16)
        m_ref[h] = m_new

    def stage_values(h, j, slot):
        vb = v_ref[0, h * V_ROWS:(h + 1) * V_ROWS, pl.ds(pl.multiple_of(j * L, L), L)]
        pv = jnp.dot(vb, p_ref[slot][h], preferred_element_type=F32)
        acc_ref[h] = alpha_ref[slot][h] * acc_ref[h] + pv

    m_ref[...] = jnp.full(m_ref.shape, MASKED, F32)
    acc_ref[...] = jnp.zeros(acc_ref.shape, F32)

    jp = jnp.maximum(qi - 1, 0)
    zero_row = jnp.zeros((1, L), F32)
    for h in range(H):
        stage_scores(h, qi, bias_ref[0, h], 0)
        stage_scores(h, jp, bias_ref[1, h] + sel_ref[h, pl.ds(jp, 1), :], 1)
    for h in range(H):
        stage_softmax(h, zero_row, 0)
    n_steps = 2 + jnp.maximum(qi - 1, 0)

    def step(t, slot):
        value_block = jnp.where(t == 0, qi, jnp.where(t == 1, jp, jnp.minimum(t - 2, last_blk)))
        j = jnp.minimum(t, last_blk)
        jr = jnp.where(t == 0, n_blocks, jnp.minimum(t - 1, last_blk))
        for h in range(H):
            stage_scores(h, j, None, slot)
            stage_softmax(h, farsel_ref[h, pl.ds(jr, 1), :], 1 - slot)
            stage_values(h, value_block, slot)

    def unrolled_steps(k, carry):
        for u in range(MOBA_UNROLL):
            step(MOBA_UNROLL * k + u, u % 2)
        return carry

    lax.fori_loop(0, (n_steps + MOBA_UNROLL - 1) // MOBA_UNROLL, unrolled_steps, 0)
    for h in range(H):
        o_ref[0, h * Dh:(h + 1) * Dh, :] = acc_ref[h, 0:Dh, :] / acc_ref[h, Dh:Dh + 1, :]


def _t5_bucket_np(dist):
    n = np.maximum(dist, 0)
    max_exact = REL_BUCKETS // 2
    nf = np.maximum(n, 1).astype(np.float32)
    large = max_exact + (np.log(nf / max_exact) / math.log(REL_MAX_DIST / max_exact)
                         * (REL_BUCKETS - max_exact)).astype(np.int32)
    large = np.minimum(large, REL_BUCKETS - 1)
    return np.where(n < max_exact, n, large)


def _moba(proj_t, k_nat, rel_table):
    B, _, S = proj_t.shape
    L = MOBA_BLOCK
    n_blocks = S // L
    table_t = rel_table.astype(F32).T
    key = np.arange(L)[:, None]
    qry = np.arange(L)[None, :]
    d_own = qry - key
    bucket = np.stack([_t5_bucket_np(d_own), _t5_bucket_np(d_own + L)])
    onehot = jnp.asarray(bucket)[:, None, None] == jnp.arange(REL_BUCKETS)[None, None, :, None, None]
    bias = jnp.sum(jnp.where(onehot, table_t[None, :, :, None, None], 0.0), axis=2)
    causal = jnp.asarray(np.stack([d_own >= 0, np.ones_like(d_own, bool)]))[:, None]
    bias = jnp.where(causal, bias * LOG2E, MASKED)
    assert int(_t5_bucket_np(np.array(L + 1))) == REL_BUCKETS - 1
    far = table_t[:, REL_BUCKETS - 1] * LOG2E

    H = ATTN_HEADS
    v_rows = H * V_ROWS
    assert PROJT_V_ROW0 % v_rows == 0
    once = dict(pipeline_mode=pl.Buffered(1))
    grid_spec = pltpu.PrefetchScalarGridSpec(
        num_scalar_prefetch=1,
        grid=(B, n_blocks),
        in_specs=[
            pl.BlockSpec((1, ATTN_WIDTH, L), lambda b, i, far: (b, 0, i)),
            pl.BlockSpec((1, S, ATTN_WIDTH), lambda b, i, far: (b, 0, 0), **once),
            pl.BlockSpec((1, v_rows, S), lambda b, i, far: (b, PROJT_V_ROW0 // v_rows, 0), **once),
            pl.BlockSpec((2, H, L, L), lambda b, i, far: (0, 0, 0, 0), **once),
        ],
        out_specs=pl.BlockSpec((1, ATTN_WIDTH, L), lambda b, i, far: (b, 0, i)),
        scratch_shapes=[
            pltpu.VMEM((n_blocks, ATTN_WIDTH), F32),
            pltpu.VMEM((H, HEAD_PAIR, L), BF16),
            pltpu.VMEM((H, n_blocks, L), F32),
            pltpu.VMEM((H, n_blocks + F32_SUBLANES, L), F32),
            *([pltpu.VMEM((H, L, L), F32), pltpu.VMEM((H, 1, L), F32),
               pltpu.VMEM((H, L, L), BF16), pltpu.VMEM((H, 1, L), F32)] * 2),
            pltpu.VMEM((H, 1, L), F32),
            pltpu.VMEM((H, V_ROWS, L), F32),
        ],
    )
    return pl.pallas_call(
        functools.partial(_moba_kernel, n_blocks=n_blocks),
        grid_spec=grid_spec,
        out_shape=jax.ShapeDtypeStruct((B, ATTN_WIDTH, S), F32),
        compiler_params=pltpu.CompilerParams(dimension_semantics=("arbitrary", "arbitrary"),
                                             vmem_limit_bytes=MOBA_VMEM_LIMIT),
        name="moba",
    )(far, proj_t, k_nat, proj_t, bias)


def _retention_kernel(cdec_ref, q_ref, k_ref, v_ref, g_ref, cos_ref, sin_ref,
                      dec_ref, qdec_ref, kdec_ref, o_ref, state_ref):
    c = pl.program_id(1)

    @pl.when(c == 0)
    def _():
        state_ref[...] = jnp.zeros(state_ref.shape, F32)

    cos = cos_ref[...]
    sin = sin_ref[...]
    half = RET_HEAD_DIM // 2

    def rope(t):
        t1, t2 = t[:half], t[half:]
        return jnp.concatenate([t1 * cos - t2 * sin, t1 * sin + t2 * cos], axis=0)

    for hh in range(RET_HEADS):
        rows = slice(hh * RET_HEAD_DIM, (hh + 1) * RET_HEAD_DIM)
        q = rope(q_ref[0, rows, :].astype(F32))
        k = rope(k_ref[0, rows, :].astype(F32)) * (RET_HEAD_DIM ** -0.5)
        v = v_ref[0, rows, :]
        qb = q.astype(BF16)
        s = lax.dot_general(k.astype(BF16), qb, _TN, preferred_element_type=F32)
        s = (s * dec_ref[hh]).astype(BF16)
        inner = jnp.dot(v, s, preferred_element_type=F32)
        state = state_ref[hh]
        cross = jnp.dot(state.astype(BF16), (q * qdec_ref[hh:hh + 1, :]).astype(BF16),
                        preferred_element_type=F32)
        kd = (k * kdec_ref[hh:hh + 1, :]).astype(BF16)
        state_ref[hh] = state * cdec_ref[hh] + lax.dot_general(v, kd, _NT, preferred_element_type=F32)
        out = inner + cross
        mu = jnp.mean(out, axis=0, keepdims=True)
        var = jnp.mean(jnp.square(out - mu), axis=0, keepdims=True)
        out = (out - mu) * lax.rsqrt(var + LN_EPS)
        g = g_ref[0, rows, :].astype(F32)
        o_ref[0, rows, :] = (g * jax.nn.sigmoid(g) * out).astype(BF16)


def _retention(proj_t):
    B, _, S = proj_t.shape
    C = RET_CHUNK
    H = RET_HEADS
    half = RET_HEAD_DIM // 2
    inv = ROPE_BASE ** (-jnp.arange(half, dtype=F32) / half)
    ang = inv[:, None] * jnp.arange(S).astype(F32)[None, :]
    cos_t, sin_t = jnp.cos(ang), jnp.sin(ang)
    gammas = 1.0 - jnp.exp(jnp.linspace(math.log(1.0 / 32), math.log(1.0 / 512), H, dtype=F32))
    log_g = jnp.log(gammas)
    idx = jnp.arange(C, dtype=F32)
    diff = idx[None, :] - idx[:, None]
    dec_t = jnp.where(diff[None] >= 0, jnp.exp(jnp.maximum(diff, 0.0)[None] * log_g[:, None, None]), 0.0)
    q_dec = jnp.exp((idx[None, :] + 1.0) * log_g[:, None])
    k_dec = jnp.exp((C - 1.0 - idx[None, :]) * log_g[:, None])
    chunk_dec = jnp.exp(C * log_g)

    grid_spec = pltpu.PrefetchScalarGridSpec(
        num_scalar_prefetch=1,
        grid=(B, S // C),
        in_specs=[
            pl.BlockSpec((1, RET_WIDTH, C), lambda b, c, cd: (b, 1, c)),
            pl.BlockSpec((1, RET_WIDTH, C), lambda b, c, cd: (b, 2, c)),
            pl.BlockSpec((1, RET_WIDTH, C), lambda b, c, cd: (b, 3, c)),
            pl.BlockSpec((1, RET_WIDTH, C), lambda b, c, cd: (b, 4, c)),
            pl.BlockSpec((half, C), lambda b, c, cd: (0, c)),
            pl.BlockSpec((half, C), lambda b, c, cd: (0, c)),
            pl.BlockSpec((H, C, C), lambda b, c, cd: (0, 0, 0)),
            pl.BlockSpec((H, C), lambda b, c, cd: (0, 0)),
            pl.BlockSpec((H, C), lambda b, c, cd: (0, 0)),
        ],
        out_specs=pl.BlockSpec((1, RET_WIDTH, C), lambda b, c, cd: (b, 0, c)),
        scratch_shapes=[pltpu.VMEM((H, RET_HEAD_DIM, RET_HEAD_DIM), F32)],
    )
    return pl.pallas_call(
        _retention_kernel,
        grid_spec=grid_spec,
        out_shape=jax.ShapeDtypeStruct((B, RET_WIDTH, S), BF16),
        compiler_params=_params("arbitrary", "arbitrary"),
        name="retention",
    )(chunk_dec, proj_t, proj_t, proj_t, proj_t, cos_t, sin_t, dec_t, q_dec, k_dec)


def _layer_norm_rows(y, g, b):
    mu = jnp.mean(y, axis=-1, keepdims=True)
    var = jnp.mean(jnp.square(y - mu), axis=-1, keepdims=True)
    return (y - mu) * lax.rsqrt(var + LN_EPS) * g + b


def _outproj_kernel(attn_ref, ret_ref, x_ref, gain_ref, wa_ref, wr_ref, g1_ref, b1_ref,
                    wrt_ref, brt_ref,
                    h_ref, eid_ref, rank_ref, cw_ref, cnt_ref, carry_ref):
    first = jnp.logical_and(pl.program_id(0) == 0, pl.program_id(1) == 0)

    @pl.when(first)
    def _():
        carry_ref[...] = jnp.zeros(carry_ref.shape, F32)

    a = attn_ref[0]
    a = a * lax.rsqrt(jnp.mean(jnp.square(a), axis=0, keepdims=True) + LN_EPS) * gain_ref[...]
    mix = (lax.dot_general(a.astype(BF16), wa_ref[...], _TN, preferred_element_type=F32)
           + lax.dot_general(ret_ref[0], wr_ref[...], _TN, preferred_element_type=F32))
    h = _layer_norm_rows(ALPHA * x_ref[0] + mix, g1_ref[...], b1_ref[...])
    h_ref[...] = h

    logits = lax.dot_general(wrt_ref[...], h, _NT, precision=lax.Precision.HIGHEST,
                             preferred_element_type=F32) + brt_ref[...]
    T = logits.shape[1]
    gl = logits[0:N_GROUPS]
    gmax = jnp.max(gl, axis=0, keepdims=True)
    grow = lax.broadcasted_iota(jnp.int32, gl.shape, 0)
    gidx = jnp.min(jnp.where(gl == gmax, grow, N_GROUPS), axis=0, keepdims=True)
    g_p = 1.0 / jnp.sum(jnp.exp(gl - gmax), axis=0, keepdims=True)
    el = jnp.zeros((EXPERTS_PER_GROUP, T), F32)
    for g in range(N_GROUPS):
        r0 = ROUTER_EXPERT_ROW0 + g * EXPERTS_PER_GROUP
        el = jnp.where(gidx == g, logits[r0:r0 + EXPERTS_PER_GROUP], el)
    erow = lax.broadcasted_iota(jnp.int32, el.shape, 0)
    e1 = jnp.max(el, axis=0, keepdims=True)
    i1 = jnp.min(jnp.where(el == e1, erow, EXPERTS_PER_GROUP), axis=0, keepdims=True)
    el2 = jnp.where(erow == i1, -jnp.inf, el)
    e2 = jnp.max(el2, axis=0, keepdims=True)
    i2 = jnp.min(jnp.where(el2 == e2, erow, EXPERTS_PER_GROUP), axis=0, keepdims=True)
    r = jnp.exp(e2 - e1)
    w1 = g_p / (1.0 + r)
    w2 = g_p * r / (1.0 + r)
    id1 = gidx * EXPERTS_PER_GROUP + i1
    id2 = gidx * EXPERTS_PER_GROUP + i2
    eid_ref[0:1, :] = id1
    eid_ref[1:2, :] = id2
    cw_ref[0:1, :] = w1
    cw_ref[1:2, :] = w2

    xrow = lax.broadcasted_iota(jnp.int32, (N_EXPERTS, T), 0)
    oh1 = (xrow == id1).astype(F32)
    oh2 = (xrow == id2).astype(F32)
    both = oh1 + oh2
    before = (lax.broadcasted_iota(jnp.int32, (T, T), 0)
              < lax.broadcasted_iota(jnp.int32, (T, T), 1)).astype(BF16)
    seen = jnp.dot(both.astype(BF16), before, preferred_element_type=F32) + carry_ref[...]
    rank_ref[0:1, :] = jnp.sum(oh1 * seen, axis=0, keepdims=True).astype(jnp.int32)
    rank_ref[1:2, :] = jnp.sum(oh2 * seen, axis=0, keepdims=True).astype(jnp.int32)
    carry = carry_ref[...] + jnp.sum(both, axis=1, keepdims=True)
    carry_ref[...] = carry
    cnt_ref[...] = carry.astype(jnp.int32)


def _outproj(attn_t, ret_t, x, gain, w_out, ln_g, ln_b, w_rg, b_rg, w_re, b_re):
    B, S, D = x.shape
    Tt = OUTPROJ_TOKENS
    n_t = S // Tt
    T = B * S
    wa = w_out[:ATTN_WIDTH].astype(BF16)
    wr = w_out[ATTN_WIDTH:].astype(BF16)
    wrt = jnp.zeros((ROUTER_ROWS, D), F32)
    wrt = wrt.at[:N_GROUPS].set(w_rg.T).at[ROUTER_EXPERT_ROW0:].set(w_re.T)
    brt = jnp.zeros((ROUTER_ROWS, 1), F32)
    brt = brt.at[:N_GROUPS, 0].set(b_rg.astype(F32)).at[ROUTER_EXPERT_ROW0:, 0].set(b_re.astype(F32))
    const = lambda b, i: (0, 0)
    tok = lambda b, i: (0, b * n_t + i)
    return pl.pallas_call(
        _outproj_kernel,
        grid=(B, n_t),
        in_specs=[
            pl.BlockSpec((1, ATTN_WIDTH, Tt), lambda b, i: (b, 0, i)),
            pl.BlockSpec((1, RET_WIDTH, Tt), lambda b, i: (b, 0, i)),
            pl.BlockSpec((1, Tt, D), lambda b, i: (b, i, 0)),
            pl.BlockSpec((ATTN_WIDTH, 1), const),
            pl.BlockSpec((ATTN_WIDTH, D), const),
            pl.BlockSpec((RET_WIDTH, D), const),
            pl.BlockSpec((1, D), const),
            pl.BlockSpec((1, D), const),
            pl.BlockSpec((ROUTER_ROWS, D), const),
            pl.BlockSpec((ROUTER_ROWS, 1), const),
        ],
        out_specs=[
            pl.BlockSpec((Tt, D), lambda b, i: (b * n_t + i, 0)),
            pl.BlockSpec((2, Tt), tok),
            pl.BlockSpec((2, Tt), tok),
            pl.BlockSpec((2, Tt), tok),
            pl.BlockSpec((N_EXPERTS, 1), const),
        ],
        out_shape=[
            jax.ShapeDtypeStruct((T, D), F32),
            jax.ShapeDtypeStruct((2, T), jnp.int32),
            jax.ShapeDtypeStruct((2, T), jnp.int32),
            jax.ShapeDtypeStruct((2, T), F32),
            jax.ShapeDtypeStruct((N_EXPERTS, 1), jnp.int32),
        ],
        scratch_shapes=[pltpu.VMEM((N_EXPERTS, 1), F32)],
        compiler_params=_params("arbitrary", "arbitrary"),
        name="outproj_ln_router",
    )(attn_t, ret_t, x, gain.reshape(ATTN_WIDTH, 1), wa, wr, ln_g.reshape(1, D), ln_b.reshape(1, D),
      wrt, brt)


def _dispatch_kernel(start_ref, end_ref, pos0_ref, pos1_ref, h_ref, xs_hbm, zero_ref, sem, zsem):
    G = ROUTE_TOKENS
    Tm = MOE_ROWS

    @pl.when(pl.program_id(0) == 0)
    def _():
        zero_ref[...] = jnp.zeros(zero_ref.shape, F32)

        def tile_clear(row0):
            row0 = pl.multiple_of(row0, Tm)
            return pltpu.make_async_copy(zero_ref, xs_hbm.at[pl.ds(row0, Tm)], zsem)

        used = end_ref[N_EXPERTS - 1]
        n_rows = xs_hbm.shape[0]
        clears = [(end_ref[e] - Tm, end_ref[e] > start_ref[e]) for e in range(N_EXPERTS)]
        clears += [(jnp.minimum(used + k * Tm, n_rows - Tm), used + k * Tm < n_rows) for k in range(N_EXPERTS)]
        for row0, cond in clears:
            @pl.when(cond)
            def _():
                tile_clear(row0).start()
        for row0, cond in clears:
            @pl.when(cond)
            def _():
                tile_clear(row0).wait()

    def issue(k, carry):
        t0 = pl.multiple_of(k * F32_SUBLANES, F32_SUBLANES)
        for u in range(F32_SUBLANES):
            for pos_ref in (pos0_ref, pos1_ref):
                pltpu.make_async_copy(h_ref.at[pl.ds(t0 + u, 1)], xs_hbm.at[pl.ds(pos_ref[t0 + u], 1)], sem).start()
        return carry

    lax.fori_loop(0, G // F32_SUBLANES, issue, 0)
    for _ in range(2):
        pltpu.make_async_copy(h_ref, xs_hbm.at[pl.ds(0, G)], sem).wait()


def _dispatch(h, pos0, pos1, starts, ends, n_rows):
    T, D = h.shape
    G = ROUTE_TOKENS
    smem_blk = pl.BlockSpec((G,), lambda i, st, en: (i,), memory_space=pltpu.SMEM)
    grid_spec = pltpu.PrefetchScalarGridSpec(
        num_scalar_prefetch=2,
        grid=(T // G,),
        in_specs=[smem_blk, smem_blk, pl.BlockSpec((G, D), lambda i, st, en: (i, 0))],
        out_specs=pl.BlockSpec(memory_space=pl.ANY),
        scratch_shapes=[pltpu.VMEM((MOE_ROWS, D), F32), pltpu.SemaphoreType.DMA(()),
                        pltpu.SemaphoreType.DMA(())],
    )
    return pl.pallas_call(
        _dispatch_kernel,
        grid_spec=grid_spec,
        out_shape=jax.ShapeDtypeStruct((n_rows, D), F32),
        compiler_params=_params("arbitrary"),
        name="moe_dispatch",
    )(starts, ends, pos0, pos1, h)


def _experts_kernel(te_ref, nv_ref, xs_ref, wgu_ref, wdn_ref, y_ref, wgu_b, wdn_b):
    i = pl.program_id(0)
    live = i < nv_ref[0]
    new_expert = jnp.logical_or(i == 0, te_ref[i] != te_ref[jnp.maximum(i - 1, 0)])

    @pl.when(jnp.logical_and(live, new_expert))
    def _():
        wgu_b[...] = wgu_ref[0].astype(BF16)
        wdn_b[...] = wdn_ref[0].astype(BF16)

    @pl.when(live)
    def _():
        gu = jnp.dot(xs_ref[...].astype(BF16), wgu_b[...], preferred_element_type=F32)
        gate, up = gu[:, :D_EXPERT], gu[:, D_EXPERT:]
        hid = gate * jax.nn.sigmoid(gate) * up
        y_ref[...] = jnp.dot(hid.astype(BF16), wdn_b[...], preferred_element_type=F32)

    @pl.when(jnp.logical_not(live))
    def _():
        y_ref[...] = jnp.zeros(y_ref.shape, F32)


def _experts(xs, w_gu, w_dn, tile_expert, n_valid):
    n_rows, D = xs.shape
    Tm = MOE_ROWS
    grid_spec = pltpu.PrefetchScalarGridSpec(
        num_scalar_prefetch=2,
        grid=(n_rows // Tm,),
        in_specs=[
            pl.BlockSpec((Tm, D), lambda i, te, nv: (jnp.minimum(i, jnp.maximum(nv[0] - 1, 0)), 0)),
            pl.BlockSpec((1, D, 2 * D_EXPERT), lambda i, te, nv: (te[i], 0, 0)),
            pl.BlockSpec((1, D_EXPERT, D), lambda i, te, nv: (te[i], 0, 0)),
        ],
        out_specs=pl.BlockSpec((Tm, D), lambda i, te, nv: (i, 0)),
        scratch_shapes=[pltpu.VMEM((D, 2 * D_EXPERT), BF16), pltpu.VMEM((D_EXPERT, D), BF16)],
    )
    return pl.pallas_call(
        _experts_kernel,
        grid_spec=grid_spec,
        out_shape=jax.ShapeDtypeStruct((n_rows, D), F32),
        compiler_params=_params("arbitrary"),
        name="moe_experts",
    )(tile_expert, n_valid, xs, w_gu, w_dn)


def _combine_kernel(pos0_ref, pos1_ref, y_hbm, h_ref, cw_ref, g_ref, b_ref, o_ref, ybuf, sem):
    G = ROUTE_TOKENS

    def issue(k, carry):
        t0 = pl.multiple_of(k * F32_SUBLANES, F32_SUBLANES)
        for u in range(F32_SUBLANES):
            for slot, pos_ref in enumerate((pos0_ref, pos1_ref)):
                pltpu.make_async_copy(y_hbm.at[pl.ds(pos_ref[t0 + u], 1)], ybuf.at[slot, pl.ds(t0 + u, 1)],
                                      sem).start()
        return carry

    lax.fori_loop(0, G // F32_SUBLANES, issue, 0)
    for slot in range(2):
        pltpu.make_async_copy(y_hbm.at[pl.ds(0, G)], ybuf.at[slot], sem).wait()

    for r in range(G // COMBINE_CHUNK):
        rows = slice(r * COMBINE_CHUNK, (r + 1) * COMBINE_CHUNK)
        cw = cw_ref[rows, :]
        ffn = ybuf[0, rows, :] * cw[:, 0:1] + ybuf[1, rows, :] * cw[:, 1:2]
        o_ref[rows, :] = _layer_norm_rows(ALPHA * h_ref[rows, :] + ffn, g_ref[...], b_ref[...])


def _combine(y, h, pos0, pos1, cw_rows, ln_g, ln_b):
    T, D = h.shape
    G = ROUTE_TOKENS
    smem_blk = pl.BlockSpec((G,), lambda i: (i,), memory_space=pltpu.SMEM)
    return pl.pallas_call(
        _combine_kernel,
        grid=(T // G,),
        in_specs=[
            smem_blk, smem_blk,
            pl.BlockSpec(memory_space=pl.ANY),
            pl.BlockSpec((G, D), lambda i: (i, 0)),
            pl.BlockSpec((G, 2), lambda i: (i, 0)),
            pl.BlockSpec((1, D), lambda i: (0, 0)),
            pl.BlockSpec((1, D), lambda i: (0, 0)),
        ],
        out_specs=pl.BlockSpec((G, D), lambda i: (i, 0)),
        scratch_shapes=[pltpu.VMEM((2, G, D), F32), pltpu.SemaphoreType.DMA(())],
        out_shape=jax.ShapeDtypeStruct((T, D), F32),
        compiler_params=_params("arbitrary"),
        name="moe_combine_ln",
    )(pos0, pos1, y, h, cw_rows, ln_g.reshape(1, D), ln_b.reshape(1, D))


def kernel(x, w_in, attn_out_gain, rel_bias_table, w_out, ln1_g, ln1_b, w_router_group,
           b_router_group, w_router_expert, b_router_expert, w_gate_up, w_down, ln2_g, ln2_b):
    B, S, D = x.shape
    assert D == D_MODEL and S % MOBA_BLOCK == 0 and S % INPROJ_TOKENS == 0
    assert w_in.shape[0] == DEPTH
    T = B * S
    h = x
    for l in range(DEPTH):
        proj_t, k_nat = _inproj(h, w_in[l])
        attn_t = _moba(proj_t, k_nat, rel_bias_table)
        ret_t = _retention(proj_t)
        h1, eid, rank, cw, counts = _outproj(
            attn_t, ret_t, h, attn_out_gain[l], w_out[l], ln1_g[l], ln1_b[l],
            w_router_group[l], b_router_group[l], w_router_expert[l], b_router_expert[l])

        Tm = MOE_ROWS
        n_tiles = (2 * T) // Tm + N_EXPERTS
        padded = ((counts[:, 0] + Tm - 1) // Tm) * Tm
        ends = jnp.cumsum(padded)
        starts = (ends - padded).astype(jnp.int32)
        tile_row0 = jnp.arange(n_tiles, dtype=jnp.int32) * Tm
        tile_expert = jnp.sum((ends[None, :] <= jnp.minimum(tile_row0, ends[-1] - Tm)[:, None]).astype(jnp.int32),
                              axis=1)
        n_valid = (ends[-1:] // Tm).astype(jnp.int32)

        expert_ids = jnp.arange(N_EXPERTS, dtype=jnp.int32)[:, None, None]
        pos = jnp.sum(jnp.where(eid[None] == expert_ids, starts[:, None, None], 0), axis=0) + rank

        xs = _dispatch(h1, pos[0], pos[1], starts, ends.astype(jnp.int32), n_tiles * Tm)
        y = _experts(xs, w_gate_up[l], w_down[l], tile_expert, n_valid)
        h2 = _combine(y, h1, pos[0], pos[1], cw.T, ln2_g[l], ln2_b[l])
        h = h2.reshape(B, S, D)
    return h
```

```python
import functools
import math

import numpy as np
import jax
import jax.numpy as jnp
from jax import lax
from jax.experimental import pallas as pl
from jax.experimental.pallas import tpu as pltpu

F32 = jnp.float32
BF16 = jnp.bfloat16

D_MODEL = 1024
ATTN_HEADS = 8
ATTN_HEAD_DIM = 64
ATTN_WIDTH = ATTN_HEADS * ATTN_HEAD_DIM
RET_HEADS = 4
RET_HEAD_DIM = 128
RET_WIDTH = RET_HEADS * RET_HEAD_DIM
N_IN_SLICES = 7
PROJT_SLICES = N_IN_SLICES - 1
HEAD_PAIR = 2 * ATTN_HEAD_DIM
BF16_SUBLANES = 16
F32_SUBLANES = 8
V_ROWS = ATTN_HEAD_DIM + BF16_SUBLANES
PROJT_V_ROW0 = (PROJT_SLICES - 1) * ATTN_WIDTH
PROJT_WIDTH = PROJT_V_ROW0 + ATTN_HEADS * V_ROWS
MOBA_BLOCK = 256
MOBA_TOPK = 3
RET_CHUNK = 256
REL_BUCKETS = 32
REL_MAX_DIST = 128
ROPE_BASE = 10000.0
N_GROUPS = 4
EXPERTS_PER_GROUP = 8
N_EXPERTS = N_GROUPS * EXPERTS_PER_GROUP
D_EXPERT = D_MODEL // 2
DEPTH = 1
ALPHA = (2.0 * DEPTH) ** 0.25
LN_EPS = 1e-5

LOG2E = math.log2(math.e)
Q_SCALE = ATTN_HEAD_DIM ** -0.5 * LOG2E

MASKED = -1e30

INPROJ_TOKENS = 512
INPROJ_ROWS = ATTN_WIDTH
OUTPROJ_TOKENS = 512
MOBA_UNROLL = 2
MOE_ROWS = 512
ROUTE_TOKENS = 1024
COMBINE_CHUNK = 256
ROUTER_EXPERT_ROW0 = 8
ROUTER_ROWS = 48

VMEM_LIMIT = 48 * 1024 * 1024
MOBA_VMEM_LIMIT = 58 * 1024 * 1024

_TN = (((0,), (0,)), ((), ()))
_NT = (((1,), (1,)), ((), ()))


def _params(*sem):
    return pltpu.CompilerParams(dimension_semantics=sem, vmem_limit_bytes=VMEM_LIMIT)


def _inproj_kernel(x_ref, wt_ref, wk_ref, o_ref, k_ref):
    xb = x_ref[0].astype(BF16)
    n_tok = xb.shape[0]
    ones_row = (lax.broadcasted_iota(jnp.int32, (BF16_SUBLANES, n_tok), 0) == 0).astype(BF16)
    for c in range(PROJT_SLICES):
        rows = slice(c * INPROJ_ROWS, (c + 1) * INPROJ_ROWS)
        acc = lax.dot_general(wt_ref[rows, :], xb, _NT, preferred_element_type=F32)
        if c == 0:
            acc = acc * Q_SCALE
        if c < PROJT_SLICES - 1:
            o_ref[0, rows, :] = acc.astype(BF16)
        else:
            for h in range(ATTN_HEADS):
                r0 = PROJT_V_ROW0 + h * V_ROWS
                o_ref[0, r0:r0 + ATTN_HEAD_DIM, :] = acc[h * ATTN_HEAD_DIM:(h + 1) * ATTN_HEAD_DIM].astype(BF16)
                o_ref[0, r0 + ATTN_HEAD_DIM:r0 + V_ROWS, :] = ones_row
    k_ref[0] = jnp.dot(xb, wk_ref[...], preferred_element_type=F32).astype(BF16)


def _inproj(x, w_in):
    B, S, D = x.shape
    W = ATTN_WIDTH
    w_t = jnp.concatenate([w_in[:, :W], w_in[:, 3 * W:], w_in[:, 2 * W:3 * W]], axis=1).T.astype(BF16)
    w_k = w_in[:, W:2 * W].astype(BF16)
    return pl.pallas_call(
        _inproj_kernel,
        grid=(B, S // INPROJ_TOKENS),
        in_specs=[
            pl.BlockSpec((1, INPROJ_TOKENS, D), lambda b, i: (b, i, 0)),
            pl.BlockSpec((PROJT_SLICES * W, D), lambda b, i: (0, 0)),
            pl.BlockSpec((D, W), lambda b, i: (0, 0)),
        ],
        out_specs=[
            pl.BlockSpec((1, PROJT_WIDTH, INPROJ_TOKENS), lambda b, i: (b, 0, i)),
            pl.BlockSpec((1, INPROJ_TOKENS, W), lambda b, i: (b, i, 0)),
        ],
        out_shape=[
            jax.ShapeDtypeStruct((B, PROJT_WIDTH, S), BF16),
            jax.ShapeDtypeStruct((B, S, W), BF16),
        ],
        compiler_params=_params("arbitrary", "arbitrary"),
        name="inproj",
    )(x, w_t, w_k)


def _moba_kernel(far_ref, q_ref, k_ref, v_ref, bias_ref, o_ref,
                 kmean_ref, qz_ref, sel_ref, farsel_ref,
                 s0_ref, smax0_ref, p0_ref, alpha0_ref, s1_ref, smax1_ref, p1_ref, alpha1_ref,
                 m_ref, acc_ref, *, n_blocks):
    L = MOBA_BLOCK
    Dh = ATTN_HEAD_DIM
    H = ATTN_HEADS
    qi = pl.program_id(1)
    last_blk = n_blocks - 1
    s_ref, smax_ref = (s0_ref, s1_ref), (smax0_ref, smax1_ref)
    p_ref, alpha_ref = (p0_ref, p1_ref), (alpha0_ref, alpha1_ref)

    @pl.when(qi == 0)
    def _():
        chunk_blocks = min(8, n_blocks)
        chunk = chunk_blocks * L
        acc = jnp.zeros((n_blocks, ATTN_WIDTH), F32)
        for c in range(n_blocks // chunk_blocks):
            blk = lax.broadcasted_iota(jnp.int32, (n_blocks, chunk), 0)
            key = lax.broadcasted_iota(jnp.int32, (n_blocks, chunk), 1)
            ind = jnp.where(blk == c * chunk_blocks + key // L, 1.0 / L, 0.0).astype(BF16)
            acc = acc + jnp.dot(ind, k_ref[0, c * chunk:(c + 1) * chunk, :], preferred_element_type=F32)
        kmean_ref[...] = acc

    feat = lax.broadcasted_iota(jnp.int32, (HEAD_PAIR, L), 0)
    blk = lax.broadcasted_iota(jnp.int32, (n_blocks, L), 0)
    past = blk < qi
    for h in range(H):
        pair, hd = divmod(h, 2)
        qp = q_ref[0, pair * HEAD_PAIR:(pair + 1) * HEAD_PAIR, :]
        qz = jnp.where((feat >= hd * Dh) & (feat < (hd + 1) * Dh), qp, jnp.zeros_like(qp))
        qz_ref[h] = qz
        gate = jnp.dot(kmean_ref[:, pair * HEAD_PAIR:(pair + 1) * HEAD_PAIR], qz.astype(F32),
                       precision=lax.Precision.HIGHEST, preferred_element_type=F32)
        gate = jnp.where(past, gate, -jnp.inf)
        chosen = jnp.zeros(gate.shape, jnp.bool_)
        for _ in range(MOBA_TOPK):
            top = jnp.max(gate, axis=0, keepdims=True)
            first = jnp.min(jnp.where(gate == top, blk, n_blocks), axis=0, keepdims=True)
            hit = blk == first
            chosen = jnp.logical_or(chosen, hit)
            gate = jnp.where(hit, -jnp.inf, gate)
        chosen = jnp.logical_and(chosen, past)
        sel_ref[h] = jnp.where(chosen, 0.0, MASKED)
        farsel_ref[h, 0:n_blocks, :] = jnp.where(jnp.logical_and(chosen, blk < qi - 1), far_ref[h], MASKED)
        farsel_ref[h, n_blocks:n_blocks + F32_SUBLANES, :] = jnp.zeros((F32_SUBLANES, L), F32)

    def stage_scores(h, j, tile, slot):
        pair = h // 2
        kb = k_ref[0, pl.ds(pl.multiple_of(j * L, L), L), pair * HEAD_PAIR:(pair + 1) * HEAD_PAIR]
        s = jnp.dot(kb, qz_ref[h], preferred_element_type=F32)
        if tile is not None:
            s = s + tile
        s_ref[slot][h] = s
        smax_ref[slot][h] = jnp.max(s, axis=0, keepdims=True)

    def stage_softmax(h, row, slot):
        m_prev = m_ref[h]
        m_new = jnp.maximum(m_prev, smax_ref[slot][h] + row)
        alpha_ref[slot][h] = jnp.exp2(m_prev - m_new)
        p_ref[slot][h] = jnp.exp2(s_ref[slot][h] - (m_new - row)).astype(BF16)
        m_ref[h] = m_new

    def stage_values(h, j, slot):
        vb = v_ref[0, h * V_ROWS:(h + 1) * V_ROWS, pl.ds(pl.multiple_of(j * L, L), L)]
        pv = jnp.dot(vb, p_ref[slot][h], preferred_element_type=F32)
        acc_ref[h] = alpha_ref[slot][h] * acc_ref[h] + pv

    m_ref[...] = jnp.full(m_ref.shape, MASKED, F32)
    acc_ref[...] = jnp.zeros(acc_ref.shape, F32)

    jp = jnp.maximum(qi - 1, 0)
    zero_row = jnp.zeros((1, L), F32)
    for h in range(H):
        stage_scores(h, qi, bias_ref[0, h], 0)
        stage_scores(h, jp, bias_ref[1, h] + sel_ref[h, pl.ds(jp, 1), :], 1)
    for h in range(H):
        stage_softmax(h, zero_row, 0)
    n_steps = 2 + jnp.maximum(qi - 1, 0)

    def step(t, slot):
        value_block = jnp.where(t == 0, qi, jnp.where(t == 1, jp, jnp.minimum(t - 2, last_blk)))
        j = jnp.minimum(t, last_blk)
        jr = jnp.where(t == 0, n_blocks, jnp.minimum(t - 1, last_blk))
        for h in range(H):
            stage_scores(h, j, None, slot)
            stage_softmax(h, farsel_ref[h, pl.ds(jr, 1), :], 1 - slot)
            stage_values(h, value_block, slot)

    def unrolled_steps(k, carry):
        for u in range(MOBA_UNROLL):
            step(MOBA_UNROLL * k + u, u % 2)
        return carry

    lax.fori_loop(0, (n_steps + MOBA_UNROLL - 1) // MOBA_UNROLL, unrolled_steps, 0)
    for h in range(H):
        o_ref[0, h * Dh:(h + 1) * Dh, :] = acc_ref[h, 0:Dh, :] / acc_ref[h, Dh:Dh + 1, :]


def _t5_bucket_np(dist):
    n = np.maximum(dist, 0)
    max_exact = REL_BUCKETS // 2
    nf = np.maximum(n, 1).astype(np.float32)
    large = max_exact + (np.log(nf / max_exact) / math.log(REL_MAX_DIST / max_exact)
                         * (REL_BUCKETS - max_exact)).astype(np.int32)
    large = np.minimum(large, REL_BUCKETS - 1)
    return np.where(n < max_exact, n, large)


def _moba(proj_t, k_nat, rel_table):
    B, _, S = proj_t.shape
    L = MOBA_BLOCK
    n_blocks = S // L
    table_t = rel_table.astype(F32).T
    key = np.arange(L)[:, None]
    qry = np.arange(L)[None, :]
    d_own = qry - key
    bucket = np.stack([_t5_bucket_np(d_own), _t5_bucket_np(d_own + L)])
    onehot = jnp.asarray(bucket)[:, None, None] == jnp.arange(REL_BUCKETS)[None, None, :, None, None]
    bias = jnp.sum(jnp.where(onehot, table_t[None, :, :, None, None], 0.0), axis=2)
    causal = jnp.asarray(np.stack([d_own >= 0, np.ones_like(d_own, bool)]))[:, None]
    bias = jnp.where(causal, bias * LOG2E, MASKED)
    assert int(_t5_bucket_np(np.array(L + 1))) == REL_BUCKETS - 1
    far = table_t[:, REL_BUCKETS - 1] * LOG2E

    H = ATTN_HEADS
    v_rows = H * V_ROWS
    assert PROJT_V_ROW0 % v_rows == 0
    once = dict(pipeline_mode=pl.Buffered(1))
    grid_spec = pltpu.PrefetchScalarGridSpec(
        num_scalar_prefetch=1,
        grid=(B, n_blocks),
        in_specs=[
            pl.BlockSpec((1, ATTN_WIDTH, L), lambda b, i, far: (b, 0, i)),
            pl.BlockSpec((1, S, ATTN_WIDTH), lambda b, i, far: (b, 0, 0), **once),
            pl.BlockSpec((1, v_rows, S), lambda b, i, far: (b, PROJT_V_ROW0 // v_rows, 0), **once),
            pl.BlockSpec((2, H, L, L), lambda b, i, far: (0, 0, 0, 0), **once),
        ],
        out_specs=pl.BlockSpec((1, ATTN_WIDTH, L), lambda b, i, far: (b, 0, i)),
        scratch_shapes=[
            pltpu.VMEM((n_blocks, ATTN_WIDTH), F32),
            pltpu.VMEM((H, HEAD_PAIR, L), BF16),
            pltpu.VMEM((H, n_blocks, L), F32),
            pltpu.VMEM((H, n_blocks + F32_SUBLANES, L), F32),
            *([pltpu.VMEM((H, L, L), F32), pltpu.VMEM((H, 1, L), F32),
               pltpu.VMEM((H, L, L), BF16), pltpu.VMEM((H, 1, L), F32)] * 2),
            pltpu.VMEM((H, 1, L), F32),
            pltpu.VMEM((H, V_ROWS, L), F32),
        ],
    )
    return pl.pallas_call(
        functools.partial(_moba_kernel, n_blocks=n_blocks),
        grid_spec=grid_spec,
        out_shape=jax.ShapeDtypeStruct((B, ATTN_WIDTH, S), F32),
        compiler_params=pltpu.CompilerParams(dimension_semantics=("arbitrary", "arbitrary"),
                                             vmem_limit_bytes=MOBA_VMEM_LIMIT),
        name="moba",
    )(far, proj_t, k_nat, proj_t, bias)


def _retention_kernel(cdec_ref, q_ref, k_ref, v_ref, g_ref, cos_ref, sin_ref,
                      dec_ref, qdec_ref, kdec_ref, o_ref, state_ref):
    c = pl.program_id(1)

    @pl.when(c == 0)
    def _():
        state_ref[...] = jnp.zeros(state_ref.shape, F32)

    cos = cos_ref[...]
    sin = sin_ref[...]
    half = RET_HEAD_DIM // 2

    def rope(t):
        t1, t2 = t[:half], t[half:]
        return jnp.concatenate([t1 * cos - t2 * sin, t1 * sin + t2 * cos], axis=0)

    for hh in range(RET_HEADS):
        rows = slice(hh * RET_HEAD_DIM, (hh + 1) * RET_HEAD_DIM)
        q = rope(q_ref[0, rows, :].astype(F32))
        k = rope(k_ref[0, rows, :].astype(F32)) * (RET_HEAD_DIM ** -0.5)
        v = v_ref[0, rows, :]
        qb = q.astype(BF16)
        s = lax.dot_general(k.astype(BF16), qb, _TN, preferred_element_type=F32)
        s = (s * dec_ref[hh]).astype(BF16)
        inner = jnp.dot(v, s, preferred_element_type=F32)
        state = state_ref[hh]
        cross = jnp.dot(state.astype(BF16), (q * qdec_ref[hh:hh + 1, :]).astype(BF16),
                        preferred_element_type=F32)
        kd = (k * kdec_ref[hh:hh + 1, :]).astype(BF16)
        state_ref[hh] = state * cdec_ref[hh] + lax.dot_general(v, kd, _NT, preferred_element_type=F32)
        out = inner + cross
        mu = jnp.mean(out, axis=0, keepdims=True)
        var = jnp.mean(jnp.square(out - mu), axis=0, keepdims=True)
        out = (out - mu) * lax.rsqrt(var + LN_EPS)
        g = g_ref[0, rows, :].astype(F32)
        o_ref[0, rows, :] = (g * jax.nn.sigmoid(g) * out).astype(BF16)


def _retention(proj_t):
    B, _, S = proj_t.shape
    C = RET_CHUNK
    H = RET_HEADS
    half = RET_HEAD_DIM // 2
    inv = ROPE_BASE ** (-jnp.arange(half, dtype=F32) / half)
    ang = inv[:, None] * jnp.arange(S).astype(F32)[None, :]
    cos_t, sin_t = jnp.cos(ang), jnp.sin(ang)
    gammas = 1.0 - jnp.exp(jnp.linspace(math.log(1.0 / 32), math.log(1.0 / 512), H, dtype=F32))
    log_g = jnp.log(gammas)
    idx = jnp.arange(C, dtype=F32)
    diff = idx[None, :] - idx[:, None]
    dec_t = jnp.where(diff[None] >= 0, jnp.exp(jnp.maximum(diff, 0.0)[None] * log_g[:, None, None]), 0.0)
    q_dec = jnp.exp((idx[None, :] + 1.0) * log_g[:, None])
    k_dec = jnp.exp((C - 1.0 - idx[None, :]) * log_g[:, None])
    chunk_dec = jnp.exp(C * log_g)

    grid_spec = pltpu.PrefetchScalarGridSpec(
        num_scalar_prefetch=1,
        grid=(B, S // C),
        in_specs=[
            pl.BlockSpec((1, RET_WIDTH, C), lambda b, c, cd: (b, 1, c)),
            pl.BlockSpec((1, RET_WIDTH, C), lambda b, c, cd: (b, 2, c)),
            pl.BlockSpec((1, RET_WIDTH, C), lambda b, c, cd: (b, 3, c)),
            pl.BlockSpec((1, RET_WIDTH, C), lambda b, c, cd: (b, 4, c)),
            pl.BlockSpec((half, C), lambda b, c, cd: (0, c)),
            pl.BlockSpec((half, C), lambda b, c, cd: (0, c)),
            pl.BlockSpec((H, C, C), lambda b, c, cd: (0, 0, 0)),
            pl.BlockSpec((H, C), lambda b, c, cd: (0, 0)),
            pl.BlockSpec((H, C), lambda b, c, cd: (0, 0)),
        ],
        out_specs=pl.BlockSpec((1, RET_WIDTH, C), lambda b, c, cd: (b, 0, c)),
        scratch_shapes=[pltpu.VMEM((H, RET_HEAD_DIM, RET_HEAD_DIM), F32)],
    )
    return pl.pallas_call(
        _retention_kernel,
        grid_spec=grid_spec,
        out_shape=jax.ShapeDtypeStruct((B, RET_WIDTH, S), BF16),
        compiler_params=_params("arbitrary", "arbitrary"),
        name="retention",
    )(chunk_dec, proj_t, proj_t, proj_t, proj_t, cos_t, sin_t, dec_t, q_dec, k_dec)


def _layer_norm_rows(y, g, b):
    mu = jnp.mean(y, axis=-1, keepdims=True)
    var = jnp.mean(jnp.square(y - mu), axis=-1, keepdims=True)
    return (y - mu) * lax.rsqrt(var + LN_EPS) * g + b


def _outproj_kernel(attn_ref, ret_ref, x_ref, gain_ref, wa_ref, wr_ref, g1_ref, b1_ref,
                    wrt_ref, brt_ref, before_ref,
                    h_ref, eid_ref, rank_ref, cw_ref, cnt_ref, carry_ref):
    first = jnp.logical_and(pl.program_id(0) == 0, pl.program_id(1) == 0)

    @pl.when(first)
    def _():
        carry_ref[...] = jnp.zeros(carry_ref.shape, F32)

    a = attn_ref[0]
    a = a * lax.rsqrt(jnp.mean(jnp.square(a), axis=0, keepdims=True) + LN_EPS) * gain_ref[...]
    mix = (lax.dot_general(a.astype(BF16), wa_ref[...], _TN, preferred_element_type=F32)
           + lax.dot_general(ret_ref[0], wr_ref[...], _TN, preferred_element_type=F32))
    h = _layer_norm_rows(ALPHA * x_ref[0] + mix, g1_ref[...], b1_ref[...])
    h_ref[...] = h

    R = ROUTER_ROWS
    h_hi = h.astype(BF16)
    h_lo = (h - h_hi.astype(F32)).astype(BF16)
    by_hi = lax.dot_general(wrt_ref[...], h_hi, _NT, preferred_element_type=F32)
    by_lo = lax.dot_general(wrt_ref[0:R, :], h_lo, _NT, preferred_element_type=F32)
    logits = by_hi[0:R] + by_hi[R:2 * R] + by_lo + brt_ref[...]
    T = logits.shape[1]
    gl = logits[0:N_GROUPS]
    gmax = jnp.max(gl, axis=0, keepdims=True)
    grow = lax.broadcasted_iota(jnp.int32, gl.shape, 0)
    gidx = jnp.min(jnp.where(gl == gmax, grow, N_GROUPS), axis=0, keepdims=True)
    g_p = 1.0 / jnp.sum(jnp.exp(gl - gmax), axis=0, keepdims=True)
    el = jnp.zeros((EXPERTS_PER_GROUP, T), F32)
    for g in range(N_GROUPS):
        r0 = ROUTER_EXPERT_ROW0 + g * EXPERTS_PER_GROUP
        el = jnp.where(gidx == g, logits[r0:r0 + EXPERTS_PER_GROUP], el)
    erow = lax.broadcasted_iota(jnp.int32, el.shape, 0)
    e1 = jnp.max(el, axis=0, keepdims=True)
    i1 = jnp.min(jnp.where(el == e1, erow, EXPERTS_PER_GROUP), axis=0, keepdims=True)
    el2 = jnp.where(erow == i1, -jnp.inf, el)
    e2 = jnp.max(el2, axis=0, keepdims=True)
    i2 = jnp.min(jnp.where(el2 == e2, erow, EXPERTS_PER_GROUP), axis=0, keepdims=True)
    r = jnp.exp(e2 - e1)
    w1 = g_p / (1.0 + r)
    w2 = g_p * r / (1.0 + r)
    id1 = gidx * EXPERTS_PER_GROUP + i1
    id2 = gidx * EXPERTS_PER_GROUP + i2
    eid_ref[0:1, :] = id1
    eid_ref[1:2, :] = id2
    cw_ref[0:1, :] = w1
    cw_ref[1:2, :] = w2

    xrow = lax.broadcasted_iota(jnp.int32, (N_EXPERTS, T), 0)
    oh1 = (xrow == id1).astype(F32)
    oh2 = (xrow == id2).astype(F32)
    both = oh1 + oh2
    seen = jnp.dot(both.astype(BF16), before_ref[...], preferred_element_type=F32) + carry_ref[...]
    rank_ref[0:1, :] = jnp.sum(oh1 * seen, axis=0, keepdims=True).astype(jnp.int32)
    rank_ref[1:2, :] = jnp.sum(oh2 * seen, axis=0, keepdims=True).astype(jnp.int32)
    carry = carry_ref[...] + jnp.sum(both, axis=1, keepdims=True)
    carry_ref[...] = carry
    cnt_ref[...] = carry.astype(jnp.int32)


def _outproj(attn_t, ret_t, x, gain, w_out, ln_g, ln_b, w_rg, b_rg, w_re, b_re):
    B, S, D = x.shape
    Tt = OUTPROJ_TOKENS
    n_t = S // Tt
    T = B * S
    wa = w_out[:ATTN_WIDTH].astype(BF16)
    wr = w_out[ATTN_WIDTH:].astype(BF16)
    e0, e1 = ROUTER_EXPERT_ROW0, ROUTER_EXPERT_ROW0 + N_EXPERTS
    wrt = jnp.zeros((ROUTER_ROWS, D), F32)
    wrt = wrt.at[:N_GROUPS].set(w_rg.T).at[e0:e1].set(w_re.T)
    wrt_hi = wrt.astype(BF16)
    wrt_lo = (wrt - wrt_hi.astype(F32)).astype(BF16)
    wrt = jnp.concatenate([wrt_hi, wrt_lo], axis=0)
    brt = jnp.zeros((ROUTER_ROWS, 1), F32)
    brt = brt.at[:N_GROUPS, 0].set(b_rg.astype(F32)).at[e0:e1, 0].set(b_re.astype(F32))
    before = (jnp.arange(Tt)[:, None] < jnp.arange(Tt)[None, :]).astype(BF16)
    const = lambda b, i: (0, 0)
    tok = lambda b, i: (0, b * n_t + i)
    return pl.pallas_call(
        _outproj_kernel,
        grid=(B, n_t),
        in_specs=[
            pl.BlockSpec((1, ATTN_WIDTH, Tt), lambda b, i: (b, 0, i)),
            pl.BlockSpec((1, RET_WIDTH, Tt), lambda b, i: (b, 0, i)),
            pl.BlockSpec((1, Tt, D), lambda b, i: (b, i, 0)),
            pl.BlockSpec((ATTN_WIDTH, 1), const),
            pl.BlockSpec((ATTN_WIDTH, D), const),
            pl.BlockSpec((RET_WIDTH, D), const),
            pl.BlockSpec((1, D), const),
            pl.BlockSpec((1, D), const),
            pl.BlockSpec((2 * ROUTER_ROWS, D), const),
            pl.BlockSpec((ROUTER_ROWS, 1), const),
            pl.BlockSpec((Tt, Tt), const),
        ],
        out_specs=[
            pl.BlockSpec((Tt, D), lambda b, i: (b * n_t + i, 0)),
            pl.BlockSpec((2, Tt), tok),
            pl.BlockSpec((2, Tt), tok),
            pl.BlockSpec((2, Tt), tok),
            pl.BlockSpec((N_EXPERTS, 1), const),
        ],
        out_shape=[
            jax.ShapeDtypeStruct((T, D), F32),
            jax.ShapeDtypeStruct((2, T), jnp.int32),
            jax.ShapeDtypeStruct((2, T), jnp.int32),
            jax.ShapeDtypeStruct((2, T), F32),
            jax.ShapeDtypeStruct((N_EXPERTS, 1), jnp.int32),
        ],
        scratch_shapes=[pltpu.VMEM((N_EXPERTS, 1), F32)],
        compiler_params=_params("arbitrary", "arbitrary"),
        name="outproj_ln_router",
    )(attn_t, ret_t, x, gain.reshape(ATTN_WIDTH, 1), wa, wr, ln_g.reshape(1, D), ln_b.reshape(1, D),
      wrt, brt, before)


def _dispatch_kernel(start_ref, end_ref, pos0_ref, pos1_ref, h_ref, xs_hbm, zero_ref, sem, zsem):
    G = ROUTE_TOKENS
    Tm = MOE_ROWS

    @pl.when(pl.program_id(0) == 0)
    def _():
        zero_ref[...] = jnp.zeros(zero_ref.shape, F32)

        def tile_clear(row0):
            row0 = pl.multiple_of(row0, Tm)
            return pltpu.make_async_copy(zero_ref, xs_hbm.at[pl.ds(row0, Tm)], zsem)

        used = end_ref[N_EXPERTS - 1]
        n_rows = xs_hbm.shape[0]
        clears = [(end_ref[e] - Tm, end_ref[e] > start_ref[e]) for e in range(N_EXPERTS)]
        clears += [(jnp.minimum(used + k * Tm, n_rows - Tm), used + k * Tm < n_rows) for k in range(N_EXPERTS)]
        for row0, cond in clears:
            @pl.when(cond)
            def _():
                tile_clear(row0).start()
        for row0, cond in clears:
            @pl.when(cond)
            def _():
                tile_clear(row0).wait()

    def issue(k, carry):
        t0 = pl.multiple_of(k * F32_SUBLANES, F32_SUBLANES)
        for u in range(F32_SUBLANES):
            for pos_ref in (pos0_ref, pos1_ref):
                pltpu.make_async_copy(h_ref.at[pl.ds(t0 + u, 1)], xs_hbm.at[pl.ds(pos_ref[t0 + u], 1)], sem).start()
        return carry

    lax.fori_loop(0, G // F32_SUBLANES, issue, 0)
    for _ in range(2):
        pltpu.make_async_copy(h_ref, xs_hbm.at[pl.ds(0, G)], sem).wait()


def _dispatch(h, pos0, pos1, starts, ends, n_rows):
    T, D = h.shape
    G = ROUTE_TOKENS
    smem_blk = pl.BlockSpec((G,), lambda i, st, en: (i,), memory_space=pltpu.SMEM)
    grid_spec = pltpu.PrefetchScalarGridSpec(
        num_scalar_prefetch=2,
        grid=(T // G,),
        in_specs=[smem_blk, smem_blk, pl.BlockSpec((G, D), lambda i, st, en: (i, 0))],
        out_specs=pl.BlockSpec(memory_space=pl.ANY),
        scratch_shapes=[pltpu.VMEM((MOE_ROWS, D), F32), pltpu.SemaphoreType.DMA(()),
                        pltpu.SemaphoreType.DMA(())],
    )
    return pl.pallas_call(
        _dispatch_kernel,
        grid_spec=grid_spec,
        out_shape=jax.ShapeDtypeStruct((n_rows, D), F32),
        compiler_params=_params("arbitrary"),
        name="moe_dispatch",
    )(starts, ends, pos0, pos1, h)


def _experts_kernel(te_ref, nv_ref, xs_ref, wgu_ref, wdn_ref, y_ref, wgu_b, wdn_b):
    i = pl.program_id(0)
    live = i < nv_ref[0]
    new_expert = jnp.logical_or(i == 0, te_ref[i] != te_ref[jnp.maximum(i - 1, 0)])

    @pl.when(jnp.logical_and(live, new_expert))
    def _():
        wgu_b[...] = wgu_ref[0].astype(BF16)
        wdn_b[...] = wdn_ref[0].astype(BF16)

    @pl.when(live)
    def _():
        gu = jnp.dot(xs_ref[...].astype(BF16), wgu_b[...], preferred_element_type=F32)
        gate, up = gu[:, :D_EXPERT], gu[:, D_EXPERT:]
        hid = gate * jax.nn.sigmoid(gate) * up
        y_ref[...] = jnp.dot(hid.astype(BF16), wdn_b[...], preferred_element_type=F32)

    @pl.when(jnp.logical_not(live))
    def _():
        y_ref[...] = jnp.zeros(y_ref.shape, F32)


def _experts(xs, w_gu, w_dn, tile_expert, n_valid):
    n_rows, D = xs.shape
    Tm = MOE_ROWS
    grid_spec = pltpu.PrefetchScalarGridSpec(
        num_scalar_prefetch=2,
        grid=(n_rows // Tm,),
        in_specs=[
            pl.BlockSpec((Tm, D), lambda i, te, nv: (jnp.minimum(i, jnp.maximum(nv[0] - 1, 0)), 0)),
            pl.BlockSpec((1, D, 2 * D_EXPERT), lambda i, te, nv: (te[i], 0, 0)),
            pl.BlockSpec((1, D_EXPERT, D), lambda i, te, nv: (te[i], 0, 0)),
        ],
        out_specs=pl.BlockSpec((Tm, D), lambda i, te, nv: (i, 0)),
        scratch_shapes=[pltpu.VMEM((D, 2 * D_EXPERT), BF16), pltpu.VMEM((D_EXPERT, D), BF16)],
    )
    return pl.pallas_call(
        _experts_kernel,
        grid_spec=grid_spec,
        out_shape=jax.ShapeDtypeStruct((n_rows, D), F32),
        compiler_params=_params("arbitrary"),
        name="moe_experts",
    )(tile_expert, n_valid, xs, w_gu, w_dn)


def _combine_kernel(pos0_ref, pos1_ref, y_hbm, h_ref, cw_ref, g_ref, b_ref, o_ref, ybuf, sem):
    G = ROUTE_TOKENS

    def issue(k, carry):
        t0 = pl.multiple_of(k * F32_SUBLANES, F32_SUBLANES)
        for u in range(F32_SUBLANES):
            for slot, pos_ref in enumerate((pos0_ref, pos1_ref)):
                pltpu.make_async_copy(y_hbm.at[pl.ds(pos_ref[t0 + u], 1)], ybuf.at[slot, pl.ds(t0 + u, 1)],
                                      sem).start()
        return carry

    lax.fori_loop(0, G // F32_SUBLANES, issue, 0)
    for slot in range(2):
        pltpu.make_async_copy(y_hbm.at[pl.ds(0, G)], ybuf.at[slot], sem).wait()

    for r in range(G // COMBINE_CHUNK):
        rows = slice(r * COMBINE_CHUNK, (r + 1) * COMBINE_CHUNK)
        cw = cw_ref[rows, :]
        ffn = ybuf[0, rows, :] * cw[:, 0:1] + ybuf[1, rows, :] * cw[:, 1:2]
        o_ref[rows, :] = _layer_norm_rows(ALPHA * h_ref[rows, :] + ffn, g_ref[...], b_ref[...])


def _combine(y, h, pos0, pos1, cw_rows, ln_g, ln_b):
    T, D = h.shape
    G = ROUTE_TOKENS
    smem_blk = pl.BlockSpec((G,), lambda i: (i,), memory_space=pltpu.SMEM)
    return pl.pallas_call(
        _combine_kernel,
        grid=(T // G,),
        in_specs=[
            smem_blk, smem_blk,
            pl.BlockSpec(memory_space=pl.ANY),
            pl.BlockSpec((G, D), lambda i: (i, 0)),
            pl.BlockSpec((G, 2), lambda i: (i, 0)),
            pl.BlockSpec((1, D), lambda i: (0, 0)),
            pl.BlockSpec((1, D), lambda i: (0, 0)),
        ],
        out_specs=pl.BlockSpec((G, D), lambda i: (i, 0)),
        scratch_shapes=[pltpu.VMEM((2, G, D), F32), pltpu.SemaphoreType.DMA(())],
        out_shape=jax.ShapeDtypeStruct((T, D), F32),
        compiler_params=_params("arbitrary"),
        name="moe_combine_ln",
    )(pos0, pos1, y, h, cw_rows, ln_g.reshape(1, D), ln_b.reshape(1, D))


def kernel(x, w_in, attn_out_gain, rel_bias_table, w_out, ln1_g, ln1_b, w_router_group,
           b_router_group, w_router_expert, b_router_expert, w_gate_up, w_down, ln2_g, ln2_b):
    B, S, D = x.shape
    assert D == D_MODEL and S % MOBA_BLOCK == 0 and S % INPROJ_TOKENS == 0
    assert w_in.shape[0] == DEPTH
    T = B * S
    h = x
    for l in range(DEPTH):
        proj_t, k_nat = _inproj(h, w_in[l])
        attn_t = _moba(proj_t, k_nat, rel_bias_table)
        ret_t = _retention(proj_t)
        h1, eid, rank, cw, counts = _outproj(
            attn_t, ret_t, h, attn_out_gain[l], w_out[l], ln1_g[l], ln1_b[l],
            w_router_group[l], b_router_group[l], w_router_expert[l], b_router_expert[l])

        Tm = MOE_ROWS
        n_tiles = (2 * T) // Tm + N_EXPERTS
        padded = ((counts[:, 0] + Tm - 1) // Tm) * Tm
        ends = jnp.cumsum(padded)
        starts = (ends - padded).astype(jnp.int32)
        tile_row0 = jnp.arange(n_tiles, dtype=jnp.int32) * Tm
        tile_expert = jnp.sum((ends[None, :] <= jnp.minimum(tile_row0, ends[-1] - Tm)[:, None]).astype(jnp.int32),
                              axis=1)
        n_valid = (ends[-1:] // Tm).astype(jnp.int32)

        expert_ids = jnp.arange(N_EXPERTS, dtype=jnp.int32)[:, None, None]
        pos = jnp.sum(jnp.where(eid[None] == expert_ids, starts[:, None, None], 0), axis=0) + rank

        xs = _dispatch(h1, pos[0], pos[1], starts, ends.astype(jnp.int32), n_tiles * Tm)
        y = _experts(xs, w_gate_up[l], w_down[l], tile_expert, n_valid)
        h2 = _combine(y, h1, pos[0], pos[1], cw.T, ln2_g[l], ln2_b[l])
        h = h2.reshape(B, S, D)
    return h
```

```python
import functools
import math

import numpy as np
import jax
import jax.numpy as jnp
from jax import lax
from jax.experimental import pallas as pl
from jax.experimental.pallas import tpu as pltpu

F32 = jnp.float32
BF16 = jnp.bfloat16

D_MODEL = 1024
ATTN_HEADS = 8
ATTN_HEAD_DIM = 64
ATTN_WIDTH = ATTN_HEADS * ATTN_HEAD_DIM
RET_HEADS = 4
RET_HEAD_DIM = 128
RET_WIDTH = RET_HEADS * RET_HEAD_DIM
N_IN_SLICES = 7
PROJT_SLICES = N_IN_SLICES - 1
HEAD_PAIR = 2 * ATTN_HEAD_DIM
BF16_SUBLANES = 16
F32_SUBLANES = 8
V_ROWS = ATTN_HEAD_DIM + BF16_SUBLANES
PROJT_V_ROW0 = (PROJT_SLICES - 1) * ATTN_WIDTH
PROJT_WIDTH = PROJT_V_ROW0 + ATTN_HEADS * V_ROWS
MOBA_BLOCK = 256
MOBA_TOPK = 3
RET_CHUNK = 256
REL_BUCKETS = 32
REL_MAX_DIST = 128
ROPE_BASE = 10000.0
N_GROUPS = 4
EXPERTS_PER_GROUP = 8
N_EXPERTS = N_GROUPS * EXPERTS_PER_GROUP
D_EXPERT = D_MODEL // 2
DEPTH = 1
ALPHA = (2.0 * DEPTH) ** 0.25
LN_EPS = 1e-5

LOG2E = math.log2(math.e)
Q_SCALE = ATTN_HEAD_DIM ** -0.5 * LOG2E

MASKED = -1e30

INPROJ_TOKENS = 512
INPROJ_ROWS = ATTN_WIDTH
OUTPROJ_TOKENS = 512
MOBA_UNROLL = 2
MOBA_HEADS = 4
MOBA_QTILES = 4
MOE_ROWS = 512
ROUTE_TOKENS = 1024
COMBINE_CHUNK = 256
ROUTER_EXPERT_ROW0 = 8
ROUTER_ROWS = 48

VMEM_LIMIT = 48 * 1024 * 1024
MOBA_VMEM_LIMIT = 58 * 1024 * 1024

_TN = (((0,), (0,)), ((), ()))
_NT = (((1,), (1,)), ((), ()))


def _params(*sem):
    return pltpu.CompilerParams(dimension_semantics=sem, vmem_limit_bytes=VMEM_LIMIT)


def _inproj_kernel(x_ref, wt_ref, wk_ref, o_ref, k_ref):
    xb = x_ref[0].astype(BF16)
    n_tok = xb.shape[0]
    ones_row = (lax.broadcasted_iota(jnp.int32, (BF16_SUBLANES, n_tok), 0) == 0).astype(BF16)
    for c in range(PROJT_SLICES):
        rows = slice(c * INPROJ_ROWS, (c + 1) * INPROJ_ROWS)
        acc = lax.dot_general(wt_ref[rows, :], xb, _NT, preferred_element_type=F32)
        if c == 0:
            acc = acc * Q_SCALE
        if c < PROJT_SLICES - 1:
            o_ref[0, rows, :] = acc.astype(BF16)
        else:
            for h in range(ATTN_HEADS):
                r0 = PROJT_V_ROW0 + h * V_ROWS
                o_ref[0, r0:r0 + ATTN_HEAD_DIM, :] = acc[h * ATTN_HEAD_DIM:(h + 1) * ATTN_HEAD_DIM].astype(BF16)
                o_ref[0, r0 + ATTN_HEAD_DIM:r0 + V_ROWS, :] = ones_row
    k_ref[0] = jnp.dot(xb, wk_ref[...], preferred_element_type=F32).astype(BF16)


def _inproj(x, w_in):
    B, S, D = x.shape
    W = ATTN_WIDTH
    w_t = jnp.concatenate([w_in[:, :W], w_in[:, 3 * W:], w_in[:, 2 * W:3 * W]], axis=1).T.astype(BF16)
    w_k = w_in[:, W:2 * W].astype(BF16)
    return pl.pallas_call(
        _inproj_kernel,
        grid=(B, S // INPROJ_TOKENS),
        in_specs=[
            pl.BlockSpec((1, INPROJ_TOKENS, D), lambda b, i: (b, i, 0)),
            pl.BlockSpec((PROJT_SLICES * W, D), lambda b, i: (0, 0)),
            pl.BlockSpec((D, W), lambda b, i: (0, 0)),
        ],
        out_specs=[
            pl.BlockSpec((1, PROJT_WIDTH, INPROJ_TOKENS), lambda b, i: (b, 0, i)),
            pl.BlockSpec((1, INPROJ_TOKENS, W), lambda b, i: (b, i, 0)),
        ],
        out_shape=[
            jax.ShapeDtypeStruct((B, PROJT_WIDTH, S), BF16),
            jax.ShapeDtypeStruct((B, S, W), BF16),
        ],
        compiler_params=_params("arbitrary", "arbitrary"),
        name="inproj",
    )(x, w_t, w_k)


def _moba_kernel(far_ref, q_ref, k_ref, v_ref, bias_ref, o_ref,
                 kmean_ref, qz_ref, sel_ref, farsel_ref,
                 s0_ref, smax0_ref, p0_ref, alpha0_ref, s1_ref, smax1_ref, p1_ref, alpha1_ref,
                 m_ref, acc_ref, *, n_blocks):
    L = MOBA_BLOCK
    Dh = ATTN_HEAD_DIM
    HG = MOBA_HEADS
    group = pl.program_id(1)
    qi0 = pl.program_id(2) * MOBA_QTILES
    last_blk = n_blocks - 1
    units = [(a, h) for a in range(MOBA_QTILES) for h in range(HG)]
    s_ref, smax_ref = (s0_ref, s1_ref), (smax0_ref, smax1_ref)
    p_ref, alpha_ref = (p0_ref, p1_ref), (alpha0_ref, alpha1_ref)

    @pl.when(qi0 == 0)
    def _():
        chunk_blocks = min(8, n_blocks)
        chunk = chunk_blocks * L
        acc = jnp.zeros((n_blocks, HG * Dh), F32)
        for c in range(n_blocks // chunk_blocks):
            blk = lax.broadcasted_iota(jnp.int32, (n_blocks, chunk), 0)
            key = lax.broadcasted_iota(jnp.int32, (n_blocks, chunk), 1)
            ind = jnp.where(blk == c * chunk_blocks + key // L, 1.0 / L, 0.0).astype(BF16)
            acc = acc + jnp.dot(ind, k_ref[0, c * chunk:(c + 1) * chunk, :], preferred_element_type=F32)
        kmean_ref[...] = acc

    feat = lax.broadcasted_iota(jnp.int32, (HEAD_PAIR, L), 0)
    blk = lax.broadcasted_iota(jnp.int32, (n_blocks, L), 0)
    for u, (a, h) in enumerate(units):
        qi = qi0 + a
        pair, hd = divmod(h, 2)
        past = blk < qi
        qp = q_ref[0, pair * HEAD_PAIR:(pair + 1) * HEAD_PAIR, a * L:(a + 1) * L]
        qz = jnp.where((feat >= hd * Dh) & (feat < (hd + 1) * Dh), qp, jnp.zeros_like(qp))
        qz_ref[u] = qz
        gate = jnp.dot(kmean_ref[:, pair * HEAD_PAIR:(pair + 1) * HEAD_PAIR], qz.astype(F32),
                       precision=lax.Precision.HIGHEST, preferred_element_type=F32)
        gate = jnp.where(past, gate, -jnp.inf)
        chosen = jnp.zeros(gate.shape, jnp.bool_)
        for _ in range(MOBA_TOPK):
            top = jnp.max(gate, axis=0, keepdims=True)
            first = jnp.min(jnp.where(gate == top, blk, n_blocks), axis=0, keepdims=True)
            hit = blk == first
            chosen = jnp.logical_or(chosen, hit)
            gate = jnp.where(hit, -jnp.inf, gate)
        chosen = jnp.logical_and(chosen, past)
        sel_ref[u] = jnp.where(chosen, 0.0, MASKED)
        farsel_ref[u, 0:n_blocks, :] = jnp.where(jnp.logical_and(chosen, blk < qi - 1),
                                                 far_ref[group * HG + h], MASKED)
        farsel_ref[u, n_blocks:n_blocks + F32_SUBLANES, :] = jnp.zeros((F32_SUBLANES, L), F32)

    def stage_scores(u, j, tile, slot):
        pair = units[u][1] // 2
        kb = k_ref[0, pl.ds(pl.multiple_of(j * L, L), L), pair * HEAD_PAIR:(pair + 1) * HEAD_PAIR]
        s = jnp.dot(kb, qz_ref[u], preferred_element_type=F32)
        if tile is not None:
            s = s + tile
        s_ref[slot][u] = s
        smax_ref[slot][u] = jnp.max(s, axis=0, keepdims=True)

    def stage_softmax(u, row, slot):
        m_prev = m_ref[u]
        m_new = jnp.maximum(m_prev, smax_ref[slot][u] + row)
        alpha_ref[slot][u] = jnp.exp2(m_prev - m_new)
        p_ref[slot][u] = jnp.exp2(s_ref[slot][u] - (m_new - row)).astype(BF16)
        m_ref[u] = m_new

    def stage_values(u, j, slot):
        h = units[u][1]
        vb = v_ref[0, h * V_ROWS:(h + 1) * V_ROWS, pl.ds(pl.multiple_of(j * L, L), L)]
        pv = jnp.dot(vb, p_ref[slot][u], preferred_element_type=F32)
        acc_ref[u] = alpha_ref[slot][u] * acc_ref[u] + pv

    m_ref[...] = jnp.full(m_ref.shape, MASKED, F32)
    acc_ref[...] = jnp.zeros(acc_ref.shape, F32)

    zero_row = jnp.zeros((1, L), F32)
    own = [qi0 + a for a, _ in units]
    prev = [jnp.maximum(qi0 + a - 1, 0) for a, _ in units]
    for u, (a, h) in enumerate(units):
        stage_scores(u, own[u], bias_ref[0, h], 0)
        stage_scores(u, prev[u], bias_ref[1, h] + sel_ref[u, pl.ds(prev[u], 1), :], 1)
    for u in range(len(units)):
        stage_softmax(u, zero_row, 0)
    n_steps = 2 + jnp.maximum(qi0 + MOBA_QTILES - 2, 0)

    def step(t, slot):
        far_value = jnp.minimum(t - 2, last_blk)
        j = jnp.minimum(t, last_blk)
        jr = jnp.where(t == 0, n_blocks, jnp.minimum(t - 1, last_blk))
        for u in range(len(units)):
            stage_scores(u, j, None, slot)
            stage_softmax(u, farsel_ref[u, pl.ds(jr, 1), :], 1 - slot)
            stage_values(u, jnp.where(t == 0, own[u], jnp.where(t == 1, prev[u], far_value)), slot)

    def unrolled_steps(k, carry):
        for s in range(MOBA_UNROLL):
            step(MOBA_UNROLL * k + s, s % 2)
        return carry

    lax.fori_loop(0, (n_steps + MOBA_UNROLL - 1) // MOBA_UNROLL, unrolled_steps, 0)
    for u, (a, h) in enumerate(units):
        o_ref[0, h * Dh:(h + 1) * Dh, a * L:(a + 1) * L] = acc_ref[u, 0:Dh, :] / acc_ref[u, Dh:Dh + 1, :]


def _t5_bucket_np(dist):
    n = np.maximum(dist, 0)
    max_exact = REL_BUCKETS // 2
    nf = np.maximum(n, 1).astype(np.float32)
    large = max_exact + (np.log(nf / max_exact) / math.log(REL_MAX_DIST / max_exact)
                         * (REL_BUCKETS - max_exact)).astype(np.int32)
    large = np.minimum(large, REL_BUCKETS - 1)
    return np.where(n < max_exact, n, large)


def _moba(proj_t, k_nat, rel_table):
    B, _, S = proj_t.shape
    L = MOBA_BLOCK
    n_blocks = S // L
    table_t = rel_table.astype(F32).T
    key = np.arange(L)[:, None]
    qry = np.arange(L)[None, :]
    d_own = qry - key
    bucket = np.stack([_t5_bucket_np(d_own), _t5_bucket_np(d_own + L)])
    onehot = jnp.asarray(bucket)[:, None, None] == jnp.arange(REL_BUCKETS)[None, None, :, None, None]
    bias = jnp.sum(jnp.where(onehot, table_t[None, :, :, None, None], 0.0), axis=2)
    causal = jnp.asarray(np.stack([d_own >= 0, np.ones_like(d_own, bool)]))[:, None]
    bias = jnp.where(causal, bias * LOG2E, MASKED)
    assert int(_t5_bucket_np(np.array(L + 1))) == REL_BUCKETS - 1
    far = table_t[:, REL_BUCKETS - 1] * LOG2E

    HG, QT = MOBA_HEADS, MOBA_QTILES
    U = HG * QT
    q_rows, v_rows = HG * ATTN_HEAD_DIM, HG * V_ROWS
    assert ATTN_HEADS % HG == 0 and HG % 2 == 0 and n_blocks % QT == 0 and PROJT_V_ROW0 % v_rows == 0
    once = dict(pipeline_mode=pl.Buffered(1))
    grid_spec = pltpu.PrefetchScalarGridSpec(
        num_scalar_prefetch=1,
        grid=(B, ATTN_HEADS // HG, n_blocks // QT),
        in_specs=[
            pl.BlockSpec((1, q_rows, QT * L), lambda b, g, i, far: (b, g, i)),
            pl.BlockSpec((1, S, q_rows), lambda b, g, i, far: (b, 0, g), **once),
            pl.BlockSpec((1, v_rows, S), lambda b, g, i, far: (b, PROJT_V_ROW0 // v_rows + g, 0), **once),
            pl.BlockSpec((2, HG, L, L), lambda b, g, i, far: (0, g, 0, 0), **once),
        ],
        out_specs=pl.BlockSpec((1, q_rows, QT * L), lambda b, g, i, far: (b, g, i)),
        scratch_shapes=[
            pltpu.VMEM((n_blocks, q_rows), F32),
            pltpu.VMEM((U, HEAD_PAIR, L), BF16),
            pltpu.VMEM((U, n_blocks, L), F32),
            pltpu.VMEM((U, n_blocks + F32_SUBLANES, L), F32),
            *([pltpu.VMEM((U, L, L), F32), pltpu.VMEM((U, 1, L), F32),
               pltpu.VMEM((U, L, L), BF16), pltpu.VMEM((U, 1, L), F32)] * 2),
            pltpu.VMEM((U, 1, L), F32),
            pltpu.VMEM((U, V_ROWS, L), F32),
        ],
    )
    return pl.pallas_call(
        functools.partial(_moba_kernel, n_blocks=n_blocks),
        grid_spec=grid_spec,
        out_shape=jax.ShapeDtypeStruct((B, ATTN_WIDTH, S), F32),
        compiler_params=pltpu.CompilerParams(dimension_semantics=("arbitrary",) * 3,
                                             vmem_limit_bytes=MOBA_VMEM_LIMIT),
        name="moba",
    )(far, proj_t, k_nat, proj_t, bias)


def _retention_kernel(cdec_ref, q_ref, k_ref, v_ref, g_ref, cos_ref, sin_ref,
                      dec_ref, qdec_ref, kdec_ref, o_ref, state_ref):
    c = pl.program_id(1)

    @pl.when(c == 0)
    def _():
        state_ref[...] = jnp.zeros(state_ref.shape, F32)

    cos = cos_ref[...]
    sin = sin_ref[...]
    half = RET_HEAD_DIM // 2

    def rope(t):
        t1, t2 = t[:half], t[half:]
        return jnp.concatenate([t1 * cos - t2 * sin, t1 * sin + t2 * cos], axis=0)

    for hh in range(RET_HEADS):
        rows = slice(hh * RET_HEAD_DIM, (hh + 1) * RET_HEAD_DIM)
        q = rope(q_ref[0, rows, :].astype(F32))
        k = rope(k_ref[0, rows, :].astype(F32)) * (RET_HEAD_DIM ** -0.5)
        v = v_ref[0, rows, :]
        qb = q.astype(BF16)
        s = lax.dot_general(k.astype(BF16), qb, _TN, preferred_element_type=F32)
        s = (s * dec_ref[hh]).astype(BF16)
        inner = jnp.dot(v, s, preferred_element_type=F32)
        state = state_ref[hh]
        cross = jnp.dot(state.astype(BF16), (q * qdec_ref[hh:hh + 1, :]).astype(BF16),
                        preferred_element_type=F32)
        kd = (k * kdec_ref[hh:hh + 1, :]).astype(BF16)
        state_ref[hh] = state * cdec_ref[hh] + lax.dot_general(v, kd, _NT, preferred_element_type=F32)
        out = inner + cross
        mu = jnp.mean(out, axis=0, keepdims=True)
        var = jnp.mean(jnp.square(out - mu), axis=0, keepdims=True)
        out = (out - mu) * lax.rsqrt(var + LN_EPS)
        g = g_ref[0, rows, :].astype(F32)
        o_ref[0, rows, :] = (g * jax.nn.sigmoid(g) * out).astype(BF16)


def _retention(proj_t):
    B, _, S = proj_t.shape
    C = RET_CHUNK
    H = RET_HEADS
    half = RET_HEAD_DIM // 2
    inv = ROPE_BASE ** (-jnp.arange(half, dtype=F32) / half)
    ang = inv[:, None] * jnp.arange(S).astype(F32)[None, :]
    cos_t, sin_t = jnp.cos(ang), jnp.sin(ang)
    gammas = 1.0 - jnp.exp(jnp.linspace(math.log(1.0 / 32), math.log(1.0 / 512), H, dtype=F32))
    log_g = jnp.log(gammas)
    idx = jnp.arange(C, dtype=F32)
    diff = idx[None, :] - idx[:, None]
    dec_t = jnp.where(diff[None] >= 0, jnp.exp(jnp.maximum(diff, 0.0)[None] * log_g[:, None, None]), 0.0)
    q_dec = jnp.exp((idx[None, :] + 1.0) * log_g[:, None])
    k_dec = jnp.exp((C - 1.0 - idx[None, :]) * log_g[:, None])
    chunk_dec = jnp.exp(C * log_g)

    grid_spec = pltpu.PrefetchScalarGridSpec(
        num_scalar_prefetch=1,
        grid=(B, S // C),
        in_specs=[
            pl.BlockSpec((1, RET_WIDTH, C), lambda b, c, cd: (b, 1, c)),
            pl.BlockSpec((1, RET_WIDTH, C), lambda b, c, cd: (b, 2, c)),
            pl.BlockSpec((1, RET_WIDTH, C), lambda b, c, cd: (b, 3, c)),
            pl.BlockSpec((1, RET_WIDTH, C), lambda b, c, cd: (b, 4, c)),
            pl.BlockSpec((half, C), lambda b, c, cd: (0, c)),
            pl.BlockSpec((half, C), lambda b, c, cd: (0, c)),
            pl.BlockSpec((H, C, C), lambda b, c, cd: (0, 0, 0)),
            pl.BlockSpec((H, C), lambda b, c, cd: (0, 0)),
            pl.BlockSpec((H, C), lambda b, c, cd: (0, 0)),
        ],
        out_specs=pl.BlockSpec((1, RET_WIDTH, C), lambda b, c, cd: (b, 0, c)),
        scratch_shapes=[pltpu.VMEM((H, RET_HEAD_DIM, RET_HEAD_DIM), F32)],
    )
    return pl.pallas_call(
        _retention_kernel,
        grid_spec=grid_spec,
        out_shape=jax.ShapeDtypeStruct((B, RET_WIDTH, S), BF16),
        compiler_params=_params("arbitrary", "arbitrary"),
        name="retention",
    )(chunk_dec, proj_t, proj_t, proj_t, proj_t, cos_t, sin_t, dec_t, q_dec, k_dec)


def _layer_norm_rows(y, g, b):
    mu = jnp.mean(y, axis=-1, keepdims=True)
    var = jnp.mean(jnp.square(y - mu), axis=-1, keepdims=True)
    return (y - mu) * lax.rsqrt(var + LN_EPS) * g + b


def _outproj_kernel(attn_ref, ret_ref, x_ref, gain_ref, wa_ref, wr_ref, g1_ref, b1_ref,
                    wrt_ref, brt_ref, before_ref,
                    h_ref, eid_ref, rank_ref, cw_ref, cnt_ref, carry_ref):
    first = jnp.logical_and(pl.program_id(0) == 0, pl.program_id(1) == 0)

    @pl.when(first)
    def _():
        carry_ref[...] = jnp.zeros(carry_ref.shape, F32)

    a = attn_ref[0]
    a = a * lax.rsqrt(jnp.mean(jnp.square(a), axis=0, keepdims=True) + LN_EPS) * gain_ref[...]
    mix = (lax.dot_general(a.astype(BF16), wa_ref[...], _TN, preferred_element_type=F32)
           + lax.dot_general(ret_ref[0], wr_ref[...], _TN, preferred_element_type=F32))
    h = _layer_norm_rows(ALPHA * x_ref[0] + mix, g1_ref[...], b1_ref[...])
    h_ref[...] = h

    R = ROUTER_ROWS
    h_hi = h.astype(BF16)
    h_lo = (h - h_hi.astype(F32)).astype(BF16)
    by_hi = lax.dot_general(wrt_ref[...], h_hi, _NT, preferred_element_type=F32)
    by_lo = lax.dot_general(wrt_ref[0:R, :], h_lo, _NT, preferred_element_type=F32)
    logits = by_hi[0:R] + by_hi[R:2 * R] + by_lo + brt_ref[...]
    T = logits.shape[1]
    gl = logits[0:N_GROUPS]
    gmax = jnp.max(gl, axis=0, keepdims=True)
    grow = lax.broadcasted_iota(jnp.int32, gl.shape, 0)
    gidx = jnp.min(jnp.where(gl == gmax, grow, N_GROUPS), axis=0, keepdims=True)
    g_p = 1.0 / jnp.sum(jnp.exp(gl - gmax), axis=0, keepdims=True)
    el = jnp.zeros((EXPERTS_PER_GROUP, T), F32)
    for g in range(N_GROUPS):
        r0 = ROUTER_EXPERT_ROW0 + g * EXPERTS_PER_GROUP
        el = jnp.where(gidx == g, logits[r0:r0 + EXPERTS_PER_GROUP], el)
    erow = lax.broadcasted_iota(jnp.int32, el.shape, 0)
    e1 = jnp.max(el, axis=0, keepdims=True)
    i1 = jnp.min(jnp.where(el == e1, erow, EXPERTS_PER_GROUP), axis=0, keepdims=True)
    el2 = jnp.where(erow == i1, -jnp.inf, el)
    e2 = jnp.max(el2, axis=0, keepdims=True)
    i2 = jnp.min(jnp.where(el2 == e2, erow, EXPERTS_PER_GROUP), axis=0, keepdims=True)
    r = jnp.exp(e2 - e1)
    w1 = g_p / (1.0 + r)
    w2 = g_p * r / (1.0 + r)
    id1 = gidx * EXPERTS_PER_GROUP + i1
    id2 = gidx * EXPERTS_PER_GROUP + i2
    eid_ref[0:1, :] = id1
    eid_ref[1:2, :] = id2
    cw_ref[0:1, :] = w1
    cw_ref[1:2, :] = w2

    xrow = lax.broadcasted_iota(jnp.int32, (N_EXPERTS, T), 0)
    oh1 = (xrow == id1).astype(F32)
    oh2 = (xrow == id2).astype(F32)
    both = oh1 + oh2
    seen = jnp.dot(both.astype(BF16), before_ref[...], preferred_element_type=F32) + carry_ref[...]
    rank_ref[0:1, :] = jnp.sum(oh1 * seen, axis=0, keepdims=True).astype(jnp.int32)
    rank_ref[1:2, :] = jnp.sum(oh2 * seen, axis=0, keepdims=True).astype(jnp.int32)
    carry = carry_ref[...] + jnp.sum(both, axis=1, keepdims=True)
    carry_ref[...] = carry
    cnt_ref[...] = carry.astype(jnp.int32)


def _outproj(attn_t, ret_t, x, gain, w_out, ln_g, ln_b, w_rg, b_rg, w_re, b_re):
    B, S, D = x.shape
    Tt = OUTPROJ_TOKENS
    n_t = S // Tt
    T = B * S
    wa = w_out[:ATTN_WIDTH].astype(BF16)
    wr = w_out[ATTN_WIDTH:].astype(BF16)
    e0, e1 = ROUTER_EXPERT_ROW0, ROUTER_EXPERT_ROW0 + N_EXPERTS
    wrt = jnp.zeros((ROUTER_ROWS, D), F32)
    wrt = wrt.at[:N_GROUPS].set(w_rg.T).at[e0:e1].set(w_re.T)
    wrt_hi = wrt.astype(BF16)
    wrt_lo = (wrt - wrt_hi.astype(F32)).astype(BF16)
    wrt = jnp.concatenate([wrt_hi, wrt_lo], axis=0)
    brt = jnp.zeros((ROUTER_ROWS, 1), F32)
    brt = brt.at[:N_GROUPS, 0].set(b_rg.astype(F32)).at[e0:e1, 0].set(b_re.astype(F32))
    before = (jnp.arange(Tt)[:, None] < jnp.arange(Tt)[None, :]).astype(BF16)
    const = lambda b, i: (0, 0)
    tok = lambda b, i: (0, b * n_t + i)
    return pl.pallas_call(
        _outproj_kernel,
        grid=(B, n_t),
        in_specs=[
            pl.BlockSpec((1, ATTN_WIDTH, Tt), lambda b, i: (b, 0, i)),
            pl.BlockSpec((1, RET_WIDTH, Tt), lambda b, i: (b, 0, i)),
            pl.BlockSpec((1, Tt, D), lambda b, i: (b, i, 0)),
            pl.BlockSpec((ATTN_WIDTH, 1), const),
            pl.BlockSpec((ATTN_WIDTH, D), const),
            pl.BlockSpec((RET_WIDTH, D), const),
            pl.BlockSpec((1, D), const),
            pl.BlockSpec((1, D), const),
            pl.BlockSpec((2 * ROUTER_ROWS, D), const),
            pl.BlockSpec((ROUTER_ROWS, 1), const),
            pl.BlockSpec((Tt, Tt), const),
        ],
        out_specs=[
            pl.BlockSpec((Tt, D), lambda b, i: (b * n_t + i, 0)),
            pl.BlockSpec((2, Tt), tok),
            pl.BlockSpec((2, Tt), tok),
            pl.BlockSpec((2, Tt), tok),
            pl.BlockSpec((N_EXPERTS, 1), const),
        ],
        out_shape=[
            jax.ShapeDtypeStruct((T, D), F32),
            jax.ShapeDtypeStruct((2, T), jnp.int32),
            jax.ShapeDtypeStruct((2, T), jnp.int32),
            jax.ShapeDtypeStruct((2, T), F32),
            jax.ShapeDtypeStruct((N_EXPERTS, 1), jnp.int32),
        ],
        scratch_shapes=[pltpu.VMEM((N_EXPERTS, 1), F32)],
        compiler_params=_params("arbitrary", "arbitrary"),
        name="outproj_ln_router",
    )(attn_t, ret_t, x, gain.reshape(ATTN_WIDTH, 1), wa, wr, ln_g.reshape(1, D), ln_b.reshape(1, D),
      wrt, brt, before)


def _dispatch_kernel(start_ref, end_ref, pos0_ref, pos1_ref, h_ref, xs_hbm, zero_ref, sem, zsem):
    G = ROUTE_TOKENS
    Tm = MOE_ROWS

    @pl.when(pl.program_id(0) == 0)
    def _():
        zero_ref[...] = jnp.zeros(zero_ref.shape, F32)

        def tile_clear(row0):
            row0 = pl.multiple_of(row0, Tm)
            return pltpu.make_async_copy(zero_ref, xs_hbm.at[pl.ds(row0, Tm)], zsem)

        used = end_ref[N_EXPERTS - 1]
        n_rows = xs_hbm.shape[0]
        clears = [(end_ref[e] - Tm, end_ref[e] > start_ref[e]) for e in range(N_EXPERTS)]
        clears += [(jnp.minimum(used + k * Tm, n_rows - Tm), used + k * Tm < n_rows) for k in range(N_EXPERTS)]
        for row0, cond in clears:
            @pl.when(cond)
            def _():
                tile_clear(row0).start()
        for row0, cond in clears:
            @pl.when(cond)
            def _():
                tile_clear(row0).wait()

    def issue(k, carry):
        t0 = pl.multiple_of(k * F32_SUBLANES, F32_SUBLANES)
        for u in range(F32_SUBLANES):
            for pos_ref in (pos0_ref, pos1_ref):
                pltpu.make_async_copy(h_ref.at[pl.ds(t0 + u, 1)], xs_hbm.at[pl.ds(pos_ref[t0 + u], 1)], sem).start()
        return carry

    lax.fori_loop(0, G // F32_SUBLANES, issue, 0)
    for _ in range(2):
        pltpu.make_async_copy(h_ref, xs_hbm.at[pl.ds(0, G)], sem).wait()


def _dispatch(h, pos0, pos1, starts, ends, n_rows):
    T, D = h.shape
    G = ROUTE_TOKENS
    smem_blk = pl.BlockSpec((G,), lambda i, st, en: (i,), memory_space=pltpu.SMEM)
    grid_spec = pltpu.PrefetchScalarGridSpec(
        num_scalar_prefetch=2,
        grid=(T // G,),
        in_specs=[smem_blk, smem_blk, pl.BlockSpec((G, D), lambda i, st, en: (i, 0))],
        out_specs=pl.BlockSpec(memory_space=pl.ANY),
        scratch_shapes=[pltpu.VMEM((MOE_ROWS, D), F32), pltpu.SemaphoreType.DMA(()),
                        pltpu.SemaphoreType.DMA(())],
    )
    return pl.pallas_call(
        _dispatch_kernel,
        grid_spec=grid_spec,
        out_shape=jax.ShapeDtypeStruct((n_rows, D), F32),
        compiler_params=_params("arbitrary"),
        name="moe_dispatch",
    )(starts, ends, pos0, pos1, h)


def _experts_kernel(te_ref, nv_ref, xs_ref, wgu_ref, wdn_ref, y_ref, wgu_b, wdn_b):
    i = pl.program_id(0)
    live = i < nv_ref[0]
    new_expert = jnp.logical_or(i == 0, te_ref[i] != te_ref[jnp.maximum(i - 1, 0)])

    @pl.when(jnp.logical_and(live, new_expert))
    def _():
        wgu_b[...] = wgu_ref[0].astype(BF16)
        wdn_b[...] = wdn_ref[0].astype(BF16)

    @pl.when(live)
    def _():
        gu = jnp.dot(xs_ref[...].astype(BF16), wgu_b[...], preferred_element_type=F32)
        gate, up = gu[:, :D_EXPERT], gu[:, D_EXPERT:]
        hid = gate * jax.nn.sigmoid(gate) * up
        y_ref[...] = jnp.dot(hid.astype(BF16), wdn_b[...], preferred_element_type=F32)

    @pl.when(jnp.logical_not(live))
    def _():
        y_ref[...] = jnp.zeros(y_ref.shape, F32)


def _experts(xs, w_gu, w_dn, tile_expert, n_valid):
    n_rows, D = xs.shape
    Tm = MOE_ROWS
    grid_spec = pltpu.PrefetchScalarGridSpec(
        num_scalar_prefetch=2,
        grid=(n_rows // Tm,),
        in_specs=[
            pl.BlockSpec((Tm, D), lambda i, te, nv: (jnp.minimum(i, jnp.maximum(nv[0] - 1, 0)), 0)),
            pl.BlockSpec((1, D, 2 * D_EXPERT), lambda i, te, nv: (te[i], 0, 0)),
            pl.BlockSpec((1, D_EXPERT, D), lambda i, te, nv: (te[i], 0, 0)),
        ],
        out_specs=pl.BlockSpec((Tm, D), lambda i, te, nv: (i, 0)),
        scratch_shapes=[pltpu.VMEM((D, 2 * D_EXPERT), BF16), pltpu.VMEM((D_EXPERT, D), BF16)],
    )
    return pl.pallas_call(
        _experts_kernel,
        grid_spec=grid_spec,
        out_shape=jax.ShapeDtypeStruct((n_rows, D), F32),
        compiler_params=_params("arbitrary"),
        name="moe_experts",
    )(tile_expert, n_valid, xs, w_gu, w_dn)


def _combine_kernel(pos0_ref, pos1_ref, y_hbm, h_ref, cw_ref, g_ref, b_ref, o_ref, ybuf, sem):
    G = ROUTE_TOKENS

    def issue(k, carry):
        t0 = pl.multiple_of(k * F32_SUBLANES, F32_SUBLANES)
        for u in range(F32_SUBLANES):
            for slot, pos_ref in enumerate((pos0_ref, pos1_ref)):
                pltpu.make_async_copy(y_hbm.at[pl.ds(pos_ref[t0 + u], 1)], ybuf.at[slot, pl.ds(t0 + u, 1)],
                                      sem).start()
        return carry

    lax.fori_loop(0, G // F32_SUBLANES, issue, 0)
    for slot in range(2):
        pltpu.make_async_copy(y_hbm.at[pl.ds(0, G)], ybuf.at[slot], sem).wait()

    for r in range(G // COMBINE_CHUNK):
        rows = slice(r * COMBINE_CHUNK, (r + 1) * COMBINE_CHUNK)
        cw = cw_ref[rows, :]
        ffn = ybuf[0, rows, :] * cw[:, 0:1] + ybuf[1, rows, :] * cw[:, 1:2]
        o_ref[rows, :] = _layer_norm_rows(ALPHA * h_ref[rows, :] + ffn, g_ref[...], b_ref[...])


def _combine(y, h, pos0, pos1, cw_rows, ln_g, ln_b):
    T, D = h.shape
    G = ROUTE_TOKENS
    smem_blk = pl.BlockSpec((G,), lambda i: (i,), memory_space=pltpu.SMEM)
    return pl.pallas_call(
        _combine_kernel,
        grid=(T // G,),
        in_specs=[
            smem_blk, smem_blk,
            pl.BlockSpec(memory_space=pl.ANY),
            pl.BlockSpec((G, D), lambda i: (i, 0)),
            pl.BlockSpec((G, 2), lambda i: (i, 0)),
            pl.BlockSpec((1, D), lambda i: (0, 0)),
            pl.BlockSpec((1, D), lambda i: (0, 0)),
        ],
        out_specs=pl.BlockSpec((G, D), lambda i: (i, 0)),
        scratch_shapes=[pltpu.VMEM((2, G, D), F32), pltpu.SemaphoreType.DMA(())],
        out_shape=jax.ShapeDtypeStruct((T, D), F32),
        compiler_params=_params("arbitrary"),
        name="moe_combine_ln",
    )(pos0, pos1, y, h, cw_rows, ln_g.reshape(1, D), ln_b.reshape(1, D))


def kernel(x, w_in, attn_out_gain, rel_bias_table, w_out, ln1_g, ln1_b, w_router_group,
           b_router_group, w_router_expert, b_router_expert, w_gate_up, w_down, ln2_g, ln2_b):
    B, S, D = x.shape
    assert D == D_MODEL and S % MOBA_BLOCK == 0 and S % INPROJ_TOKENS == 0
    assert w_in.shape[0] == DEPTH
    T = B * S
    h = x
    for l in range(DEPTH):
        proj_t, k_nat = _inproj(h, w_in[l])
        attn_t = _moba(proj_t, k_nat, rel_bias_table)
        ret_t = _retention(proj_t)
        h1, eid, rank, cw, counts = _outproj(
            attn_t, ret_t, h, attn_out_gain[l], w_out[l], ln1_g[l], ln1_b[l],
            w_router_group[l], b_router_group[l], w_router_expert[l], b_router_expert[l])

        Tm = MOE_ROWS
        n_tiles = (2 * T) // Tm + N_EXPERTS
        padded = ((counts[:, 0] + Tm - 1) // Tm) * Tm
        ends = jnp.cumsum(padded)
        starts = (ends - padded).astype(jnp.int32)
        tile_row0 = jnp.arange(n_tiles, dtype=jnp.int32) * Tm
        tile_expert = jnp.sum((ends[None, :] <= jnp.minimum(tile_row0, ends[-1] - Tm)[:, None]).astype(jnp.int32),
                              axis=1)
        n_valid = (ends[-1:] // Tm).astype(jnp.int32)

        expert_ids = jnp.arange(N_EXPERTS, dtype=jnp.int32)[:, None, None]
        pos = jnp.sum(jnp.where(eid[None] == expert_ids, starts[:, None, None], 0), axis=0) + rank

        xs = _dispatch(h1, pos[0], pos[1], starts, ends.astype(jnp.int32), n_tiles * Tm)
        y = _experts(xs, w_gate_up[l], w_down[l], tile_expert, n_valid)
        h2 = _combine(y, h1, pos[0], pos[1], cw.T, ln2_g[l], ln2_b[l])
        h = h2.reshape(B, S, D)
    return h
```

```python
import functools
import math

import numpy as np
import jax
import jax.numpy as jnp
from jax import lax
from jax.experimental import pallas as pl
from jax.experimental.pallas import tpu as pltpu

F32 = jnp.float32
BF16 = jnp.bfloat16

D_MODEL = 1024
ATTN_HEADS = 8
ATTN_HEAD_DIM = 64
ATTN_WIDTH = ATTN_HEADS * ATTN_HEAD_DIM
RET_HEADS = 4
RET_HEAD_DIM = 128
RET_WIDTH = RET_HEADS * RET_HEAD_DIM
N_IN_SLICES = 7
PROJT_SLICES = N_IN_SLICES - 1
HEAD_PAIR = 2 * ATTN_HEAD_DIM
BF16_SUBLANES = 16
F32_SUBLANES = 8
V_ROWS = ATTN_HEAD_DIM + BF16_SUBLANES
PROJT_V_ROW0 = (PROJT_SLICES - 1) * ATTN_WIDTH
PROJT_WIDTH = PROJT_V_ROW0 + ATTN_HEADS * V_ROWS
MOBA_BLOCK = 256
MOBA_TOPK = 3
RET_CHUNK = 256
REL_BUCKETS = 32
REL_MAX_DIST = 128
ROPE_BASE = 10000.0
N_GROUPS = 4
EXPERTS_PER_GROUP = 8
N_EXPERTS = N_GROUPS * EXPERTS_PER_GROUP
D_EXPERT = D_MODEL // 2
DEPTH = 1
ALPHA = (2.0 * DEPTH) ** 0.25
LN_EPS = 1e-5

LOG2E = math.log2(math.e)
Q_SCALE = ATTN_HEAD_DIM ** -0.5 * LOG2E

MASKED = -1e30

INPROJ_TOKENS = 512
INPROJ_ROWS = ATTN_WIDTH
OUTPROJ_TOKENS = 512
MOBA_UNROLL = 2
MOBA_HEADS = 4
MOBA_QTILES = 4
MOE_ROWS = 512
ROUTE_TOKENS = 1024
COMBINE_CHUNK = 256
ROUTER_EXPERT_ROW0 = 8
ROUTER_ROWS = 48

VMEM_LIMIT = 48 * 1024 * 1024
MOBA_VMEM_LIMIT = 58 * 1024 * 1024

_TN = (((0,), (0,)), ((), ()))
_NT = (((1,), (1,)), ((), ()))


def _params(*sem):
    return pltpu.CompilerParams(dimension_semantics=sem, vmem_limit_bytes=VMEM_LIMIT)


def _inproj_kernel(x_ref, wt_ref, wk_ref, o_ref, k_ref):
    xb = x_ref[0].astype(BF16)
    n_tok = xb.shape[0]
    ones_row = (lax.broadcasted_iota(jnp.int32, (BF16_SUBLANES, n_tok), 0) == 0).astype(BF16)
    for c in range(PROJT_SLICES):
        rows = slice(c * INPROJ_ROWS, (c + 1) * INPROJ_ROWS)
        acc = lax.dot_general(wt_ref[rows, :], xb, _NT, preferred_element_type=F32)
        if c == 0:
            acc = acc * Q_SCALE
        if c < PROJT_SLICES - 1:
            o_ref[0, rows, :] = acc.astype(BF16)
        else:
            for h in range(ATTN_HEADS):
                r0 = PROJT_V_ROW0 + h * V_ROWS
                o_ref[0, r0:r0 + ATTN_HEAD_DIM, :] = acc[h * ATTN_HEAD_DIM:(h + 1) * ATTN_HEAD_DIM].astype(BF16)
                o_ref[0, r0 + ATTN_HEAD_DIM:r0 + V_ROWS, :] = ones_row
    k_ref[0] = jnp.dot(xb, wk_ref[...], preferred_element_type=F32).astype(BF16)


def _inproj(x, w_in):
    B, S, D = x.shape
    W = ATTN_WIDTH
    w_t = jnp.concatenate([w_in[:, :W], w_in[:, 3 * W:], w_in[:, 2 * W:3 * W]], axis=1).T.astype(BF16)
    w_k = w_in[:, W:2 * W].astype(BF16)
    return pl.pallas_call(
        _inproj_kernel,
        grid=(B, S // INPROJ_TOKENS),
        in_specs=[
            pl.BlockSpec((1, INPROJ_TOKENS, D), lambda b, i: (b, i, 0)),
            pl.BlockSpec((PROJT_SLICES * W, D), lambda b, i: (0, 0)),
            pl.BlockSpec((D, W), lambda b, i: (0, 0)),
        ],
        out_specs=[
            pl.BlockSpec((1, PROJT_WIDTH, INPROJ_TOKENS), lambda b, i: (b, 0, i)),
            pl.BlockSpec((1, INPROJ_TOKENS, W), lambda b, i: (b, i, 0)),
        ],
        out_shape=[
            jax.ShapeDtypeStruct((B, PROJT_WIDTH, S), BF16),
            jax.ShapeDtypeStruct((B, S, W), BF16),
        ],
        compiler_params=_params("arbitrary", "arbitrary"),
        name="inproj",
    )(x, w_t, w_k)


def _moba_kernel(far_ref, q_ref, k_ref, v_ref, bias_ref, o_ref,
                 kmean_ref, qz_ref, sel_ref, farsel_ref,
                 s0_ref, smax0_ref, p0_ref, alpha0_ref, s1_ref, smax1_ref, p1_ref, alpha1_ref,
                 m_ref, acc_ref, *, n_blocks):
    L = MOBA_BLOCK
    Dh = ATTN_HEAD_DIM
    HG = MOBA_HEADS
    group = pl.program_id(1)
    qi0 = pl.program_id(2) * MOBA_QTILES
    last_blk = n_blocks - 1
    units = [(a, h) for a in range(MOBA_QTILES) for h in range(HG)]
    s_ref, smax_ref = (s0_ref, s1_ref), (smax0_ref, smax1_ref)
    p_ref, alpha_ref = (p0_ref, p1_ref), (alpha0_ref, alpha1_ref)

    @pl.when(qi0 == 0)
    def _():
        chunk_blocks = min(8, n_blocks)
        chunk = chunk_blocks * L
        acc = jnp.zeros((n_blocks, HG * Dh), F32)
        for c in range(n_blocks // chunk_blocks):
            blk = lax.broadcasted_iota(jnp.int32, (n_blocks, chunk), 0)
            key = lax.broadcasted_iota(jnp.int32, (n_blocks, chunk), 1)
            ind = jnp.where(blk == c * chunk_blocks + key // L, 1.0 / L, 0.0).astype(BF16)
            acc = acc + jnp.dot(ind, k_ref[0, c * chunk:(c + 1) * chunk, :], preferred_element_type=F32)
        kmean_ref[...] = acc

    feat = lax.broadcasted_iota(jnp.int32, (HEAD_PAIR, L), 0)
    blk = lax.broadcasted_iota(jnp.int32, (n_blocks, L), 0)
    for u, (a, h) in enumerate(units):
        qi = qi0 + a
        pair, hd = divmod(h, 2)
        past = blk < qi
        qp = q_ref[0, pair * HEAD_PAIR:(pair + 1) * HEAD_PAIR, a * L:(a + 1) * L]
        qz = jnp.where((feat >= hd * Dh) & (feat < (hd + 1) * Dh), qp, jnp.zeros_like(qp))
        qz_ref[u] = qz
        gate = jnp.dot(kmean_ref[:, pair * HEAD_PAIR:(pair + 1) * HEAD_PAIR], qz.astype(F32),
                       precision=lax.Precision.HIGHEST, preferred_element_type=F32)
        gate = jnp.where(past, gate, -jnp.inf)
        chosen = jnp.zeros(gate.shape, jnp.bool_)
        for _ in range(MOBA_TOPK):
            top = jnp.max(gate, axis=0, keepdims=True)
            first = jnp.min(jnp.where(gate == top, blk, n_blocks), axis=0, keepdims=True)
            hit = blk == first
            chosen = jnp.logical_or(chosen, hit)
            gate = jnp.where(hit, -jnp.inf, gate)
        chosen = jnp.logical_and(chosen, past)
        sel_ref[u] = jnp.where(chosen, 0.0, MASKED)
        farsel_ref[u, 0:n_blocks, :] = jnp.where(jnp.logical_and(chosen, blk < qi - 1),
                                                 far_ref[group * HG + h], MASKED)
        farsel_ref[u, n_blocks:n_blocks + F32_SUBLANES, :] = jnp.zeros((F32_SUBLANES, L), F32)

    def stage_scores(u, j, tile, slot):
        pair = units[u][1] // 2
        kb = k_ref[0, pl.ds(pl.multiple_of(j * L, L), L), pair * HEAD_PAIR:(pair + 1) * HEAD_PAIR]
        s = jnp.dot(kb, qz_ref[u], preferred_element_type=F32)
        if tile is not None:
            s = s + tile
        s_ref[slot][u] = s
        smax_ref[slot][u] = jnp.max(s, axis=0, keepdims=True)

    def stage_softmax(u, row, slot):
        m_prev = m_ref[u]
        m_new = jnp.maximum(m_prev, smax_ref[slot][u] + row)
        alpha_ref[slot][u] = jnp.exp2(m_prev - m_new)
        p_ref[slot][u] = jnp.exp2(s_ref[slot][u] - (m_new - row)).astype(BF16)
        m_ref[u] = m_new

    def stage_values(u, j, slot):
        h = units[u][1]
        vb = v_ref[0, h * V_ROWS:(h + 1) * V_ROWS, pl.ds(pl.multiple_of(j * L, L), L)]
        pv = jnp.dot(vb, p_ref[slot][u], preferred_element_type=F32)
        acc_ref[u] = alpha_ref[slot][u] * acc_ref[u] + pv

    m_ref[...] = jnp.full(m_ref.shape, MASKED, F32)
    acc_ref[...] = jnp.zeros(acc_ref.shape, F32)

    zero_row = jnp.zeros((1, L), F32)
    own = [qi0 + a for a, _ in units]
    prev = [jnp.maximum(qi0 + a - 1, 0) for a, _ in units]
    for u, (a, h) in enumerate(units):
        stage_scores(u, own[u], bias_ref[0, h], 0)
        stage_scores(u, prev[u], bias_ref[1, h] + sel_ref[u, pl.ds(prev[u], 1), :], 1)
    for u in range(len(units)):
        stage_softmax(u, zero_row, 0)
    n_steps = 2 + jnp.maximum(qi0 + MOBA_QTILES - 2, 0)

    def step(t, slot):
        far_value = jnp.minimum(t - 2, last_blk)
        j = jnp.minimum(t, last_blk)
        jr = jnp.where(t == 0, n_blocks, jnp.minimum(t - 1, last_blk))
        for u in range(len(units)):
            stage_scores(u, j, None, slot)
            stage_softmax(u, farsel_ref[u, pl.ds(jr, 1), :], 1 - slot)
            stage_values(u, jnp.where(t == 0, own[u], jnp.where(t == 1, prev[u], far_value)), slot)

    def unrolled_steps(k, carry):
        for s in range(MOBA_UNROLL):
            step(MOBA_UNROLL * k + s, s % 2)
        return carry

    lax.fori_loop(0, (n_steps + MOBA_UNROLL - 1) // MOBA_UNROLL, unrolled_steps, 0)
    for u, (a, h) in enumerate(units):
        o_ref[0, h * Dh:(h + 1) * Dh, a * L:(a + 1) * L] = acc_ref[u, 0:Dh, :] / acc_ref[u, Dh:Dh + 1, :]


def _t5_bucket_np(dist):
    n = np.maximum(dist, 0)
    max_exact = REL_BUCKETS // 2
    nf = np.maximum(n, 1).astype(np.float32)
    large = max_exact + (np.log(nf / max_exact) / math.log(REL_MAX_DIST / max_exact)
                         * (REL_BUCKETS - max_exact)).astype(np.int32)
    large = np.minimum(large, REL_BUCKETS - 1)
    return np.where(n < max_exact, n, large)


def _moba(proj_t, k_nat, rel_table):
    B, _, S = proj_t.shape
    L = MOBA_BLOCK
    n_blocks = S // L
    H = ATTN_HEADS
    table_t = rel_table.astype(F32).T
    dist = np.arange(-(L - 1), 2 * L)
    onehot = jnp.asarray(_t5_bucket_np(dist))[None, None, :] == jnp.arange(REL_BUCKETS)[None, :, None]
    by_dist = jnp.sum(jnp.where(onehot, table_t[:, :, None], 0.0), axis=1) * LOG2E
    by_dist = jnp.where(jnp.asarray(dist >= 0)[None, :], by_dist, MASKED)

    def toeplitz(vec):
        W = 2 * L - 1
        flat = jnp.broadcast_to(vec[:, None, :], (H, L, W)).reshape(H, L * W)
        return flat[:, L - 1:L - 1 + L * (W - 1)].reshape(H, L, W - 1)[:, :, :L]

    bias = jnp.stack([toeplitz(by_dist[:, :2 * L - 1]), toeplitz(by_dist[:, L:])])
    assert int(_t5_bucket_np(np.array(L + 1))) == REL_BUCKETS - 1
    far = table_t[:, REL_BUCKETS - 1] * LOG2E

    HG, QT = MOBA_HEADS, MOBA_QTILES
    U = HG * QT
    q_rows, v_rows = HG * ATTN_HEAD_DIM, HG * V_ROWS
    assert ATTN_HEADS % HG == 0 and HG % 2 == 0 and n_blocks % QT == 0 and PROJT_V_ROW0 % v_rows == 0
    once = dict(pipeline_mode=pl.Buffered(1))
    grid_spec = pltpu.PrefetchScalarGridSpec(
        num_scalar_prefetch=1,
        grid=(B, ATTN_HEADS // HG, n_blocks // QT),
        in_specs=[
            pl.BlockSpec((1, q_rows, QT * L), lambda b, g, i, far: (b, g, i)),
            pl.BlockSpec((1, S, q_rows), lambda b, g, i, far: (b, 0, g), **once),
            pl.BlockSpec((1, v_rows, S), lambda b, g, i, far: (b, PROJT_V_ROW0 // v_rows + g, 0), **once),
            pl.BlockSpec((2, HG, L, L), lambda b, g, i, far: (0, g, 0, 0), **once),
        ],
        out_specs=pl.BlockSpec((1, q_rows, QT * L), lambda b, g, i, far: (b, g, i)),
        scratch_shapes=[
            pltpu.VMEM((n_blocks, q_rows), F32),
            pltpu.VMEM((U, HEAD_PAIR, L), BF16),
            pltpu.VMEM((U, n_blocks, L), F32),
            pltpu.VMEM((U, n_blocks + F32_SUBLANES, L), F32),
            *([pltpu.VMEM((U, L, L), F32), pltpu.VMEM((U, 1, L), F32),
               pltpu.VMEM((U, L, L), BF16), pltpu.VMEM((U, 1, L), F32)] * 2),
            pltpu.VMEM((U, 1, L), F32),
            pltpu.VMEM((U, V_ROWS, L), F32),
        ],
    )
    return pl.pallas_call(
        functools.partial(_moba_kernel, n_blocks=n_blocks),
        grid_spec=grid_spec,
        out_shape=jax.ShapeDtypeStruct((B, ATTN_WIDTH, S), F32),
        compiler_params=pltpu.CompilerParams(dimension_semantics=("arbitrary",) * 3,
                                             vmem_limit_bytes=MOBA_VMEM_LIMIT),
        name="moba",
    )(far, proj_t, k_nat, proj_t, bias)


def _retention_kernel(cdec_ref, q_ref, k_ref, v_ref, g_ref, cos_ref, sin_ref,
                      dec_ref, qdec_ref, kdec_ref, o_ref, state_ref):
    c = pl.program_id(0)

    @pl.when(c == 0)
    def _():
        state_ref[...] = jnp.zeros(state_ref.shape, F32)

    cos = cos_ref[...]
    sin = sin_ref[...]
    half = RET_HEAD_DIM // 2

    def rope(t):
        t1, t2 = t[:half], t[half:]
        return jnp.concatenate([t1 * cos - t2 * sin, t1 * sin + t2 * cos], axis=0)

    for b in range(q_ref.shape[0]):
        for hh in range(RET_HEADS):
            rows = slice(hh * RET_HEAD_DIM, (hh + 1) * RET_HEAD_DIM)
            q = rope(q_ref[b, rows, :].astype(F32))
            k = rope(k_ref[b, rows, :].astype(F32)) * (RET_HEAD_DIM ** -0.5)
            v = v_ref[b, rows, :]
            qb = q.astype(BF16)
            s = lax.dot_general(k.astype(BF16), qb, _TN, preferred_element_type=F32)
            s = (s * dec_ref[hh]).astype(BF16)
            inner = jnp.dot(v, s, preferred_element_type=F32)
            state = state_ref[b, hh]
            cross = jnp.dot(state.astype(BF16), (q * qdec_ref[hh:hh + 1, :]).astype(BF16),
                            preferred_element_type=F32)
            kd = (k * kdec_ref[hh:hh + 1, :]).astype(BF16)
            state_ref[b, hh] = state * cdec_ref[hh] + lax.dot_general(v, kd, _NT, preferred_element_type=F32)
            out = inner + cross
            mu = jnp.mean(out, axis=0, keepdims=True)
            var = jnp.mean(jnp.square(out - mu), axis=0, keepdims=True)
            out = (out - mu) * lax.rsqrt(var + LN_EPS)
            g = g_ref[b, rows, :].astype(F32)
            o_ref[b, rows, :] = (g * jax.nn.sigmoid(g) * out).astype(BF16)


def _retention(proj_t):
    B, _, S = proj_t.shape
    C = RET_CHUNK
    H = RET_HEADS
    half = RET_HEAD_DIM // 2
    inv = ROPE_BASE ** (-jnp.arange(half, dtype=F32) / half)
    ang = inv[:, None] * jnp.arange(S).astype(F32)[None, :]
    cos_t, sin_t = jnp.cos(ang), jnp.sin(ang)
    gammas = 1.0 - jnp.exp(jnp.linspace(math.log(1.0 / 32), math.log(1.0 / 512), H, dtype=F32))
    log_g = jnp.log(gammas)
    idx = jnp.arange(C, dtype=F32)
    diff = idx[None, :] - idx[:, None]
    dec_t = jnp.where(diff[None] >= 0, jnp.exp(jnp.maximum(diff, 0.0)[None] * log_g[:, None, None]), 0.0)
    q_dec = jnp.exp((idx[None, :] + 1.0) * log_g[:, None])
    k_dec = jnp.exp((C - 1.0 - idx[None, :]) * log_g[:, None])
    chunk_dec = jnp.exp(C * log_g)

    grid_spec = pltpu.PrefetchScalarGridSpec(
        num_scalar_prefetch=1,
        grid=(S // C,),
        in_specs=[
            pl.BlockSpec((B, RET_WIDTH, C), lambda c, cd: (0, 1, c)),
            pl.BlockSpec((B, RET_WIDTH, C), lambda c, cd: (0, 2, c)),
            pl.BlockSpec((B, RET_WIDTH, C), lambda c, cd: (0, 3, c)),
            pl.BlockSpec((B, RET_WIDTH, C), lambda c, cd: (0, 4, c)),
            pl.BlockSpec((half, C), lambda c, cd: (0, c)),
            pl.BlockSpec((half, C), lambda c, cd: (0, c)),
            pl.BlockSpec((H, C, C), lambda c, cd: (0, 0, 0)),
            pl.BlockSpec((H, C), lambda c, cd: (0, 0)),
            pl.BlockSpec((H, C), lambda c, cd: (0, 0)),
        ],
        out_specs=pl.BlockSpec((B, RET_WIDTH, C), lambda c, cd: (0, 0, c)),
        scratch_shapes=[pltpu.VMEM((B, H, RET_HEAD_DIM, RET_HEAD_DIM), F32)],
    )
    return pl.pallas_call(
        _retention_kernel,
        grid_spec=grid_spec,
        out_shape=jax.ShapeDtypeStruct((B, RET_WIDTH, S), BF16),
        compiler_params=_params("arbitrary"),
        name="retention",
    )(chunk_dec, proj_t, proj_t, proj_t, proj_t, cos_t, sin_t, dec_t, q_dec, k_dec)


def _layer_norm_rows(y, g, b):
    mu = jnp.mean(y, axis=-1, keepdims=True)
    var = jnp.mean(jnp.square(y - mu), axis=-1, keepdims=True)
    return (y - mu) * lax.rsqrt(var + LN_EPS) * g + b


def _outproj_kernel(attn_ref, ret_ref, x_ref, gain_ref, wa_ref, wr_ref, g1_ref, b1_ref,
                    wrt_ref, brt_ref, before_ref,
                    h_ref, eid_ref, rank_ref, cw_ref, cnt_ref, carry_ref):
    first = jnp.logical_and(pl.program_id(0) == 0, pl.program_id(1) == 0)

    @pl.when(first)
    def _():
        carry_ref[...] = jnp.zeros(carry_ref.shape, F32)

    a = attn_ref[0]
    a = a * lax.rsqrt(jnp.mean(jnp.square(a), axis=0, keepdims=True) + LN_EPS) * gain_ref[...]
    mix = (lax.dot_general(a.astype(BF16), wa_ref[...], _TN, preferred_element_type=F32)
           + lax.dot_general(ret_ref[0], wr_ref[...], _TN, preferred_element_type=F32))
    h = _layer_norm_rows(ALPHA * x_ref[0] + mix, g1_ref[...], b1_ref[...])
    h_ref[...] = h

    R = ROUTER_ROWS
    h_hi = h.astype(BF16)
    h_lo = (h - h_hi.astype(F32)).astype(BF16)
    by_hi = lax.dot_general(wrt_ref[...], h_hi, _NT, preferred_element_type=F32)
    by_lo = lax.dot_general(wrt_ref[0:R, :], h_lo, _NT, preferred_element_type=F32)
    logits = by_hi[0:R] + by_hi[R:2 * R] + by_lo + brt_ref[...]
    T = logits.shape[1]
    gl = logits[0:N_GROUPS]
    gmax = jnp.max(gl, axis=0, keepdims=True)
    grow = lax.broadcasted_iota(jnp.int32, gl.shape, 0)
    gidx = jnp.min(jnp.where(gl == gmax, grow, N_GROUPS), axis=0, keepdims=True)
    g_p = 1.0 / jnp.sum(jnp.exp(gl - gmax), axis=0, keepdims=True)
    el = jnp.zeros((EXPERTS_PER_GROUP, T), F32)
    for g in range(N_GROUPS):
        r0 = ROUTER_EXPERT_ROW0 + g * EXPERTS_PER_GROUP
        el = jnp.where(gidx == g, logits[r0:r0 + EXPERTS_PER_GROUP], el)
    erow = lax.broadcasted_iota(jnp.int32, el.shape, 0)
    e1 = jnp.max(el, axis=0, keepdims=True)
    i1 = jnp.min(jnp.where(el == e1, erow, EXPERTS_PER_GROUP), axis=0, keepdims=True)
    el2 = jnp.where(erow == i1, -jnp.inf, el)
    e2 = jnp.max(el2, axis=0, keepdims=True)
    i2 = jnp.min(jnp.where(el2 == e2, erow, EXPERTS_PER_GROUP), axis=0, keepdims=True)
    r = jnp.exp(e2 - e1)
    w1 = g_p / (1.0 + r)
    w2 = g_p * r / (1.0 + r)
    id1 = gidx * EXPERTS_PER_GROUP + i1
    id2 = gidx * EXPERTS_PER_GROUP + i2
    eid_ref[0:1, :] = id1
    eid_ref[1:2, :] = id2
    cw_ref[0:1, :] = w1
    cw_ref[1:2, :] = w2

    xrow = lax.broadcasted_iota(jnp.int32, (N_EXPERTS, T), 0)
    oh1 = (xrow == id1).astype(F32)
    oh2 = (xrow == id2).astype(F32)
    both = oh1 + oh2
    seen = jnp.dot(both.astype(BF16), before_ref[...], preferred_element_type=F32) + carry_ref[...]
    rank_ref[0:1, :] = jnp.sum(oh1 * seen, axis=0, keepdims=True).astype(jnp.int32)
    rank_ref[1:2, :] = jnp.sum(oh2 * seen, axis=0, keepdims=True).astype(jnp.int32)
    carry = carry_ref[...] + jnp.sum(both, axis=1, keepdims=True)
    carry_ref[...] = carry
    cnt_ref[...] = carry.astype(jnp.int32)


def _outproj(attn_t, ret_t, x, gain, w_out, ln_g, ln_b, w_rg, b_rg, w_re, b_re):
    B, S, D = x.shape
    Tt = OUTPROJ_TOKENS
    n_t = S // Tt
    T = B * S
    wa = w_out[:ATTN_WIDTH].astype(BF16)
    wr = w_out[ATTN_WIDTH:].astype(BF16)
    e0, e1 = ROUTER_EXPERT_ROW0, ROUTER_EXPERT_ROW0 + N_EXPERTS
    wrt = jnp.zeros((ROUTER_ROWS, D), F32)
    wrt = wrt.at[:N_GROUPS].set(w_rg.T).at[e0:e1].set(w_re.T)
    wrt_hi = wrt.astype(BF16)
    wrt_lo = (wrt - wrt_hi.astype(F32)).astype(BF16)
    wrt = jnp.concatenate([wrt_hi, wrt_lo], axis=0)
    brt = jnp.zeros((ROUTER_ROWS, 1), F32)
    brt = brt.at[:N_GROUPS, 0].set(b_rg.astype(F32)).at[e0:e1, 0].set(b_re.astype(F32))
    before = (jnp.arange(Tt)[:, None] < jnp.arange(Tt)[None, :]).astype(BF16)
    const = lambda b, i: (0, 0)
    tok = lambda b, i: (0, b * n_t + i)
    return pl.pallas_call(
        _outproj_kernel,
        grid=(B, n_t),
        in_specs=[
            pl.BlockSpec((1, ATTN_WIDTH, Tt), lambda b, i: (b, 0, i)),
            pl.BlockSpec((1, RET_WIDTH, Tt), lambda b, i: (b, 0, i)),
            pl.BlockSpec((1, Tt, D), lambda b, i: (b, i, 0)),
            pl.BlockSpec((ATTN_WIDTH, 1), const),
            pl.BlockSpec((ATTN_WIDTH, D), const),
            pl.BlockSpec((RET_WIDTH, D), const),
            pl.BlockSpec((1, D), const),
            pl.BlockSpec((1, D), const),
            pl.BlockSpec((2 * ROUTER_ROWS, D), const),
            pl.BlockSpec((ROUTER_ROWS, 1), const),
            pl.BlockSpec((Tt, Tt), const),
        ],
        out_specs=[
            pl.BlockSpec((Tt, D), lambda b, i: (b * n_t + i, 0)),
            pl.BlockSpec((2, Tt), tok),
            pl.BlockSpec((2, Tt), tok),
            pl.BlockSpec((2, Tt), tok),
            pl.BlockSpec((N_EXPERTS, 1), const),
        ],
        out_shape=[
            jax.ShapeDtypeStruct((T, D), F32),
            jax.ShapeDtypeStruct((2, T), jnp.int32),
            jax.ShapeDtypeStruct((2, T), jnp.int32),
            jax.ShapeDtypeStruct((2, T), F32),
            jax.ShapeDtypeStruct((N_EXPERTS, 1), jnp.int32),
        ],
        scratch_shapes=[pltpu.VMEM((N_EXPERTS, 1), F32)],
        compiler_params=_params("arbitrary", "arbitrary"),
        name="outproj_ln_router",
    )(attn_t, ret_t, x, gain.reshape(ATTN_WIDTH, 1), wa, wr, ln_g.reshape(1, D), ln_b.reshape(1, D),
      wrt, brt, before)


def _dispatch_kernel(start_ref, end_ref, pos0_ref, pos1_ref, h_ref, xs_hbm, zero_ref, sem, zsem):
    G = ROUTE_TOKENS
    Tm = MOE_ROWS

    @pl.when(pl.program_id(0) == 0)
    def _():
        zero_ref[...] = jnp.zeros(zero_ref.shape, F32)

        def tile_clear(row0):
            row0 = pl.multiple_of(row0, Tm)
            return pltpu.make_async_copy(zero_ref, xs_hbm.at[pl.ds(row0, Tm)], zsem)

        used = end_ref[N_EXPERTS - 1]
        n_rows = xs_hbm.shape[0]
        clears = [(end_ref[e] - Tm, end_ref[e] > start_ref[e]) for e in range(N_EXPERTS)]
        clears += [(jnp.minimum(used + k * Tm, n_rows - Tm), used + k * Tm < n_rows) for k in range(N_EXPERTS)]
        for row0, cond in clears:
            @pl.when(cond)
            def _():
                tile_clear(row0).start()
        for row0, cond in clears:
            @pl.when(cond)
            def _():
                tile_clear(row0).wait()

    def issue(k, carry):
        t0 = pl.multiple_of(k * F32_SUBLANES, F32_SUBLANES)
        for u in range(F32_SUBLANES):
            for pos_ref in (pos0_ref, pos1_ref):
                pltpu.make_async_copy(h_ref.at[pl.ds(t0 + u, 1)], xs_hbm.at[pl.ds(pos_ref[t0 + u], 1)], sem).start()
        return carry

    lax.fori_loop(0, G // F32_SUBLANES, issue, 0)
    for _ in range(2):
        pltpu.make_async_copy(h_ref, xs_hbm.at[pl.ds(0, G)], sem).wait()


def _dispatch(h, pos0, pos1, starts, ends, n_rows):
    T, D = h.shape
    G = ROUTE_TOKENS
    smem_blk = pl.BlockSpec((G,), lambda i, st, en: (i,), memory_space=pltpu.SMEM)
    grid_spec = pltpu.PrefetchScalarGridSpec(
        num_scalar_prefetch=2,
        grid=(T // G,),
        in_specs=[smem_blk, smem_blk, pl.BlockSpec((G, D), lambda i, st, en: (i, 0))],
        out_specs=pl.BlockSpec(memory_space=pl.ANY),
        scratch_shapes=[pltpu.VMEM((MOE_ROWS, D), F32), pltpu.SemaphoreType.DMA(()),
                        pltpu.SemaphoreType.DMA(())],
    )
    return pl.pallas_call(
        _dispatch_kernel,
        grid_spec=grid_spec,
        out_shape=jax.ShapeDtypeStruct((n_rows, D), F32),
        compiler_params=_params("arbitrary"),
        name="moe_dispatch",
    )(starts, ends, pos0, pos1, h)


def _experts_kernel(te_ref, nv_ref, xs_ref, wgu_ref, wdn_ref, y_ref, wgu_b, wdn_b):
    i = pl.program_id(0)
    live = i < nv_ref[0]
    new_expert = jnp.logical_or(i == 0, te_ref[i] != te_ref[jnp.maximum(i - 1, 0)])

    @pl.when(jnp.logical_and(live, new_expert))
    def _():
        wgu_b[...] = wgu_ref[0].astype(BF16)
        wdn_b[...] = wdn_ref[0].astype(BF16)

    @pl.when(live)
    def _():
        gu = jnp.dot(xs_ref[...].astype(BF16), wgu_b[...], preferred_element_type=F32)
        gate, up = gu[:, :D_EXPERT], gu[:, D_EXPERT:]
        hid = gate * jax.nn.sigmoid(gate) * up
        y_ref[...] = jnp.dot(hid.astype(BF16), wdn_b[...], preferred_element_type=F32)

    @pl.when(jnp.logical_not(live))
    def _():
        y_ref[...] = jnp.zeros(y_ref.shape, F32)


def _experts(xs, w_gu, w_dn, tile_expert, n_valid):
    n_rows, D = xs.shape
    Tm = MOE_ROWS
    grid_spec = pltpu.PrefetchScalarGridSpec(
        num_scalar_prefetch=2,
        grid=(n_rows // Tm,),
        in_specs=[
            pl.BlockSpec((Tm, D), lambda i, te, nv: (jnp.minimum(i, jnp.maximum(nv[0] - 1, 0)), 0)),
            pl.BlockSpec((1, D, 2 * D_EXPERT), lambda i, te, nv: (te[i], 0, 0)),
            pl.BlockSpec((1, D_EXPERT, D), lambda i, te, nv: (te[i], 0, 0)),
        ],
        out_specs=pl.BlockSpec((Tm, D), lambda i, te, nv: (i, 0)),
        scratch_shapes=[pltpu.VMEM((D, 2 * D_EXPERT), BF16), pltpu.VMEM((D_EXPERT, D), BF16)],
    )
    return pl.pallas_call(
        _experts_kernel,
        grid_spec=grid_spec,
        out_shape=jax.ShapeDtypeStruct((n_rows, D), F32),
        compiler_params=_params("arbitrary"),
        name="moe_experts",
    )(tile_expert, n_valid, xs, w_gu, w_dn)


def _combine_kernel(pos0_ref, pos1_ref, y_hbm, h_ref, cw_ref, g_ref, b_ref, o_ref, ybuf, sem):
    G = ROUTE_TOKENS

    def issue(k, carry):
        t0 = pl.multiple_of(k * F32_SUBLANES, F32_SUBLANES)
        for u in range(F32_SUBLANES):
            for slot, pos_ref in enumerate((pos0_ref, pos1_ref)):
                pltpu.make_async_copy(y_hbm.at[pl.ds(pos_ref[t0 + u], 1)], ybuf.at[slot, pl.ds(t0 + u, 1)],
                                      sem).start()
        return carry

    lax.fori_loop(0, G // F32_SUBLANES, issue, 0)
    for slot in range(2):
        pltpu.make_async_copy(y_hbm.at[pl.ds(0, G)], ybuf.at[slot], sem).wait()

    for r in range(G // COMBINE_CHUNK):
        rows = slice(r * COMBINE_CHUNK, (r + 1) * COMBINE_CHUNK)
        cw = cw_ref[rows, :]
        ffn = ybuf[0, rows, :] * cw[:, 0:1] + ybuf[1, rows, :] * cw[:, 1:2]
        o_ref[rows, :] = _layer_norm_rows(ALPHA * h_ref[rows, :] + ffn, g_ref[...], b_ref[...])


def _combine(y, h, pos0, pos1, cw_rows, ln_g, ln_b):
    T, D = h.shape
    G = ROUTE_TOKENS
    smem_blk = pl.BlockSpec((G,), lambda i: (i,), memory_space=pltpu.SMEM)
    return pl.pallas_call(
        _combine_kernel,
        grid=(T // G,),
        in_specs=[
            smem_blk, smem_blk,
            pl.BlockSpec(memory_space=pl.ANY),
            pl.BlockSpec((G, D), lambda i: (i, 0)),
            pl.BlockSpec((G, 2), lambda i: (i, 0)),
            pl.BlockSpec((1, D), lambda i: (0, 0)),
            pl.BlockSpec((1, D), lambda i: (0, 0)),
        ],
        out_specs=pl.BlockSpec((G, D), lambda i: (i, 0)),
        scratch_shapes=[pltpu.VMEM((2, G, D), F32), pltpu.SemaphoreType.DMA(())],
        out_shape=jax.ShapeDtypeStruct((T, D), F32),
        compiler_params=_params("arbitrary"),
        name="moe_combine_ln",
    )(pos0, pos1, y, h, cw_rows, ln_g.reshape(1, D), ln_b.reshape(1, D))


def kernel(x, w_in, attn_out_gain, rel_bias_table, w_out, ln1_g, ln1_b, w_router_group,
           b_router_group, w_router_expert, b_router_expert, w_gate_up, w_down, ln2_g, ln2_b):
    B, S, D = x.shape
    assert D == D_MODEL and S % MOBA_BLOCK == 0 and S % INPROJ_TOKENS == 0
    assert w_in.shape[0] == DEPTH
    T = B * S
    h = x
    for l in range(DEPTH):
        proj_t, k_nat = _inproj(h, w_in[l])
        attn_t = _moba(proj_t, k_nat, rel_bias_table)
        ret_t = _retention(proj_t)
        h1, eid, rank, cw, counts = _outproj(
            attn_t, ret_t, h, attn_out_gain[l], w_out[l], ln1_g[l], ln1_b[l],
            w_router_group[l], b_router_group[l], w_router_expert[l], b_router_expert[l])

        Tm = MOE_ROWS
        n_tiles = (2 * T) // Tm + N_EXPERTS
        padded = ((counts[:, 0] + Tm - 1) // Tm) * Tm
        ends = jnp.cumsum(padded)
        starts = (ends - padded).astype(jnp.int32)
        tile_row0 = jnp.arange(n_tiles, dtype=jnp.int32) * Tm
        tile_expert = jnp.sum((ends[None, :] <= jnp.minimum(tile_row0, ends[-1] - Tm)[:, None]).astype(jnp.int32),
                              axis=1)
        n_valid = (ends[-1:] // Tm).astype(jnp.int32)

        expert_ids = jnp.arange(N_EXPERTS, dtype=jnp.int32)[:, None, None]
        pos = jnp.sum(jnp.where(eid[None] == expert_ids, starts[:, None, None], 0), axis=0) + rank

        xs = _dispatch(h1, pos[0], pos[1], starts, ends.astype(jnp.int32), n_tiles * Tm)
        y = _experts(xs, w_gate_up[l], w_down[l], tile_expert, n_valid)
        h2 = _combine(y, h1, pos[0], pos[1], cw.T, ln2_g[l], ln2_b[l])
        h = h2.reshape(B, S, D)
    return h
```

```python
import functools
import math

import numpy as np
import jax
import jax.numpy as jnp
from jax import lax
from jax.experimental import pallas as pl
from jax.experimental.pallas import tpu as pltpu

F32 = jnp.float32
BF16 = jnp.bfloat16

D_MODEL = 1024
ATTN_HEADS = 8
ATTN_HEAD_DIM = 64
ATTN_WIDTH = ATTN_HEADS * ATTN_HEAD_DIM
RET_HEADS = 4
RET_HEAD_DIM = 128
RET_WIDTH = RET_HEADS * RET_HEAD_DIM
N_IN_SLICES = 7
PROJT_SLICES = N_IN_SLICES - 1
HEAD_PAIR = 2 * ATTN_HEAD_DIM
BF16_SUBLANES = 16
F32_SUBLANES = 8
V_ROWS = ATTN_HEAD_DIM + BF16_SUBLANES
PROJT_V_ROW0 = (PROJT_SLICES - 1) * ATTN_WIDTH
PROJT_WIDTH = PROJT_V_ROW0 + ATTN_HEADS * V_ROWS
MOBA_BLOCK = 256
MOBA_TOPK = 3
RET_CHUNK = 256
REL_BUCKETS = 32
REL_MAX_DIST = 128
ROPE_BASE = 10000.0
N_GROUPS = 4
EXPERTS_PER_GROUP = 8
N_EXPERTS = N_GROUPS * EXPERTS_PER_GROUP
D_EXPERT = D_MODEL // 2
DEPTH = 1
ALPHA = (2.0 * DEPTH) ** 0.25
LN_EPS = 1e-5

LOG2E = math.log2(math.e)
Q_SCALE = ATTN_HEAD_DIM ** -0.5 * LOG2E

MASKED = -1e30

INPROJ_TOKENS = 1024
INPROJ_ROWS = ATTN_WIDTH
OUTPROJ_TOKENS = 512
MOBA_UNROLL = 2
MOBA_HEADS = 4
MOBA_QTILES = 4
MOE_ROWS = 512
ROUTE_TOKENS = 1024
COMBINE_CHUNK = 256
ROUTER_EXPERT_ROW0 = 8
ROUTER_ROWS = 48

V7X_VMEM_BYTES = 64 * 1024 * 1024
VMEM_LIMIT = V7X_VMEM_BYTES * 3 // 4

_TN = (((0,), (0,)), ((), ()))
_NT = (((1,), (1,)), ((), ()))


def _params(*sem):
    return pltpu.CompilerParams(dimension_semantics=sem, vmem_limit_bytes=VMEM_LIMIT)


def _inproj_kernel(x_ref, wt_ref, wk_ref, o_ref, k_ref):
    xb = x_ref[0].astype(BF16)
    n_tok = xb.shape[0]
    ones_row = (lax.broadcasted_iota(jnp.int32, (BF16_SUBLANES, n_tok), 0) == 0).astype(BF16)
    for c in range(PROJT_SLICES):
        rows = slice(c * INPROJ_ROWS, (c + 1) * INPROJ_ROWS)
        acc = lax.dot_general(wt_ref[rows, :], xb, _NT, preferred_element_type=F32)
        if c == 0:
            acc = acc * Q_SCALE
        if c < PROJT_SLICES - 1:
            o_ref[0, rows, :] = acc.astype(BF16)
        else:
            for h in range(ATTN_HEADS):
                r0 = PROJT_V_ROW0 + h * V_ROWS
                o_ref[0, r0:r0 + ATTN_HEAD_DIM, :] = acc[h * ATTN_HEAD_DIM:(h + 1) * ATTN_HEAD_DIM].astype(BF16)
                o_ref[0, r0 + ATTN_HEAD_DIM:r0 + V_ROWS, :] = ones_row
    k_ref[0] = jnp.dot(xb, wk_ref[...], preferred_element_type=F32).astype(BF16)


def _inproj(x, w_in):
    B, S, D = x.shape
    W = ATTN_WIDTH
    w_t = jnp.concatenate([w_in[:, :W], w_in[:, 3 * W:], w_in[:, 2 * W:3 * W]], axis=1).T.astype(BF16)
    w_k = w_in[:, W:2 * W].astype(BF16)
    return pl.pallas_call(
        _inproj_kernel,
        grid=(B, S // INPROJ_TOKENS),
        in_specs=[
            pl.BlockSpec((1, INPROJ_TOKENS, D), lambda b, i: (b, i, 0)),
            pl.BlockSpec((PROJT_SLICES * W, D), lambda b, i: (0, 0)),
            pl.BlockSpec((D, W), lambda b, i: (0, 0)),
        ],
        out_specs=[
            pl.BlockSpec((1, PROJT_WIDTH, INPROJ_TOKENS), lambda b, i: (b, 0, i)),
            pl.BlockSpec((1, INPROJ_TOKENS, W), lambda b, i: (b, i, 0)),
        ],
        out_shape=[
            jax.ShapeDtypeStruct((B, PROJT_WIDTH, S), BF16),
            jax.ShapeDtypeStruct((B, S, W), BF16),
        ],
        compiler_params=_params("arbitrary", "arbitrary"),
        name="inproj",
    )(x, w_t, w_k)


def _moba_kernel(far_ref, q_ref, k_ref, v_ref, bias_ref, o_ref,
                 kmean_ref, qz_ref, sel_ref, farsel_ref,
                 s0_ref, smax0_ref, p0_ref, alpha0_ref, s1_ref, smax1_ref, p1_ref, alpha1_ref,
                 m_ref, acc_ref, *, n_blocks):
    L = MOBA_BLOCK
    Dh = ATTN_HEAD_DIM
    HG = MOBA_HEADS
    group = pl.program_id(1)
    qi0 = pl.program_id(2) * MOBA_QTILES
    last_blk = n_blocks - 1
    units = [(a, h) for a in range(MOBA_QTILES) for h in range(HG)]
    s_ref, smax_ref = (s0_ref, s1_ref), (smax0_ref, smax1_ref)
    p_ref, alpha_ref = (p0_ref, p1_ref), (alpha0_ref, alpha1_ref)

    @pl.when(qi0 == 0)
    def _():
        chunk_blocks = min(8, n_blocks)
        chunk = chunk_blocks * L
        acc = jnp.zeros((n_blocks, HG * Dh), F32)
        for c in range(n_blocks // chunk_blocks):
            blk = lax.broadcasted_iota(jnp.int32, (n_blocks, chunk), 0)
            key = lax.broadcasted_iota(jnp.int32, (n_blocks, chunk), 1)
            ind = jnp.where(blk == c * chunk_blocks + key // L, 1.0 / L, 0.0).astype(BF16)
            acc = acc + jnp.dot(ind, k_ref[0, c * chunk:(c + 1) * chunk, :], preferred_element_type=F32)
        hi = acc.astype(BF16)
        lo = (acc - hi.astype(F32)).astype(BF16)
        for pair in range(HG // 2):
            cols = slice(pair * HEAD_PAIR, (pair + 1) * HEAD_PAIR)
            kmean_ref[pair, 0:n_blocks, :] = hi[:, cols]
            kmean_ref[pair, n_blocks:2 * n_blocks, :] = lo[:, cols]

    feat = lax.broadcasted_iota(jnp.int32, (HEAD_PAIR, L), 0)
    blk = lax.broadcasted_iota(jnp.int32, (n_blocks, L), 0)
    for u, (a, h) in enumerate(units):
        qi = qi0 + a
        pair, hd = divmod(h, 2)
        past = blk < qi
        qp = q_ref[0, pair * HEAD_PAIR:(pair + 1) * HEAD_PAIR, a * L:(a + 1) * L]
        qz = jnp.where((feat >= hd * Dh) & (feat < (hd + 1) * Dh), qp, jnp.zeros_like(qp))
        qz_ref[u] = qz
        gate = jnp.dot(kmean_ref[pair], qz, preferred_element_type=F32)
        gate = gate[0:n_blocks] + gate[n_blocks:2 * n_blocks]
        gate = jnp.where(past, gate, -jnp.inf)
        chosen = jnp.zeros(gate.shape, jnp.bool_)
        for _ in range(MOBA_TOPK):
            top = jnp.max(gate, axis=0, keepdims=True)
            first = jnp.min(jnp.where(gate == top, blk, n_blocks), axis=0, keepdims=True)
            hit = blk == first
            chosen = jnp.logical_or(chosen, hit)
            gate = jnp.where(hit, -jnp.inf, gate)
        chosen = jnp.logical_and(chosen, past)
        sel_ref[u] = jnp.where(chosen, 0.0, MASKED)
        farsel_ref[u, 0:n_blocks, :] = jnp.where(jnp.logical_and(chosen, blk < qi - 1),
                                                 far_ref[group * HG + h], MASKED)
        farsel_ref[u, n_blocks:n_blocks + F32_SUBLANES, :] = jnp.zeros((F32_SUBLANES, L), F32)

    def stage_scores(u, j, tile, slot):
        pair = units[u][1] // 2
        kb = k_ref[0, pl.ds(pl.multiple_of(j * L, L), L), pair * HEAD_PAIR:(pair + 1) * HEAD_PAIR]
        s = jnp.dot(kb, qz_ref[u], preferred_element_type=F32)
        if tile is not None:
            s = s + tile
        s_ref[slot][u] = s
        smax_ref[slot][u] = jnp.max(s, axis=0, keepdims=True)

    def stage_softmax(u, row, slot):
        m_prev = m_ref[u]
        m_new = jnp.maximum(m_prev, smax_ref[slot][u] + row)
        alpha_ref[slot][u] = jnp.exp2(m_prev - m_new)
        p_ref[slot][u] = jnp.exp2(s_ref[slot][u] - (m_new - row)).astype(BF16)
        m_ref[u] = m_new

    def stage_values(u, j, slot):
        h = units[u][1]
        vb = v_ref[0, h * V_ROWS:(h + 1) * V_ROWS, pl.ds(pl.multiple_of(j * L, L), L)]
        pv = jnp.dot(vb, p_ref[slot][u], preferred_element_type=F32)
        acc_ref[u] = alpha_ref[slot][u] * acc_ref[u] + pv

    m_ref[...] = jnp.full(m_ref.shape, MASKED, F32)
    acc_ref[...] = jnp.zeros(acc_ref.shape, F32)

    zero_row = jnp.zeros((1, L), F32)
    own = [qi0 + a for a, _ in units]
    prev = [jnp.maximum(qi0 + a - 1, 0) for a, _ in units]
    for u, (a, h) in enumerate(units):
        stage_scores(u, own[u], bias_ref[0, h], 0)
        stage_scores(u, prev[u], bias_ref[1, h] + sel_ref[u, pl.ds(prev[u], 1), :], 1)
    for u in range(len(units)):
        stage_softmax(u, zero_row, 0)
    n_steps = 2 + jnp.maximum(qi0 + MOBA_QTILES - 2, 0)

    def step(t, slot):
        far_value = jnp.minimum(t - 2, last_blk)
        j = jnp.minimum(t, last_blk)
        jr = jnp.where(t == 0, n_blocks, jnp.minimum(t - 1, last_blk))
        for u in range(len(units)):
            stage_scores(u, j, None, slot)
            stage_softmax(u, farsel_ref[u, pl.ds(jr, 1), :], 1 - slot)
            stage_values(u, jnp.where(t == 0, own[u], jnp.where(t == 1, prev[u], far_value)), slot)

    def unrolled_steps(k, carry):
        for s in range(MOBA_UNROLL):
            step(MOBA_UNROLL * k + s, s % 2)
        return carry

    lax.fori_loop(0, (n_steps + MOBA_UNROLL - 1) // MOBA_UNROLL, unrolled_steps, 0)
    for u, (a, h) in enumerate(units):
        o_ref[0, h * Dh:(h + 1) * Dh, a * L:(a + 1) * L] = acc_ref[u, 0:Dh, :] / acc_ref[u, Dh:Dh + 1, :]


def _t5_bucket_np(dist):
    n = np.maximum(dist, 0)
    max_exact = REL_BUCKETS // 2
    nf = np.maximum(n, 1).astype(np.float32)
    large = max_exact + (np.log(nf / max_exact) / math.log(REL_MAX_DIST / max_exact)
                         * (REL_BUCKETS - max_exact)).astype(np.int32)
    large = np.minimum(large, REL_BUCKETS - 1)
    return np.where(n < max_exact, n, large)


def _moba(proj_t, k_nat, rel_table):
    B, _, S = proj_t.shape
    L = MOBA_BLOCK
    n_blocks = S // L
    H = ATTN_HEADS
    table_t = rel_table.astype(F32).T
    dist = np.arange(-(L - 1), 2 * L)
    onehot = jnp.asarray(_t5_bucket_np(dist))[None, None, :] == jnp.arange(REL_BUCKETS)[None, :, None]
    by_dist = jnp.sum(jnp.where(onehot, table_t[:, :, None], 0.0), axis=1) * LOG2E
    by_dist = jnp.where(jnp.asarray(dist >= 0)[None, :], by_dist, MASKED)

    def toeplitz(vec):
        W = 2 * L - 1
        flat = jnp.broadcast_to(vec[:, None, :], (H, L, W)).reshape(H, L * W)
        return flat[:, L - 1:L - 1 + L * (W - 1)].reshape(H, L, W - 1)[:, :, :L]

    bias = jnp.stack([toeplitz(by_dist[:, :2 * L - 1]), toeplitz(by_dist[:, L:])])
    assert int(_t5_bucket_np(np.array(L + 1))) == REL_BUCKETS - 1
    far = table_t[:, REL_BUCKETS - 1] * LOG2E

    HG, QT = MOBA_HEADS, MOBA_QTILES
    U = HG * QT
    q_rows, v_rows = HG * ATTN_HEAD_DIM, HG * V_ROWS
    assert ATTN_HEADS % HG == 0 and HG % 2 == 0 and n_blocks % QT == 0 and PROJT_V_ROW0 % v_rows == 0
    once = dict(pipeline_mode=pl.Buffered(1))
    grid_spec = pltpu.PrefetchScalarGridSpec(
        num_scalar_prefetch=1,
        grid=(B, ATTN_HEADS // HG, n_blocks // QT),
        in_specs=[
            pl.BlockSpec((1, q_rows, QT * L), lambda b, g, i, far: (b, g, i)),
            pl.BlockSpec((1, S, q_rows), lambda b, g, i, far: (b, 0, g), **once),
            pl.BlockSpec((1, v_rows, S), lambda b, g, i, far: (b, PROJT_V_ROW0 // v_rows + g, 0), **once),
            pl.BlockSpec((2, HG, L, L), lambda b, g, i, far: (0, g, 0, 0), **once),
        ],
        out_specs=pl.BlockSpec((1, q_rows, QT * L), lambda b, g, i, far: (b, g, i)),
        scratch_shapes=[
            pltpu.VMEM((HG // 2, 2 * n_blocks, HEAD_PAIR), BF16),
            pltpu.VMEM((U, HEAD_PAIR, L), BF16),
            pltpu.VMEM((U, n_blocks, L), F32),
            pltpu.VMEM((U, n_blocks + F32_SUBLANES, L), F32),
            *([pltpu.VMEM((U, L, L), F32), pltpu.VMEM((U, 1, L), F32),
               pltpu.VMEM((U, L, L), BF16), pltpu.VMEM((U, 1, L), F32)] * 2),
            pltpu.VMEM((U, 1, L), F32),
            pltpu.VMEM((U, V_ROWS, L), F32),
        ],
    )
    return pl.pallas_call(
        functools.partial(_moba_kernel, n_blocks=n_blocks),
        grid_spec=grid_spec,
        out_shape=jax.ShapeDtypeStruct((B, ATTN_WIDTH, S), F32),
        compiler_params=_params("arbitrary", "arbitrary", "arbitrary"),
        name="moba",
    )(far, proj_t, k_nat, proj_t, bias)


def _retention_kernel(cdec_ref, q_ref, k_ref, v_ref, g_ref, cos_ref, sin_ref,
                      dec_ref, qdec_ref, kdec_ref, o_ref, state_ref):
    c = pl.program_id(0)

    @pl.when(c == 0)
    def _():
        state_ref[...] = jnp.zeros(state_ref.shape, F32)

    cos = cos_ref[...]
    sin = sin_ref[...]
    half = RET_HEAD_DIM // 2

    def rope(t):
        t1, t2 = t[:half], t[half:]
        return jnp.concatenate([t1 * cos - t2 * sin, t1 * sin + t2 * cos], axis=0)

    for b in range(q_ref.shape[0]):
        for hh in range(RET_HEADS):
            rows = slice(hh * RET_HEAD_DIM, (hh + 1) * RET_HEAD_DIM)
            q = rope(q_ref[b, rows, :].astype(F32))
            k = rope(k_ref[b, rows, :].astype(F32)) * (RET_HEAD_DIM ** -0.5)
            v = v_ref[b, rows, :]
            qb = q.astype(BF16)
            s = lax.dot_general(k.astype(BF16), qb, _TN, preferred_element_type=F32)
            s = (s * dec_ref[hh]).astype(BF16)
            inner = jnp.dot(v, s, preferred_element_type=F32)
            state = state_ref[b, hh]
            cross = jnp.dot(state.astype(BF16), (q * qdec_ref[hh:hh + 1, :]).astype(BF16),
                            preferred_element_type=F32)
            kd = (k * kdec_ref[hh:hh + 1, :]).astype(BF16)
            state_ref[b, hh] = state * cdec_ref[hh] + lax.dot_general(v, kd, _NT, preferred_element_type=F32)
            out = inner + cross
            mu = jnp.mean(out, axis=0, keepdims=True)
            var = jnp.mean(jnp.square(out - mu), axis=0, keepdims=True)
            out = (out - mu) * lax.rsqrt(var + LN_EPS)
            g = g_ref[b, rows, :].astype(F32)
            o_ref[b, rows, :] = (g * jax.nn.sigmoid(g) * out).astype(BF16)


def _retention(proj_t):
    B, _, S = proj_t.shape
    C = RET_CHUNK
    H = RET_HEADS
    half = RET_HEAD_DIM // 2
    inv = ROPE_BASE ** (-jnp.arange(half, dtype=F32) / half)
    ang = inv[:, None] * jnp.arange(S).astype(F32)[None, :]
    cos_t, sin_t = jnp.cos(ang), jnp.sin(ang)
    gammas = 1.0 - jnp.exp(jnp.linspace(math.log(1.0 / 32), math.log(1.0 / 512), H, dtype=F32))
    log_g = jnp.log(gammas)
    idx = jnp.arange(C, dtype=F32)
    diff = idx[None, :] - idx[:, None]
    dec_t = jnp.where(diff[None] >= 0, jnp.exp(jnp.maximum(diff, 0.0)[None] * log_g[:, None, None]), 0.0)
    q_dec = jnp.exp((idx[None, :] + 1.0) * log_g[:, None])
    k_dec = jnp.exp((C - 1.0 - idx[None, :]) * log_g[:, None])
    chunk_dec = jnp.exp(C * log_g)

    grid_spec = pltpu.PrefetchScalarGridSpec(
        num_scalar_prefetch=1,
        grid=(S // C,),
        in_specs=[
            pl.BlockSpec((B, RET_WIDTH, C), lambda c, cd: (0, 1, c)),
            pl.BlockSpec((B, RET_WIDTH, C), lambda c, cd: (0, 2, c)),
            pl.BlockSpec((B, RET_WIDTH, C), lambda c, cd: (0, 3, c)),
            pl.BlockSpec((B, RET_WIDTH, C), lambda c, cd: (0, 4, c)),
            pl.BlockSpec((half, C), lambda c, cd: (0, c)),
            pl.BlockSpec((half, C), lambda c, cd: (0, c)),
            pl.BlockSpec((H, C, C), lambda c, cd: (0, 0, 0)),
            pl.BlockSpec((H, C), lambda c, cd: (0, 0)),
            pl.BlockSpec((H, C), lambda c, cd: (0, 0)),
        ],
        out_specs=pl.BlockSpec((B, RET_WIDTH, C), lambda c, cd: (0, 0, c)),
        scratch_shapes=[pltpu.VMEM((B, H, RET_HEAD_DIM, RET_HEAD_DIM), F32)],
    )
    return pl.pallas_call(
        _retention_kernel,
        grid_spec=grid_spec,
        out_shape=jax.ShapeDtypeStruct((B, RET_WIDTH, S), BF16),
        compiler_params=_params("arbitrary"),
        name="retention",
    )(chunk_dec, proj_t, proj_t, proj_t, proj_t, cos_t, sin_t, dec_t, q_dec, k_dec)


def _layer_norm_rows(y, g, b):
    mu = jnp.mean(y, axis=-1, keepdims=True)
    var = jnp.mean(jnp.square(y - mu), axis=-1, keepdims=True)
    return (y - mu) * lax.rsqrt(var + LN_EPS) * g + b


def _outproj_kernel(attn_ref, ret_ref, x_ref, gain_ref, wa_ref, wr_ref, g1_ref, b1_ref,
                    wrt_ref, brt_ref, before_ref,
                    h_ref, eid_ref, rank_ref, cw_ref, cnt_ref, carry_ref):
    first = jnp.logical_and(pl.program_id(0) == 0, pl.program_id(1) == 0)

    @pl.when(first)
    def _():
        carry_ref[...] = jnp.zeros(carry_ref.shape, F32)

    a = attn_ref[0]
    a = a * lax.rsqrt(jnp.mean(jnp.square(a), axis=0, keepdims=True) + LN_EPS) * gain_ref[...]
    mix = (lax.dot_general(a.astype(BF16), wa_ref[...], _TN, preferred_element_type=F32)
           + lax.dot_general(ret_ref[0], wr_ref[...], _TN, preferred_element_type=F32))
    h = _layer_norm_rows(ALPHA * x_ref[0] + mix, g1_ref[...], b1_ref[...])
    h_ref[...] = h

    R = ROUTER_ROWS
    h_hi = h.astype(BF16)
    h_lo = (h - h_hi.astype(F32)).astype(BF16)
    by_hi = lax.dot_general(wrt_ref[...], h_hi, _NT, preferred_element_type=F32)
    by_lo = lax.dot_general(wrt_ref[0:R, :], h_lo, _NT, preferred_element_type=F32)
    logits = by_hi[0:R] + by_hi[R:2 * R] + by_lo + brt_ref[...]
    T = logits.shape[1]
    gl = logits[0:N_GROUPS]
    gmax = jnp.max(gl, axis=0, keepdims=True)
    grow = lax.broadcasted_iota(jnp.int32, gl.shape, 0)
    gidx = jnp.min(jnp.where(gl == gmax, grow, N_GROUPS), axis=0, keepdims=True)
    g_p = 1.0 / jnp.sum(jnp.exp(gl - gmax), axis=0, keepdims=True)
    el = jnp.zeros((EXPERTS_PER_GROUP, T), F32)
    for g in range(N_GROUPS):
        r0 = ROUTER_EXPERT_ROW0 + g * EXPERTS_PER_GROUP
        el = jnp.where(gidx == g, logits[r0:r0 + EXPERTS_PER_GROUP], el)
    erow = lax.broadcasted_iota(jnp.int32, el.shape, 0)
    e1 = jnp.max(el, axis=0, keepdims=True)
    i1 = jnp.min(jnp.where(el == e1, erow, EXPERTS_PER_GROUP), axis=0, keepdims=True)
    el2 = jnp.where(erow == i1, -jnp.inf, el)
    e2 = jnp.max(el2, axis=0, keepdims=True)
    i2 = jnp.min(jnp.where(el2 == e2, erow, EXPERTS_PER_GROUP), axis=0, keepdims=True)
    r = jnp.exp(e2 - e1)
    w1 = g_p / (1.0 + r)
    w2 = g_p * r / (1.0 + r)
    id1 = gidx * EXPERTS_PER_GROUP + i1
    id2 = gidx * EXPERTS_PER_GROUP + i2
    eid_ref[0:1, :] = id1
    eid_ref[1:2, :] = id2
    cw_ref[0:1, :] = w1
    cw_ref[1:2, :] = w2

    xrow = lax.broadcasted_iota(jnp.int32, (N_EXPERTS, T), 0)
    oh1 = (xrow == id1).astype(F32)
    oh2 = (xrow == id2).astype(F32)
    both = oh1 + oh2
    seen = jnp.dot(both.astype(BF16), before_ref[...], preferred_element_type=F32) + carry_ref[...]
    rank_ref[0:1, :] = jnp.sum(oh1 * seen, axis=0, keepdims=True).astype(jnp.int32)
    rank_ref[1:2, :] = jnp.sum(oh2 * seen, axis=0, keepdims=True).astype(jnp.int32)
    carry = carry_ref[...] + jnp.sum(both, axis=1, keepdims=True)
    carry_ref[...] = carry
    cnt_ref[...] = carry.astype(jnp.int32)


def _outproj(attn_t, ret_t, x, gain, w_out, ln_g, ln_b, w_rg, b_rg, w_re, b_re):
    B, S, D = x.shape
    Tt = OUTPROJ_TOKENS
    n_t = S // Tt
    T = B * S
    wa = w_out[:ATTN_WIDTH].astype(BF16)
    wr = w_out[ATTN_WIDTH:].astype(BF16)
    e0, e1 = ROUTER_EXPERT_ROW0, ROUTER_EXPERT_ROW0 + N_EXPERTS
    wrt = jnp.zeros((ROUTER_ROWS, D), F32)
    wrt = wrt.at[:N_GROUPS].set(w_rg.T).at[e0:e1].set(w_re.T)
    wrt_hi = wrt.astype(BF16)
    wrt_lo = (wrt - wrt_hi.astype(F32)).astype(BF16)
    wrt = jnp.concatenate([wrt_hi, wrt_lo], axis=0)
    brt = jnp.zeros((ROUTER_ROWS, 1), F32)
    brt = brt.at[:N_GROUPS, 0].set(b_rg.astype(F32)).at[e0:e1, 0].set(b_re.astype(F32))
    before = (jnp.arange(Tt)[:, None] < jnp.arange(Tt)[None, :]).astype(BF16)
    const = lambda b, i: (0, 0)
    tok = lambda b, i: (0, b * n_t + i)
    return pl.pallas_call(
        _outproj_kernel,
        grid=(B, n_t),
        in_specs=[
            pl.BlockSpec((1, ATTN_WIDTH, Tt), lambda b, i: (b, 0, i)),
            pl.BlockSpec((1, RET_WIDTH, Tt), lambda b, i: (b, 0, i)),
            pl.BlockSpec((1, Tt, D), lambda b, i: (b, i, 0)),
            pl.BlockSpec((ATTN_WIDTH, 1), const),
            pl.BlockSpec((ATTN_WIDTH, D), const),
            pl.BlockSpec((RET_WIDTH, D), const),
            pl.BlockSpec((1, D), const),
            pl.BlockSpec((1, D), const),
            pl.BlockSpec((2 * ROUTER_ROWS, D), const),
            pl.BlockSpec((ROUTER_ROWS, 1), const),
            pl.BlockSpec((Tt, Tt), const),
        ],
        out_specs=[
            pl.BlockSpec((Tt, D), lambda b, i: (b * n_t + i, 0)),
            pl.BlockSpec((2, Tt), tok),
            pl.BlockSpec((2, Tt), tok),
            pl.BlockSpec((2, Tt), tok),
            pl.BlockSpec((N_EXPERTS, 1), const),
        ],
        out_shape=[
            jax.ShapeDtypeStruct((T, D), F32),
            jax.ShapeDtypeStruct((2, T), jnp.int32),
            jax.ShapeDtypeStruct((2, T), jnp.int32),
            jax.ShapeDtypeStruct((2, T), F32),
            jax.ShapeDtypeStruct((N_EXPERTS, 1), jnp.int32),
        ],
        scratch_shapes=[pltpu.VMEM((N_EXPERTS, 1), F32)],
        compiler_params=_params("arbitrary", "arbitrary"),
        name="outproj_ln_router",
    )(attn_t, ret_t, x, gain.reshape(ATTN_WIDTH, 1), wa, wr, ln_g.reshape(1, D), ln_b.reshape(1, D),
      wrt, brt, before)


def _dispatch_kernel(start_ref, end_ref, pos0_ref, pos1_ref, h_ref, xs_hbm, zero_ref, sem, zsem):
    G = ROUTE_TOKENS
    Tm = MOE_ROWS

    @pl.when(pl.program_id(0) == 0)
    def _():
        zero_ref[...] = jnp.zeros(zero_ref.shape, F32)

        def tile_clear(row0):
            row0 = pl.multiple_of(row0, Tm)
            return pltpu.make_async_copy(zero_ref, xs_hbm.at[pl.ds(row0, Tm)], zsem)

        used = end_ref[N_EXPERTS - 1]
        n_rows = xs_hbm.shape[0]
        clears = [(end_ref[e] - Tm, end_ref[e] > start_ref[e]) for e in range(N_EXPERTS)]
        clears += [(jnp.minimum(used + k * Tm, n_rows - Tm), used + k * Tm < n_rows) for k in range(N_EXPERTS)]
        for row0, cond in clears:
            @pl.when(cond)
            def _():
                tile_clear(row0).start()
        for row0, cond in clears:
            @pl.when(cond)
            def _():
                tile_clear(row0).wait()

    def issue(k, carry):
        t0 = pl.multiple_of(k * F32_SUBLANES, F32_SUBLANES)
        for u in range(F32_SUBLANES):
            for pos_ref in (pos0_ref, pos1_ref):
                pltpu.make_async_copy(h_ref.at[pl.ds(t0 + u, 1)], xs_hbm.at[pl.ds(pos_ref[t0 + u], 1)], sem).start()
        return carry

    lax.fori_loop(0, G // F32_SUBLANES, issue, 0)
    for _ in range(2):
        pltpu.make_async_copy(h_ref, xs_hbm.at[pl.ds(0, G)], sem).wait()


def _dispatch(h, pos0, pos1, starts, ends, n_rows):
    T, D = h.shape
    G = ROUTE_TOKENS
    smem_blk = pl.BlockSpec((G,), lambda i, st, en: (i,), memory_space=pltpu.SMEM)
    grid_spec = pltpu.PrefetchScalarGridSpec(
        num_scalar_prefetch=2,
        grid=(T // G,),
        in_specs=[smem_blk, smem_blk, pl.BlockSpec((G, D), lambda i, st, en: (i, 0))],
        out_specs=pl.BlockSpec(memory_space=pl.ANY),
        scratch_shapes=[pltpu.VMEM((MOE_ROWS, D), F32), pltpu.SemaphoreType.DMA(()),
                        pltpu.SemaphoreType.DMA(())],
    )
    return pl.pallas_call(
        _dispatch_kernel,
        grid_spec=grid_spec,
        out_shape=jax.ShapeDtypeStruct((n_rows, D), F32),
        compiler_params=_params("arbitrary"),
        name="moe_dispatch",
    )(starts, ends, pos0, pos1, h)


def _experts_kernel(te_ref, nv_ref, xs_ref, wgu_ref, wdn_ref, y_ref, wgu_b, wdn_b):
    i = pl.program_id(0)
    live = i < nv_ref[0]
    new_expert = jnp.logical_or(i == 0, te_ref[i] != te_ref[jnp.maximum(i - 1, 0)])

    @pl.when(jnp.logical_and(live, new_expert))
    def _():
        wgu_b[...] = wgu_ref[0].astype(BF16)
        wdn_b[...] = wdn_ref[0].astype(BF16)

    @pl.when(live)
    def _():
        gu = jnp.dot(xs_ref[...].astype(BF16), wgu_b[...], preferred_element_type=F32)
        gate, up = gu[:, :D_EXPERT], gu[:, D_EXPERT:]
        hid = gate * jax.nn.sigmoid(gate) * up
        y_ref[...] = jnp.dot(hid.astype(BF16), wdn_b[...], preferred_element_type=F32)

    @pl.when(jnp.logical_not(live))
    def _():
        y_ref[...] = jnp.zeros(y_ref.shape, F32)


def _experts(xs, w_gu, w_dn, tile_expert, n_valid):
    n_rows, D = xs.shape
    Tm = MOE_ROWS
    grid_spec = pltpu.PrefetchScalarGridSpec(
        num_scalar_prefetch=2,
        grid=(n_rows // Tm,),
        in_specs=[
            pl.BlockSpec((Tm, D), lambda i, te, nv: (jnp.minimum(i, jnp.maximum(nv[0] - 1, 0)), 0)),
            pl.BlockSpec((1, D, 2 * D_EXPERT), lambda i, te, nv: (te[i], 0, 0)),
            pl.BlockSpec((1, D_EXPERT, D), lambda i, te, nv: (te[i], 0, 0)),
        ],
        out_specs=pl.BlockSpec((Tm, D), lambda i, te, nv: (i, 0)),
        scratch_shapes=[pltpu.VMEM((D, 2 * D_EXPERT), BF16), pltpu.VMEM((D_EXPERT, D), BF16)],
    )
    return pl.pallas_call(
        _experts_kernel,
        grid_spec=grid_spec,
        out_shape=jax.ShapeDtypeStruct((n_rows, D), F32),
        compiler_params=_params("arbitrary"),
        name="moe_experts",
    )(tile_expert, n_valid, xs, w_gu, w_dn)


def _combine_kernel(pos0_ref, pos1_ref, y_hbm, h_ref, cw_ref, g_ref, b_ref, o_ref, ybuf, sem):
    G = ROUTE_TOKENS

    def issue(k, carry):
        t0 = pl.multiple_of(k * F32_SUBLANES, F32_SUBLANES)
        for u in range(F32_SUBLANES):
            for slot, pos_ref in enumerate((pos0_ref, pos1_ref)):
                pltpu.make_async_copy(y_hbm.at[pl.ds(pos_ref[t0 + u], 1)], ybuf.at[slot, pl.ds(t0 + u, 1)],
                                      sem).start()
        return carry

    lax.fori_loop(0, G // F32_SUBLANES, issue, 0)
    for slot in range(2):
        pltpu.make_async_copy(y_hbm.at[pl.ds(0, G)], ybuf.at[slot], sem).wait()

    for r in range(G // COMBINE_CHUNK):
        rows = slice(r * COMBINE_CHUNK, (r + 1) * COMBINE_CHUNK)
        cw = cw_ref[rows, :]
        ffn = ybuf[0, rows, :] * cw[:, 0:1] + ybuf[1, rows, :] * cw[:, 1:2]
        o_ref[rows, :] = _layer_norm_rows(ALPHA * h_ref[rows, :] + ffn, g_ref[...], b_ref[...])


def _combine(y, h, pos0, pos1, cw_rows, ln_g, ln_b):
    T, D = h.shape
    G = ROUTE_TOKENS
    smem_blk = pl.BlockSpec((G,), lambda i: (i,), memory_space=pltpu.SMEM)
    return pl.pallas_call(
        _combine_kernel,
        grid=(T // G,),
        in_specs=[
            smem_blk, smem_blk,
            pl.BlockSpec(memory_space=pl.ANY),
            pl.BlockSpec((G, D), lambda i: (i, 0)),
            pl.BlockSpec((G, 2), lambda i: (i, 0)),
            pl.BlockSpec((1, D), lambda i: (0, 0)),
            pl.BlockSpec((1, D), lambda i: (0, 0)),
        ],
        out_specs=pl.BlockSpec((G, D), lambda i: (i, 0)),
        scratch_shapes=[pltpu.VMEM((2, G, D), F32), pltpu.SemaphoreType.DMA(())],
        out_shape=jax.ShapeDtypeStruct((T, D), F32),
        compiler_params=_params("arbitrary"),
        name="moe_combine_ln",
    )(pos0, pos1, y, h, cw_rows, ln_g.reshape(1, D), ln_b.reshape(1, D))


def kernel(x, w_in, attn_out_gain, rel_bias_table, w_out, ln1_g, ln1_b, w_router_group,
           b_router_group, w_router_expert, b_router_expert, w_gate_up, w_down, ln2_g, ln2_b):
    B, S, D = x.shape
    assert D == D_MODEL and S % MOBA_BLOCK == 0 and S % INPROJ_TOKENS == 0
    assert w_in.shape[0] == DEPTH
    T = B * S
    h = x
    for l in range(DEPTH):
        proj_t, k_nat = _inproj(h, w_in[l])
        attn_t = _moba(proj_t, k_nat, rel_bias_table)
        ret_t = _retention(proj_t)
        h1, eid, rank, cw, counts = _outproj(
            attn_t, ret_t, h, attn_out_gain[l], w_out[l], ln1_g[l], ln1_b[l],
            w_router_group[l], b_router_group[l], w_router_expert[l], b_router_expert[l])

        Tm = MOE_ROWS
        n_tiles = (2 * T) // Tm + N_EXPERTS
        padded = ((counts[:, 0] + Tm - 1) // Tm) * Tm
        ends = jnp.cumsum(padded)
        starts = (ends - padded).astype(jnp.int32)
        tile_row0 = jnp.arange(n_tiles, dtype=jnp.int32) * Tm
        tile_expert = jnp.sum((ends[None, :] <= jnp.minimum(tile_row0, ends[-1] - Tm)[:, None]).astype(jnp.int32),
                              axis=1)
        n_valid = (ends[-1:] // Tm).astype(jnp.int32)

        expert_ids = jnp.arange(N_EXPERTS, dtype=jnp.int32)[:, None, None]
        pos = jnp.sum(jnp.where(eid[None] == expert_ids, starts[:, None, None], 0), axis=0) + rank

        xs = _dispatch(h1, pos[0], pos[1], starts, ends.astype(jnp.int32), n_tiles * Tm)
        y = _experts(xs, w_gate_up[l], w_down[l], tile_expert, n_valid)
        h2 = _combine(y, h1, pos[0], pos[1], cw.T, ln2_g[l], ln2_b[l])
        h = h2.reshape(B, S, D)
    return h
```

```python
import functools
import math

import numpy as np
import jax
import jax.numpy as jnp
from jax import lax
from jax.experimental import pallas as pl
from jax.experimental.pallas import tpu as pltpu

F32 = jnp.float32
BF16 = jnp.bfloat16

D_MODEL = 1024
ATTN_HEADS = 8
ATTN_HEAD_DIM = 64
ATTN_WIDTH = ATTN_HEADS * ATTN_HEAD_DIM
RET_HEADS = 4
RET_HEAD_DIM = 128
RET_WIDTH = RET_HEADS * RET_HEAD_DIM
N_IN_SLICES = 7
PROJT_SLICES = N_IN_SLICES - 1
HEAD_PAIR = 2 * ATTN_HEAD_DIM
BF16_SUBLANES = 16
F32_SUBLANES = 8
V_ROWS = ATTN_HEAD_DIM + BF16_SUBLANES
PROJT_V_ROW0 = (PROJT_SLICES - 1) * ATTN_WIDTH
PROJT_WIDTH = PROJT_V_ROW0 + ATTN_HEADS * V_ROWS
MOBA_BLOCK = 256
MOBA_TOPK = 3
RET_CHUNK = 256
REL_BUCKETS = 32
REL_MAX_DIST = 128
ROPE_BASE = 10000.0
N_GROUPS = 4
EXPERTS_PER_GROUP = 8
N_EXPERTS = N_GROUPS * EXPERTS_PER_GROUP
D_EXPERT = D_MODEL // 2
DEPTH = 1
ALPHA = (2.0 * DEPTH) ** 0.25
LN_EPS = 1e-5

LOG2E = math.log2(math.e)
Q_SCALE = ATTN_HEAD_DIM ** -0.5 * LOG2E

MASKED = -1e30

INPROJ_TOKENS = 1024
INPROJ_ROWS = ATTN_WIDTH
OUTPROJ_TOKENS = 512
MOBA_UNROLL = 2
MOBA_HEADS = 4
MOBA_QTILES = 4
MOE_ROWS = 512
ROUTE_TOKENS = 1024
COMBINE_CHUNK = 256
ROW_DMA_UNROLL = 32
ROUTER_EXPERT_ROW0 = 8
ROUTER_ROWS = 48

V7X_VMEM_BYTES = 64 * 1024 * 1024
VMEM_LIMIT = V7X_VMEM_BYTES * 3 // 4

_TN = (((0,), (0,)), ((), ()))
_NT = (((1,), (1,)), ((), ()))


def _params(*sem):
    return pltpu.CompilerParams(dimension_semantics=sem, vmem_limit_bytes=VMEM_LIMIT)


def _inproj_kernel(x_ref, wt_ref, wk_ref, o_ref, k_ref):
    xb = x_ref[0].astype(BF16)
    n_tok = xb.shape[0]
    ones_row = (lax.broadcasted_iota(jnp.int32, (BF16_SUBLANES, n_tok), 0) == 0).astype(BF16)
    for c in range(PROJT_SLICES):
        rows = slice(c * INPROJ_ROWS, (c + 1) * INPROJ_ROWS)
        acc = lax.dot_general(wt_ref[rows, :], xb, _NT, preferred_element_type=F32)
        if c == 0:
            acc = acc * Q_SCALE
        if c < PROJT_SLICES - 1:
            o_ref[0, rows, :] = acc.astype(BF16)
        else:
            for h in range(ATTN_HEADS):
                r0 = PROJT_V_ROW0 + h * V_ROWS
                o_ref[0, r0:r0 + ATTN_HEAD_DIM, :] = acc[h * ATTN_HEAD_DIM:(h + 1) * ATTN_HEAD_DIM].astype(BF16)
                o_ref[0, r0 + ATTN_HEAD_DIM:r0 + V_ROWS, :] = ones_row
    k_ref[0] = jnp.dot(xb, wk_ref[...], preferred_element_type=F32).astype(BF16)


def _inproj(x, w_in):
    B, S, D = x.shape
    W = ATTN_WIDTH
    w_t = jnp.concatenate([w_in[:, :W], w_in[:, 3 * W:], w_in[:, 2 * W:3 * W]], axis=1).T.astype(BF16)
    w_k = w_in[:, W:2 * W].astype(BF16)
    return pl.pallas_call(
        _inproj_kernel,
        grid=(B, S // INPROJ_TOKENS),
        in_specs=[
            pl.BlockSpec((1, INPROJ_TOKENS, D), lambda b, i: (b, i, 0)),
            pl.BlockSpec((PROJT_SLICES * W, D), lambda b, i: (0, 0)),
            pl.BlockSpec((D, W), lambda b, i: (0, 0)),
        ],
        out_specs=[
            pl.BlockSpec((1, PROJT_WIDTH, INPROJ_TOKENS), lambda b, i: (b, 0, i)),
            pl.BlockSpec((1, INPROJ_TOKENS, W), lambda b, i: (b, i, 0)),
        ],
        out_shape=[
            jax.ShapeDtypeStruct((B, PROJT_WIDTH, S), BF16),
            jax.ShapeDtypeStruct((B, S, W), BF16),
        ],
        compiler_params=_params("arbitrary", "arbitrary"),
        name="inproj",
    )(x, w_t, w_k)


def _moba_kernel(far_ref, q_ref, k_ref, v_ref, bias_ref, o_ref,
                 kmean_ref, qz_ref, sel_ref, farsel_ref,
                 s0_ref, smax0_ref, p0_ref, alpha0_ref, s1_ref, smax1_ref, p1_ref, alpha1_ref,
                 m_ref, acc_ref, *, n_blocks):
    L = MOBA_BLOCK
    Dh = ATTN_HEAD_DIM
    HG = MOBA_HEADS
    group = pl.program_id(1)
    qi0 = pl.program_id(2) * MOBA_QTILES
    last_blk = n_blocks - 1
    units = [(a, h) for a in range(MOBA_QTILES) for h in range(HG)]
    s_ref, smax_ref = (s0_ref, s1_ref), (smax0_ref, smax1_ref)
    p_ref, alpha_ref = (p0_ref, p1_ref), (alpha0_ref, alpha1_ref)

    @pl.when(qi0 == 0)
    def _():
        chunk_blocks = min(8, n_blocks)
        chunk = chunk_blocks * L
        acc = jnp.zeros((n_blocks, HG * Dh), F32)
        for c in range(n_blocks // chunk_blocks):
            blk = lax.broadcasted_iota(jnp.int32, (n_blocks, chunk), 0)
            key = lax.broadcasted_iota(jnp.int32, (n_blocks, chunk), 1)
            ind = jnp.where(blk == c * chunk_blocks + key // L, 1.0 / L, 0.0).astype(BF16)
            acc = acc + jnp.dot(ind, k_ref[0, c * chunk:(c + 1) * chunk, :], preferred_element_type=F32)
        hi = acc.astype(BF16)
        lo = (acc - hi.astype(F32)).astype(BF16)
        for pair in range(HG // 2):
            cols = slice(pair * HEAD_PAIR, (pair + 1) * HEAD_PAIR)
            kmean_ref[pair, 0:n_blocks, :] = hi[:, cols]
            kmean_ref[pair, n_blocks:2 * n_blocks, :] = lo[:, cols]

    feat = lax.broadcasted_iota(jnp.int32, (HEAD_PAIR, L), 0)
    blk = lax.broadcasted_iota(jnp.int32, (n_blocks, L), 0)
    for u, (a, h) in enumerate(units):
        qi = qi0 + a
        pair, hd = divmod(h, 2)
        past = blk < qi
        qp = q_ref[0, pair * HEAD_PAIR:(pair + 1) * HEAD_PAIR, a * L:(a + 1) * L]
        qz = jnp.where((feat >= hd * Dh) & (feat < (hd + 1) * Dh), qp, jnp.zeros_like(qp))
        qz_ref[u] = qz
        gate = jnp.dot(kmean_ref[pair], qz, preferred_element_type=F32)
        gate = gate[0:n_blocks] + gate[n_blocks:2 * n_blocks]
        gate = jnp.where(past, gate, -jnp.inf)
        chosen = jnp.zeros(gate.shape, jnp.bool_)
        for _ in range(MOBA_TOPK):
            top = jnp.max(gate, axis=0, keepdims=True)
            first = jnp.min(jnp.where(gate == top, blk, n_blocks), axis=0, keepdims=True)
            hit = blk == first
            chosen = jnp.logical_or(chosen, hit)
            gate = jnp.where(hit, -jnp.inf, gate)
        chosen = jnp.logical_and(chosen, past)
        sel_ref[u] = jnp.where(chosen, 0.0, MASKED)
        farsel_ref[u, 0:n_blocks, :] = jnp.where(jnp.logical_and(chosen, blk < qi - 1),
                                                 far_ref[group * HG + h], MASKED)
        farsel_ref[u, n_blocks:n_blocks + F32_SUBLANES, :] = jnp.zeros((F32_SUBLANES, L), F32)

    def stage_scores(u, j, tile, slot):
        pair = units[u][1] // 2
        kb = k_ref[0, pl.ds(pl.multiple_of(j * L, L), L), pair * HEAD_PAIR:(pair + 1) * HEAD_PAIR]
        s = jnp.dot(kb, qz_ref[u], preferred_element_type=F32)
        if tile is not None:
            s = s + tile
        s_ref[slot][u] = s
        smax_ref[slot][u] = jnp.max(s, axis=0, keepdims=True)

    def stage_softmax(u, row, slot):
        m_prev = m_ref[u]
        m_new = jnp.maximum(m_prev, smax_ref[slot][u] + row)
        alpha_ref[slot][u] = jnp.exp2(m_prev - m_new)
        p_ref[slot][u] = jnp.exp2(s_ref[slot][u] - (m_new - row)).astype(BF16)
        m_ref[u] = m_new

    def stage_values(u, j, slot):
        h = units[u][1]
        vb = v_ref[0, h * V_ROWS:(h + 1) * V_ROWS, pl.ds(pl.multiple_of(j * L, L), L)]
        pv = jnp.dot(vb, p_ref[slot][u], preferred_element_type=F32)
        acc_ref[u] = alpha_ref[slot][u] * acc_ref[u] + pv

    m_ref[...] = jnp.full(m_ref.shape, MASKED, F32)
    acc_ref[...] = jnp.zeros(acc_ref.shape, F32)

    zero_row = jnp.zeros((1, L), F32)
    own = [qi0 + a for a, _ in units]
    prev = [jnp.maximum(qi0 + a - 1, 0) for a, _ in units]
    for u, (a, h) in enumerate(units):
        stage_scores(u, own[u], bias_ref[0, h], 0)
        stage_scores(u, prev[u], bias_ref[1, h] + sel_ref[u, pl.ds(prev[u], 1), :], 1)
    for u in range(len(units)):
        stage_softmax(u, zero_row, 0)
    everyone = range(len(units))

    def step(t, slot, score_units=everyone, softmax_units=everyone, value_units=everyone):
        far_value = jnp.minimum(t - 2, last_blk)
        j = jnp.minimum(t, last_blk)
        jr = jnp.where(t == 0, n_blocks, jnp.minimum(t - 1, last_blk))
        for u in everyone:
            if u in score_units:
                stage_scores(u, j, None, slot)
            if u in softmax_units:
                stage_softmax(u, farsel_ref[u, pl.ds(jr, 1), :], 1 - slot)
            if u in value_units:
                stage_values(u, jnp.where(t == 0, own[u], jnp.where(t == 1, prev[u], far_value)), slot)

    def run_steps(trips):
        def unrolled_steps(k, carry):
            for s in range(MOBA_UNROLL):
                step(MOBA_UNROLL * k + s, s % 2)
            return carry
        lax.fori_loop(0, trips, unrolled_steps, 0)

    QT = MOBA_QTILES

    @pl.when(qi0 < QT)
    def _():
        n_steps = 2 + jnp.maximum(qi0 + QT - 2, 0)
        run_steps((n_steps + MOBA_UNROLL - 1) // MOBA_UNROLL)

    @pl.when(qi0 >= QT)
    def _():
        n_main = qi0 - 2
        run_steps(n_main // MOBA_UNROLL)
        for k in range(QT + 2):
            with_block = lambda first: [u for u, (a, _) in enumerate(units) if a >= first]
            step(n_main + k, k % 2, with_block(k), with_block(k - 1), with_block(k - 2))
    for u, (a, h) in enumerate(units):
        o_ref[0, h * Dh:(h + 1) * Dh, a * L:(a + 1) * L] = acc_ref[u, 0:Dh, :] / acc_ref[u, Dh:Dh + 1, :]


def _t5_bucket_np(dist):
    n = np.maximum(dist, 0)
    max_exact = REL_BUCKETS // 2
    nf = np.maximum(n, 1).astype(np.float32)
    large = max_exact + (np.log(nf / max_exact) / math.log(REL_MAX_DIST / max_exact)
                         * (REL_BUCKETS - max_exact)).astype(np.int32)
    large = np.minimum(large, REL_BUCKETS - 1)
    return np.where(n < max_exact, n, large)


def _moba(proj_t, k_nat, rel_table):
    B, _, S = proj_t.shape
    L = MOBA_BLOCK
    n_blocks = S // L
    H = ATTN_HEADS
    table_t = rel_table.astype(F32).T
    dist = np.arange(-(L - 1), 2 * L)
    onehot = jnp.asarray(_t5_bucket_np(dist))[None, None, :] == jnp.arange(REL_BUCKETS)[None, :, None]
    by_dist = jnp.sum(jnp.where(onehot, table_t[:, :, None], 0.0), axis=1) * LOG2E
    by_dist = jnp.where(jnp.asarray(dist >= 0)[None, :], by_dist, MASKED)

    def toeplitz(vec):
        W = 2 * L - 1
        flat = jnp.broadcast_to(vec[:, None, :], (H, L, W)).reshape(H, L * W)
        return flat[:, L - 1:L - 1 + L * (W - 1)].reshape(H, L, W - 1)[:, :, :L]

    bias = jnp.stack([toeplitz(by_dist[:, :2 * L - 1]), toeplitz(by_dist[:, L:])])
    assert int(_t5_bucket_np(np.array(L + 1))) == REL_BUCKETS - 1
    far = table_t[:, REL_BUCKETS - 1] * LOG2E

    HG, QT = MOBA_HEADS, MOBA_QTILES
    U = HG * QT
    q_rows, v_rows = HG * ATTN_HEAD_DIM, HG * V_ROWS
    assert ATTN_HEADS % HG == 0 and HG % 2 == 0 and n_blocks % QT == 0 and PROJT_V_ROW0 % v_rows == 0
    once = dict(pipeline_mode=pl.Buffered(1))
    grid_spec = pltpu.PrefetchScalarGridSpec(
        num_scalar_prefetch=1,
        grid=(B, ATTN_HEADS // HG, n_blocks // QT),
        in_specs=[
            pl.BlockSpec((1, q_rows, QT * L), lambda b, g, i, far: (b, g, i)),
            pl.BlockSpec((1, S, q_rows), lambda b, g, i, far: (b, 0, g), **once),
            pl.BlockSpec((1, v_rows, S), lambda b, g, i, far: (b, PROJT_V_ROW0 // v_rows + g, 0), **once),
            pl.BlockSpec((2, HG, L, L), lambda b, g, i, far: (0, g, 0, 0), **once),
        ],
        out_specs=pl.BlockSpec((1, q_rows, QT * L), lambda b, g, i, far: (b, g, i)),
        scratch_shapes=[
            pltpu.VMEM((HG // 2, 2 * n_blocks, HEAD_PAIR), BF16),
            pltpu.VMEM((U, HEAD_PAIR, L), BF16),
            pltpu.VMEM((U, n_blocks, L), F32),
            pltpu.VMEM((U, n_blocks + F32_SUBLANES, L), F32),
            *([pltpu.VMEM((U, L, L), F32), pltpu.VMEM((U, 1, L), F32),
               pltpu.VMEM((U, L, L), BF16), pltpu.VMEM((U, 1, L), F32)] * 2),
            pltpu.VMEM((U, 1, L), F32),
            pltpu.VMEM((U, V_ROWS, L), F32),
        ],
    )
    return pl.pallas_call(
        functools.partial(_moba_kernel, n_blocks=n_blocks),
        grid_spec=grid_spec,
        out_shape=jax.ShapeDtypeStruct((B, ATTN_WIDTH, S), F32),
        compiler_params=_params("arbitrary", "arbitrary", "arbitrary"),
        name="moba",
    )(far, proj_t, k_nat, proj_t, bias)


def _retention_kernel(cdec_ref, q_ref, k_ref, v_ref, g_ref, cos_ref, sin_ref,
                      dec_ref, qdec_ref, kdec_ref, o_ref, state_ref):
    c = pl.program_id(0)

    @pl.when(c == 0)
    def _():
        state_ref[...] = jnp.zeros(state_ref.shape, F32)

    cos = cos_ref[...]
    sin = sin_ref[...]
    half = RET_HEAD_DIM // 2

    def rope(t):
        t1, t2 = t[:half], t[half:]
        return jnp.concatenate([t1 * cos - t2 * sin, t1 * sin + t2 * cos], axis=0)

    for b in range(q_ref.shape[0]):
        for hh in range(RET_HEADS):
            rows = slice(hh * RET_HEAD_DIM, (hh + 1) * RET_HEAD_DIM)
            q = rope(q_ref[b, rows, :].astype(F32))
            k = rope(k_ref[b, rows, :].astype(F32)) * (RET_HEAD_DIM ** -0.5)
            v = v_ref[b, rows, :]
            qb = q.astype(BF16)
            s = lax.dot_general(k.astype(BF16), qb, _TN, preferred_element_type=F32)
            s = (s * dec_ref[hh]).astype(BF16)
            inner = jnp.dot(v, s, preferred_element_type=F32)
            state = state_ref[b, hh]
            cross = jnp.dot(state.astype(BF16), (q * qdec_ref[hh:hh + 1, :]).astype(BF16),
                            preferred_element_type=F32)
            kd = (k * kdec_ref[hh:hh + 1, :]).astype(BF16)
            state_ref[b, hh] = state * cdec_ref[hh] + lax.dot_general(v, kd, _NT, preferred_element_type=F32)
            out = inner + cross
            mu = jnp.mean(out, axis=0, keepdims=True)
            var = jnp.mean(jnp.square(out - mu), axis=0, keepdims=True)
            out = (out - mu) * lax.rsqrt(var + LN_EPS)
            g = g_ref[b, rows, :].astype(F32)
            o_ref[b, rows, :] = (g * jax.nn.sigmoid(g) * out).astype(BF16)


def _retention(proj_t):
    B, _, S = proj_t.shape
    C = RET_CHUNK
    H = RET_HEADS
    half = RET_HEAD_DIM // 2
    inv = ROPE_BASE ** (-jnp.arange(half, dtype=F32) / half)
    ang = inv[:, None] * jnp.arange(S).astype(F32)[None, :]
    cos_t, sin_t = jnp.cos(ang), jnp.sin(ang)
    gammas = 1.0 - jnp.exp(jnp.linspace(math.log(1.0 / 32), math.log(1.0 / 512), H, dtype=F32))
    log_g = jnp.log(gammas)
    idx = jnp.arange(C, dtype=F32)
    diff = idx[None, :] - idx[:, None]
    dec_t = jnp.where(diff[None] >= 0, jnp.exp(jnp.maximum(diff, 0.0)[None] * log_g[:, None, None]), 0.0)
    q_dec = jnp.exp((idx[None, :] + 1.0) * log_g[:, None])
    k_dec = jnp.exp((C - 1.0 - idx[None, :]) * log_g[:, None])
    chunk_dec = jnp.exp(C * log_g)

    grid_spec = pltpu.PrefetchScalarGridSpec(
        num_scalar_prefetch=1,
        grid=(S // C,),
        in_specs=[
            pl.BlockSpec((B, RET_WIDTH, C), lambda c, cd: (0, 1, c)),
            pl.BlockSpec((B, RET_WIDTH, C), lambda c, cd: (0, 2, c)),
            pl.BlockSpec((B, RET_WIDTH, C), lambda c, cd: (0, 3, c)),
            pl.BlockSpec((B, RET_WIDTH, C), lambda c, cd: (0, 4, c)),
            pl.BlockSpec((half, C), lambda c, cd: (0, c)),
            pl.BlockSpec((half, C), lambda c, cd: (0, c)),
            pl.BlockSpec((H, C, C), lambda c, cd: (0, 0, 0)),
            pl.BlockSpec((H, C), lambda c, cd: (0, 0)),
            pl.BlockSpec((H, C), lambda c, cd: (0, 0)),
        ],
        out_specs=pl.BlockSpec((B, RET_WIDTH, C), lambda c, cd: (0, 0, c)),
        scratch_shapes=[pltpu.VMEM((B, H, RET_HEAD_DIM, RET_HEAD_DIM), F32)],
    )
    return pl.pallas_call(
        _retention_kernel,
        grid_spec=grid_spec,
        out_shape=jax.ShapeDtypeStruct((B, RET_WIDTH, S), BF16),
        compiler_params=_params("arbitrary"),
        name="retention",
    )(chunk_dec, proj_t, proj_t, proj_t, proj_t, cos_t, sin_t, dec_t, q_dec, k_dec)


def _layer_norm_rows(y, g, b):
    mu = jnp.mean(y, axis=-1, keepdims=True)
    var = jnp.mean(jnp.square(y - mu), axis=-1, keepdims=True)
    return (y - mu) * lax.rsqrt(var + LN_EPS) * g + b


def _outproj_kernel(attn_ref, ret_ref, x_ref, gain_ref, wa_ref, wr_ref, g1_ref, b1_ref,
                    wrt_ref, brt_ref, before_ref,
                    h_ref, eid_ref, rank_ref, cw_ref, cnt_ref, carry_ref):
    first = jnp.logical_and(pl.program_id(0) == 0, pl.program_id(1) == 0)

    @pl.when(first)
    def _():
        carry_ref[...] = jnp.zeros(carry_ref.shape, F32)

    a = attn_ref[0]
    a = a * lax.rsqrt(jnp.mean(jnp.square(a), axis=0, keepdims=True) + LN_EPS) * gain_ref[...]
    mix = (lax.dot_general(a.astype(BF16), wa_ref[...], _TN, preferred_element_type=F32)
           + lax.dot_general(ret_ref[0], wr_ref[...], _TN, preferred_element_type=F32))
    h = _layer_norm_rows(ALPHA * x_ref[0] + mix, g1_ref[...], b1_ref[...])
    h_ref[...] = h

    R = ROUTER_ROWS
    h_hi = h.astype(BF16)
    h_lo = (h - h_hi.astype(F32)).astype(BF16)
    by_hi = lax.dot_general(wrt_ref[...], h_hi, _NT, preferred_element_type=F32)
    by_lo = lax.dot_general(wrt_ref[0:R, :], h_lo, _NT, preferred_element_type=F32)
    logits = by_hi[0:R] + by_hi[R:2 * R] + by_lo + brt_ref[...]
    T = logits.shape[1]
    gl = logits[0:N_GROUPS]
    gmax = jnp.max(gl, axis=0, keepdims=True)
    grow = lax.broadcasted_iota(jnp.int32, gl.shape, 0)
    gidx = jnp.min(jnp.where(gl == gmax, grow, N_GROUPS), axis=0, keepdims=True)
    g_p = 1.0 / jnp.sum(jnp.exp(gl - gmax), axis=0, keepdims=True)
    el = jnp.zeros((EXPERTS_PER_GROUP, T), F32)
    for g in range(N_GROUPS):
        r0 = ROUTER_EXPERT_ROW0 + g * EXPERTS_PER_GROUP
        el = jnp.where(gidx == g, logits[r0:r0 + EXPERTS_PER_GROUP], el)
    erow = lax.broadcasted_iota(jnp.int32, el.shape, 0)
    e1 = jnp.max(el, axis=0, keepdims=True)
    i1 = jnp.min(jnp.where(el == e1, erow, EXPERTS_PER_GROUP), axis=0, keepdims=True)
    el2 = jnp.where(erow == i1, -jnp.inf, el)
    e2 = jnp.max(el2, axis=0, keepdims=True)
    i2 = jnp.min(jnp.where(el2 == e2, erow, EXPERTS_PER_GROUP), axis=0, keepdims=True)
    r = jnp.exp(e2 - e1)
    w1 = g_p / (1.0 + r)
    w2 = g_p * r / (1.0 + r)
    id1 = gidx * EXPERTS_PER_GROUP + i1
    id2 = gidx * EXPERTS_PER_GROUP + i2
    eid_ref[0:1, :] = id1
    eid_ref[1:2, :] = id2
    cw_ref[0:1, :] = w1
    cw_ref[1:2, :] = w2

    xrow = lax.broadcasted_iota(jnp.int32, (N_EXPERTS, T), 0)
    oh1 = (xrow == id1).astype(F32)
    oh2 = (xrow == id2).astype(F32)
    both = oh1 + oh2
    seen = jnp.dot(both.astype(BF16), before_ref[...], preferred_element_type=F32) + carry_ref[...]
    rank_ref[0:1, :] = jnp.sum(oh1 * seen, axis=0, keepdims=True).astype(jnp.int32)
    rank_ref[1:2, :] = jnp.sum(oh2 * seen, axis=0, keepdims=True).astype(jnp.int32)
    carry = carry_ref[...] + jnp.sum(both, axis=1, keepdims=True)
    carry_ref[...] = carry
    cnt_ref[...] = carry.astype(jnp.int32)


def _outproj(attn_t, ret_t, x, gain, w_out, ln_g, ln_b, w_rg, b_rg, w_re, b_re):
    B, S, D = x.shape
    Tt = OUTPROJ_TOKENS
    n_t = S // Tt
    T = B * S
    wa = w_out[:ATTN_WIDTH].astype(BF16)
    wr = w_out[ATTN_WIDTH:].astype(BF16)
    e0, e1 = ROUTER_EXPERT_ROW0, ROUTER_EXPERT_ROW0 + N_EXPERTS
    wrt = jnp.zeros((ROUTER_ROWS, D), F32)
    wrt = wrt.at[:N_GROUPS].set(w_rg.T).at[e0:e1].set(w_re.T)
    wrt_hi = wrt.astype(BF16)
    wrt_lo = (wrt - wrt_hi.astype(F32)).astype(BF16)
    wrt = jnp.concatenate([wrt_hi, wrt_lo], axis=0)
    brt = jnp.zeros((ROUTER_ROWS, 1), F32)
    brt = brt.at[:N_GROUPS, 0].set(b_rg.astype(F32)).at[e0:e1, 0].set(b_re.astype(F32))
    before = (jnp.arange(Tt)[:, None] < jnp.arange(Tt)[None, :]).astype(BF16)
    const = lambda b, i: (0, 0)
    tok = lambda b, i: (0, b * n_t + i)
    return pl.pallas_call(
        _outproj_kernel,
        grid=(B, n_t),
        in_specs=[
            pl.BlockSpec((1, ATTN_WIDTH, Tt), lambda b, i: (b, 0, i)),
            pl.BlockSpec((1, RET_WIDTH, Tt), lambda b, i: (b, 0, i)),
            pl.BlockSpec((1, Tt, D), lambda b, i: (b, i, 0)),
            pl.BlockSpec((ATTN_WIDTH, 1), const),
            pl.BlockSpec((ATTN_WIDTH, D), const),
            pl.BlockSpec((RET_WIDTH, D), const),
            pl.BlockSpec((1, D), const),
            pl.BlockSpec((1, D), const),
            pl.BlockSpec((2 * ROUTER_ROWS, D), const),
            pl.BlockSpec((ROUTER_ROWS, 1), const),
            pl.BlockSpec((Tt, Tt), const),
        ],
        out_specs=[
            pl.BlockSpec((Tt, D), lambda b, i: (b * n_t + i, 0)),
            pl.BlockSpec((2, Tt), tok),
            pl.BlockSpec((2, Tt), tok),
            pl.BlockSpec((2, Tt), tok),
            pl.BlockSpec((N_EXPERTS, 1), const),
        ],
        out_shape=[
            jax.ShapeDtypeStruct((T, D), F32),
            jax.ShapeDtypeStruct((2, T), jnp.int32),
            jax.ShapeDtypeStruct((2, T), jnp.int32),
            jax.ShapeDtypeStruct((2, T), F32),
            jax.ShapeDtypeStruct((N_EXPERTS, 1), jnp.int32),
        ],
        scratch_shapes=[pltpu.VMEM((N_EXPERTS, 1), F32)],
        compiler_params=_params("arbitrary", "arbitrary"),
        name="outproj_ln_router",
    )(attn_t, ret_t, x, gain.reshape(ATTN_WIDTH, 1), wa, wr, ln_g.reshape(1, D), ln_b.reshape(1, D),
      wrt, brt, before)


def _dispatch_kernel(start_ref, end_ref, pos0_ref, pos1_ref, h_ref, xs_hbm, zero_ref, sem, zsem):
    G = ROUTE_TOKENS
    Tm = MOE_ROWS

    @pl.when(pl.program_id(0) == 0)
    def _():
        zero_ref[...] = jnp.zeros(zero_ref.shape, F32)

        def tile_clear(row0):
            row0 = pl.multiple_of(row0, Tm)
            return pltpu.make_async_copy(zero_ref, xs_hbm.at[pl.ds(row0, Tm)], zsem)

        used = end_ref[N_EXPERTS - 1]
        n_rows = xs_hbm.shape[0]
        clears = [(end_ref[e] - Tm, end_ref[e] > start_ref[e]) for e in range(N_EXPERTS)]
        clears += [(jnp.minimum(used + k * Tm, n_rows - Tm), used + k * Tm < n_rows) for k in range(N_EXPERTS)]
        for row0, cond in clears:
            @pl.when(cond)
            def _():
                tile_clear(row0).start()
        for row0, cond in clears:
            @pl.when(cond)
            def _():
                tile_clear(row0).wait()

    def issue(k, carry):
        t0 = pl.multiple_of(k * ROW_DMA_UNROLL, ROW_DMA_UNROLL)
        for u in range(ROW_DMA_UNROLL):
            for pos_ref in (pos0_ref, pos1_ref):
                pltpu.make_async_copy(h_ref.at[pl.ds(t0 + u, 1)], xs_hbm.at[pl.ds(pos_ref[t0 + u], 1)], sem).start()
        return carry

    lax.fori_loop(0, G // ROW_DMA_UNROLL, issue, 0)
    for _ in range(2):
        pltpu.make_async_copy(h_ref, xs_hbm.at[pl.ds(0, G)], sem).wait()


def _dispatch(h, pos0, pos1, starts, ends, n_rows):
    T, D = h.shape
    G = ROUTE_TOKENS
    smem_blk = pl.BlockSpec((G,), lambda i, st, en: (i,), memory_space=pltpu.SMEM)
    grid_spec = pltpu.PrefetchScalarGridSpec(
        num_scalar_prefetch=2,
        grid=(T // G,),
        in_specs=[smem_blk, smem_blk, pl.BlockSpec((G, D), lambda i, st, en: (i, 0))],
        out_specs=pl.BlockSpec(memory_space=pl.ANY),
        scratch_shapes=[pltpu.VMEM((MOE_ROWS, D), F32), pltpu.SemaphoreType.DMA(()),
                        pltpu.SemaphoreType.DMA(())],
    )
    return pl.pallas_call(
        _dispatch_kernel,
        grid_spec=grid_spec,
        out_shape=jax.ShapeDtypeStruct((n_rows, D), F32),
        compiler_params=_params("arbitrary"),
        name="moe_dispatch",
    )(starts, ends, pos0, pos1, h)


def _experts_kernel(te_ref, nv_ref, xs_ref, wgu_ref, wdn_ref, y_ref, wgu_b, wdn_b):
    i = pl.program_id(0)
    live = i < nv_ref[0]
    new_expert = jnp.logical_or(i == 0, te_ref[i] != te_ref[jnp.maximum(i - 1, 0)])

    @pl.when(jnp.logical_and(live, new_expert))
    def _():
        wgu_b[...] = wgu_ref[0].astype(BF16)
        wdn_b[...] = wdn_ref[0].astype(BF16)

    @pl.when(live)
    def _():
        gu = jnp.dot(xs_ref[...].astype(BF16), wgu_b[...], preferred_element_type=F32)
        gate, up = gu[:, :D_EXPERT], gu[:, D_EXPERT:]
        hid = gate * jax.nn.sigmoid(gate) * up
        y_ref[...] = jnp.dot(hid.astype(BF16), wdn_b[...], preferred_element_type=F32)

    @pl.when(jnp.logical_not(live))
    def _():
        y_ref[...] = jnp.zeros(y_ref.shape, F32)


def _experts(xs, w_gu, w_dn, tile_expert, n_valid):
    n_rows, D = xs.shape
    Tm = MOE_ROWS
    grid_spec = pltpu.PrefetchScalarGridSpec(
        num_scalar_prefetch=2,
        grid=(n_rows // Tm,),
        in_specs=[
            pl.BlockSpec((Tm, D), lambda i, te, nv: (jnp.minimum(i, jnp.maximum(nv[0] - 1, 0)), 0)),
            pl.BlockSpec((1, D, 2 * D_EXPERT), lambda i, te, nv: (te[i], 0, 0)),
            pl.BlockSpec((1, D_EXPERT, D), lambda i, te, nv: (te[i], 0, 0)),
        ],
        out_specs=pl.BlockSpec((Tm, D), lambda i, te, nv: (i, 0)),
        scratch_shapes=[pltpu.VMEM((D, 2 * D_EXPERT), BF16), pltpu.VMEM((D_EXPERT, D), BF16)],
    )
    return pl.pallas_call(
        _experts_kernel,
        grid_spec=grid_spec,
        out_shape=jax.ShapeDtypeStruct((n_rows, D), F32),
        compiler_params=_params("arbitrary"),
        name="moe_experts",
    )(tile_expert, n_valid, xs, w_gu, w_dn)


def _combine_kernel(pos0_ref, pos1_ref, y_hbm, h_ref, cw_ref, g_ref, b_ref, o_ref, ybuf, sem):
    G = ROUTE_TOKENS

    def issue(k, carry):
        t0 = pl.multiple_of(k * ROW_DMA_UNROLL, ROW_DMA_UNROLL)
        for u in range(ROW_DMA_UNROLL):
            for slot, pos_ref in enumerate((pos0_ref, pos1_ref)):
                pltpu.make_async_copy(y_hbm.at[pl.ds(pos_ref[t0 + u], 1)], ybuf.at[slot, pl.ds(t0 + u, 1)],
                                      sem).start()
        return carry

    lax.fori_loop(0, G // ROW_DMA_UNROLL, issue, 0)
    for slot in range(2):
        pltpu.make_async_copy(y_hbm.at[pl.ds(0, G)], ybuf.at[slot], sem).wait()

    for r in range(G // COMBINE_CHUNK):
        rows = slice(r * COMBINE_CHUNK, (r + 1) * COMBINE_CHUNK)
        cw = cw_ref[rows, :]
        ffn = ybuf[0, rows, :] * cw[:, 0:1] + ybuf[1, rows, :] * cw[:, 1:2]
        o_ref[rows, :] = _layer_norm_rows(ALPHA * h_ref[rows, :] + ffn, g_ref[...], b_ref[...])


def _combine(y, h, pos0, pos1, cw_rows, ln_g, ln_b):
    T, D = h.shape
    G = ROUTE_TOKENS
    smem_blk = pl.BlockSpec((G,), lambda i: (i,), memory_space=pltpu.SMEM)
    return pl.pallas_call(
        _combine_kernel,
        grid=(T // G,),
        in_specs=[
            smem_blk, smem_blk,
            pl.BlockSpec(memory_space=pl.ANY),
            pl.BlockSpec((G, D), lambda i: (i, 0)),
            pl.BlockSpec((G, 2), lambda i: (i, 0)),
            pl.BlockSpec((1, D), lambda i: (0, 0)),
            pl.BlockSpec((1, D), lambda i: (0, 0)),
        ],
        out_specs=pl.BlockSpec((G, D), lambda i: (i, 0)),
        scratch_shapes=[pltpu.VMEM((2, G, D), F32), pltpu.SemaphoreType.DMA(())],
        out_shape=jax.ShapeDtypeStruct((T, D), F32),
        compiler_params=_params("arbitrary"),
        name="moe_combine_ln",
    )(pos0, pos1, y, h, cw_rows, ln_g.reshape(1, D), ln_b.reshape(1, D))


def kernel(x, w_in, attn_out_gain, rel_bias_table, w_out, ln1_g, ln1_b, w_router_group,
           b_router_group, w_router_expert, b_router_expert, w_gate_up, w_down, ln2_g, ln2_b):
    B, S, D = x.shape
    assert D == D_MODEL and S % MOBA_BLOCK == 0 and S % INPROJ_TOKENS == 0
    assert w_in.shape[0] == DEPTH
    T = B * S
    h = x
    for l in range(DEPTH):
        proj_t, k_nat = _inproj(h, w_in[l])
        attn_t = _moba(proj_t, k_nat, rel_bias_table)
        ret_t = _retention(proj_t)
        h1, eid, rank, cw, counts = _outproj(
            attn_t, ret_t, h, attn_out_gain[l], w_out[l], ln1_g[l], ln1_b[l],
            w_router_group[l], b_router_group[l], w_router_expert[l], b_router_expert[l])

        Tm = MOE_ROWS
        n_tiles = (2 * T) // Tm + N_EXPERTS
        padded = ((counts[:, 0] + Tm - 1) // Tm) * Tm
        ends = jnp.cumsum(padded)
        starts = (ends - padded).astype(jnp.int32)
        tile_row0 = jnp.arange(n_tiles, dtype=jnp.int32) * Tm
        tile_expert = jnp.sum((ends[None, :] <= jnp.minimum(tile_row0, ends[-1] - Tm)[:, None]).astype(jnp.int32),
                              axis=1)
        n_valid = (ends[-1:] // Tm).astype(jnp.int32)

        expert_ids = jnp.arange(N_EXPERTS, dtype=jnp.int32)[:, None, None]
        pos = jnp.sum(jnp.where(eid[None] == expert_ids, starts[:, None, None], 0), axis=0) + rank

        xs = _dispatch(h1, pos[0], pos[1], starts, ends.astype(jnp.int32), n_tiles * Tm)
        y = _experts(xs, w_gate_up[l], w_down[l], tile_expert, n_valid)
        h2 = _combine(y, h1, pos[0], pos[1], cw.T, ln2_g[l], ln2_b[l])
        h = h2.reshape(B, S, D)
    return h
```

```python
import functools
import math

import numpy as np
import jax
import jax.numpy as jnp
from jax import lax
from jax.experimental import pallas as pl
from jax.experimental.pallas import tpu as pltpu

F32 = jnp.float32
BF16 = jnp.bfloat16

D_MODEL = 1024
ATTN_HEADS = 8
ATTN_HEAD_DIM = 64
ATTN_WIDTH = ATTN_HEADS * ATTN_HEAD_DIM
RET_HEADS = 4
RET_HEAD_DIM = 128
RET_WIDTH = RET_HEADS * RET_HEAD_DIM
N_IN_SLICES = 7
PROJT_SLICES = N_IN_SLICES - 1
HEAD_PAIR = 2 * ATTN_HEAD_DIM
BF16_SUBLANES = 16
F32_SUBLANES = 8
V_ROWS = ATTN_HEAD_DIM + BF16_SUBLANES
PROJT_V_ROW0 = (PROJT_SLICES - 1) * ATTN_WIDTH
PROJT_WIDTH = PROJT_V_ROW0 + ATTN_HEADS * V_ROWS
MOBA_BLOCK = 256
MOBA_TOPK = 3
RET_CHUNK = 256
REL_BUCKETS = 32
REL_MAX_DIST = 128
ROPE_BASE = 10000.0
N_GROUPS = 4
EXPERTS_PER_GROUP = 8
N_EXPERTS = N_GROUPS * EXPERTS_PER_GROUP
D_EXPERT = D_MODEL // 2
DEPTH = 1
ALPHA = (2.0 * DEPTH) ** 0.25
LN_EPS = 1e-5

LOG2E = math.log2(math.e)
Q_SCALE = ATTN_HEAD_DIM ** -0.5 * LOG2E

MASKED = -1e30

INPROJ_TOKENS = 1024
INPROJ_ROWS = ATTN_WIDTH
OUTPROJ_TOKENS = 512
MOBA_UNROLL = 2
MOBA_HEADS = 4
MOBA_QTILES = 4
MOE_ROWS = 512
ROUTE_TOKENS = 1024
COMBINE_CHUNK = 256
ROW_DMA_UNROLL = 32
ROUTER_EXPERT_ROW0 = 8
ROUTER_ROWS = 48

V7X_VMEM_BYTES = 64 * 1024 * 1024
VMEM_LIMIT = V7X_VMEM_BYTES * 3 // 4

_TN = (((0,), (0,)), ((), ()))
_NT = (((1,), (1,)), ((), ()))


def _params(*sem):
    return pltpu.CompilerParams(dimension_semantics=sem, vmem_limit_bytes=VMEM_LIMIT)


def _inproj_kernel(x_ref, wt_ref, wk_ref, o_ref, k_ref):
    xb = x_ref[0].astype(BF16)
    n_tok = xb.shape[0]
    ones_row = (lax.broadcasted_iota(jnp.int32, (BF16_SUBLANES, n_tok), 0) == 0).astype(BF16)
    for c in range(PROJT_SLICES):
        rows = slice(c * INPROJ_ROWS, (c + 1) * INPROJ_ROWS)
        acc = lax.dot_general(wt_ref[rows, :], xb, _NT, preferred_element_type=F32)
        if c == 0:
            acc = acc * Q_SCALE
        if c < PROJT_SLICES - 1:
            o_ref[0, rows, :] = acc.astype(BF16)
        else:
            for h in range(ATTN_HEADS):
                r0 = PROJT_V_ROW0 + h * V_ROWS
                o_ref[0, r0:r0 + ATTN_HEAD_DIM, :] = acc[h * ATTN_HEAD_DIM:(h + 1) * ATTN_HEAD_DIM].astype(BF16)
                o_ref[0, r0 + ATTN_HEAD_DIM:r0 + V_ROWS, :] = ones_row
    k_ref[0] = jnp.dot(xb, wk_ref[...], preferred_element_type=F32).astype(BF16)


def _inproj(x, w_in):
    B, S, D = x.shape
    W = ATTN_WIDTH
    w_t = jnp.concatenate([w_in[:, :W], w_in[:, 3 * W:], w_in[:, 2 * W:3 * W]], axis=1).T.astype(BF16)
    w_k = w_in[:, W:2 * W].astype(BF16)
    return pl.pallas_call(
        _inproj_kernel,
        grid=(B, S // INPROJ_TOKENS),
        in_specs=[
            pl.BlockSpec((1, INPROJ_TOKENS, D), lambda b, i: (b, i, 0)),
            pl.BlockSpec((PROJT_SLICES * W, D), lambda b, i: (0, 0)),
            pl.BlockSpec((D, W), lambda b, i: (0, 0)),
        ],
        out_specs=[
            pl.BlockSpec((1, PROJT_WIDTH, INPROJ_TOKENS), lambda b, i: (b, 0, i)),
            pl.BlockSpec((1, INPROJ_TOKENS, W), lambda b, i: (b, i, 0)),
        ],
        out_shape=[
            jax.ShapeDtypeStruct((B, PROJT_WIDTH, S), BF16),
            jax.ShapeDtypeStruct((B, S, W), BF16),
        ],
        compiler_params=_params("arbitrary", "arbitrary"),
        name="inproj",
    )(x, w_t, w_k)


def _moba_kernel(far_ref, q_ref, k_ref, v_ref, bias_ref, o_ref,
                 kmean_ref, qz_ref, sel_ref, farsel_ref,
                 s0_ref, smax0_ref, p0_ref, alpha0_ref, s1_ref, smax1_ref, p1_ref, alpha1_ref,
                 m_ref, acc_ref, *, n_blocks):
    L = MOBA_BLOCK
    Dh = ATTN_HEAD_DIM
    HG = MOBA_HEADS
    group = pl.program_id(1)
    qi0 = pl.program_id(2) * MOBA_QTILES
    last_blk = n_blocks - 1
    units = [(a, h) for a in range(MOBA_QTILES) for h in range(HG)]
    s_ref, smax_ref = (s0_ref, s1_ref), (smax0_ref, smax1_ref)
    p_ref, alpha_ref = (p0_ref, p1_ref), (alpha0_ref, alpha1_ref)

    @pl.when(qi0 == 0)
    def _():
        chunk_blocks = min(8, n_blocks)
        chunk = chunk_blocks * L
        acc = jnp.zeros((n_blocks, HG * Dh), F32)
        for c in range(n_blocks // chunk_blocks):
            blk = lax.broadcasted_iota(jnp.int32, (n_blocks, chunk), 0)
            key = lax.broadcasted_iota(jnp.int32, (n_blocks, chunk), 1)
            ind = jnp.where(blk == c * chunk_blocks + key // L, 1.0 / L, 0.0).astype(BF16)
            acc = acc + jnp.dot(ind, k_ref[0, c * chunk:(c + 1) * chunk, :], preferred_element_type=F32)
        hi = acc.astype(BF16)
        lo = (acc - hi.astype(F32)).astype(BF16)
        for pair in range(HG // 2):
            cols = slice(pair * HEAD_PAIR, (pair + 1) * HEAD_PAIR)
            kmean_ref[pair, 0:n_blocks, :] = hi[:, cols]
            kmean_ref[pair, n_blocks:2 * n_blocks, :] = lo[:, cols]

    blk = lax.broadcasted_iota(jnp.int32, (n_blocks, L), 0)
    for u, (a, h) in enumerate(units):
        qi = qi0 + a
        pair, hd = divmod(h, 2)
        past = blk < qi
        qh = q_ref[0, h * Dh:(h + 1) * Dh, a * L:(a + 1) * L]
        no_q = jnp.zeros_like(qh)
        qz = jnp.concatenate([qh, no_q] if hd == 0 else [no_q, qh], axis=0)
        qz_ref[u] = qz
        gate = jnp.dot(kmean_ref[pair], qz, preferred_element_type=F32)
        gate = gate[0:n_blocks] + gate[n_blocks:2 * n_blocks]
        gate = jnp.where(past, gate, -jnp.inf)
        for _ in range(MOBA_TOPK):
            top = jnp.max(gate, axis=0, keepdims=True)
            first = jnp.min(jnp.where(gate == top, blk, n_blocks), axis=0, keepdims=True)
            gate = jnp.where(blk == first, -jnp.inf, gate)
        knocked_out = gate == -jnp.inf
        sel_ref[u] = jnp.where(knocked_out, jnp.where(past, 0.0, MASKED), MASKED)
        far_row = jnp.where(blk < qi - 1, far_ref[group * HG + h], MASKED)
        farsel_ref[u, 0:n_blocks, :] = jnp.where(knocked_out, far_row, MASKED)
        farsel_ref[u, n_blocks:n_blocks + F32_SUBLANES, :] = jnp.zeros((F32_SUBLANES, L), F32)

    def stage_scores(u, j, tile, slot):
        pair = units[u][1] // 2
        kb = k_ref[0, pl.ds(pl.multiple_of(j * L, L), L), pair * HEAD_PAIR:(pair + 1) * HEAD_PAIR]
        s = jnp.dot(kb, qz_ref[u], preferred_element_type=F32)
        if tile is not None:
            s = s + tile
        s_ref[slot][u] = s
        smax_ref[slot][u] = jnp.max(s, axis=0, keepdims=True)

    def stage_softmax(u, row, slot):
        m_prev = m_ref[u]
        m_new = jnp.maximum(m_prev, smax_ref[slot][u] + row)
        alpha_ref[slot][u] = jnp.exp2(m_prev - m_new)
        p_ref[slot][u] = jnp.exp2(s_ref[slot][u] - (m_new - row)).astype(BF16)
        m_ref[u] = m_new

    def stage_values(u, j, slot):
        h = units[u][1]
        vb = v_ref[0, h * V_ROWS:(h + 1) * V_ROWS, pl.ds(pl.multiple_of(j * L, L), L)]
        pv = jnp.dot(vb, p_ref[slot][u], preferred_element_type=F32)
        acc_ref[u] = alpha_ref[slot][u] * acc_ref[u] + pv

    m_ref[...] = jnp.full(m_ref.shape, MASKED, F32)
    acc_ref[...] = jnp.zeros(acc_ref.shape, F32)

    zero_row = jnp.zeros((1, L), F32)
    own = [qi0 + a for a, _ in units]
    prev = [jnp.maximum(qi0 + a - 1, 0) for a, _ in units]
    for u, (a, h) in enumerate(units):
        stage_scores(u, own[u], bias_ref[0, h], 0)
        stage_scores(u, prev[u], bias_ref[1, h] + sel_ref[u, pl.ds(prev[u], 1), :], 1)
    for u in range(len(units)):
        stage_softmax(u, zero_row, 0)
    everyone = range(len(units))

    def step(t, slot, score_units=everyone, softmax_units=everyone, value_units=everyone):
        far_value = jnp.minimum(t - 2, last_blk)
        j = jnp.minimum(t, last_blk)
        jr = jnp.where(t == 0, n_blocks, jnp.minimum(t - 1, last_blk))
        for u in everyone:
            if u in score_units:
                stage_scores(u, j, None, slot)
            if u in softmax_units:
                stage_softmax(u, farsel_ref[u, pl.ds(jr, 1), :], 1 - slot)
            if u in value_units:
                stage_values(u, jnp.where(t == 0, own[u], jnp.where(t == 1, prev[u], far_value)), slot)

    def run_steps(trips, unroll):
        def unrolled_steps(k, carry):
            for s in range(unroll):
                step(unroll * k + s, s % 2)
            return carry
        lax.fori_loop(0, trips, unrolled_steps, 0)

    QT = MOBA_QTILES

    @pl.when(qi0 < QT)
    def _():
        n_steps = 2 + jnp.maximum(qi0 + QT - 2, 0)
        run_steps((n_steps + 1) // 2, 2)

    @pl.when(qi0 >= QT)
    def _():
        n_main = qi0 - 2
        left = (MOBA_UNROLL - 2) % MOBA_UNROLL
        run_steps((n_main - left) // MOBA_UNROLL, MOBA_UNROLL)
        for k in range(-left, QT + 2):
            with_block = lambda first: [u for u, (a, _) in enumerate(units) if a >= first]
            step(n_main + k, k % 2, with_block(k), with_block(k - 1), with_block(k - 2))
    for u, (a, h) in enumerate(units):
        o_ref[0, h * Dh:(h + 1) * Dh, a * L:(a + 1) * L] = acc_ref[u, 0:Dh, :] / acc_ref[u, Dh:Dh + 1, :]


def _t5_bucket_np(dist):
    n = np.maximum(dist, 0)
    max_exact = REL_BUCKETS // 2
    nf = np.maximum(n, 1).astype(np.float32)
    large = max_exact + (np.log(nf / max_exact) / math.log(REL_MAX_DIST / max_exact)
                         * (REL_BUCKETS - max_exact)).astype(np.int32)
    large = np.minimum(large, REL_BUCKETS - 1)
    return np.where(n < max_exact, n, large)


def _moba(proj_t, k_nat, rel_table):
    B, _, S = proj_t.shape
    L = MOBA_BLOCK
    n_blocks = S // L
    H = ATTN_HEADS
    table_t = rel_table.astype(F32).T
    dist = np.arange(-(L - 1), 2 * L)
    onehot = jnp.asarray(_t5_bucket_np(dist))[None, None, :] == jnp.arange(REL_BUCKETS)[None, :, None]
    by_dist = jnp.sum(jnp.where(onehot, table_t[:, :, None], 0.0), axis=1) * LOG2E
    by_dist = jnp.where(jnp.asarray(dist >= 0)[None, :], by_dist, MASKED)

    def toeplitz(vec):
        W = 2 * L - 1
        flat = jnp.broadcast_to(vec[:, None, :], (H, L, W)).reshape(H, L * W)
        return flat[:, L - 1:L - 1 + L * (W - 1)].reshape(H, L, W - 1)[:, :, :L]

    bias = jnp.stack([toeplitz(by_dist[:, :2 * L - 1]), toeplitz(by_dist[:, L:])])
    assert int(_t5_bucket_np(np.array(L + 1))) == REL_BUCKETS - 1
    far = table_t[:, REL_BUCKETS - 1] * LOG2E

    HG, QT = MOBA_HEADS, MOBA_QTILES
    U = HG * QT
    q_rows, v_rows = HG * ATTN_HEAD_DIM, HG * V_ROWS
    assert ATTN_HEADS % HG == 0 and HG % 2 == 0 and n_blocks % QT == 0 and PROJT_V_ROW0 % v_rows == 0
    assert MOBA_UNROLL % 2 == 0 and QT % MOBA_UNROLL == 0
    once = dict(pipeline_mode=pl.Buffered(1))
    grid_spec = pltpu.PrefetchScalarGridSpec(
        num_scalar_prefetch=1,
        grid=(B, ATTN_HEADS // HG, n_blocks // QT),
        in_specs=[
            pl.BlockSpec((1, q_rows, QT * L), lambda b, g, i, far: (b, g, i)),
            pl.BlockSpec((1, S, q_rows), lambda b, g, i, far: (b, 0, g), **once),
            pl.BlockSpec((1, v_rows, S), lambda b, g, i, far: (b, PROJT_V_ROW0 // v_rows + g, 0), **once),
            pl.BlockSpec((2, HG, L, L), lambda b, g, i, far: (0, g, 0, 0), **once),
        ],
        out_specs=pl.BlockSpec((1, q_rows, QT * L), lambda b, g, i, far: (b, g, i)),
        scratch_shapes=[
            pltpu.VMEM((HG // 2, 2 * n_blocks, HEAD_PAIR), BF16),
            pltpu.VMEM((U, HEAD_PAIR, L), BF16),
            pltpu.VMEM((U, n_blocks, L), F32),
            pltpu.VMEM((U, n_blocks + F32_SUBLANES, L), F32),
            *([pltpu.VMEM((U, L, L), F32), pltpu.VMEM((U, 1, L), F32),
               pltpu.VMEM((U, L, L), BF16), pltpu.VMEM((U, 1, L), F32)] * 2),
            pltpu.VMEM((U, 1, L), F32),
            pltpu.VMEM((U, V_ROWS, L), F32),
        ],
    )
    return pl.pallas_call(
        functools.partial(_moba_kernel, n_blocks=n_blocks),
        grid_spec=grid_spec,
        out_shape=jax.ShapeDtypeStruct((B, ATTN_WIDTH, S), F32),
        compiler_params=_params("arbitrary", "arbitrary", "arbitrary"),
        name="moba",
    )(far, proj_t, k_nat, proj_t, bias)


def _retention_kernel(cdec_ref, q_ref, k_ref, v_ref, g_ref, cos_ref, sin_ref,
                      dec_ref, qdec_ref, kdec_ref, o_ref, state_ref):
    c = pl.program_id(0)

    @pl.when(c == 0)
    def _():
        state_ref[...] = jnp.zeros(state_ref.shape, F32)

    cos = cos_ref[...]
    sin = sin_ref[...]
    half = RET_HEAD_DIM // 2

    def rope(t):
        t1, t2 = t[:half], t[half:]
        return jnp.concatenate([t1 * cos - t2 * sin, t1 * sin + t2 * cos], axis=0)

    for b in range(q_ref.shape[0]):
        for hh in range(RET_HEADS):
            rows = slice(hh * RET_HEAD_DIM, (hh + 1) * RET_HEAD_DIM)
            q = rope(q_ref[b, rows, :].astype(F32))
            k = rope(k_ref[b, rows, :].astype(F32)) * (RET_HEAD_DIM ** -0.5)
            v = v_ref[b, rows, :]
            qb = q.astype(BF16)
            s = lax.dot_general(k.astype(BF16), qb, _TN, preferred_element_type=F32)
            s = (s * dec_ref[hh]).astype(BF16)
            inner = jnp.dot(v, s, preferred_element_type=F32)
            state = state_ref[b, hh]
            cross = jnp.dot(state.astype(BF16), (q * qdec_ref[hh:hh + 1, :]).astype(BF16),
                            preferred_element_type=F32)
            kd = (k * kdec_ref[hh:hh + 1, :]).astype(BF16)
            state_ref[b, hh] = state * cdec_ref[hh] + lax.dot_general(v, kd, _NT, preferred_element_type=F32)
            out = inner + cross
            mu = jnp.mean(out, axis=0, keepdims=True)
            var = jnp.mean(jnp.square(out - mu), axis=0, keepdims=True)
            out = (out - mu) * lax.rsqrt(var + LN_EPS)
            g = g_ref[b, rows, :].astype(F32)
            o_ref[b, rows, :] = (g * jax.nn.sigmoid(g) * out).astype(BF16)


def _retention(proj_t):
    B, _, S = proj_t.shape
    C = RET_CHUNK
    H = RET_HEADS
    half = RET_HEAD_DIM // 2
    inv = ROPE_BASE ** (-jnp.arange(half, dtype=F32) / half)
    ang = inv[:, None] * jnp.arange(S).astype(F32)[None, :]
    cos_t, sin_t = jnp.cos(ang), jnp.sin(ang)
    gammas = 1.0 - jnp.exp(jnp.linspace(math.log(1.0 / 32), math.log(1.0 / 512), H, dtype=F32))
    log_g = jnp.log(gammas)
    idx = jnp.arange(C, dtype=F32)
    diff = idx[None, :] - idx[:, None]
    dec_t = jnp.where(diff[None] >= 0, jnp.exp(jnp.maximum(diff, 0.0)[None] * log_g[:, None, None]), 0.0)
    q_dec = jnp.exp((idx[None, :] + 1.0) * log_g[:, None])
    k_dec = jnp.exp((C - 1.0 - idx[None, :]) * log_g[:, None])
    chunk_dec = jnp.exp(C * log_g)

    grid_spec = pltpu.PrefetchScalarGridSpec(
        num_scalar_prefetch=1,
        grid=(S // C,),
        in_specs=[
            pl.BlockSpec((B, RET_WIDTH, C), lambda c, cd: (0, 1, c)),
            pl.BlockSpec((B, RET_WIDTH, C), lambda c, cd: (0, 2, c)),
            pl.BlockSpec((B, RET_WIDTH, C), lambda c, cd: (0, 3, c)),
            pl.BlockSpec((B, RET_WIDTH, C), lambda c, cd: (0, 4, c)),
            pl.BlockSpec((half, C), lambda c, cd: (0, c)),
            pl.BlockSpec((half, C), lambda c, cd: (0, c)),
            pl.BlockSpec((H, C, C), lambda c, cd: (0, 0, 0)),
            pl.BlockSpec((H, C), lambda c, cd: (0, 0)),
            pl.BlockSpec((H, C), lambda c, cd: (0, 0)),
        ],
        out_specs=pl.BlockSpec((B, RET_WIDTH, C), lambda c, cd: (0, 0, c)),
        scratch_shapes=[pltpu.VMEM((B, H, RET_HEAD_DIM, RET_HEAD_DIM), F32)],
    )
    return pl.pallas_call(
        _retention_kernel,
        grid_spec=grid_spec,
        out_shape=jax.ShapeDtypeStruct((B, RET_WIDTH, S), BF16),
        compiler_params=_params("arbitrary"),
        name="retention",
    )(chunk_dec, proj_t, proj_t, proj_t, proj_t, cos_t, sin_t, dec_t, q_dec, k_dec)


def _layer_norm_rows(y, g, b):
    mu = jnp.mean(y, axis=-1, keepdims=True)
    var = jnp.mean(jnp.square(y - mu), axis=-1, keepdims=True)
    return (y - mu) * lax.rsqrt(var + LN_EPS) * g + b


def _outproj_kernel(attn_ref, ret_ref, x_ref, gain_ref, wa_ref, wr_ref, g1_ref, b1_ref,
                    wrt_ref, brt_ref, before_ref,
                    h_ref, eid_ref, rank_ref, cw_ref, cnt_ref, carry_ref):
    first = jnp.logical_and(pl.program_id(0) == 0, pl.program_id(1) == 0)

    @pl.when(first)
    def _():
        carry_ref[...] = jnp.zeros(carry_ref.shape, F32)

    a = attn_ref[0]
    a = a * lax.rsqrt(jnp.mean(jnp.square(a), axis=0, keepdims=True) + LN_EPS) * gain_ref[...]
    mix = (lax.dot_general(a.astype(BF16), wa_ref[...], _TN, preferred_element_type=F32)
           + lax.dot_general(ret_ref[0], wr_ref[...], _TN, preferred_element_type=F32))
    h = _layer_norm_rows(ALPHA * x_ref[0] + mix, g1_ref[...], b1_ref[...])
    h_ref[...] = h

    R = ROUTER_ROWS
    h_hi = h.astype(BF16)
    h_lo = (h - h_hi.astype(F32)).astype(BF16)
    by_hi = lax.dot_general(wrt_ref[...], h_hi, _NT, preferred_element_type=F32)
    by_lo = lax.dot_general(wrt_ref[0:R, :], h_lo, _NT, preferred_element_type=F32)
    logits = by_hi[0:R] + by_hi[R:2 * R] + by_lo + brt_ref[...]
    T = logits.shape[1]
    gl = logits[0:N_GROUPS]
    gmax = jnp.max(gl, axis=0, keepdims=True)
    grow = lax.broadcasted_iota(jnp.int32, gl.shape, 0)
    gidx = jnp.min(jnp.where(gl == gmax, grow, N_GROUPS), axis=0, keepdims=True)
    g_p = 1.0 / jnp.sum(jnp.exp(gl - gmax), axis=0, keepdims=True)
    el = jnp.zeros((EXPERTS_PER_GROUP, T), F32)
    for g in range(N_GROUPS):
        r0 = ROUTER_EXPERT_ROW0 + g * EXPERTS_PER_GROUP
        el = jnp.where(gidx == g, logits[r0:r0 + EXPERTS_PER_GROUP], el)
    erow = lax.broadcasted_iota(jnp.int32, el.shape, 0)
    e1 = jnp.max(el, axis=0, keepdims=True)
    i1 = jnp.min(jnp.where(el == e1, erow, EXPERTS_PER_GROUP), axis=0, keepdims=True)
    el2 = jnp.where(erow == i1, -jnp.inf, el)
    e2 = jnp.max(el2, axis=0, keepdims=True)
    i2 = jnp.min(jnp.where(el2 == e2, erow, EXPERTS_PER_GROUP), axis=0, keepdims=True)
    r = jnp.exp(e2 - e1)
    w1 = g_p / (1.0 + r)
    w2 = g_p * r / (1.0 + r)
    id1 = gidx * EXPERTS_PER_GROUP + i1
    id2 = gidx * EXPERTS_PER_GROUP + i2
    eid_ref[0:1, :] = id1
    eid_ref[1:2, :] = id2
    cw_ref[0:1, :] = w1
    cw_ref[1:2, :] = w2

    xrow = lax.broadcasted_iota(jnp.int32, (N_EXPERTS, T), 0)
    oh1 = (xrow == id1).astype(F32)
    oh2 = (xrow == id2).astype(F32)
    both = oh1 + oh2
    seen = jnp.dot(both.astype(BF16), before_ref[...], preferred_element_type=F32) + carry_ref[...]
    rank_ref[0:1, :] = jnp.sum(oh1 * seen, axis=0, keepdims=True).astype(jnp.int32)
    rank_ref[1:2, :] = jnp.sum(oh2 * seen, axis=0, keepdims=True).astype(jnp.int32)
    carry = carry_ref[...] + jnp.sum(both, axis=1, keepdims=True)
    carry_ref[...] = carry
    cnt_ref[...] = carry.astype(jnp.int32)


def _outproj(attn_t, ret_t, x, gain, w_out, ln_g, ln_b, w_rg, b_rg, w_re, b_re):
    B, S, D = x.shape
    Tt = OUTPROJ_TOKENS
    n_t = S // Tt
    T = B * S
    wa = w_out[:ATTN_WIDTH].astype(BF16)
    wr = w_out[ATTN_WIDTH:].astype(BF16)
    e0, e1 = ROUTER_EXPERT_ROW0, ROUTER_EXPERT_ROW0 + N_EXPERTS
    wrt = jnp.zeros((ROUTER_ROWS, D), F32)
    wrt = wrt.at[:N_GROUPS].set(w_rg.T).at[e0:e1].set(w_re.T)
    wrt_hi = wrt.astype(BF16)
    wrt_lo = (wrt - wrt_hi.astype(F32)).astype(BF16)
    wrt = jnp.concatenate([wrt_hi, wrt_lo], axis=0)
    brt = jnp.zeros((ROUTER_ROWS, 1), F32)
    brt = brt.at[:N_GROUPS, 0].set(b_rg.astype(F32)).at[e0:e1, 0].set(b_re.astype(F32))
    before = (jnp.arange(Tt)[:, None] < jnp.arange(Tt)[None, :]).astype(BF16)
    const = lambda b, i: (0, 0)
    tok = lambda b, i: (0, b * n_t + i)
    return pl.pallas_call(
        _outproj_kernel,
        grid=(B, n_t),
        in_specs=[
            pl.BlockSpec((1, ATTN_WIDTH, Tt), lambda b, i: (b, 0, i)),
            pl.BlockSpec((1, RET_WIDTH, Tt), lambda b, i: (b, 0, i)),
            pl.BlockSpec((1, Tt, D), lambda b, i: (b, i, 0)),
            pl.BlockSpec((ATTN_WIDTH, 1), const),
            pl.BlockSpec((ATTN_WIDTH, D), const),
            pl.BlockSpec((RET_WIDTH, D), const),
            pl.BlockSpec((1, D), const),
            pl.BlockSpec((1, D), const),
            pl.BlockSpec((2 * ROUTER_ROWS, D), const),
            pl.BlockSpec((ROUTER_ROWS, 1), const),
            pl.BlockSpec((Tt, Tt), const),
        ],
        out_specs=[
            pl.BlockSpec((Tt, D), lambda b, i: (b * n_t + i, 0)),
            pl.BlockSpec((2, Tt), tok),
            pl.BlockSpec((2, Tt), tok),
            pl.BlockSpec((2, Tt), tok),
            pl.BlockSpec((N_EXPERTS, 1), const),
        ],
        out_shape=[
            jax.ShapeDtypeStruct((T, D), F32),
            jax.ShapeDtypeStruct((2, T), jnp.int32),
            jax.ShapeDtypeStruct((2, T), jnp.int32),
            jax.ShapeDtypeStruct((2, T), F32),
            jax.ShapeDtypeStruct((N_EXPERTS, 1), jnp.int32),
        ],
        scratch_shapes=[pltpu.VMEM((N_EXPERTS, 1), F32)],
        compiler_params=_params("arbitrary", "arbitrary"),
        name="outproj_ln_router",
    )(attn_t, ret_t, x, gain.reshape(ATTN_WIDTH, 1), wa, wr, ln_g.reshape(1, D), ln_b.reshape(1, D),
      wrt, brt, before)


def _dispatch_kernel(start_ref, end_ref, pos0_ref, pos1_ref, h_ref, xs_hbm, zero_ref, sem, zsem):
    G = ROUTE_TOKENS
    Tm = MOE_ROWS

    @pl.when(pl.program_id(0) == 0)
    def _():
        zero_ref[...] = jnp.zeros(zero_ref.shape, F32)

        def tile_clear(row0):
            row0 = pl.multiple_of(row0, Tm)
            return pltpu.make_async_copy(zero_ref, xs_hbm.at[pl.ds(row0, Tm)], zsem)

        used = end_ref[N_EXPERTS - 1]
        n_rows = xs_hbm.shape[0]
        clears = [(end_ref[e] - Tm, end_ref[e] > start_ref[e]) for e in range(N_EXPERTS)]
        clears += [(jnp.minimum(used + k * Tm, n_rows - Tm), used + k * Tm < n_rows) for k in range(N_EXPERTS)]
        for row0, cond in clears:
            @pl.when(cond)
            def _():
                tile_clear(row0).start()
        for row0, cond in clears:
            @pl.when(cond)
            def _():
                tile_clear(row0).wait()

    def issue(k, carry):
        t0 = pl.multiple_of(k * ROW_DMA_UNROLL, ROW_DMA_UNROLL)
        for u in range(ROW_DMA_UNROLL):
            for pos_ref in (pos0_ref, pos1_ref):
                pltpu.make_async_copy(h_ref.at[pl.ds(t0 + u, 1)], xs_hbm.at[pl.ds(pos_ref[t0 + u], 1)], sem).start()
        return carry

    lax.fori_loop(0, G // ROW_DMA_UNROLL, issue, 0)
    for _ in range(2):
        pltpu.make_async_copy(h_ref, xs_hbm.at[pl.ds(0, G)], sem).wait()


def _dispatch(h, pos0, pos1, starts, ends, n_rows):
    T, D = h.shape
    G = ROUTE_TOKENS
    smem_blk = pl.BlockSpec((G,), lambda i, st, en: (i,), memory_space=pltpu.SMEM)
    grid_spec = pltpu.PrefetchScalarGridSpec(
        num_scalar_prefetch=2,
        grid=(T // G,),
        in_specs=[smem_blk, smem_blk, pl.BlockSpec((G, D), lambda i, st, en: (i, 0))],
        out_specs=pl.BlockSpec(memory_space=pl.ANY),
        scratch_shapes=[pltpu.VMEM((MOE_ROWS, D), F32), pltpu.SemaphoreType.DMA(()),
                        pltpu.SemaphoreType.DMA(())],
    )
    return pl.pallas_call(
        _dispatch_kernel,
        grid_spec=grid_spec,
        out_shape=jax.ShapeDtypeStruct((n_rows, D), F32),
        compiler_params=_params("arbitrary"),
        name="moe_dispatch",
    )(starts, ends, pos0, pos1, h)


def _experts_kernel(te_ref, nv_ref, xs_ref, wgu_ref, wdn_ref, y_ref, wgu_b, wdn_b):
    i = pl.program_id(0)
    live = i < nv_ref[0]
    new_expert = jnp.logical_or(i == 0, te_ref[i] != te_ref[jnp.maximum(i - 1, 0)])

    @pl.when(jnp.logical_and(live, new_expert))
    def _():
        wgu_b[...] = wgu_ref[0].astype(BF16)
        wdn_b[...] = wdn_ref[0].astype(BF16)

    @pl.when(live)
    def _():
        gu = jnp.dot(xs_ref[...].astype(BF16), wgu_b[...], preferred_element_type=F32)
        gate, up = gu[:, :D_EXPERT], gu[:, D_EXPERT:]
        hid = gate * jax.nn.sigmoid(gate) * up
        y_ref[...] = jnp.dot(hid.astype(BF16), wdn_b[...], preferred_element_type=F32)

    @pl.when(jnp.logical_not(live))
    def _():
        y_ref[...] = jnp.zeros(y_ref.shape, F32)


def _experts(xs, w_gu, w_dn, tile_expert, n_valid):
    n_rows, D = xs.shape
    Tm = MOE_ROWS
    grid_spec = pltpu.PrefetchScalarGridSpec(
        num_scalar_prefetch=2,
        grid=(n_rows // Tm,),
        in_specs=[
            pl.BlockSpec((Tm, D), lambda i, te, nv: (jnp.minimum(i, jnp.maximum(nv[0] - 1, 0)), 0)),
            pl.BlockSpec((1, D, 2 * D_EXPERT), lambda i, te, nv: (te[i], 0, 0)),
            pl.BlockSpec((1, D_EXPERT, D), lambda i, te, nv: (te[i], 0, 0)),
        ],
        out_specs=pl.BlockSpec((Tm, D), lambda i, te, nv: (i, 0)),
        scratch_shapes=[pltpu.VMEM((D, 2 * D_EXPERT), BF16), pltpu.VMEM((D_EXPERT, D), BF16)],
    )
    return pl.pallas_call(
        _experts_kernel,
        grid_spec=grid_spec,
        out_shape=jax.ShapeDtypeStruct((n_rows, D), F32),
        compiler_params=_params("arbitrary"),
        name="moe_experts",
    )(tile_expert, n_valid, xs, w_gu, w_dn)


def _combine_kernel(pos0_ref, pos1_ref, next0_ref, next1_ref, y_hbm, h_ref, cw_ref, g_ref, b_ref, o_ref,
                    ybuf_a, ybuf_b, sem_a, sem_b):
    G = ROUTE_TOKENS
    C = ROW_DMA_UNROLL
    i = pl.program_id(0)
    last = pl.num_programs(0) - 1

    def gather(pos_refs, buf, sem, t0):
        for u in range(C):
            for slot, pos_ref in enumerate(pos_refs):
                pltpu.make_async_copy(y_hbm.at[pl.ds(pos_ref[t0 + u], 1)], buf.at[slot, pl.ds(t0 + u, 1)],
                                      sem).start()

    def wait_tile(buf, sem):
        for slot in range(2):
            pltpu.make_async_copy(y_hbm.at[pl.ds(0, G)], buf.at[slot], sem).wait()

    @pl.when(i == 0)
    def _():
        def first_tile(k, carry):
            gather((pos0_ref, pos1_ref), ybuf_a, sem_a, pl.multiple_of(k * C, C))
            return carry
        lax.fori_loop(0, G // C, first_tile, 0)

    def run(buf, sem, spare, spare_sem):
        wait_tile(buf, sem)

        @pl.when(i < last)
        def _():
            def next_tile(k, carry):
                gather((next0_ref, next1_ref), spare, spare_sem, pl.multiple_of(k * C, C))
                return carry
            lax.fori_loop(0, G // C, next_tile, 0)

        for r in range(G // COMBINE_CHUNK):
            rows = slice(r * COMBINE_CHUNK, (r + 1) * COMBINE_CHUNK)
            cw = cw_ref[rows, :]
            ffn = buf[0, rows, :] * cw[:, 0:1] + buf[1, rows, :] * cw[:, 1:2]
            o_ref[rows, :] = _layer_norm_rows(ALPHA * h_ref[rows, :] + ffn, g_ref[...], b_ref[...])

    odd = jnp.bitwise_and(i, 1)

    @pl.when(odd == 0)
    def _():
        run(ybuf_a, sem_a, ybuf_b, sem_b)

    @pl.when(odd == 1)
    def _():
        run(ybuf_b, sem_b, ybuf_a, sem_a)


def _combine(y, h, pos0, pos1, cw_rows, ln_g, ln_b):
    T, D = h.shape
    G = ROUTE_TOKENS
    n = T // G
    smem_blk = pl.BlockSpec((G,), lambda i: (i,), memory_space=pltpu.SMEM)
    smem_next = pl.BlockSpec((G,), lambda i: (jnp.minimum(i + 1, n - 1),), memory_space=pltpu.SMEM)
    return pl.pallas_call(
        _combine_kernel,
        grid=(n,),
        in_specs=[
            smem_blk, smem_blk, smem_next, smem_next,
            pl.BlockSpec(memory_space=pl.ANY),
            pl.BlockSpec((G, D), lambda i: (i, 0)),
            pl.BlockSpec((G, 2), lambda i: (i, 0)),
            pl.BlockSpec((1, D), lambda i: (0, 0)),
            pl.BlockSpec((1, D), lambda i: (0, 0)),
        ],
        out_specs=pl.BlockSpec((G, D), lambda i: (i, 0)),
        scratch_shapes=[pltpu.VMEM((2, G, D), F32), pltpu.VMEM((2, G, D), F32),
                        pltpu.SemaphoreType.DMA(()), pltpu.SemaphoreType.DMA(())],
        out_shape=jax.ShapeDtypeStruct((T, D), F32),
        compiler_params=_params("arbitrary"),
        name="moe_combine_ln",
    )(pos0, pos1, pos0, pos1, y, h, cw_rows, ln_g.reshape(1, D), ln_b.reshape(1, D))


def kernel(x, w_in, attn_out_gain, rel_bias_table, w_out, ln1_g, ln1_b, w_router_group,
           b_router_group, w_router_expert, b_router_expert, w_gate_up, w_down, ln2_g, ln2_b):
    B, S, D = x.shape
    assert D == D_MODEL and S % MOBA_BLOCK == 0 and S % INPROJ_TOKENS == 0
    assert w_in.shape[0] == DEPTH
    T = B * S
    h = x
    for l in range(DEPTH):
        proj_t, k_nat = _inproj(h, w_in[l])
        attn_t = _moba(proj_t, k_nat, rel_bias_table)
        ret_t = _retention(proj_t)
        h1, eid, rank, cw, counts = _outproj(
            attn_t, ret_t, h, attn_out_gain[l], w_out[l], ln1_g[l], ln1_b[l],
            w_router_group[l], b_router_group[l], w_router_expert[l], b_router_expert[l])

        Tm = MOE_ROWS
        n_tiles = (2 * T) // Tm + N_EXPERTS
        padded = ((counts[:, 0] + Tm - 1) // Tm) * Tm
        ends = jnp.cumsum(padded)
        starts = (ends - padded).astype(jnp.int32)
        tile_row0 = jnp.arange(n_tiles, dtype=jnp.int32) * Tm
        tile_expert = jnp.sum((ends[None, :] <= jnp.minimum(tile_row0, ends[-1] - Tm)[:, None]).astype(jnp.int32),
                              axis=1)
        n_valid = (ends[-1:] // Tm).astype(jnp.int32)

        expert_ids = jnp.arange(N_EXPERTS, dtype=jnp.int32)[:, None, None]
        pos = jnp.sum(jnp.where(eid[None] == expert_ids, starts[:, None, None], 0), axis=0) + rank

        xs = _dispatch(h1, pos[0], pos[1], starts, ends.astype(jnp.int32), n_tiles * Tm)
        y = _experts(xs, w_gate_up[l], w_down[l], tile_expert, n_valid)
        h2 = _combine(y, h1, pos[0], pos[1], cw.T, ln2_g[l], ln2_b[l])
        h = h2.reshape(B, S, D)
    return h
```

```python
import functools
import math

import numpy as np
import jax
import jax.numpy as jnp
from jax import lax
from jax.experimental import pallas as pl
from jax.experimental.pallas import tpu as pltpu

F32 = jnp.float32
BF16 = jnp.bfloat16

D_MODEL = 1024
ATTN_HEADS = 8
ATTN_HEAD_DIM = 64
ATTN_WIDTH = ATTN_HEADS * ATTN_HEAD_DIM
RET_HEADS = 4
RET_HEAD_DIM = 128
RET_WIDTH = RET_HEADS * RET_HEAD_DIM
N_IN_SLICES = 7
PROJT_SLICES = N_IN_SLICES - 1
HEAD_PAIR = 2 * ATTN_HEAD_DIM
BF16_SUBLANES = 16
F32_SUBLANES = 8
V_ROWS = ATTN_HEAD_DIM + BF16_SUBLANES
PROJT_V_ROW0 = (PROJT_SLICES - 1) * ATTN_WIDTH
PROJT_WIDTH = PROJT_V_ROW0 + ATTN_HEADS * V_ROWS
MOBA_BLOCK = 256
MOBA_TOPK = 3
RET_CHUNK = 256
REL_BUCKETS = 32
REL_MAX_DIST = 128
ROPE_BASE = 10000.0
N_GROUPS = 4
EXPERTS_PER_GROUP = 8
N_EXPERTS = N_GROUPS * EXPERTS_PER_GROUP
D_EXPERT = D_MODEL // 2
DEPTH = 1
ALPHA = (2.0 * DEPTH) ** 0.25
LN_EPS = 1e-5

LOG2E = math.log2(math.e)
Q_SCALE = ATTN_HEAD_DIM ** -0.5 * LOG2E

MASKED = -1e30

INPROJ_TOKENS = 1024
INPROJ_ROWS = ATTN_WIDTH
OUTPROJ_TOKENS = 1024
MOBA_UNROLL = 2
MOBA_HEADS = 4
MOBA_QTILES = 4
MOE_ROWS = 512
DISPATCH_TOKENS = 2048
ROUTE_TOKENS = 1024
COMBINE_CHUNK = 256
ROW_DMA_UNROLL = 32
ROUTER_EXPERT_ROW0 = 8
ROUTER_ROWS = -(-(ROUTER_EXPERT_ROW0 + N_EXPERTS) // BF16_SUBLANES) * BF16_SUBLANES

V7X_VMEM_BYTES = 64 * 1024 * 1024
VMEM_LIMIT = V7X_VMEM_BYTES * 3 // 4

_TN = (((0,), (0,)), ((), ()))
_NT = (((1,), (1,)), ((), ()))


def _params(*sem):
    return pltpu.CompilerParams(dimension_semantics=sem, vmem_limit_bytes=VMEM_LIMIT)


def _inproj_kernel(x_ref, wt_ref, wk_ref, o_ref, k_ref):
    xb = x_ref[0].astype(BF16)
    n_tok = xb.shape[0]
    ones_row = (lax.broadcasted_iota(jnp.int32, (BF16_SUBLANES, n_tok), 0) == 0).astype(BF16)
    for c in range(PROJT_SLICES):
        rows = slice(c * INPROJ_ROWS, (c + 1) * INPROJ_ROWS)
        acc = lax.dot_general(wt_ref[rows, :], xb, _NT, preferred_element_type=F32)
        if c == 0:
            acc = acc * Q_SCALE
        if c < PROJT_SLICES - 1:
            o_ref[0, rows, :] = acc.astype(BF16)
        else:
            for h in range(ATTN_HEADS):
                r0 = PROJT_V_ROW0 + h * V_ROWS
                o_ref[0, r0:r0 + ATTN_HEAD_DIM, :] = acc[h * ATTN_HEAD_DIM:(h + 1) * ATTN_HEAD_DIM].astype(BF16)
                o_ref[0, r0 + ATTN_HEAD_DIM:r0 + V_ROWS, :] = ones_row
    k_ref[0] = jnp.dot(xb, wk_ref[...], preferred_element_type=F32).astype(BF16)


def _inproj(x, w_in):
    B, S, D = x.shape
    W = ATTN_WIDTH
    w_t = jnp.concatenate([w_in[:, :W], w_in[:, 3 * W:], w_in[:, 2 * W:3 * W]], axis=1).T.astype(BF16)
    w_k = w_in[:, W:2 * W].astype(BF16)
    return pl.pallas_call(
        _inproj_kernel,
        grid=(B, S // INPROJ_TOKENS),
        in_specs=[
            pl.BlockSpec((1, INPROJ_TOKENS, D), lambda b, i: (b, i, 0)),
            pl.BlockSpec((PROJT_SLICES * W, D), lambda b, i: (0, 0)),
            pl.BlockSpec((D, W), lambda b, i: (0, 0)),
        ],
        out_specs=[
            pl.BlockSpec((1, PROJT_WIDTH, INPROJ_TOKENS), lambda b, i: (b, 0, i)),
            pl.BlockSpec((1, INPROJ_TOKENS, W), lambda b, i: (b, i, 0)),
        ],
        out_shape=[
            jax.ShapeDtypeStruct((B, PROJT_WIDTH, S), BF16),
            jax.ShapeDtypeStruct((B, S, W), BF16),
        ],
        compiler_params=_params("arbitrary", "arbitrary"),
        name="inproj",
    )(x, w_t, w_k)


def _moba_kernel(far_ref, q_ref, k_ref, v_ref, bias_ref, o_ref,
                 kmean_ref, qz_ref, sel_ref, farsel_ref,
                 s0_ref, smax0_ref, p0_ref, alpha0_ref, s1_ref, smax1_ref, p1_ref, alpha1_ref,
                 m_ref, acc_ref, *, n_blocks):
    L = MOBA_BLOCK
    Dh = ATTN_HEAD_DIM
    HG = MOBA_HEADS
    group = pl.program_id(1)
    qi0 = pl.program_id(2) * MOBA_QTILES
    last_blk = n_blocks - 1
    units = [(a, h) for a in range(MOBA_QTILES) for h in range(HG)]
    s_ref, smax_ref = (s0_ref, s1_ref), (smax0_ref, smax1_ref)
    p_ref, alpha_ref = (p0_ref, p1_ref), (alpha0_ref, alpha1_ref)

    @pl.when(qi0 == 0)
    def _():
        chunk_blocks = min(8, n_blocks)
        chunk = chunk_blocks * L
        acc = jnp.zeros((n_blocks, HG * Dh), F32)
        for c in range(n_blocks // chunk_blocks):
            blk = lax.broadcasted_iota(jnp.int32, (n_blocks, chunk), 0)
            key = lax.broadcasted_iota(jnp.int32, (n_blocks, chunk), 1)
            ind = jnp.where(blk == c * chunk_blocks + key // L, 1.0 / L, 0.0).astype(BF16)
            acc = acc + jnp.dot(ind, k_ref[0, c * chunk:(c + 1) * chunk, :], preferred_element_type=F32)
        hi = acc.astype(BF16)
        lo = (acc - hi.astype(F32)).astype(BF16)
        for pair in range(HG // 2):
            cols = slice(pair * HEAD_PAIR, (pair + 1) * HEAD_PAIR)
            kmean_ref[pair, 0:n_blocks, :] = hi[:, cols]
            kmean_ref[pair, n_blocks:2 * n_blocks, :] = lo[:, cols]

    blk = lax.broadcasted_iota(jnp.int32, (n_blocks, L), 0)
    for u, (a, h) in enumerate(units):
        qi = qi0 + a
        pair, hd = divmod(h, 2)
        past = blk < qi
        qh = q_ref[0, h * Dh:(h + 1) * Dh, a * L:(a + 1) * L]
        no_q = jnp.zeros_like(qh)
        qz = jnp.concatenate([qh, no_q] if hd == 0 else [no_q, qh], axis=0)
        qz_ref[u] = qz
        gate = jnp.dot(kmean_ref[pair], qz, preferred_element_type=F32)
        gate = gate[0:n_blocks] + gate[n_blocks:2 * n_blocks]
        gate = jnp.where(past, gate, -jnp.inf)
        for _ in range(MOBA_TOPK):
            top = jnp.max(gate, axis=0, keepdims=True)
            first = jnp.min(jnp.where(gate == top, blk, n_blocks), axis=0, keepdims=True)
            gate = jnp.where(blk == first, -jnp.inf, gate)
        knocked_out = gate == -jnp.inf
        sel_ref[u] = jnp.where(knocked_out, jnp.where(past, 0.0, MASKED), MASKED)
        far_row = jnp.where(blk < qi - 1, far_ref[group * HG + h], MASKED)
        farsel_ref[u, 0:n_blocks, :] = jnp.where(knocked_out, far_row, MASKED)
        farsel_ref[u, n_blocks:n_blocks + F32_SUBLANES, :] = jnp.zeros((F32_SUBLANES, L), F32)

    def stage_scores(u, j, tile, slot):
        pair = units[u][1] // 2
        kb = k_ref[0, pl.ds(pl.multiple_of(j * L, L), L), pair * HEAD_PAIR:(pair + 1) * HEAD_PAIR]
        s = jnp.dot(kb, qz_ref[u], preferred_element_type=F32)
        if tile is not None:
            s = s + tile
        s_ref[slot][u] = s
        smax_ref[slot][u] = jnp.max(s, axis=0, keepdims=True)

    def stage_softmax(u, row, slot):
        m_prev = m_ref[u]
        m_new = jnp.maximum(m_prev, smax_ref[slot][u] + row)
        alpha_ref[slot][u] = jnp.exp2(m_prev - m_new)
        p_ref[slot][u] = jnp.exp2(s_ref[slot][u] - (m_new - row)).astype(BF16)
        m_ref[u] = m_new

    def stage_values(u, j, slot):
        h = units[u][1]
        vb = v_ref[0, h * V_ROWS:(h + 1) * V_ROWS, pl.ds(pl.multiple_of(j * L, L), L)]
        pv = jnp.dot(vb, p_ref[slot][u], preferred_element_type=F32)
        acc_ref[u] = alpha_ref[slot][u] * acc_ref[u] + pv

    m_ref[...] = jnp.full(m_ref.shape, MASKED, F32)
    acc_ref[...] = jnp.zeros(acc_ref.shape, F32)

    zero_row = jnp.zeros((1, L), F32)
    own = [qi0 + a for a, _ in units]
    prev = [jnp.maximum(qi0 + a - 1, 0) for a, _ in units]
    for u, (a, h) in enumerate(units):
        stage_scores(u, own[u], bias_ref[0, h], 0)
        stage_scores(u, prev[u], bias_ref[1, h] + sel_ref[u, pl.ds(prev[u], 1), :], 1)
    for u in range(len(units)):
        stage_softmax(u, zero_row, 0)
    everyone = range(len(units))

    def step(t, slot, score_units=everyone, softmax_units=everyone, value_units=everyone):
        far_value = jnp.minimum(t - 2, last_blk)
        j = jnp.minimum(t, last_blk)
        jr = jnp.where(t == 0, n_blocks, jnp.minimum(t - 1, last_blk))
        for u in everyone:
            if u in score_units:
                stage_scores(u, j, None, slot)
            if u in softmax_units:
                stage_softmax(u, farsel_ref[u, pl.ds(jr, 1), :], 1 - slot)
            if u in value_units:
                stage_values(u, jnp.where(t == 0, own[u], jnp.where(t == 1, prev[u], far_value)), slot)

    def run_steps(trips, unroll):
        def unrolled_steps(k, carry):
            for s in range(unroll):
                step(unroll * k + s, s % 2)
            return carry
        lax.fori_loop(0, trips, unrolled_steps, 0)

    QT = MOBA_QTILES

    @pl.when(qi0 < QT)
    def _():
        n_steps = 2 + jnp.maximum(qi0 + QT - 2, 0)
        run_steps((n_steps + 1) // 2, 2)

    @pl.when(qi0 >= QT)
    def _():
        n_main = qi0 - 2
        left = (MOBA_UNROLL - 2) % MOBA_UNROLL
        run_steps((n_main - left) // MOBA_UNROLL, MOBA_UNROLL)
        for k in range(-left, QT + 2):
            with_block = lambda first: [u for u, (a, _) in enumerate(units) if a >= first]
            step(n_main + k, k % 2, with_block(k), with_block(k - 1), with_block(k - 2))
    for u, (a, h) in enumerate(units):
        o_ref[0, h * Dh:(h + 1) * Dh, a * L:(a + 1) * L] = acc_ref[u, 0:Dh, :] / acc_ref[u, Dh:Dh + 1, :]


def _t5_bucket_np(dist):
    n = np.maximum(dist, 0)
    max_exact = REL_BUCKETS // 2
    nf = np.maximum(n, 1).astype(np.float32)
    large = max_exact + (np.log(nf / max_exact) / math.log(REL_MAX_DIST / max_exact)
                         * (REL_BUCKETS - max_exact)).astype(np.int32)
    large = np.minimum(large, REL_BUCKETS - 1)
    return np.where(n < max_exact, n, large)


def _moba(proj_t, k_nat, rel_table):
    B, _, S = proj_t.shape
    L = MOBA_BLOCK
    n_blocks = S // L
    H = ATTN_HEADS
    table_t = rel_table.astype(F32).T
    dist = np.arange(-(L - 1), 2 * L)
    onehot = jnp.asarray(_t5_bucket_np(dist))[None, None, :] == jnp.arange(REL_BUCKETS)[None, :, None]
    by_dist = jnp.sum(jnp.where(onehot, table_t[:, :, None], 0.0), axis=1) * LOG2E
    by_dist = jnp.where(jnp.asarray(dist >= 0)[None, :], by_dist, MASKED)

    def toeplitz(vec):
        W = 2 * L - 1
        flat = jnp.broadcast_to(vec[:, None, :], (H, L, W)).reshape(H, L * W)
        return flat[:, L - 1:L - 1 + L * (W - 1)].reshape(H, L, W - 1)[:, :, :L]

    bias = jnp.stack([toeplitz(by_dist[:, :2 * L - 1]), toeplitz(by_dist[:, L:])])
    assert int(_t5_bucket_np(np.array(L + 1))) == REL_BUCKETS - 1
    far = table_t[:, REL_BUCKETS - 1] * LOG2E

    HG, QT = MOBA_HEADS, MOBA_QTILES
    U = HG * QT
    q_rows, v_rows = HG * ATTN_HEAD_DIM, HG * V_ROWS
    assert ATTN_HEADS % HG == 0 and HG % 2 == 0 and n_blocks % QT == 0 and PROJT_V_ROW0 % v_rows == 0
    assert MOBA_UNROLL % 2 == 0 and QT % MOBA_UNROLL == 0
    once = dict(pipeline_mode=pl.Buffered(1))
    grid_spec = pltpu.PrefetchScalarGridSpec(
        num_scalar_prefetch=1,
        grid=(B, ATTN_HEADS // HG, n_blocks // QT),
        in_specs=[
            pl.BlockSpec((1, q_rows, QT * L), lambda b, g, i, far: (b, g, i)),
            pl.BlockSpec((1, S, q_rows), lambda b, g, i, far: (b, 0, g), **once),
            pl.BlockSpec((1, v_rows, S), lambda b, g, i, far: (b, PROJT_V_ROW0 // v_rows + g, 0), **once),
            pl.BlockSpec((2, HG, L, L), lambda b, g, i, far: (0, g, 0, 0), **once),
        ],
        out_specs=pl.BlockSpec((1, q_rows, QT * L), lambda b, g, i, far: (b, g, i)),
        scratch_shapes=[
            pltpu.VMEM((HG // 2, 2 * n_blocks, HEAD_PAIR), BF16),
            pltpu.VMEM((U, HEAD_PAIR, L), BF16),
            pltpu.VMEM((U, n_blocks, L), F32),
            pltpu.VMEM((U, n_blocks + F32_SUBLANES, L), F32),
            *([pltpu.VMEM((U, L, L), F32), pltpu.VMEM((U, 1, L), F32),
               pltpu.VMEM((U, L, L), BF16), pltpu.VMEM((U, 1, L), F32)] * 2),
            pltpu.VMEM((U, 1, L), F32),
            pltpu.VMEM((U, V_ROWS, L), F32),
        ],
    )
    return pl.pallas_call(
        functools.partial(_moba_kernel, n_blocks=n_blocks),
        grid_spec=grid_spec,
        out_shape=jax.ShapeDtypeStruct((B, ATTN_WIDTH, S), F32),
        compiler_params=_params("arbitrary", "arbitrary", "arbitrary"),
        name="moba",
    )(far, proj_t, k_nat, proj_t, bias)


def _retention_kernel(cdec_ref, q_ref, k_ref, v_ref, g_ref, cos_ref, sin_ref,
                      dec_ref, qdec_ref, kdec_ref, o_ref, state_ref):
    c = pl.program_id(0)

    @pl.when(c == 0)
    def _():
        state_ref[...] = jnp.zeros(state_ref.shape, F32)

    cos = cos_ref[...]
    sin = sin_ref[...]
    half = RET_HEAD_DIM // 2

    def rope(t):
        t1, t2 = t[:half], t[half:]
        return jnp.concatenate([t1 * cos - t2 * sin, t1 * sin + t2 * cos], axis=0)

    for b in range(q_ref.shape[0]):
        for hh in range(RET_HEADS):
            rows = slice(hh * RET_HEAD_DIM, (hh + 1) * RET_HEAD_DIM)
            q = rope(q_ref[b, rows, :].astype(F32))
            k = rope(k_ref[b, rows, :].astype(F32)) * (RET_HEAD_DIM ** -0.5)
            v = v_ref[b, rows, :]
            qb = q.astype(BF16)
            s = lax.dot_general(k.astype(BF16), qb, _TN, preferred_element_type=F32)
            s = (s * dec_ref[hh]).astype(BF16)
            inner = jnp.dot(v, s, preferred_element_type=F32)
            state = state_ref[b, hh]
            cross = jnp.dot(state.astype(BF16), (q * qdec_ref[hh:hh + 1, :]).astype(BF16),
                            preferred_element_type=F32)
            kd = (k * kdec_ref[hh:hh + 1, :]).astype(BF16)
            state_ref[b, hh] = state * cdec_ref[hh] + lax.dot_general(v, kd, _NT, preferred_element_type=F32)
            out = inner + cross
            mu = jnp.mean(out, axis=0, keepdims=True)
            var = jnp.mean(jnp.square(out - mu), axis=0, keepdims=True)
            out = (out - mu) * lax.rsqrt(var + LN_EPS)
            g = g_ref[b, rows, :].astype(F32)
            o_ref[b, rows, :] = (g * jax.nn.sigmoid(g) * out).astype(BF16)


def _retention(proj_t):
    B, _, S = proj_t.shape
    C = RET_CHUNK
    H = RET_HEADS
    half = RET_HEAD_DIM // 2
    inv = ROPE_BASE ** (-jnp.arange(half, dtype=F32) / half)
    ang = inv[:, None] * jnp.arange(S).astype(F32)[None, :]
    cos_t, sin_t = jnp.cos(ang), jnp.sin(ang)
    gammas = 1.0 - jnp.exp(jnp.linspace(math.log(1.0 / 32), math.log(1.0 / 512), H, dtype=F32))
    log_g = jnp.log(gammas)
    idx = jnp.arange(C, dtype=F32)
    diff = idx[None, :] - idx[:, None]
    dec_t = jnp.where(diff[None] >= 0, jnp.exp(jnp.maximum(diff, 0.0)[None] * log_g[:, None, None]), 0.0)
    q_dec = jnp.exp((idx[None, :] + 1.0) * log_g[:, None])
    k_dec = jnp.exp((C - 1.0 - idx[None, :]) * log_g[:, None])
    chunk_dec = jnp.exp(C * log_g)

    grid_spec = pltpu.PrefetchScalarGridSpec(
        num_scalar_prefetch=1,
        grid=(S // C,),
        in_specs=[
            pl.BlockSpec((B, RET_WIDTH, C), lambda c, cd: (0, 1, c)),
            pl.BlockSpec((B, RET_WIDTH, C), lambda c, cd: (0, 2, c)),
            pl.BlockSpec((B, RET_WIDTH, C), lambda c, cd: (0, 3, c)),
            pl.BlockSpec((B, RET_WIDTH, C), lambda c, cd: (0, 4, c)),
            pl.BlockSpec((half, C), lambda c, cd: (0, c)),
            pl.BlockSpec((half, C), lambda c, cd: (0, c)),
            pl.BlockSpec((H, C, C), lambda c, cd: (0, 0, 0)),
            pl.BlockSpec((H, C), lambda c, cd: (0, 0)),
            pl.BlockSpec((H, C), lambda c, cd: (0, 0)),
        ],
        out_specs=pl.BlockSpec((B, RET_WIDTH, C), lambda c, cd: (0, 0, c)),
        scratch_shapes=[pltpu.VMEM((B, H, RET_HEAD_DIM, RET_HEAD_DIM), F32)],
    )
    return pl.pallas_call(
        _retention_kernel,
        grid_spec=grid_spec,
        out_shape=jax.ShapeDtypeStruct((B, RET_WIDTH, S), BF16),
        compiler_params=_params("arbitrary"),
        name="retention",
    )(chunk_dec, proj_t, proj_t, proj_t, proj_t, cos_t, sin_t, dec_t, q_dec, k_dec)


def _layer_norm_rows(y, g, b):
    mu = jnp.mean(y, axis=-1, keepdims=True)
    var = jnp.mean(jnp.square(y - mu), axis=-1, keepdims=True)
    return (y - mu) * lax.rsqrt(var + LN_EPS) * g + b


def _outproj_kernel(attn_ref, ret_ref, x_ref, gain_ref, wa_ref, wr_ref, g1_ref, b1_ref,
                    wrt_ref, brt_ref, before_ref,
                    h_ref, eid_ref, rank_ref, cw_ref, cnt_ref, carry_ref):
    first = jnp.logical_and(pl.program_id(0) == 0, pl.program_id(1) == 0)

    @pl.when(first)
    def _():
        carry_ref[...] = jnp.zeros(carry_ref.shape, F32)

    a = attn_ref[0]
    a = a * lax.rsqrt(jnp.mean(jnp.square(a), axis=0, keepdims=True) + LN_EPS) * gain_ref[...]
    mix = (lax.dot_general(a.astype(BF16), wa_ref[...], _TN, preferred_element_type=F32)
           + lax.dot_general(ret_ref[0], wr_ref[...], _TN, preferred_element_type=F32))
    h = _layer_norm_rows(ALPHA * x_ref[0] + mix, g1_ref[...], b1_ref[...])
    h_ref[...] = h

    R = ROUTER_ROWS
    h_hi = h.astype(BF16)
    h_lo = (h - h_hi.astype(F32)).astype(BF16)
    by_hi = lax.dot_general(wrt_ref[...], h_hi, _NT, preferred_element_type=F32)
    by_lo = lax.dot_general(wrt_ref[0:R, :], h_lo, _NT, preferred_element_type=F32)
    logits = by_hi[0:R] + by_hi[R:2 * R] + by_lo + brt_ref[...]
    T = logits.shape[1]
    gl = logits[0:N_GROUPS]
    gmax = jnp.max(gl, axis=0, keepdims=True)
    grow = lax.broadcasted_iota(jnp.int32, gl.shape, 0)
    gidx = jnp.min(jnp.where(gl == gmax, grow, N_GROUPS), axis=0, keepdims=True)
    g_p = 1.0 / jnp.sum(jnp.exp(gl - gmax), axis=0, keepdims=True)
    el = jnp.zeros((EXPERTS_PER_GROUP, T), F32)
    for g in range(N_GROUPS):
        r0 = ROUTER_EXPERT_ROW0 + g * EXPERTS_PER_GROUP
        el = jnp.where(gidx == g, logits[r0:r0 + EXPERTS_PER_GROUP], el)
    erow = lax.broadcasted_iota(jnp.int32, el.shape, 0)
    e1 = jnp.max(el, axis=0, keepdims=True)
    i1 = jnp.min(jnp.where(el == e1, erow, EXPERTS_PER_GROUP), axis=0, keepdims=True)
    el2 = jnp.where(erow == i1, -jnp.inf, el)
    e2 = jnp.max(el2, axis=0, keepdims=True)
    i2 = jnp.min(jnp.where(el2 == e2, erow, EXPERTS_PER_GROUP), axis=0, keepdims=True)
    r = jnp.exp(e2 - e1)
    w1 = g_p / (1.0 + r)
    w2 = g_p * r / (1.0 + r)
    id1 = gidx * EXPERTS_PER_GROUP + i1
    id2 = gidx * EXPERTS_PER_GROUP + i2
    eid_ref[0:1, :] = id1
    eid_ref[1:2, :] = id2
    cw_ref[0:1, :] = w1
    cw_ref[1:2, :] = w2

    xrow = lax.broadcasted_iota(jnp.int32, (N_EXPERTS, T), 0)
    oh1 = (xrow == id1).astype(F32)
    oh2 = (xrow == id2).astype(F32)
    both = oh1 + oh2
    seen = jnp.dot(both.astype(BF16), before_ref[...], preferred_element_type=F32) + carry_ref[...]
    rank_ref[0:1, :] = jnp.sum(oh1 * seen, axis=0, keepdims=True).astype(jnp.int32)
    rank_ref[1:2, :] = jnp.sum(oh2 * seen, axis=0, keepdims=True).astype(jnp.int32)
    carry = carry_ref[...] + jnp.sum(both, axis=1, keepdims=True)
    carry_ref[...] = carry
    cnt_ref[...] = carry.astype(jnp.int32)


def _outproj(attn_t, ret_t, x, gain, w_out, ln_g, ln_b, w_rg, b_rg, w_re, b_re):
    B, S, D = x.shape
    Tt = OUTPROJ_TOKENS
    n_t = S // Tt
    T = B * S
    wa = w_out[:ATTN_WIDTH].astype(BF16)
    wr = w_out[ATTN_WIDTH:].astype(BF16)
    e0, e1 = ROUTER_EXPERT_ROW0, ROUTER_EXPERT_ROW0 + N_EXPERTS
    wrt = jnp.zeros((ROUTER_ROWS, D), F32)
    wrt = wrt.at[:N_GROUPS].set(w_rg.T).at[e0:e1].set(w_re.T)
    wrt_hi = wrt.astype(BF16)
    wrt_lo = (wrt - wrt_hi.astype(F32)).astype(BF16)
    wrt = jnp.concatenate([wrt_hi, wrt_lo], axis=0)
    brt = jnp.zeros((ROUTER_ROWS, 1), F32)
    brt = brt.at[:N_GROUPS, 0].set(b_rg.astype(F32)).at[e0:e1, 0].set(b_re.astype(F32))
    before = (jnp.arange(Tt)[:, None] < jnp.arange(Tt)[None, :]).astype(BF16)
    const = lambda b, i: (0, 0)
    tok = lambda b, i: (0, b * n_t + i)
    return pl.pallas_call(
        _outproj_kernel,
        grid=(B, n_t),
        in_specs=[
            pl.BlockSpec((1, ATTN_WIDTH, Tt), lambda b, i: (b, 0, i)),
            pl.BlockSpec((1, RET_WIDTH, Tt), lambda b, i: (b, 0, i)),
            pl.BlockSpec((1, Tt, D), lambda b, i: (b, i, 0)),
            pl.BlockSpec((ATTN_WIDTH, 1), const),
            pl.BlockSpec((ATTN_WIDTH, D), const),
            pl.BlockSpec((RET_WIDTH, D), const),
            pl.BlockSpec((1, D), const),
            pl.BlockSpec((1, D), const),
            pl.BlockSpec((2 * ROUTER_ROWS, D), const),
            pl.BlockSpec((ROUTER_ROWS, 1), const),
            pl.BlockSpec((Tt, Tt), const),
        ],
        out_specs=[
            pl.BlockSpec((Tt, D), lambda b, i: (b * n_t + i, 0)),
            pl.BlockSpec((2, Tt), tok),
            pl.BlockSpec((2, Tt), tok),
            pl.BlockSpec((2, Tt), tok),
            pl.BlockSpec((N_EXPERTS, 1), const),
        ],
        out_shape=[
            jax.ShapeDtypeStruct((T, D), F32),
            jax.ShapeDtypeStruct((2, T), jnp.int32),
            jax.ShapeDtypeStruct((2, T), jnp.int32),
            jax.ShapeDtypeStruct((2, T), F32),
            jax.ShapeDtypeStruct((N_EXPERTS, 1), jnp.int32),
        ],
        scratch_shapes=[pltpu.VMEM((N_EXPERTS, 1), F32)],
        compiler_params=_params("arbitrary", "arbitrary"),
        name="outproj_ln_router",
    )(attn_t, ret_t, x, gain.reshape(ATTN_WIDTH, 1), wa, wr, ln_g.reshape(1, D), ln_b.reshape(1, D),
      wrt, brt, before)


def _dispatch_kernel(start_ref, end_ref, pos0_ref, pos1_ref, h_ref, xs_hbm, zero_ref, sem, zsem):
    G = h_ref.shape[0]
    Tm = MOE_ROWS

    @pl.when(pl.program_id(0) == 0)
    def _():
        zero_ref[...] = jnp.zeros(zero_ref.shape, F32)

        def tile_clear(row0):
            row0 = pl.multiple_of(row0, Tm)
            return pltpu.make_async_copy(zero_ref, xs_hbm.at[pl.ds(row0, Tm)], zsem)

        used = end_ref[N_EXPERTS - 1]
        n_rows = xs_hbm.shape[0]
        clears = [(end_ref[e] - Tm, end_ref[e] > start_ref[e]) for e in range(N_EXPERTS)]
        clears += [(jnp.minimum(used + k * Tm, n_rows - Tm), used + k * Tm < n_rows) for k in range(N_EXPERTS)]
        for row0, cond in clears:
            @pl.when(cond)
            def _():
                tile_clear(row0).start()
        for row0, cond in clears:
            @pl.when(cond)
            def _():
                tile_clear(row0).wait()

    def issue(k, carry):
        t0 = pl.multiple_of(k * ROW_DMA_UNROLL, ROW_DMA_UNROLL)
        for u in range(ROW_DMA_UNROLL):
            for pos_ref in (pos0_ref, pos1_ref):
                pltpu.make_async_copy(h_ref.at[pl.ds(t0 + u, 1)], xs_hbm.at[pl.ds(pos_ref[t0 + u], 1)], sem).start()
        return carry

    lax.fori_loop(0, G // ROW_DMA_UNROLL, issue, 0)
    for _ in range(2):
        pltpu.make_async_copy(h_ref, xs_hbm.at[pl.ds(0, G)], sem).wait()


def _dispatch(h, pos0, pos1, starts, ends, n_rows):
    T, D = h.shape
    G = DISPATCH_TOKENS if T % DISPATCH_TOKENS == 0 else ROUTE_TOKENS
    smem_blk = pl.BlockSpec((G,), lambda i, st, en: (i,), memory_space=pltpu.SMEM)
    grid_spec = pltpu.PrefetchScalarGridSpec(
        num_scalar_prefetch=2,
        grid=(T // G,),
        in_specs=[smem_blk, smem_blk, pl.BlockSpec((G, D), lambda i, st, en: (i, 0))],
        out_specs=pl.BlockSpec(memory_space=pl.ANY),
        scratch_shapes=[pltpu.VMEM((MOE_ROWS, D), F32), pltpu.SemaphoreType.DMA(()),
                        pltpu.SemaphoreType.DMA(())],
    )
    return pl.pallas_call(
        _dispatch_kernel,
        grid_spec=grid_spec,
        out_shape=jax.ShapeDtypeStruct((n_rows, D), F32),
        compiler_params=_params("arbitrary"),
        name="moe_dispatch",
    )(starts, ends, pos0, pos1, h)


def _experts_kernel(te_ref, nv_ref, xs_ref, wgu_ref, wdn_ref, y_ref, wgu_b, wdn_b):
    i = pl.program_id(0)
    live = i < nv_ref[0]
    new_expert = jnp.logical_or(i == 0, te_ref[i] != te_ref[jnp.maximum(i - 1, 0)])

    @pl.when(jnp.logical_and(live, new_expert))
    def _():
        wgu_b[...] = wgu_ref[0].astype(BF16)
        wdn_b[...] = wdn_ref[0].astype(BF16)

    @pl.when(live)
    def _():
        gu = jnp.dot(xs_ref[...].astype(BF16), wgu_b[...], preferred_element_type=F32)
        gate, up = gu[:, :D_EXPERT], gu[:, D_EXPERT:]
        hid = gate * jax.nn.sigmoid(gate) * up
        y_ref[...] = jnp.dot(hid.astype(BF16), wdn_b[...], preferred_element_type=F32)

    @pl.when(jnp.logical_not(live))
    def _():
        y_ref[...] = jnp.zeros(y_ref.shape, F32)


def _experts(xs, w_gu, w_dn, tile_expert, n_valid):
    n_rows, D = xs.shape
    Tm = MOE_ROWS
    grid_spec = pltpu.PrefetchScalarGridSpec(
        num_scalar_prefetch=2,
        grid=(n_rows // Tm,),
        in_specs=[
            pl.BlockSpec((Tm, D), lambda i, te, nv: (jnp.minimum(i, jnp.maximum(nv[0] - 1, 0)), 0)),
            pl.BlockSpec((1, D, 2 * D_EXPERT), lambda i, te, nv: (te[i], 0, 0)),
            pl.BlockSpec((1, D_EXPERT, D), lambda i, te, nv: (te[i], 0, 0)),
        ],
        out_specs=pl.BlockSpec((Tm, D), lambda i, te, nv: (i, 0)),
        scratch_shapes=[pltpu.VMEM((D, 2 * D_EXPERT), BF16), pltpu.VMEM((D_EXPERT, D), BF16)],
    )
    return pl.pallas_call(
        _experts_kernel,
        grid_spec=grid_spec,
        out_shape=jax.ShapeDtypeStruct((n_rows, D), F32),
        compiler_params=_params("arbitrary"),
        name="moe_experts",
    )(tile_expert, n_valid, xs, w_gu, w_dn)


def _combine_kernel(pos0_ref, pos1_ref, next0_ref, next1_ref, y_hbm, h_ref, cw_ref, g_ref, b_ref, o_ref,
                    ybuf_a, ybuf_b, sem_a, sem_b):
    G = ROUTE_TOKENS
    C = ROW_DMA_UNROLL
    i = pl.program_id(0)
    last = pl.num_programs(0) - 1

    def gather(pos_refs, buf, sem, t0):
        for u in range(C):
            for slot, pos_ref in enumerate(pos_refs):
                pltpu.make_async_copy(y_hbm.at[pl.ds(pos_ref[t0 + u], 1)], buf.at[slot, pl.ds(t0 + u, 1)],
                                      sem).start()

    def wait_tile(buf, sem):
        for slot in range(2):
            pltpu.make_async_copy(y_hbm.at[pl.ds(0, G)], buf.at[slot], sem).wait()

    @pl.when(i == 0)
    def _():
        def first_tile(k, carry):
            gather((pos0_ref, pos1_ref), ybuf_a, sem_a, pl.multiple_of(k * C, C))
            return carry
        lax.fori_loop(0, G // C, first_tile, 0)

    def run(buf, sem, spare, spare_sem):
        wait_tile(buf, sem)

        @pl.when(i < last)
        def _():
            def next_tile(k, carry):
                gather((next0_ref, next1_ref), spare, spare_sem, pl.multiple_of(k * C, C))
                return carry
            lax.fori_loop(0, G // C, next_tile, 0)

        for r in range(G // COMBINE_CHUNK):
            rows = slice(r * COMBINE_CHUNK, (r + 1) * COMBINE_CHUNK)
            cw = cw_ref[rows, :]
            ffn = buf[0, rows, :] * cw[:, 0:1] + buf[1, rows, :] * cw[:, 1:2]
            o_ref[rows, :] = _layer_norm_rows(ALPHA * h_ref[rows, :] + ffn, g_ref[...], b_ref[...])

    odd = jnp.bitwise_and(i, 1)

    @pl.when(odd == 0)
    def _():
        run(ybuf_a, sem_a, ybuf_b, sem_b)

    @pl.when(odd == 1)
    def _():
        run(ybuf_b, sem_b, ybuf_a, sem_a)


def _combine(y, h, pos0, pos1, cw_rows, ln_g, ln_b):
    T, D = h.shape
    G = ROUTE_TOKENS
    n = T // G
    smem_blk = pl.BlockSpec((G,), lambda i: (i,), memory_space=pltpu.SMEM)
    smem_next = pl.BlockSpec((G,), lambda i: (jnp.minimum(i + 1, n - 1),), memory_space=pltpu.SMEM)
    return pl.pallas_call(
        _combine_kernel,
        grid=(n,),
        in_specs=[
            smem_blk, smem_blk, smem_next, smem_next,
            pl.BlockSpec(memory_space=pl.ANY),
            pl.BlockSpec((G, D), lambda i: (i, 0)),
            pl.BlockSpec((G, 2), lambda i: (i, 0)),
            pl.BlockSpec((1, D), lambda i: (0, 0)),
            pl.BlockSpec((1, D), lambda i: (0, 0)),
        ],
        out_specs=pl.BlockSpec((G, D), lambda i: (i, 0)),
        scratch_shapes=[pltpu.VMEM((2, G, D), F32), pltpu.VMEM((2, G, D), F32),
                        pltpu.SemaphoreType.DMA(()), pltpu.SemaphoreType.DMA(())],
        out_shape=jax.ShapeDtypeStruct((T, D), F32),
        compiler_params=_params("arbitrary"),
        name="moe_combine_ln",
    )(pos0, pos1, pos0, pos1, y, h, cw_rows, ln_g.reshape(1, D), ln_b.reshape(1, D))


def kernel(x, w_in, attn_out_gain, rel_bias_table, w_out, ln1_g, ln1_b, w_router_group,
           b_router_group, w_router_expert, b_router_expert, w_gate_up, w_down, ln2_g, ln2_b):
    B, S, D = x.shape
    assert D == D_MODEL and S % MOBA_BLOCK == 0 and S % INPROJ_TOKENS == 0
    assert w_in.shape[0] == DEPTH
    T = B * S
    h = x
    for l in range(DEPTH):
        proj_t, k_nat = _inproj(h, w_in[l])
        attn_t = _moba(proj_t, k_nat, rel_bias_table)
        ret_t = _retention(proj_t)
        h1, eid, rank, cw, counts = _outproj(
            attn_t, ret_t, h, attn_out_gain[l], w_out[l], ln1_g[l], ln1_b[l],
            w_router_group[l], b_router_group[l], w_router_expert[l], b_router_expert[l])

        Tm = MOE_ROWS
        n_tiles = (2 * T) // Tm + N_EXPERTS
        padded = ((counts[:, 0] + Tm - 1) // Tm) * Tm
        ends = jnp.cumsum(padded)
        starts = (ends - padded).astype(jnp.int32)
        tile_row0 = jnp.arange(n_tiles, dtype=jnp.int32) * Tm
        tile_expert = jnp.sum((ends[None, :] <= jnp.minimum(tile_row0, ends[-1] - Tm)[:, None]).astype(jnp.int32),
                              axis=1)
        n_valid = (ends[-1:] // Tm).astype(jnp.int32)

        expert_ids = jnp.arange(N_EXPERTS, dtype=jnp.int32)[:, None, None]
        pos = jnp.sum(jnp.where(eid[None] == expert_ids, starts[:, None, None], 0), axis=0) + rank

        xs = _dispatch(h1, pos[0], pos[1], starts, ends.astype(jnp.int32), n_tiles * Tm)
        y = _experts(xs, w_gate_up[l], w_down[l], tile_expert, n_valid)
        h2 = _combine(y, h1, pos[0], pos[1], cw.T, ln2_g[l], ln2_b[l])
        h = h2.reshape(B, S, D)
    return h
```

```python
import functools
import math

import numpy as np
import jax
import jax.numpy as jnp
from jax import lax
from jax.experimental import pallas as pl
from jax.experimental.pallas import tpu as pltpu

F32 = jnp.float32
BF16 = jnp.bfloat16

D_MODEL = 1024
ATTN_HEADS = 8
ATTN_HEAD_DIM = 64
ATTN_WIDTH = ATTN_HEADS * ATTN_HEAD_DIM
RET_HEADS = 4
RET_HEAD_DIM = 128
RET_WIDTH = RET_HEADS * RET_HEAD_DIM
N_IN_SLICES = 7
PROJT_SLICES = N_IN_SLICES - 1
HEAD_PAIR = 2 * ATTN_HEAD_DIM
BF16_SUBLANES = 16
F32_SUBLANES = 8
V_ROWS = ATTN_HEAD_DIM + BF16_SUBLANES
PROJT_V_ROW0 = (PROJT_SLICES - 1) * ATTN_WIDTH
PROJT_WIDTH = PROJT_V_ROW0 + ATTN_HEADS * V_ROWS
MOBA_BLOCK = 256
MOBA_TOPK = 3
RET_CHUNK = 256
REL_BUCKETS = 32
REL_MAX_DIST = 128
ROPE_BASE = 10000.0
N_GROUPS = 4
EXPERTS_PER_GROUP = 8
N_EXPERTS = N_GROUPS * EXPERTS_PER_GROUP
D_EXPERT = D_MODEL // 2
DEPTH = 1
ALPHA = (2.0 * DEPTH) ** 0.25
LN_EPS = 1e-5

LOG2E = math.log2(math.e)
Q_SCALE = ATTN_HEAD_DIM ** -0.5 * LOG2E

MASKED = -1e30

INPROJ_TOKENS = 1024
INPROJ_ROWS = ATTN_WIDTH
OUTPROJ_TOKENS = 1024
MOBA_UNROLL = 2
MOBA_HEADS = 4
MOBA_QTILES = 4
MOE_ROWS = 512
DISPATCH_TOKENS = 2048
ROUTE_TOKENS = 1024
COMBINE_CHUNK = 256
PACK_CHUNK = 256
ROW_DMA_UNROLL = 32
ROUTER_EXPERT_ROW0 = 8
ROUTER_ROWS = -(-(ROUTER_EXPERT_ROW0 + N_EXPERTS) // BF16_SUBLANES) * BF16_SUBLANES

V7X_VMEM_BYTES = 64 * 1024 * 1024
VMEM_LIMIT = V7X_VMEM_BYTES * 3 // 4

_TN = (((0,), (0,)), ((), ()))
_NT = (((1,), (1,)), ((), ()))


def _params(*sem):
    return pltpu.CompilerParams(dimension_semantics=sem, vmem_limit_bytes=VMEM_LIMIT)


def _inproj_kernel(x_ref, wt_ref, wk_ref, o_ref, k_ref):
    xb = x_ref[0].astype(BF16)
    n_tok = xb.shape[0]
    ones_row = (lax.broadcasted_iota(jnp.int32, (BF16_SUBLANES, n_tok), 0) == 0).astype(BF16)
    for c in range(PROJT_SLICES):
        rows = slice(c * INPROJ_ROWS, (c + 1) * INPROJ_ROWS)
        acc = lax.dot_general(wt_ref[rows, :], xb, _NT, preferred_element_type=F32)
        if c == 0:
            acc = acc * Q_SCALE
        if c < PROJT_SLICES - 1:
            o_ref[0, rows, :] = acc.astype(BF16)
        else:
            for h in range(ATTN_HEADS):
                r0 = PROJT_V_ROW0 + h * V_ROWS
                o_ref[0, r0:r0 + ATTN_HEAD_DIM, :] = acc[h * ATTN_HEAD_DIM:(h + 1) * ATTN_HEAD_DIM].astype(BF16)
                o_ref[0, r0 + ATTN_HEAD_DIM:r0 + V_ROWS, :] = ones_row
    k_ref[0] = jnp.dot(xb, wk_ref[...], preferred_element_type=F32).astype(BF16)


def _inproj(x, w_in):
    B, S, D = x.shape
    W = ATTN_WIDTH
    w_t = jnp.concatenate([w_in[:, :W], w_in[:, 3 * W:], w_in[:, 2 * W:3 * W]], axis=1).T.astype(BF16)
    w_k = w_in[:, W:2 * W].astype(BF16)
    return pl.pallas_call(
        _inproj_kernel,
        grid=(B, S // INPROJ_TOKENS),
        in_specs=[
            pl.BlockSpec((1, INPROJ_TOKENS, D), lambda b, i: (b, i, 0)),
            pl.BlockSpec((PROJT_SLICES * W, D), lambda b, i: (0, 0)),
            pl.BlockSpec((D, W), lambda b, i: (0, 0)),
        ],
        out_specs=[
            pl.BlockSpec((1, PROJT_WIDTH, INPROJ_TOKENS), lambda b, i: (b, 0, i)),
            pl.BlockSpec((1, INPROJ_TOKENS, W), lambda b, i: (b, i, 0)),
        ],
        out_shape=[
            jax.ShapeDtypeStruct((B, PROJT_WIDTH, S), BF16),
            jax.ShapeDtypeStruct((B, S, W), BF16),
        ],
        compiler_params=_params("arbitrary", "arbitrary"),
        name="inproj",
    )(x, w_t, w_k)


def _moba_kernel(far_ref, q_ref, k_ref, v_ref, bias_ref, o_ref,
                 kmean_ref, qz_ref, sel_ref, farsel_ref,
                 s0_ref, smax0_ref, p0_ref, alpha0_ref, s1_ref, smax1_ref, p1_ref, alpha1_ref,
                 m_ref, acc_ref, *, n_blocks):
    L = MOBA_BLOCK
    Dh = ATTN_HEAD_DIM
    HG = MOBA_HEADS
    group = pl.program_id(1)
    qi0 = pl.program_id(2) * MOBA_QTILES
    last_blk = n_blocks - 1
    units = [(a, h) for a in range(MOBA_QTILES) for h in range(HG)]
    s_ref, smax_ref = (s0_ref, s1_ref), (smax0_ref, smax1_ref)
    p_ref, alpha_ref = (p0_ref, p1_ref), (alpha0_ref, alpha1_ref)

    @pl.when(qi0 == 0)
    def _():
        chunk_blocks = min(8, n_blocks)
        chunk = chunk_blocks * L
        acc = jnp.zeros((n_blocks, HG * Dh), F32)
        for c in range(n_blocks // chunk_blocks):
            blk = lax.broadcasted_iota(jnp.int32, (n_blocks, chunk), 0)
            key = lax.broadcasted_iota(jnp.int32, (n_blocks, chunk), 1)
            ind = jnp.where(blk == c * chunk_blocks + key // L, 1.0 / L, 0.0).astype(BF16)
            acc = acc + jnp.dot(ind, k_ref[0, c * chunk:(c + 1) * chunk, :], preferred_element_type=F32)
        hi = acc.astype(BF16)
        lo = (acc - hi.astype(F32)).astype(BF16)
        for pair in range(HG // 2):
            cols = slice(pair * HEAD_PAIR, (pair + 1) * HEAD_PAIR)
            kmean_ref[pair, 0:n_blocks, :] = hi[:, cols]
            kmean_ref[pair, n_blocks:2 * n_blocks, :] = lo[:, cols]

    blk = lax.broadcasted_iota(jnp.int32, (n_blocks, L), 0)
    for u, (a, h) in enumerate(units):
        qi = qi0 + a
        pair, hd = divmod(h, 2)
        past = blk < qi
        qh = q_ref[0, h * Dh:(h + 1) * Dh, a * L:(a + 1) * L]
        no_q = jnp.zeros_like(qh)
        qz = jnp.concatenate([qh, no_q] if hd == 0 else [no_q, qh], axis=0)
        qz_ref[u] = qz
        gate = jnp.dot(kmean_ref[pair], qz, preferred_element_type=F32)
        gate = gate[0:n_blocks] + gate[n_blocks:2 * n_blocks]
        gate = jnp.where(past, gate, -jnp.inf)
        for _ in range(MOBA_TOPK):
            top = jnp.max(gate, axis=0, keepdims=True)
            first = jnp.min(jnp.where(gate == top, blk, n_blocks), axis=0, keepdims=True)
            gate = jnp.where(blk == first, -jnp.inf, gate)
        knocked_out = gate == -jnp.inf
        sel_ref[u] = jnp.where(knocked_out, jnp.where(past, 0.0, MASKED), MASKED)
        far_row = jnp.where(blk < qi - 1, far_ref[group * HG + h], MASKED)
        farsel_ref[u, 0:n_blocks, :] = jnp.where(knocked_out, far_row, MASKED)
        farsel_ref[u, n_blocks:n_blocks + F32_SUBLANES, :] = jnp.zeros((F32_SUBLANES, L), F32)

    def stage_scores(u, j, tile, slot):
        pair = units[u][1] // 2
        kb = k_ref[0, pl.ds(pl.multiple_of(j * L, L), L), pair * HEAD_PAIR:(pair + 1) * HEAD_PAIR]
        s = jnp.dot(kb, qz_ref[u], preferred_element_type=F32)
        if tile is not None:
            s = s + tile
        s_ref[slot][u] = s
        smax_ref[slot][u] = jnp.max(s, axis=0, keepdims=True)

    def stage_softmax(u, row, slot):
        m_prev = m_ref[u]
        m_new = jnp.maximum(m_prev, smax_ref[slot][u] + row)
        alpha_ref[slot][u] = jnp.exp2(m_prev - m_new)
        p_ref[slot][u] = jnp.exp2(s_ref[slot][u] - (m_new - row)).astype(BF16)
        m_ref[u] = m_new

    def stage_values(u, j, slot):
        h = units[u][1]
        vb = v_ref[0, h * V_ROWS:(h + 1) * V_ROWS, pl.ds(pl.multiple_of(j * L, L), L)]
        pv = jnp.dot(vb, p_ref[slot][u], preferred_element_type=F32)
        acc_ref[u] = alpha_ref[slot][u] * acc_ref[u] + pv

    m_ref[...] = jnp.full(m_ref.shape, MASKED, F32)
    acc_ref[...] = jnp.zeros(acc_ref.shape, F32)

    zero_row = jnp.zeros((1, L), F32)
    own = [qi0 + a for a, _ in units]
    prev = [jnp.maximum(qi0 + a - 1, 0) for a, _ in units]
    for u, (a, h) in enumerate(units):
        stage_scores(u, own[u], bias_ref[0, h], 0)
        stage_scores(u, prev[u], bias_ref[1, h] + sel_ref[u, pl.ds(prev[u], 1), :], 1)
    for u in range(len(units)):
        stage_softmax(u, zero_row, 0)
    everyone = range(len(units))

    def step(t, slot, score_units=everyone, softmax_units=everyone, value_units=everyone):
        far_value = jnp.minimum(t - 2, last_blk)
        j = jnp.minimum(t, last_blk)
        jr = jnp.where(t == 0, n_blocks, jnp.minimum(t - 1, last_blk))
        for u in everyone:
            if u in score_units:
                stage_scores(u, j, None, slot)
            if u in softmax_units:
                stage_softmax(u, farsel_ref[u, pl.ds(jr, 1), :], 1 - slot)
            if u in value_units:
                stage_values(u, jnp.where(t == 0, own[u], jnp.where(t == 1, prev[u], far_value)), slot)

    def run_steps(trips, unroll):
        def unrolled_steps(k, carry):
            for s in range(unroll):
                step(unroll * k + s, s % 2)
            return carry
        lax.fori_loop(0, trips, unrolled_steps, 0)

    QT = MOBA_QTILES

    @pl.when(qi0 < QT)
    def _():
        n_steps = 2 + jnp.maximum(qi0 + QT - 2, 0)
        run_steps((n_steps + 1) // 2, 2)

    @pl.when(qi0 >= QT)
    def _():
        n_main = qi0 - 2
        left = (MOBA_UNROLL - 2) % MOBA_UNROLL
        run_steps((n_main - left) // MOBA_UNROLL, MOBA_UNROLL)
        for k in range(-left, QT + 2):
            with_block = lambda first: [u for u, (a, _) in enumerate(units) if a >= first]
            step(n_main + k, k % 2, with_block(k), with_block(k - 1), with_block(k - 2))
    for u, (a, h) in enumerate(units):
        o_ref[0, h * Dh:(h + 1) * Dh, a * L:(a + 1) * L] = acc_ref[u, 0:Dh, :] / acc_ref[u, Dh:Dh + 1, :]


def _t5_bucket_np(dist):
    n = np.maximum(dist, 0)
    max_exact = REL_BUCKETS // 2
    nf = np.maximum(n, 1).astype(np.float32)
    large = max_exact + (np.log(nf / max_exact) / math.log(REL_MAX_DIST / max_exact)
                         * (REL_BUCKETS - max_exact)).astype(np.int32)
    large = np.minimum(large, REL_BUCKETS - 1)
    return np.where(n < max_exact, n, large)


def _moba(proj_t, k_nat, rel_table):
    B, _, S = proj_t.shape
    L = MOBA_BLOCK
    n_blocks = S // L
    H = ATTN_HEADS
    table_t = rel_table.astype(F32).T
    dist = np.arange(-(L - 1), 2 * L)
    onehot = jnp.asarray(_t5_bucket_np(dist))[None, None, :] == jnp.arange(REL_BUCKETS)[None, :, None]
    by_dist = jnp.sum(jnp.where(onehot, table_t[:, :, None], 0.0), axis=1) * LOG2E
    by_dist = jnp.where(jnp.asarray(dist >= 0)[None, :], by_dist, MASKED)

    def toeplitz(vec):
        W = 2 * L - 1
        flat = jnp.broadcast_to(vec[:, None, :], (H, L, W)).reshape(H, L * W)
        return flat[:, L - 1:L - 1 + L * (W - 1)].reshape(H, L, W - 1)[:, :, :L]

    bias = jnp.stack([toeplitz(by_dist[:, :2 * L - 1]), toeplitz(by_dist[:, L:])])
    assert int(_t5_bucket_np(np.array(L + 1))) == REL_BUCKETS - 1
    far = table_t[:, REL_BUCKETS - 1] * LOG2E

    HG, QT = MOBA_HEADS, MOBA_QTILES
    U = HG * QT
    q_rows, v_rows = HG * ATTN_HEAD_DIM, HG * V_ROWS
    assert ATTN_HEADS % HG == 0 and HG % 2 == 0 and n_blocks % QT == 0 and PROJT_V_ROW0 % v_rows == 0
    assert MOBA_UNROLL % 2 == 0 and QT % MOBA_UNROLL == 0
    once = dict(pipeline_mode=pl.Buffered(1))
    grid_spec = pltpu.PrefetchScalarGridSpec(
        num_scalar_prefetch=1,
        grid=(B, ATTN_HEADS // HG, n_blocks // QT),
        in_specs=[
            pl.BlockSpec((1, q_rows, QT * L), lambda b, g, i, far: (b, g, i)),
            pl.BlockSpec((1, S, q_rows), lambda b, g, i, far: (b, 0, g), **once),
            pl.BlockSpec((1, v_rows, S), lambda b, g, i, far: (b, PROJT_V_ROW0 // v_rows + g, 0), **once),
            pl.BlockSpec((2, HG, L, L), lambda b, g, i, far: (0, g, 0, 0), **once),
        ],
        out_specs=pl.BlockSpec((1, q_rows, QT * L), lambda b, g, i, far: (b, g, i)),
        scratch_shapes=[
            pltpu.VMEM((HG // 2, 2 * n_blocks, HEAD_PAIR), BF16),
            pltpu.VMEM((U, HEAD_PAIR, L), BF16),
            pltpu.VMEM((U, n_blocks, L), F32),
            pltpu.VMEM((U, n_blocks + F32_SUBLANES, L), F32),
            *([pltpu.VMEM((U, L, L), F32), pltpu.VMEM((U, 1, L), F32),
               pltpu.VMEM((U, L, L), BF16), pltpu.VMEM((U, 1, L), F32)] * 2),
            pltpu.VMEM((U, 1, L), F32),
            pltpu.VMEM((U, V_ROWS, L), F32),
        ],
    )
    return pl.pallas_call(
        functools.partial(_moba_kernel, n_blocks=n_blocks),
        grid_spec=grid_spec,
        out_shape=jax.ShapeDtypeStruct((B, ATTN_WIDTH, S), F32),
        compiler_params=_params("arbitrary", "arbitrary", "arbitrary"),
        name="moba",
    )(far, proj_t, k_nat, proj_t, bias)


def _retention_kernel(cdec_ref, q_ref, k_ref, v_ref, g_ref, cos_ref, sin_ref,
                      dec_ref, qdec_ref, kdec_ref, o_ref, state_ref):
    c = pl.program_id(0)

    @pl.when(c == 0)
    def _():
        state_ref[...] = jnp.zeros(state_ref.shape, F32)

    cos = cos_ref[...]
    sin = sin_ref[...]
    half = RET_HEAD_DIM // 2

    def rope(t):
        t1, t2 = t[:half], t[half:]
        return jnp.concatenate([t1 * cos - t2 * sin, t1 * sin + t2 * cos], axis=0)

    for b in range(q_ref.shape[0]):
        for hh in range(RET_HEADS):
            rows = slice(hh * RET_HEAD_DIM, (hh + 1) * RET_HEAD_DIM)
            q = rope(q_ref[b, rows, :].astype(F32))
            k = rope(k_ref[b, rows, :].astype(F32)) * (RET_HEAD_DIM ** -0.5)
            v = v_ref[b, rows, :]
            qb = q.astype(BF16)
            s = lax.dot_general(k.astype(BF16), qb, _TN, preferred_element_type=F32)
            s = (s * dec_ref[hh]).astype(BF16)
            inner = jnp.dot(v, s, preferred_element_type=F32)
            state = state_ref[b, hh]
            cross = jnp.dot(state.astype(BF16), (q * qdec_ref[hh:hh + 1, :]).astype(BF16),
                            preferred_element_type=F32)
            kd = (k * kdec_ref[hh:hh + 1, :]).astype(BF16)
            state_ref[b, hh] = state * cdec_ref[hh] + lax.dot_general(v, kd, _NT, preferred_element_type=F32)
            out = inner + cross
            mu = jnp.mean(out, axis=0, keepdims=True)
            var = jnp.mean(jnp.square(out - mu), axis=0, keepdims=True)
            out = (out - mu) * lax.rsqrt(var + LN_EPS)
            g = g_ref[b, rows, :].astype(F32)
            o_ref[b, rows, :] = (g * jax.nn.sigmoid(g) * out).astype(BF16)


def _retention(proj_t):
    B, _, S = proj_t.shape
    C = RET_CHUNK
    H = RET_HEADS
    half = RET_HEAD_DIM // 2
    inv = ROPE_BASE ** (-jnp.arange(half, dtype=F32) / half)
    ang = inv[:, None] * jnp.arange(S).astype(F32)[None, :]
    cos_t, sin_t = jnp.cos(ang), jnp.sin(ang)
    gammas = 1.0 - jnp.exp(jnp.linspace(math.log(1.0 / 32), math.log(1.0 / 512), H, dtype=F32))
    log_g = jnp.log(gammas)
    idx = jnp.arange(C, dtype=F32)
    diff = idx[None, :] - idx[:, None]
    dec_t = jnp.where(diff[None] >= 0, jnp.exp(jnp.maximum(diff, 0.0)[None] * log_g[:, None, None]), 0.0)
    q_dec = jnp.exp((idx[None, :] + 1.0) * log_g[:, None])
    k_dec = jnp.exp((C - 1.0 - idx[None, :]) * log_g[:, None])
    chunk_dec = jnp.exp(C * log_g)

    grid_spec = pltpu.PrefetchScalarGridSpec(
        num_scalar_prefetch=1,
        grid=(S // C,),
        in_specs=[
            pl.BlockSpec((B, RET_WIDTH, C), lambda c, cd: (0, 1, c)),
            pl.BlockSpec((B, RET_WIDTH, C), lambda c, cd: (0, 2, c)),
            pl.BlockSpec((B, RET_WIDTH, C), lambda c, cd: (0, 3, c)),
            pl.BlockSpec((B, RET_WIDTH, C), lambda c, cd: (0, 4, c)),
            pl.BlockSpec((half, C), lambda c, cd: (0, c)),
            pl.BlockSpec((half, C), lambda c, cd: (0, c)),
            pl.BlockSpec((H, C, C), lambda c, cd: (0, 0, 0)),
            pl.BlockSpec((H, C), lambda c, cd: (0, 0)),
            pl.BlockSpec((H, C), lambda c, cd: (0, 0)),
        ],
        out_specs=pl.BlockSpec((B, RET_WIDTH, C), lambda c, cd: (0, 0, c)),
        scratch_shapes=[pltpu.VMEM((B, H, RET_HEAD_DIM, RET_HEAD_DIM), F32)],
    )
    return pl.pallas_call(
        _retention_kernel,
        grid_spec=grid_spec,
        out_shape=jax.ShapeDtypeStruct((B, RET_WIDTH, S), BF16),
        compiler_params=_params("arbitrary"),
        name="retention",
    )(chunk_dec, proj_t, proj_t, proj_t, proj_t, cos_t, sin_t, dec_t, q_dec, k_dec)


def _layer_norm_rows(y, g, b):
    mu = jnp.mean(y, axis=-1, keepdims=True)
    var = jnp.mean(jnp.square(y - mu), axis=-1, keepdims=True)
    return (y - mu) * lax.rsqrt(var + LN_EPS) * g + b


def _outproj_kernel(attn_ref, ret_ref, x_ref, gain_ref, wa_ref, wr_ref, g1_ref, b1_ref,
                    wrt_ref, brt_ref, before_ref,
                    h_ref, eid_ref, rank_ref, cw_ref, cnt_ref, carry_ref):
    first = jnp.logical_and(pl.program_id(0) == 0, pl.program_id(1) == 0)

    @pl.when(first)
    def _():
        carry_ref[...] = jnp.zeros(carry_ref.shape, F32)

    a = attn_ref[0]
    a = a * lax.rsqrt(jnp.mean(jnp.square(a), axis=0, keepdims=True) + LN_EPS) * gain_ref[...]
    mix = (lax.dot_general(a.astype(BF16), wa_ref[...], _TN, preferred_element_type=F32)
           + lax.dot_general(ret_ref[0], wr_ref[...], _TN, preferred_element_type=F32))
    h = _layer_norm_rows(ALPHA * x_ref[0] + mix, g1_ref[...], b1_ref[...])
    h_ref[...] = h

    R = ROUTER_ROWS
    h_hi = h.astype(BF16)
    h_lo = (h - h_hi.astype(F32)).astype(BF16)
    by_hi = lax.dot_general(wrt_ref[...], h_hi, _NT, preferred_element_type=F32)
    by_lo = lax.dot_general(wrt_ref[0:R, :], h_lo, _NT, preferred_element_type=F32)
    logits = by_hi[0:R] + by_hi[R:2 * R] + by_lo + brt_ref[...]
    T = logits.shape[1]
    gl = logits[0:N_GROUPS]
    gmax = jnp.max(gl, axis=0, keepdims=True)
    grow = lax.broadcasted_iota(jnp.int32, gl.shape, 0)
    gidx = jnp.min(jnp.where(gl == gmax, grow, N_GROUPS), axis=0, keepdims=True)
    g_p = 1.0 / jnp.sum(jnp.exp(gl - gmax), axis=0, keepdims=True)
    el = jnp.zeros((EXPERTS_PER_GROUP, T), F32)
    for g in range(N_GROUPS):
        r0 = ROUTER_EXPERT_ROW0 + g * EXPERTS_PER_GROUP
        el = jnp.where(gidx == g, logits[r0:r0 + EXPERTS_PER_GROUP], el)
    erow = lax.broadcasted_iota(jnp.int32, el.shape, 0)
    e1 = jnp.max(el, axis=0, keepdims=True)
    i1 = jnp.min(jnp.where(el == e1, erow, EXPERTS_PER_GROUP), axis=0, keepdims=True)
    el2 = jnp.where(erow == i1, -jnp.inf, el)
    e2 = jnp.max(el2, axis=0, keepdims=True)
    i2 = jnp.min(jnp.where(el2 == e2, erow, EXPERTS_PER_GROUP), axis=0, keepdims=True)
    r = jnp.exp(e2 - e1)
    w1 = g_p / (1.0 + r)
    w2 = g_p * r / (1.0 + r)
    id1 = gidx * EXPERTS_PER_GROUP + i1
    id2 = gidx * EXPERTS_PER_GROUP + i2
    eid_ref[0:1, :] = id1
    eid_ref[1:2, :] = id2
    cw_ref[0:1, :] = w1
    cw_ref[1:2, :] = w2

    xrow = lax.broadcasted_iota(jnp.int32, (N_EXPERTS, T), 0)
    oh1 = (xrow == id1).astype(F32)
    oh2 = (xrow == id2).astype(F32)
    both = oh1 + oh2
    seen = jnp.dot(both.astype(BF16), before_ref[...], preferred_element_type=F32) + carry_ref[...]
    rank_ref[0:1, :] = jnp.sum(oh1 * seen, axis=0, keepdims=True).astype(jnp.int32)
    rank_ref[1:2, :] = jnp.sum(oh2 * seen, axis=0, keepdims=True).astype(jnp.int32)
    carry = carry_ref[...] + jnp.sum(both, axis=1, keepdims=True)
    carry_ref[...] = carry
    cnt_ref[...] = carry.astype(jnp.int32)


def _outproj(attn_t, ret_t, x, gain, w_out, ln_g, ln_b, w_rg, b_rg, w_re, b_re):
    B, S, D = x.shape
    Tt = OUTPROJ_TOKENS
    n_t = S // Tt
    T = B * S
    wa = w_out[:ATTN_WIDTH].astype(BF16)
    wr = w_out[ATTN_WIDTH:].astype(BF16)
    e0, e1 = ROUTER_EXPERT_ROW0, ROUTER_EXPERT_ROW0 + N_EXPERTS
    wrt = jnp.zeros((ROUTER_ROWS, D), F32)
    wrt = wrt.at[:N_GROUPS].set(w_rg.T).at[e0:e1].set(w_re.T)
    wrt_hi = wrt.astype(BF16)
    wrt_lo = (wrt - wrt_hi.astype(F32)).astype(BF16)
    wrt = jnp.concatenate([wrt_hi, wrt_lo], axis=0)
    brt = jnp.zeros((ROUTER_ROWS, 1), F32)
    brt = brt.at[:N_GROUPS, 0].set(b_rg.astype(F32)).at[e0:e1, 0].set(b_re.astype(F32))
    before = (jnp.arange(Tt)[:, None] < jnp.arange(Tt)[None, :]).astype(BF16)
    const = lambda b, i: (0, 0)
    tok = lambda b, i: (0, b * n_t + i)
    return pl.pallas_call(
        _outproj_kernel,
        grid=(B, n_t),
        in_specs=[
            pl.BlockSpec((1, ATTN_WIDTH, Tt), lambda b, i: (b, 0, i)),
            pl.BlockSpec((1, RET_WIDTH, Tt), lambda b, i: (b, 0, i)),
            pl.BlockSpec((1, Tt, D), lambda b, i: (b, i, 0)),
            pl.BlockSpec((ATTN_WIDTH, 1), const),
            pl.BlockSpec((ATTN_WIDTH, D), const),
            pl.BlockSpec((RET_WIDTH, D), const),
            pl.BlockSpec((1, D), const),
            pl.BlockSpec((1, D), const),
            pl.BlockSpec((2 * ROUTER_ROWS, D), const),
            pl.BlockSpec((ROUTER_ROWS, 1), const),
            pl.BlockSpec((Tt, Tt), const),
        ],
        out_specs=[
            pl.BlockSpec((Tt, D), lambda b, i: (b * n_t + i, 0)),
            pl.BlockSpec((2, Tt), tok),
            pl.BlockSpec((2, Tt), tok),
            pl.BlockSpec((2, Tt), tok),
            pl.BlockSpec((N_EXPERTS, 1), const),
        ],
        out_shape=[
            jax.ShapeDtypeStruct((T, D), F32),
            jax.ShapeDtypeStruct((2, T), jnp.int32),
            jax.ShapeDtypeStruct((2, T), jnp.int32),
            jax.ShapeDtypeStruct((2, T), F32),
            jax.ShapeDtypeStruct((N_EXPERTS, 1), jnp.int32),
        ],
        scratch_shapes=[pltpu.VMEM((N_EXPERTS, 1), F32)],
        compiler_params=_params("arbitrary", "arbitrary"),
        name="outproj_ln_router",
    )(attn_t, ret_t, x, gain.reshape(ATTN_WIDTH, 1), wa, wr, ln_g.reshape(1, D), ln_b.reshape(1, D),
      wrt, brt, before)


def _bf16_bits(x):
    b = lax.bitcast_convert_type(x, jnp.int32)
    return lax.shift_right_logical(b + 0x7FFF + (lax.shift_right_logical(b, 16) & 1), 16)


def _dispatch_kernel(start_ref, end_ref, pos0_ref, pos1_ref, h_ref, xs_hbm, packed_ref, zero_ref, sem, zsem):
    G = h_ref.shape[0]
    Tm = MOE_ROWS
    half = h_ref.shape[1] // 2

    for r in range(G // PACK_CHUNK):
        rows = slice(r * PACK_CHUNK, (r + 1) * PACK_CHUNK)
        packed_ref[rows, :] = (_bf16_bits(h_ref[rows, 0:half])
                               | lax.shift_left(_bf16_bits(h_ref[rows, half:2 * half]), 16))

    @pl.when(pl.program_id(0) == 0)
    def _():
        zero_ref[...] = jnp.zeros(zero_ref.shape, jnp.int32)

        def tile_clear(row0):
            row0 = pl.multiple_of(row0, Tm)
            return pltpu.make_async_copy(zero_ref, xs_hbm.at[pl.ds(row0, Tm)], zsem)

        used = end_ref[N_EXPERTS - 1]
        n_rows = xs_hbm.shape[0]
        clears = [(end_ref[e] - Tm, end_ref[e] > start_ref[e]) for e in range(N_EXPERTS)]
        clears += [(jnp.minimum(used + k * Tm, n_rows - Tm), used + k * Tm < n_rows) for k in range(N_EXPERTS)]
        for row0, cond in clears:
            @pl.when(cond)
            def _():
                tile_clear(row0).start()
        for row0, cond in clears:
            @pl.when(cond)
            def _():
                tile_clear(row0).wait()

    def issue(k, carry):
        t0 = pl.multiple_of(k * ROW_DMA_UNROLL, ROW_DMA_UNROLL)
        for u in range(ROW_DMA_UNROLL):
            for pos_ref in (pos0_ref, pos1_ref):
                pltpu.make_async_copy(packed_ref.at[pl.ds(t0 + u, 1)], xs_hbm.at[pl.ds(pos_ref[t0 + u], 1)],
                                      sem).start()
        return carry

    lax.fori_loop(0, G // ROW_DMA_UNROLL, issue, 0)
    for _ in range(2):
        pltpu.make_async_copy(packed_ref, xs_hbm.at[pl.ds(0, G)], sem).wait()


def _dispatch(h, pos0, pos1, starts, ends, n_rows):
    T, D = h.shape
    G = DISPATCH_TOKENS if T % DISPATCH_TOKENS == 0 else ROUTE_TOKENS
    smem_blk = pl.BlockSpec((G,), lambda i, st, en: (i,), memory_space=pltpu.SMEM)
    grid_spec = pltpu.PrefetchScalarGridSpec(
        num_scalar_prefetch=2,
        grid=(T // G,),
        in_specs=[smem_blk, smem_blk, pl.BlockSpec((G, D), lambda i, st, en: (i, 0))],
        out_specs=pl.BlockSpec(memory_space=pl.ANY),
        scratch_shapes=[pltpu.VMEM((G, D // 2), jnp.int32), pltpu.VMEM((MOE_ROWS, D // 2), jnp.int32),
                        pltpu.SemaphoreType.DMA(()), pltpu.SemaphoreType.DMA(())],
    )
    return pl.pallas_call(
        _dispatch_kernel,
        grid_spec=grid_spec,
        out_shape=jax.ShapeDtypeStruct((n_rows, D // 2), jnp.int32),
        compiler_params=_params("arbitrary"),
        name="moe_dispatch",
    )(starts, ends, pos0, pos1, h)


def _experts_kernel(te_ref, nv_ref, xs_ref, wgu_ref, wdn_ref, y_ref, wgu_b, wdn_b):
    i = pl.program_id(0)
    live = i < nv_ref[0]
    new_expert = jnp.logical_or(i == 0, te_ref[i] != te_ref[jnp.maximum(i - 1, 0)])

    @pl.when(jnp.logical_and(live, new_expert))
    def _():
        wgu_b[...] = wgu_ref[0].astype(BF16)
        wdn_b[...] = wdn_ref[0].astype(BF16)

    @pl.when(live)
    def _():
        words = xs_ref[...]
        half = words.shape[1]
        x_lo = lax.bitcast_convert_type(lax.shift_left(words, 16), F32).astype(BF16)
        x_hi = lax.bitcast_convert_type(words & -65536, F32).astype(BF16)
        gu = (jnp.dot(x_lo, wgu_b[0:half, :], preferred_element_type=F32)
              + jnp.dot(x_hi, wgu_b[half:2 * half, :], preferred_element_type=F32))
        gate, up = gu[:, :D_EXPERT], gu[:, D_EXPERT:]
        hid = gate * jax.nn.sigmoid(gate) * up
        y_ref[...] = jnp.dot(hid.astype(BF16), wdn_b[...], preferred_element_type=F32)

    @pl.when(jnp.logical_not(live))
    def _():
        y_ref[...] = jnp.zeros(y_ref.shape, F32)


def _experts(xs, w_gu, w_dn, tile_expert, n_valid):
    n_rows, D = xs.shape[0], w_dn.shape[-1]
    Tm = MOE_ROWS
    grid_spec = pltpu.PrefetchScalarGridSpec(
        num_scalar_prefetch=2,
        grid=(n_rows // Tm,),
        in_specs=[
            pl.BlockSpec((Tm, D // 2), lambda i, te, nv: (jnp.minimum(i, jnp.maximum(nv[0] - 1, 0)), 0)),
            pl.BlockSpec((1, D, 2 * D_EXPERT), lambda i, te, nv: (te[i], 0, 0)),
            pl.BlockSpec((1, D_EXPERT, D), lambda i, te, nv: (te[i], 0, 0)),
        ],
        out_specs=pl.BlockSpec((Tm, D), lambda i, te, nv: (i, 0)),
        scratch_shapes=[pltpu.VMEM((D, 2 * D_EXPERT), BF16), pltpu.VMEM((D_EXPERT, D), BF16)],
    )
    return pl.pallas_call(
        _experts_kernel,
        grid_spec=grid_spec,
        out_shape=jax.ShapeDtypeStruct((n_rows, D), F32),
        compiler_params=_params("arbitrary"),
        name="moe_experts",
    )(tile_expert, n_valid, xs, w_gu, w_dn)


def _combine_kernel(pos0_ref, pos1_ref, next0_ref, next1_ref, y_hbm, h_ref, cw_ref, g_ref, b_ref, o_ref,
                    ybuf_a, ybuf_b, sem_a, sem_b):
    G = ROUTE_TOKENS
    C = ROW_DMA_UNROLL
    i = pl.program_id(0)
    last = pl.num_programs(0) - 1

    def gather(pos_refs, buf, sem, t0):
        for u in range(C):
            for slot, pos_ref in enumerate(pos_refs):
                pltpu.make_async_copy(y_hbm.at[pl.ds(pos_ref[t0 + u], 1)], buf.at[slot, pl.ds(t0 + u, 1)],
                                      sem).start()

    def wait_tile(buf, sem):
        for slot in range(2):
            pltpu.make_async_copy(y_hbm.at[pl.ds(0, G)], buf.at[slot], sem).wait()

    @pl.when(i == 0)
    def _():
        def first_tile(k, carry):
            gather((pos0_ref, pos1_ref), ybuf_a, sem_a, pl.multiple_of(k * C, C))
            return carry
        lax.fori_loop(0, G // C, first_tile, 0)

    def run(buf, sem, spare, spare_sem):
        wait_tile(buf, sem)

        @pl.when(i < last)
        def _():
            def next_tile(k, carry):
                gather((next0_ref, next1_ref), spare, spare_sem, pl.multiple_of(k * C, C))
                return carry
            lax.fori_loop(0, G // C, next_tile, 0)

        for r in range(G // COMBINE_CHUNK):
            rows = slice(r * COMBINE_CHUNK, (r + 1) * COMBINE_CHUNK)
            cw = cw_ref[rows, :]
            ffn = buf[0, rows, :] * cw[:, 0:1] + buf[1, rows, :] * cw[:, 1:2]
            o_ref[rows, :] = _layer_norm_rows(ALPHA * h_ref[rows, :] + ffn, g_ref[...], b_ref[...])

    odd = jnp.bitwise_and(i, 1)

    @pl.when(odd == 0)
    def _():
        run(ybuf_a, sem_a, ybuf_b, sem_b)

    @pl.when(odd == 1)
    def _():
        run(ybuf_b, sem_b, ybuf_a, sem_a)


def _combine(y, h, pos0, pos1, cw_rows, ln_g, ln_b):
    T, D = h.shape
    G = ROUTE_TOKENS
    n = T // G
    smem_blk = pl.BlockSpec((G,), lambda i: (i,), memory_space=pltpu.SMEM)
    smem_next = pl.BlockSpec((G,), lambda i: (jnp.minimum(i + 1, n - 1),), memory_space=pltpu.SMEM)
    return pl.pallas_call(
        _combine_kernel,
        grid=(n,),
        in_specs=[
            smem_blk, smem_blk, smem_next, smem_next,
            pl.BlockSpec(memory_space=pl.ANY),
            pl.BlockSpec((G, D), lambda i: (i, 0)),
            pl.BlockSpec((G, 2), lambda i: (i, 0)),
            pl.BlockSpec((1, D), lambda i: (0, 0)),
            pl.BlockSpec((1, D), lambda i: (0, 0)),
        ],
        out_specs=pl.BlockSpec((G, D), lambda i: (i, 0)),
        scratch_shapes=[pltpu.VMEM((2, G, D), F32), pltpu.VMEM((2, G, D), F32),
                        pltpu.SemaphoreType.DMA(()), pltpu.SemaphoreType.DMA(())],
        out_shape=jax.ShapeDtypeStruct((T, D), F32),
        compiler_params=_params("arbitrary"),
        name="moe_combine_ln",
    )(pos0, pos1, pos0, pos1, y, h, cw_rows, ln_g.reshape(1, D), ln_b.reshape(1, D))


def kernel(x, w_in, attn_out_gain, rel_bias_table, w_out, ln1_g, ln1_b, w_router_group,
           b_router_group, w_router_expert, b_router_expert, w_gate_up, w_down, ln2_g, ln2_b):
    B, S, D = x.shape
    assert D == D_MODEL and S % MOBA_BLOCK == 0 and S % INPROJ_TOKENS == 0
    assert w_in.shape[0] == DEPTH
    T = B * S
    h = x
    for l in range(DEPTH):
        proj_t, k_nat = _inproj(h, w_in[l])
        attn_t = _moba(proj_t, k_nat, rel_bias_table)
        ret_t = _retention(proj_t)
        h1, eid, rank, cw, counts = _outproj(
            attn_t, ret_t, h, attn_out_gain[l], w_out[l], ln1_g[l], ln1_b[l],
            w_router_group[l], b_router_group[l], w_router_expert[l], b_router_expert[l])

        Tm = MOE_ROWS
        n_tiles = (2 * T) // Tm + N_EXPERTS
        padded = ((counts[:, 0] + Tm - 1) // Tm) * Tm
        ends = jnp.cumsum(padded)
        starts = (ends - padded).astype(jnp.int32)
        tile_row0 = jnp.arange(n_tiles, dtype=jnp.int32) * Tm
        tile_expert = jnp.sum((ends[None, :] <= jnp.minimum(tile_row0, ends[-1] - Tm)[:, None]).astype(jnp.int32),
                              axis=1)
        n_valid = (ends[-1:] // Tm).astype(jnp.int32)

        expert_ids = jnp.arange(N_EXPERTS, dtype=jnp.int32)[:, None, None]
        pos = jnp.sum(jnp.where(eid[None] == expert_ids, starts[:, None, None], 0), axis=0) + rank

        xs = _dispatch(h1, pos[0], pos[1], starts, ends.astype(jnp.int32), n_tiles * Tm)
        y = _experts(xs, w_gate_up[l], w_down[l], tile_expert, n_valid)
        h2 = _combine(y, h1, pos[0], pos[1], cw.T, ln2_g[l], ln2_b[l])
        h = h2.reshape(B, S, D)
    return h
```

```python
import functools
import math

import numpy as np
import jax
import jax.numpy as jnp
from jax import lax
from jax.experimental import pallas as pl
from jax.experimental.pallas import tpu as pltpu

F32 = jnp.float32
BF16 = jnp.bfloat16

D_MODEL = 1024
ATTN_HEADS = 8
ATTN_HEAD_DIM = 64
ATTN_WIDTH = ATTN_HEADS * ATTN_HEAD_DIM
RET_HEADS = 4
RET_HEAD_DIM = 128
RET_WIDTH = RET_HEADS * RET_HEAD_DIM
N_IN_SLICES = 7
PROJT_SLICES = N_IN_SLICES - 1
HEAD_PAIR = 2 * ATTN_HEAD_DIM
BF16_SUBLANES = 16
F32_SUBLANES = 8
V_ROWS = ATTN_HEAD_DIM + BF16_SUBLANES
PROJT_V_ROW0 = (PROJT_SLICES - 1) * ATTN_WIDTH
PROJT_WIDTH = PROJT_V_ROW0 + ATTN_HEADS * V_ROWS
MOBA_BLOCK = 256
MOBA_TOPK = 3
RET_CHUNK = 256
REL_BUCKETS = 32
REL_MAX_DIST = 128
ROPE_BASE = 10000.0
N_GROUPS = 4
EXPERTS_PER_GROUP = 8
N_EXPERTS = N_GROUPS * EXPERTS_PER_GROUP
D_EXPERT = D_MODEL // 2
DEPTH = 1
ALPHA = (2.0 * DEPTH) ** 0.25
LN_EPS = 1e-5

LOG2E = math.log2(math.e)
Q_SCALE = ATTN_HEAD_DIM ** -0.5 * LOG2E

MASKED = -1e30

INPROJ_TOKENS = 1024
INPROJ_ROWS = ATTN_WIDTH
OUTPROJ_TOKENS = 1024
MOBA_UNROLL = 2
MOBA_HEADS = 4
MOBA_QTILES = 4
MOE_ROWS = 512
DISPATCH_TOKENS = 2048
ROUTE_TOKENS = 1024
COMBINE_CHUNK = 256
ROW_DMA_UNROLL = 32
ROUTER_EXPERT_ROW0 = 8
ROUTER_ROWS = -(-(ROUTER_EXPERT_ROW0 + N_EXPERTS) // BF16_SUBLANES) * BF16_SUBLANES

V7X_VMEM_BYTES = 64 * 1024 * 1024
VMEM_LIMIT = V7X_VMEM_BYTES * 3 // 4

_TN = (((0,), (0,)), ((), ()))
_NT = (((1,), (1,)), ((), ()))


def _params(*sem):
    return pltpu.CompilerParams(dimension_semantics=sem, vmem_limit_bytes=VMEM_LIMIT)


def _inproj_kernel(x_ref, wt_ref, wk_ref, o_ref, k_ref):
    xb = x_ref[0].astype(BF16)
    n_tok = xb.shape[0]
    ones_row = (lax.broadcasted_iota(jnp.int32, (BF16_SUBLANES, n_tok), 0) == 0).astype(BF16)
    for c in range(PROJT_SLICES):
        rows = slice(c * INPROJ_ROWS, (c + 1) * INPROJ_ROWS)
        acc = lax.dot_general(wt_ref[rows, :], xb, _NT, preferred_element_type=F32)
        if c == 0:
            acc = acc * Q_SCALE
        if c < PROJT_SLICES - 1:
            o_ref[0, rows, :] = acc.astype(BF16)
        else:
            for h in range(ATTN_HEADS):
                r0 = PROJT_V_ROW0 + h * V_ROWS
                o_ref[0, r0:r0 + ATTN_HEAD_DIM, :] = acc[h * ATTN_HEAD_DIM:(h + 1) * ATTN_HEAD_DIM].astype(BF16)
                o_ref[0, r0 + ATTN_HEAD_DIM:r0 + V_ROWS, :] = ones_row
    k_ref[0] = jnp.dot(xb, wk_ref[...], preferred_element_type=F32).astype(BF16)


def _inproj(x, w_in):
    B, S, D = x.shape
    W = ATTN_WIDTH
    w_t = jnp.concatenate([w_in[:, :W], w_in[:, 3 * W:], w_in[:, 2 * W:3 * W]], axis=1).T.astype(BF16)
    w_k = w_in[:, W:2 * W].astype(BF16)
    return pl.pallas_call(
        _inproj_kernel,
        grid=(B, S // INPROJ_TOKENS),
        in_specs=[
            pl.BlockSpec((1, INPROJ_TOKENS, D), lambda b, i: (b, i, 0)),
            pl.BlockSpec((PROJT_SLICES * W, D), lambda b, i: (0, 0)),
            pl.BlockSpec((D, W), lambda b, i: (0, 0)),
        ],
        out_specs=[
            pl.BlockSpec((1, PROJT_WIDTH, INPROJ_TOKENS), lambda b, i: (b, 0, i)),
            pl.BlockSpec((1, INPROJ_TOKENS, W), lambda b, i: (b, i, 0)),
        ],
        out_shape=[
            jax.ShapeDtypeStruct((B, PROJT_WIDTH, S), BF16),
            jax.ShapeDtypeStruct((B, S, W), BF16),
        ],
        compiler_params=_params("arbitrary", "arbitrary"),
        name="inproj",
    )(x, w_t, w_k)


def _moba_kernel(far_ref, q_ref, k_ref, v_ref, bias_ref, o_ref,
                 kmean_ref, qz_ref, sel_ref, farsel_ref,
                 s0_ref, smax0_ref, p0_ref, alpha0_ref, s1_ref, smax1_ref, p1_ref, alpha1_ref,
                 m_ref, acc_ref, *, n_blocks):
    L = MOBA_BLOCK
    Dh = ATTN_HEAD_DIM
    HG = MOBA_HEADS
    group = pl.program_id(1)
    qi0 = pl.program_id(2) * MOBA_QTILES
    last_blk = n_blocks - 1
    units = [(a, h) for a in range(MOBA_QTILES) for h in range(HG)]
    s_ref, smax_ref = (s0_ref, s1_ref), (smax0_ref, smax1_ref)
    p_ref, alpha_ref = (p0_ref, p1_ref), (alpha0_ref, alpha1_ref)

    @pl.when(qi0 == 0)
    def _():
        chunk_blocks = min(8, n_blocks)
        chunk = chunk_blocks * L
        acc = jnp.zeros((n_blocks, HG * Dh), F32)
        for c in range(n_blocks // chunk_blocks):
            blk = lax.broadcasted_iota(jnp.int32, (n_blocks, chunk), 0)
            key = lax.broadcasted_iota(jnp.int32, (n_blocks, chunk), 1)
            ind = jnp.where(blk == c * chunk_blocks + key // L, 1.0 / L, 0.0).astype(BF16)
            acc = acc + jnp.dot(ind, k_ref[0, c * chunk:(c + 1) * chunk, :], preferred_element_type=F32)
        hi = acc.astype(BF16)
        lo = (acc - hi.astype(F32)).astype(BF16)
        for pair in range(HG // 2):
            cols = slice(pair * HEAD_PAIR, (pair + 1) * HEAD_PAIR)
            kmean_ref[pair, 0:n_blocks, :] = hi[:, cols]
            kmean_ref[pair, n_blocks:2 * n_blocks, :] = lo[:, cols]

    blk = lax.broadcasted_iota(jnp.int32, (n_blocks, L), 0)
    for u, (a, h) in enumerate(units):
        qi = qi0 + a
        pair, hd = divmod(h, 2)
        past = blk < qi
        qh = q_ref[0, h * Dh:(h + 1) * Dh, a * L:(a + 1) * L]
        no_q = jnp.zeros_like(qh)
        qz = jnp.concatenate([qh, no_q] if hd == 0 else [no_q, qh], axis=0)
        qz_ref[u] = qz
        gate = jnp.dot(kmean_ref[pair], qz, preferred_element_type=F32)
        gate = gate[0:n_blocks] + gate[n_blocks:2 * n_blocks]
        gate = jnp.where(past, gate, -jnp.inf)
        for _ in range(MOBA_TOPK):
            top = jnp.max(gate, axis=0, keepdims=True)
            first = jnp.min(jnp.where(gate == top, blk, n_blocks), axis=0, keepdims=True)
            gate = jnp.where(blk == first, -jnp.inf, gate)
        knocked_out = gate == -jnp.inf
        sel_ref[u] = jnp.where(knocked_out, jnp.where(past, 0.0, MASKED), MASKED)
        far_row = jnp.where(blk < qi - 1, far_ref[group * HG + h], MASKED)
        farsel_ref[u, 0:n_blocks, :] = jnp.where(knocked_out, far_row, MASKED)
        farsel_ref[u, n_blocks:n_blocks + F32_SUBLANES, :] = jnp.zeros((F32_SUBLANES, L), F32)

    def stage_scores(u, j, tile, slot):
        pair = units[u][1] // 2
        kb = k_ref[0, pl.ds(pl.multiple_of(j * L, L), L), pair * HEAD_PAIR:(pair + 1) * HEAD_PAIR]
        s = jnp.dot(kb, qz_ref[u], preferred_element_type=F32)
        if tile is not None:
            s = s + tile
        s_ref[slot][u] = s
        smax_ref[slot][u] = jnp.max(s, axis=0, keepdims=True)

    def stage_softmax(u, row, slot):
        m_prev = m_ref[u]
        m_new = jnp.maximum(m_prev, smax_ref[slot][u] + row)
        alpha_ref[slot][u] = jnp.exp2(m_prev - m_new)
        p_ref[slot][u] = jnp.exp2(s_ref[slot][u] - (m_new - row)).astype(BF16)
        m_ref[u] = m_new

    def stage_values(u, j, slot):
        h = units[u][1]
        vb = v_ref[0, h * V_ROWS:(h + 1) * V_ROWS, pl.ds(pl.multiple_of(j * L, L), L)]
        pv = jnp.dot(vb, p_ref[slot][u], preferred_element_type=F32)
        acc_ref[u] = alpha_ref[slot][u] * acc_ref[u] + pv

    m_ref[...] = jnp.full(m_ref.shape, MASKED, F32)
    acc_ref[...] = jnp.zeros(acc_ref.shape, F32)

    zero_row = jnp.zeros((1, L), F32)
    own = [qi0 + a for a, _ in units]
    prev = [jnp.maximum(qi0 + a - 1, 0) for a, _ in units]
    for u, (a, h) in enumerate(units):
        stage_scores(u, own[u], bias_ref[0, h], 0)
        stage_scores(u, prev[u], bias_ref[1, h] + sel_ref[u, pl.ds(prev[u], 1), :], 1)
    for u in range(len(units)):
        stage_softmax(u, zero_row, 0)
    everyone = range(len(units))

    def step(t, slot, score_units=everyone, softmax_units=everyone, value_units=everyone):
        far_value = jnp.minimum(t - 2, last_blk)
        j = jnp.minimum(t, last_blk)
        jr = jnp.where(t == 0, n_blocks, jnp.minimum(t - 1, last_blk))
        for u in everyone:
            if u in score_units:
                stage_scores(u, j, None, slot)
            if u in softmax_units:
                stage_softmax(u, farsel_ref[u, pl.ds(jr, 1), :], 1 - slot)
            if u in value_units:
                stage_values(u, jnp.where(t == 0, own[u], jnp.where(t == 1, prev[u], far_value)), slot)

    def run_steps(trips, unroll):
        def unrolled_steps(k, carry):
            for s in range(unroll):
                step(unroll * k + s, s % 2)
            return carry
        lax.fori_loop(0, trips, unrolled_steps, 0)

    QT = MOBA_QTILES

    @pl.when(qi0 < QT)
    def _():
        n_steps = 2 + jnp.maximum(qi0 + QT - 2, 0)
        run_steps((n_steps + 1) // 2, 2)

    @pl.when(qi0 >= QT)
    def _():
        n_main = qi0 - 2
        left = (MOBA_UNROLL - 2) % MOBA_UNROLL
        run_steps((n_main - left) // MOBA_UNROLL, MOBA_UNROLL)
        for k in range(-left, QT + 2):
            with_block = lambda first: [u for u, (a, _) in enumerate(units) if a >= first]
            step(n_main + k, k % 2, with_block(k), with_block(k - 1), with_block(k - 2))
    for u, (a, h) in enumerate(units):
        o_ref[0, h * Dh:(h + 1) * Dh, a * L:(a + 1) * L] = acc_ref[u, 0:Dh, :] / acc_ref[u, Dh:Dh + 1, :]


def _t5_bucket_np(dist):
    n = np.maximum(dist, 0)
    max_exact = REL_BUCKETS // 2
    nf = np.maximum(n, 1).astype(np.float32)
    large = max_exact + (np.log(nf / max_exact) / math.log(REL_MAX_DIST / max_exact)
                         * (REL_BUCKETS - max_exact)).astype(np.int32)
    large = np.minimum(large, REL_BUCKETS - 1)
    return np.where(n < max_exact, n, large)


def _moba(proj_t, k_nat, rel_table):
    B, _, S = proj_t.shape
    L = MOBA_BLOCK
    n_blocks = S // L
    H = ATTN_HEADS
    table_t = rel_table.astype(F32).T
    dist = np.arange(-(L - 1), 2 * L)
    onehot = jnp.asarray(_t5_bucket_np(dist))[None, None, :] == jnp.arange(REL_BUCKETS)[None, :, None]
    by_dist = jnp.sum(jnp.where(onehot, table_t[:, :, None], 0.0), axis=1) * LOG2E
    by_dist = jnp.where(jnp.asarray(dist >= 0)[None, :], by_dist, MASKED)

    def toeplitz(vec):
        W = 2 * L - 1
        flat = jnp.broadcast_to(vec[:, None, :], (H, L, W)).reshape(H, L * W)
        return flat[:, L - 1:L - 1 + L * (W - 1)].reshape(H, L, W - 1)[:, :, :L]

    bias = jnp.stack([toeplitz(by_dist[:, :2 * L - 1]), toeplitz(by_dist[:, L:])])
    assert int(_t5_bucket_np(np.array(L + 1))) == REL_BUCKETS - 1
    far = table_t[:, REL_BUCKETS - 1] * LOG2E

    HG, QT = MOBA_HEADS, MOBA_QTILES
    U = HG * QT
    q_rows, v_rows = HG * ATTN_HEAD_DIM, HG * V_ROWS
    assert ATTN_HEADS % HG == 0 and HG % 2 == 0 and n_blocks % QT == 0 and PROJT_V_ROW0 % v_rows == 0
    assert MOBA_UNROLL % 2 == 0 and QT % MOBA_UNROLL == 0
    once = dict(pipeline_mode=pl.Buffered(1))
    grid_spec = pltpu.PrefetchScalarGridSpec(
        num_scalar_prefetch=1,
        grid=(B, ATTN_HEADS // HG, n_blocks // QT),
        in_specs=[
            pl.BlockSpec((1, q_rows, QT * L), lambda b, g, i, far: (b, g, i)),
            pl.BlockSpec((1, S, q_rows), lambda b, g, i, far: (b, 0, g), **once),
            pl.BlockSpec((1, v_rows, S), lambda b, g, i, far: (b, PROJT_V_ROW0 // v_rows + g, 0), **once),
            pl.BlockSpec((2, HG, L, L), lambda b, g, i, far: (0, g, 0, 0), **once),
        ],
        out_specs=pl.BlockSpec((1, q_rows, QT * L), lambda b, g, i, far: (b, g, i)),
        scratch_shapes=[
            pltpu.VMEM((HG // 2, 2 * n_blocks, HEAD_PAIR), BF16),
            pltpu.VMEM((U, HEAD_PAIR, L), BF16),
            pltpu.VMEM((U, n_blocks, L), F32),
            pltpu.VMEM((U, n_blocks + F32_SUBLANES, L), F32),
            *([pltpu.VMEM((U, L, L), F32), pltpu.VMEM((U, 1, L), F32),
               pltpu.VMEM((U, L, L), BF16), pltpu.VMEM((U, 1, L), F32)] * 2),
            pltpu.VMEM((U, 1, L), F32),
            pltpu.VMEM((U, V_ROWS, L), F32),
        ],
    )
    return pl.pallas_call(
        functools.partial(_moba_kernel, n_blocks=n_blocks),
        grid_spec=grid_spec,
        out_shape=jax.ShapeDtypeStruct((B, ATTN_WIDTH, S), F32),
        compiler_params=_params("arbitrary", "arbitrary", "arbitrary"),
        name="moba",
    )(far, proj_t, k_nat, proj_t, bias)


def _retention_kernel(cdec_ref, q_ref, k_ref, v_ref, g_ref, cos_ref, sin_ref,
                      dec_ref, qdec_ref, kdec_ref, o_ref, state_ref):
    c = pl.program_id(0)

    @pl.when(c == 0)
    def _():
        state_ref[...] = jnp.zeros(state_ref.shape, F32)

    cos = cos_ref[...]
    sin = sin_ref[...]
    half = RET_HEAD_DIM // 2

    def rope(t):
        t1, t2 = t[:half], t[half:]
        return jnp.concatenate([t1 * cos - t2 * sin, t1 * sin + t2 * cos], axis=0)

    for b in range(q_ref.shape[0]):
        for hh in range(RET_HEADS):
            rows = slice(hh * RET_HEAD_DIM, (hh + 1) * RET_HEAD_DIM)
            q = rope(q_ref[b, rows, :].astype(F32))
            k = rope(k_ref[b, rows, :].astype(F32)) * (RET_HEAD_DIM ** -0.5)
            v = v_ref[b, rows, :]
            qb = q.astype(BF16)
            s = lax.dot_general(k.astype(BF16), qb, _TN, preferred_element_type=F32)
            s = (s * dec_ref[hh]).astype(BF16)
            inner = jnp.dot(v, s, preferred_element_type=F32)
            state = state_ref[b, hh]
            cross = jnp.dot(state.astype(BF16), (q * qdec_ref[hh:hh + 1, :]).astype(BF16),
                            preferred_element_type=F32)
            kd = (k * kdec_ref[hh:hh + 1, :]).astype(BF16)
            state_ref[b, hh] = state * cdec_ref[hh] + lax.dot_general(v, kd, _NT, preferred_element_type=F32)
            out = inner + cross
            mu = jnp.mean(out, axis=0, keepdims=True)
            var = jnp.mean(jnp.square(out - mu), axis=0, keepdims=True)
            out = (out - mu) * lax.rsqrt(var + LN_EPS)
            g = g_ref[b, rows, :].astype(F32)
            o_ref[b, rows, :] = (g * jax.nn.sigmoid(g) * out).astype(BF16)


def _retention(proj_t):
    B, _, S = proj_t.shape
    C = RET_CHUNK
    H = RET_HEADS
    half = RET_HEAD_DIM // 2
    inv = ROPE_BASE ** (-jnp.arange(half, dtype=F32) / half)
    ang = inv[:, None] * jnp.arange(S).astype(F32)[None, :]
    cos_t, sin_t = jnp.cos(ang), jnp.sin(ang)
    gammas = 1.0 - jnp.exp(jnp.linspace(math.log(1.0 / 32), math.log(1.0 / 512), H, dtype=F32))
    log_g = jnp.log(gammas)
    idx = jnp.arange(C, dtype=F32)
    diff = idx[None, :] - idx[:, None]
    dec_t = jnp.where(diff[None] >= 0, jnp.exp(jnp.maximum(diff, 0.0)[None] * log_g[:, None, None]), 0.0)
    q_dec = jnp.exp((idx[None, :] + 1.0) * log_g[:, None])
    k_dec = jnp.exp((C - 1.0 - idx[None, :]) * log_g[:, None])
    chunk_dec = jnp.exp(C * log_g)

    grid_spec = pltpu.PrefetchScalarGridSpec(
        num_scalar_prefetch=1,
        grid=(S // C,),
        in_specs=[
            pl.BlockSpec((B, RET_WIDTH, C), lambda c, cd: (0, 1, c)),
            pl.BlockSpec((B, RET_WIDTH, C), lambda c, cd: (0, 2, c)),
            pl.BlockSpec((B, RET_WIDTH, C), lambda c, cd: (0, 3, c)),
            pl.BlockSpec((B, RET_WIDTH, C), lambda c, cd: (0, 4, c)),
            pl.BlockSpec((half, C), lambda c, cd: (0, c)),
            pl.BlockSpec((half, C), lambda c, cd: (0, c)),
            pl.BlockSpec((H, C, C), lambda c, cd: (0, 0, 0)),
            pl.BlockSpec((H, C), lambda c, cd: (0, 0)),
            pl.BlockSpec((H, C), lambda c, cd: (0, 0)),
        ],
        out_specs=pl.BlockSpec((B, RET_WIDTH, C), lambda c, cd: (0, 0, c)),
        scratch_shapes=[pltpu.VMEM((B, H, RET_HEAD_DIM, RET_HEAD_DIM), F32)],
    )
    return pl.pallas_call(
        _retention_kernel,
        grid_spec=grid_spec,
        out_shape=jax.ShapeDtypeStruct((B, RET_WIDTH, S), BF16),
        compiler_params=_params("arbitrary"),
        name="retention",
    )(chunk_dec, proj_t, proj_t, proj_t, proj_t, cos_t, sin_t, dec_t, q_dec, k_dec)


def _layer_norm_rows(y, g, b):
    mu = jnp.mean(y, axis=-1, keepdims=True)
    var = jnp.mean(jnp.square(y - mu), axis=-1, keepdims=True)
    return (y - mu) * lax.rsqrt(var + LN_EPS) * g + b


def _outproj_kernel(attn_ref, ret_ref, x_ref, gain_ref, wa_ref, wr_ref, g1_ref, b1_ref,
                    wrt_ref, brt_ref, before_ref,
                    h_ref, eid_ref, rank_ref, cw_ref, cnt_ref, carry_ref):
    first = jnp.logical_and(pl.program_id(0) == 0, pl.program_id(1) == 0)

    @pl.when(first)
    def _():
        carry_ref[...] = jnp.zeros(carry_ref.shape, F32)

    a = attn_ref[0]
    a = a * lax.rsqrt(jnp.mean(jnp.square(a), axis=0, keepdims=True) + LN_EPS) * gain_ref[...]
    mix = (lax.dot_general(a.astype(BF16), wa_ref[...], _TN, preferred_element_type=F32)
           + lax.dot_general(ret_ref[0], wr_ref[...], _TN, preferred_element_type=F32))
    h = _layer_norm_rows(ALPHA * x_ref[0] + mix, g1_ref[...], b1_ref[...])
    h_ref[...] = h

    R = ROUTER_ROWS
    h_hi = h.astype(BF16)
    h_lo = (h - h_hi.astype(F32)).astype(BF16)
    by_hi = lax.dot_general(wrt_ref[...], h_hi, _NT, preferred_element_type=F32)
    by_lo = lax.dot_general(wrt_ref[0:R, :], h_lo, _NT, preferred_element_type=F32)
    logits = by_hi[0:R] + by_hi[R:2 * R] + by_lo + brt_ref[...]
    T = logits.shape[1]
    gl = logits[0:N_GROUPS]
    gmax = jnp.max(gl, axis=0, keepdims=True)
    grow = lax.broadcasted_iota(jnp.int32, gl.shape, 0)
    gidx = jnp.min(jnp.where(gl == gmax, grow, N_GROUPS), axis=0, keepdims=True)
    g_p = 1.0 / jnp.sum(jnp.exp(gl - gmax), axis=0, keepdims=True)
    el = jnp.zeros((EXPERTS_PER_GROUP, T), F32)
    for g in range(N_GROUPS):
        r0 = ROUTER_EXPERT_ROW0 + g * EXPERTS_PER_GROUP
        el = jnp.where(gidx == g, logits[r0:r0 + EXPERTS_PER_GROUP], el)
    erow = lax.broadcasted_iota(jnp.int32, el.shape, 0)
    e1 = jnp.max(el, axis=0, keepdims=True)
    i1 = jnp.min(jnp.where(el == e1, erow, EXPERTS_PER_GROUP), axis=0, keepdims=True)
    el2 = jnp.where(erow == i1, -jnp.inf, el)
    e2 = jnp.max(el2, axis=0, keepdims=True)
    i2 = jnp.min(jnp.where(el2 == e2, erow, EXPERTS_PER_GROUP), axis=0, keepdims=True)
    r = jnp.exp(e2 - e1)
    w1 = g_p / (1.0 + r)
    w2 = g_p * r / (1.0 + r)
    id1 = gidx * EXPERTS_PER_GROUP + i1
    id2 = gidx * EXPERTS_PER_GROUP + i2
    eid_ref[0:1, :] = id1
    eid_ref[1:2, :] = id2
    cw_ref[0:1, :] = w1
    cw_ref[1:2, :] = w2

    xrow = lax.broadcasted_iota(jnp.int32, (N_EXPERTS, T), 0)
    oh1 = (xrow == id1).astype(F32)
    oh2 = (xrow == id2).astype(F32)
    both = oh1 + oh2
    seen = jnp.dot(both.astype(BF16), before_ref[...], preferred_element_type=F32) + carry_ref[...]
    rank_ref[0:1, :] = jnp.sum(oh1 * seen, axis=0, keepdims=True).astype(jnp.int32)
    rank_ref[1:2, :] = jnp.sum(oh2 * seen, axis=0, keepdims=True).astype(jnp.int32)
    carry = carry_ref[...] + jnp.sum(both, axis=1, keepdims=True)
    carry_ref[...] = carry
    cnt_ref[...] = carry.astype(jnp.int32)


def _outproj(attn_t, ret_t, x, gain, w_out, ln_g, ln_b, w_rg, b_rg, w_re, b_re):
    B, S, D = x.shape
    Tt = OUTPROJ_TOKENS
    n_t = S // Tt
    T = B * S
    wa = w_out[:ATTN_WIDTH].astype(BF16)
    wr = w_out[ATTN_WIDTH:].astype(BF16)
    e0, e1 = ROUTER_EXPERT_ROW0, ROUTER_EXPERT_ROW0 + N_EXPERTS
    wrt = jnp.zeros((ROUTER_ROWS, D), F32)
    wrt = wrt.at[:N_GROUPS].set(w_rg.T).at[e0:e1].set(w_re.T)
    wrt_hi = wrt.astype(BF16)
    wrt_lo = (wrt - wrt_hi.astype(F32)).astype(BF16)
    wrt = jnp.concatenate([wrt_hi, wrt_lo], axis=0)
    brt = jnp.zeros((ROUTER_ROWS, 1), F32)
    brt = brt.at[:N_GROUPS, 0].set(b_rg.astype(F32)).at[e0:e1, 0].set(b_re.astype(F32))
    before = (jnp.arange(Tt)[:, None] < jnp.arange(Tt)[None, :]).astype(BF16)
    const = lambda b, i: (0, 0)
    tok = lambda b, i: (0, b * n_t + i)
    return pl.pallas_call(
        _outproj_kernel,
        grid=(B, n_t),
        in_specs=[
            pl.BlockSpec((1, ATTN_WIDTH, Tt), lambda b, i: (b, 0, i)),
            pl.BlockSpec((1, RET_WIDTH, Tt), lambda b, i: (b, 0, i)),
            pl.BlockSpec((1, Tt, D), lambda b, i: (b, i, 0)),
            pl.BlockSpec((ATTN_WIDTH, 1), const),
            pl.BlockSpec((ATTN_WIDTH, D), const),
            pl.BlockSpec((RET_WIDTH, D), const),
            pl.BlockSpec((1, D), const),
            pl.BlockSpec((1, D), const),
            pl.BlockSpec((2 * ROUTER_ROWS, D), const),
            pl.BlockSpec((ROUTER_ROWS, 1), const),
            pl.BlockSpec((Tt, Tt), const),
        ],
        out_specs=[
            pl.BlockSpec((Tt, D), lambda b, i: (b * n_t + i, 0)),
            pl.BlockSpec((2, Tt), tok),
            pl.BlockSpec((2, Tt), tok),
            pl.BlockSpec((2, Tt), tok),
            pl.BlockSpec((N_EXPERTS, 1), const),
        ],
        out_shape=[
            jax.ShapeDtypeStruct((T, D), F32),
            jax.ShapeDtypeStruct((2, T), jnp.int32),
            jax.ShapeDtypeStruct((2, T), jnp.int32),
            jax.ShapeDtypeStruct((2, T), F32),
            jax.ShapeDtypeStruct((N_EXPERTS, 1), jnp.int32),
        ],
        scratch_shapes=[pltpu.VMEM((N_EXPERTS, 1), F32)],
        compiler_params=_params("arbitrary", "arbitrary"),
        name="outproj_ln_router",
    )(attn_t, ret_t, x, gain.reshape(ATTN_WIDTH, 1), wa, wr, ln_g.reshape(1, D), ln_b.reshape(1, D),
      wrt, brt, before)


def _dispatch_kernel(start_ref, end_ref, pos0_ref, pos1_ref, h_ref, xs_hbm, zero_ref, sem, zsem):
    G = h_ref.shape[0]
    Tm = MOE_ROWS

    @pl.when(pl.program_id(0) == 0)
    def _():
        zero_ref[...] = jnp.zeros(zero_ref.shape, F32)

        def tile_clear(row0):
            row0 = pl.multiple_of(row0, Tm)
            return pltpu.make_async_copy(zero_ref, xs_hbm.at[pl.ds(row0, Tm)], zsem)

        used = end_ref[N_EXPERTS - 1]
        n_rows = xs_hbm.shape[0]
        clears = [(end_ref[e] - Tm, end_ref[e] > start_ref[e]) for e in range(N_EXPERTS)]
        clears += [(jnp.minimum(used + k * Tm, n_rows - Tm), used + k * Tm < n_rows) for k in range(N_EXPERTS)]
        for row0, cond in clears:
            @pl.when(cond)
            def _():
                tile_clear(row0).start()
        for row0, cond in clears:
            @pl.when(cond)
            def _():
                tile_clear(row0).wait()

    def issue(k, carry):
        t0 = pl.multiple_of(k * ROW_DMA_UNROLL, ROW_DMA_UNROLL)
        for u in range(ROW_DMA_UNROLL):
            for pos_ref in (pos0_ref, pos1_ref):
                pltpu.make_async_copy(h_ref.at[pl.ds(t0 + u, 1)], xs_hbm.at[pl.ds(pos_ref[t0 + u], 1)], sem).start()
        return carry

    lax.fori_loop(0, G // ROW_DMA_UNROLL, issue, 0)
    for _ in range(2):
        pltpu.make_async_copy(h_ref, xs_hbm.at[pl.ds(0, G)], sem).wait()


def _dispatch(h, pos0, pos1, starts, ends, n_rows):
    T, D = h.shape
    G = DISPATCH_TOKENS if T % DISPATCH_TOKENS == 0 else ROUTE_TOKENS
    smem_blk = pl.BlockSpec((G,), lambda i, st, en: (i,), memory_space=pltpu.SMEM)
    grid_spec = pltpu.PrefetchScalarGridSpec(
        num_scalar_prefetch=2,
        grid=(T // G,),
        in_specs=[smem_blk, smem_blk, pl.BlockSpec((G, D), lambda i, st, en: (i, 0))],
        out_specs=pl.BlockSpec(memory_space=pl.ANY),
        scratch_shapes=[pltpu.VMEM((MOE_ROWS, D), F32), pltpu.SemaphoreType.DMA(()),
                        pltpu.SemaphoreType.DMA(())],
    )
    return pl.pallas_call(
        _dispatch_kernel,
        grid_spec=grid_spec,
        out_shape=jax.ShapeDtypeStruct((n_rows, D), F32),
        compiler_params=_params("arbitrary"),
        name="moe_dispatch",
    )(starts, ends, pos0, pos1, h)


def _experts_kernel(te_ref, nv_ref, next_ref, slot_ref, xs_ref, wgu_hbm, wdn_hbm, y_ref,
                    wgu_f, wdn_f, wgu_b, wdn_b, sems):
    i = pl.program_id(0)
    live = i < nv_ref[0]
    expert = te_ref[i]
    new_expert = jnp.logical_or(i == 0, expert != te_ref[jnp.maximum(i - 1, 0)])

    def weight_copies(e, slot):
        return (pltpu.make_async_copy(wgu_hbm.at[e], wgu_f.at[slot], sems.at[slot]),
                pltpu.make_async_copy(wdn_hbm.at[e], wdn_f.at[slot], sems.at[slot]))

    @pl.when(jnp.logical_and(live, new_expert))
    def _():
        slot = slot_ref[expert]

        @pl.when(i == 0)
        def _():
            for copy in weight_copies(expert, slot):
                copy.start()

        for copy in weight_copies(expert, slot):
            copy.wait()
        wgu_b[...] = wgu_f[slot].astype(BF16)
        wdn_b[...] = wdn_f[slot].astype(BF16)
        following = next_ref[expert]

        @pl.when(following >= 0)
        def _():
            for copy in weight_copies(following, 1 - slot):
                copy.start()

    @pl.when(live)
    def _():
        gu = jnp.dot(xs_ref[...].astype(BF16), wgu_b[...], preferred_element_type=F32)
        gate, up = gu[:, :D_EXPERT], gu[:, D_EXPERT:]
        hid = gate * jax.nn.sigmoid(gate) * up
        y_ref[...] = jnp.dot(hid.astype(BF16), wdn_b[...], preferred_element_type=F32)

    @pl.when(jnp.logical_not(live))
    def _():
        y_ref[...] = jnp.zeros(y_ref.shape, F32)


def _experts(xs, w_gu, w_dn, tile_expert, n_valid, has_rows):
    n_rows, D = xs.shape
    Tm = MOE_ROWS
    ids = jnp.arange(N_EXPERTS, dtype=jnp.int32)
    later = jnp.logical_and(has_rows[None, :], ids[None, :] > ids[:, None])
    next_expert = jnp.where(jnp.any(later, axis=1), jnp.argmax(later, axis=1), -1).astype(jnp.int32)
    slot = ((jnp.cumsum(has_rows.astype(jnp.int32)) - has_rows.astype(jnp.int32)) % 2).astype(jnp.int32)
    grid_spec = pltpu.PrefetchScalarGridSpec(
        num_scalar_prefetch=4,
        grid=(n_rows // Tm,),
        in_specs=[
            pl.BlockSpec((Tm, D), lambda i, te, nv, nx, sl: (jnp.minimum(i, jnp.maximum(nv[0] - 1, 0)), 0)),
            pl.BlockSpec(memory_space=pl.ANY),
            pl.BlockSpec(memory_space=pl.ANY),
        ],
        out_specs=pl.BlockSpec((Tm, D), lambda i, te, nv, nx, sl: (i, 0)),
        scratch_shapes=[
            pltpu.VMEM((2, D, 2 * D_EXPERT), F32), pltpu.VMEM((2, D_EXPERT, D), F32),
            pltpu.VMEM((D, 2 * D_EXPERT), BF16), pltpu.VMEM((D_EXPERT, D), BF16),
            pltpu.SemaphoreType.DMA((2,)),
        ],
    )
    return pl.pallas_call(
        _experts_kernel,
        grid_spec=grid_spec,
        out_shape=jax.ShapeDtypeStruct((n_rows, D), F32),
        compiler_params=_params("arbitrary"),
        name="moe_experts",
    )(tile_expert, n_valid, next_expert, slot, xs, w_gu, w_dn)


def _combine_kernel(pos0_ref, pos1_ref, next0_ref, next1_ref, y_hbm, h_ref, cw_ref, g_ref, b_ref, o_ref,
                    ybuf_a, ybuf_b, sem_a, sem_b):
    G = ROUTE_TOKENS
    C = ROW_DMA_UNROLL
    i = pl.program_id(0)
    last = pl.num_programs(0) - 1

    def gather(pos_refs, buf, sem, t0):
        for u in range(C):
            for slot, pos_ref in enumerate(pos_refs):
                pltpu.make_async_copy(y_hbm.at[pl.ds(pos_ref[t0 + u], 1)], buf.at[slot, pl.ds(t0 + u, 1)],
                                      sem).start()

    def wait_tile(buf, sem):
        for slot in range(2):
            pltpu.make_async_copy(y_hbm.at[pl.ds(0, G)], buf.at[slot], sem).wait()

    @pl.when(i == 0)
    def _():
        def first_tile(k, carry):
            gather((pos0_ref, pos1_ref), ybuf_a, sem_a, pl.multiple_of(k * C, C))
            return carry
        lax.fori_loop(0, G // C, first_tile, 0)

    def run(buf, sem, spare, spare_sem):
        wait_tile(buf, sem)

        @pl.when(i < last)
        def _():
            def next_tile(k, carry):
                gather((next0_ref, next1_ref), spare, spare_sem, pl.multiple_of(k * C, C))
                return carry
            lax.fori_loop(0, G // C, next_tile, 0)

        for r in range(G // COMBINE_CHUNK):
            rows = slice(r * COMBINE_CHUNK, (r + 1) * COMBINE_CHUNK)
            cw = cw_ref[rows, :]
            ffn = buf[0, rows, :] * cw[:, 0:1] + buf[1, rows, :] * cw[:, 1:2]
            o_ref[rows, :] = _layer_norm_rows(ALPHA * h_ref[rows, :] + ffn, g_ref[...], b_ref[...])

    odd = jnp.bitwise_and(i, 1)

    @pl.when(odd == 0)
    def _():
        run(ybuf_a, sem_a, ybuf_b, sem_b)

    @pl.when(odd == 1)
    def _():
        run(ybuf_b, sem_b, ybuf_a, sem_a)


def _combine(y, h, pos0, pos1, cw_rows, ln_g, ln_b):
    T, D = h.shape
    G = ROUTE_TOKENS
    n = T // G
    smem_blk = pl.BlockSpec((G,), lambda i: (i,), memory_space=pltpu.SMEM)
    smem_next = pl.BlockSpec((G,), lambda i: (jnp.minimum(i + 1, n - 1),), memory_space=pltpu.SMEM)
    return pl.pallas_call(
        _combine_kernel,
        grid=(n,),
        in_specs=[
            smem_blk, smem_blk, smem_next, smem_next,
            pl.BlockSpec(memory_space=pl.ANY),
            pl.BlockSpec((G, D), lambda i: (i, 0)),
            pl.BlockSpec((G, 2), lambda i: (i, 0)),
            pl.BlockSpec((1, D), lambda i: (0, 0)),
            pl.BlockSpec((1, D), lambda i: (0, 0)),
        ],
        out_specs=pl.BlockSpec((G, D), lambda i: (i, 0)),
        scratch_shapes=[pltpu.VMEM((2, G, D), F32), pltpu.VMEM((2, G, D), F32),
                        pltpu.SemaphoreType.DMA(()), pltpu.SemaphoreType.DMA(())],
        out_shape=jax.ShapeDtypeStruct((T, D), F32),
        compiler_params=_params("arbitrary"),
        name="moe_combine_ln",
    )(pos0, pos1, pos0, pos1, y, h, cw_rows, ln_g.reshape(1, D), ln_b.reshape(1, D))


def kernel(x, w_in, attn_out_gain, rel_bias_table, w_out, ln1_g, ln1_b, w_router_group,
           b_router_group, w_router_expert, b_router_expert, w_gate_up, w_down, ln2_g, ln2_b):
    B, S, D = x.shape
    assert D == D_MODEL and S % MOBA_BLOCK == 0 and S % INPROJ_TOKENS == 0
    assert w_in.shape[0] == DEPTH
    T = B * S
    h = x
    for l in range(DEPTH):
        proj_t, k_nat = _inproj(h, w_in[l])
        attn_t = _moba(proj_t, k_nat, rel_bias_table)
        ret_t = _retention(proj_t)
        h1, eid, rank, cw, counts = _outproj(
            attn_t, ret_t, h, attn_out_gain[l], w_out[l], ln1_g[l], ln1_b[l],
            w_router_group[l], b_router_group[l], w_router_expert[l], b_router_expert[l])

        Tm = MOE_ROWS
        n_tiles = (2 * T) // Tm + N_EXPERTS
        padded = ((counts[:, 0] + Tm - 1) // Tm) * Tm
        ends = jnp.cumsum(padded)
        starts = (ends - padded).astype(jnp.int32)
        tile_row0 = jnp.arange(n_tiles, dtype=jnp.int32) * Tm
        tile_expert = jnp.sum((ends[None, :] <= jnp.minimum(tile_row0, ends[-1] - Tm)[:, None]).astype(jnp.int32),
                              axis=1)
        n_valid = (ends[-1:] // Tm).astype(jnp.int32)

        expert_ids = jnp.arange(N_EXPERTS, dtype=jnp.int32)[:, None, None]
        pos = jnp.sum(jnp.where(eid[None] == expert_ids, starts[:, None, None], 0), axis=0) + rank

        xs = _dispatch(h1, pos[0], pos[1], starts, ends.astype(jnp.int32), n_tiles * Tm)
        y = _experts(xs, w_gate_up[l], w_down[l], tile_expert, n_valid, padded > 0)
        h2 = _combine(y, h1, pos[0], pos[1], cw.T, ln2_g[l], ln2_b[l])
        h = h2.reshape(B, S, D)
    return h
```

```python
import functools
import math

import numpy as np
import jax
import jax.numpy as jnp
from jax import lax
from jax.experimental import pallas as pl
from jax.experimental.pallas import tpu as pltpu

F32 = jnp.float32
BF16 = jnp.bfloat16

D_MODEL = 1024
ATTN_HEADS = 8
ATTN_HEAD_DIM = 64
ATTN_WIDTH = ATTN_HEADS * ATTN_HEAD_DIM
RET_HEADS = 4
RET_HEAD_DIM = 128
RET_WIDTH = RET_HEADS * RET_HEAD_DIM
N_IN_SLICES = 7
PROJT_SLICES = N_IN_SLICES - 1
HEAD_PAIR = 2 * ATTN_HEAD_DIM
BF16_SUBLANES = 16
F32_SUBLANES = 8
V_ROWS = ATTN_HEAD_DIM + BF16_SUBLANES
PROJT_V_ROW0 = (PROJT_SLICES - 1) * ATTN_WIDTH
PROJT_WIDTH = PROJT_V_ROW0 + ATTN_HEADS * V_ROWS
MOBA_BLOCK = 256
MOBA_TOPK = 3
RET_CHUNK = 256
REL_BUCKETS = 32
REL_MAX_DIST = 128
ROPE_BASE = 10000.0
N_GROUPS = 4
EXPERTS_PER_GROUP = 8
N_EXPERTS = N_GROUPS * EXPERTS_PER_GROUP
D_EXPERT = D_MODEL // 2
DEPTH = 1
ALPHA = (2.0 * DEPTH) ** 0.25
LN_EPS = 1e-5

LOG2E = math.log2(math.e)
Q_SCALE = ATTN_HEAD_DIM ** -0.5 * LOG2E

MASKED = -1e30

INPROJ_TOKENS = 1024
INPROJ_ROWS = ATTN_WIDTH
OUTPROJ_TOKENS = 1024
MOBA_UNROLL = 2
MOBA_HEADS = 4
MOBA_QTILES = 4
MOE_ROWS = 512
DISPATCH_TOKENS = 2048
ROUTE_TOKENS = 1024
COMBINE_CHUNK = 256
ROW_DMA_UNROLL = 32
ROUTER_EXPERT_ROW0 = 8
ROUTER_ROWS = -(-(ROUTER_EXPERT_ROW0 + N_EXPERTS) // BF16_SUBLANES) * BF16_SUBLANES

V7X_VMEM_BYTES = 64 * 1024 * 1024
VMEM_LIMIT = V7X_VMEM_BYTES * 3 // 4

_TN = (((0,), (0,)), ((), ()))
_NT = (((1,), (1,)), ((), ()))


def _params(*sem):
    return pltpu.CompilerParams(dimension_semantics=sem, vmem_limit_bytes=VMEM_LIMIT)


def _inproj_kernel(x_ref, wt_ref, wk_ref, o_ref, k_ref):
    xb = x_ref[0].astype(BF16)
    n_tok = xb.shape[0]
    ones_row = (lax.broadcasted_iota(jnp.int32, (BF16_SUBLANES, n_tok), 0) == 0).astype(BF16)
    for c in range(PROJT_SLICES):
        rows = slice(c * INPROJ_ROWS, (c + 1) * INPROJ_ROWS)
        acc = lax.dot_general(wt_ref[rows, :], xb, _NT, preferred_element_type=F32)
        if c == 0:
            acc = acc * Q_SCALE
        if c < PROJT_SLICES - 1:
            o_ref[0, rows, :] = acc.astype(BF16)
        else:
            for h in range(ATTN_HEADS):
                r0 = PROJT_V_ROW0 + h * V_ROWS
                o_ref[0, r0:r0 + ATTN_HEAD_DIM, :] = acc[h * ATTN_HEAD_DIM:(h + 1) * ATTN_HEAD_DIM].astype(BF16)
                o_ref[0, r0 + ATTN_HEAD_DIM:r0 + V_ROWS, :] = ones_row
    k_ref[0] = jnp.dot(xb, wk_ref[...], preferred_element_type=F32).astype(BF16)


def _inproj(x, w_in):
    B, S, D = x.shape
    W = ATTN_WIDTH
    w_t = jnp.concatenate([w_in[:, :W], w_in[:, 3 * W:], w_in[:, 2 * W:3 * W]], axis=1).T.astype(BF16)
    w_k = w_in[:, W:2 * W].astype(BF16)
    return pl.pallas_call(
        _inproj_kernel,
        grid=(B, S // INPROJ_TOKENS),
        in_specs=[
            pl.BlockSpec((1, INPROJ_TOKENS, D), lambda b, i: (b, i, 0)),
            pl.BlockSpec((PROJT_SLICES * W, D), lambda b, i: (0, 0)),
            pl.BlockSpec((D, W), lambda b, i: (0, 0)),
        ],
        out_specs=[
            pl.BlockSpec((1, PROJT_WIDTH, INPROJ_TOKENS), lambda b, i: (b, 0, i)),
            pl.BlockSpec((1, INPROJ_TOKENS, W), lambda b, i: (b, i, 0)),
        ],
        out_shape=[
            jax.ShapeDtypeStruct((B, PROJT_WIDTH, S), BF16),
            jax.ShapeDtypeStruct((B, S, W), BF16),
        ],
        compiler_params=_params("arbitrary", "arbitrary"),
        name="inproj",
    )(x, w_t, w_k)


def _moba_kernel(far_ref, q_ref, k_ref, v_ref, bias_rows_ref, o_ref,
                 bias_ref, kmean_ref, qz_ref, sel_ref, farsel_ref,
                 s0_ref, smax0_ref, p0_ref, alpha0_ref, s1_ref, smax1_ref, p1_ref, alpha1_ref,
                 m_ref, acc_ref, *, n_blocks):
    L = MOBA_BLOCK
    Dh = ATTN_HEAD_DIM
    HG = MOBA_HEADS
    group = pl.program_id(1)
    qi0 = pl.program_id(2) * MOBA_QTILES
    last_blk = n_blocks - 1
    units = [(a, h) for a in range(MOBA_QTILES) for h in range(HG)]
    s_ref, smax_ref = (s0_ref, s1_ref), (smax0_ref, smax1_ref)
    p_ref, alpha_ref = (p0_ref, p1_ref), (alpha0_ref, alpha1_ref)

    @pl.when(qi0 == 0)
    def _():
        chunk_blocks = min(8, n_blocks)
        chunk = chunk_blocks * L
        acc = jnp.zeros((n_blocks, HG * Dh), F32)
        for c in range(n_blocks // chunk_blocks):
            blk = lax.broadcasted_iota(jnp.int32, (n_blocks, chunk), 0)
            key = lax.broadcasted_iota(jnp.int32, (n_blocks, chunk), 1)
            ind = jnp.where(blk == c * chunk_blocks + key // L, 1.0 / L, 0.0).astype(BF16)
            acc = acc + jnp.dot(ind, k_ref[0, c * chunk:(c + 1) * chunk, :], preferred_element_type=F32)
        for kind in range(2):
            for h in range(HG):
                row = jnp.broadcast_to(bias_rows_ref[kind, h], (L, 2 * L))
                bias_ref[kind, h] = pltpu.roll(row, 0, 1, stride=1, stride_axis=0)[:, 0:L]
        hi = acc.astype(BF16)
        lo = (acc - hi.astype(F32)).astype(BF16)
        for pair in range(HG // 2):
            cols = slice(pair * HEAD_PAIR, (pair + 1) * HEAD_PAIR)
            kmean_ref[pair, 0:n_blocks, :] = hi[:, cols]
            kmean_ref[pair, n_blocks:2 * n_blocks, :] = lo[:, cols]

    blk = lax.broadcasted_iota(jnp.int32, (n_blocks, L), 0)
    for u, (a, h) in enumerate(units):
        qi = qi0 + a
        pair, hd = divmod(h, 2)
        past = blk < qi
        qh = q_ref[0, h * Dh:(h + 1) * Dh, a * L:(a + 1) * L]
        no_q = jnp.zeros_like(qh)
        qz = jnp.concatenate([qh, no_q] if hd == 0 else [no_q, qh], axis=0)
        qz_ref[u] = qz
        gate = jnp.dot(kmean_ref[pair], qz, preferred_element_type=F32)
        gate = gate[0:n_blocks] + gate[n_blocks:2 * n_blocks]
        gate = jnp.where(past, gate, -jnp.inf)
        for _ in range(MOBA_TOPK):
            top = jnp.max(gate, axis=0, keepdims=True)
            first = jnp.min(jnp.where(gate == top, blk, n_blocks), axis=0, keepdims=True)
            gate = jnp.where(blk == first, -jnp.inf, gate)
        knocked_out = gate == -jnp.inf
        sel_ref[u] = jnp.where(knocked_out, jnp.where(past, 0.0, MASKED), MASKED)
        far_row = jnp.where(blk < qi - 1, far_ref[group * HG + h], MASKED)
        farsel_ref[u, 0:n_blocks, :] = jnp.where(knocked_out, far_row, MASKED)
        farsel_ref[u, n_blocks:n_blocks + F32_SUBLANES, :] = jnp.zeros((F32_SUBLANES, L), F32)

    def stage_scores(u, j, tile, slot):
        pair = units[u][1] // 2
        kb = k_ref[0, pl.ds(pl.multiple_of(j * L, L), L), pair * HEAD_PAIR:(pair + 1) * HEAD_PAIR]
        s = jnp.dot(kb, qz_ref[u], preferred_element_type=F32)
        if tile is not None:
            s = s + tile
        s_ref[slot][u] = s
        smax_ref[slot][u] = jnp.max(s, axis=0, keepdims=True)

    def stage_softmax(u, row, slot):
        m_prev = m_ref[u]
        m_new = jnp.maximum(m_prev, smax_ref[slot][u] + row)
        alpha_ref[slot][u] = jnp.exp2(m_prev - m_new)
        p_ref[slot][u] = jnp.exp2(s_ref[slot][u] - (m_new - row)).astype(BF16)
        m_ref[u] = m_new

    def stage_values(u, j, slot):
        h = units[u][1]
        vb = v_ref[0, h * V_ROWS:(h + 1) * V_ROWS, pl.ds(pl.multiple_of(j * L, L), L)]
        pv = jnp.dot(vb, p_ref[slot][u], preferred_element_type=F32)
        acc_ref[u] = alpha_ref[slot][u] * acc_ref[u] + pv

    m_ref[...] = jnp.full(m_ref.shape, MASKED, F32)
    acc_ref[...] = jnp.zeros(acc_ref.shape, F32)

    zero_row = jnp.zeros((1, L), F32)
    own = [qi0 + a for a, _ in units]
    prev = [jnp.maximum(qi0 + a - 1, 0) for a, _ in units]
    for u, (a, h) in enumerate(units):
        stage_scores(u, own[u], bias_ref[0, h], 0)
        stage_scores(u, prev[u], bias_ref[1, h] + sel_ref[u, pl.ds(prev[u], 1), :], 1)
    for u in range(len(units)):
        stage_softmax(u, zero_row, 0)
    everyone = range(len(units))

    def step(t, slot, score_units=everyone, softmax_units=everyone, value_units=everyone):
        far_value = jnp.minimum(t - 2, last_blk)
        j = jnp.minimum(t, last_blk)
        jr = jnp.where(t == 0, n_blocks, jnp.minimum(t - 1, last_blk))
        for u in everyone:
            if u in score_units:
                stage_scores(u, j, None, slot)
            if u in softmax_units:
                stage_softmax(u, farsel_ref[u, pl.ds(jr, 1), :], 1 - slot)
            if u in value_units:
                stage_values(u, jnp.where(t == 0, own[u], jnp.where(t == 1, prev[u], far_value)), slot)

    def run_steps(trips, unroll):
        def unrolled_steps(k, carry):
            for s in range(unroll):
                step(unroll * k + s, s % 2)
            return carry
        lax.fori_loop(0, trips, unrolled_steps, 0)

    QT = MOBA_QTILES

    @pl.when(qi0 < QT)
    def _():
        n_steps = 2 + jnp.maximum(qi0 + QT - 2, 0)
        run_steps((n_steps + 1) // 2, 2)

    @pl.when(qi0 >= QT)
    def _():
        n_main = qi0 - 2
        left = (MOBA_UNROLL - 2) % MOBA_UNROLL
        run_steps((n_main - left) // MOBA_UNROLL, MOBA_UNROLL)
        for k in range(-left, QT + 2):
            with_block = lambda first: [u for u, (a, _) in enumerate(units) if a >= first]
            step(n_main + k, k % 2, with_block(k), with_block(k - 1), with_block(k - 2))
    for u, (a, h) in enumerate(units):
        o_ref[0, h * Dh:(h + 1) * Dh, a * L:(a + 1) * L] = acc_ref[u, 0:Dh, :] / acc_ref[u, Dh:Dh + 1, :]


def _t5_bucket_np(dist):
    n = np.maximum(dist, 0)
    max_exact = REL_BUCKETS // 2
    nf = np.maximum(n, 1).astype(np.float32)
    large = max_exact + (np.log(nf / max_exact) / math.log(REL_MAX_DIST / max_exact)
                         * (REL_BUCKETS - max_exact)).astype(np.int32)
    large = np.minimum(large, REL_BUCKETS - 1)
    return np.where(n < max_exact, n, large)


def _moba(proj_t, k_nat, rel_table):
    B, _, S = proj_t.shape
    L = MOBA_BLOCK
    n_blocks = S // L
    H = ATTN_HEADS
    table_t = rel_table.astype(F32).T
    wrapped = np.concatenate([np.arange(L), np.arange(-L, 0)])
    dist = np.stack([wrapped, wrapped + L])
    onehot = jnp.asarray(_t5_bucket_np(dist))[:, None, None, :] == jnp.arange(REL_BUCKETS)[None, None, :, None]
    bias_rows = jnp.sum(jnp.where(onehot, table_t[None, :, :, None], 0.0), axis=2) * LOG2E
    bias_rows = jnp.where(jnp.asarray(dist >= 0)[:, None, :], bias_rows, MASKED)[:, :, None, :]
    assert int(_t5_bucket_np(np.array(L + 1))) == REL_BUCKETS - 1
    far = table_t[:, REL_BUCKETS - 1] * LOG2E

    HG, QT = MOBA_HEADS, MOBA_QTILES
    U = HG * QT
    q_rows, v_rows = HG * ATTN_HEAD_DIM, HG * V_ROWS
    assert ATTN_HEADS % HG == 0 and HG % 2 == 0 and n_blocks % QT == 0 and PROJT_V_ROW0 % v_rows == 0
    assert MOBA_UNROLL % 2 == 0 and QT % MOBA_UNROLL == 0
    once = dict(pipeline_mode=pl.Buffered(1))
    grid_spec = pltpu.PrefetchScalarGridSpec(
        num_scalar_prefetch=1,
        grid=(B, ATTN_HEADS // HG, n_blocks // QT),
        in_specs=[
            pl.BlockSpec((1, q_rows, QT * L), lambda b, g, i, far: (b, g, i)),
            pl.BlockSpec((1, S, q_rows), lambda b, g, i, far: (b, 0, g), **once),
            pl.BlockSpec((1, v_rows, S), lambda b, g, i, far: (b, PROJT_V_ROW0 // v_rows + g, 0), **once),
            pl.BlockSpec((2, HG, 1, 2 * L), lambda b, g, i, far: (0, g, 0, 0)),
        ],
        out_specs=pl.BlockSpec((1, q_rows, QT * L), lambda b, g, i, far: (b, g, i)),
        scratch_shapes=[
            pltpu.VMEM((2, HG, L, L), F32),
            pltpu.VMEM((HG // 2, 2 * n_blocks, HEAD_PAIR), BF16),
            pltpu.VMEM((U, HEAD_PAIR, L), BF16),
            pltpu.VMEM((U, n_blocks, L), F32),
            pltpu.VMEM((U, n_blocks + F32_SUBLANES, L), F32),
            *([pltpu.VMEM((U, L, L), F32), pltpu.VMEM((U, 1, L), F32),
               pltpu.VMEM((U, L, L), BF16), pltpu.VMEM((U, 1, L), F32)] * 2),
            pltpu.VMEM((U, 1, L), F32),
            pltpu.VMEM((U, V_ROWS, L), F32),
        ],
    )
    return pl.pallas_call(
        functools.partial(_moba_kernel, n_blocks=n_blocks),
        grid_spec=grid_spec,
        out_shape=jax.ShapeDtypeStruct((B, ATTN_WIDTH, S), F32),
        compiler_params=_params("arbitrary", "arbitrary", "arbitrary"),
        name="moba",
    )(far, proj_t, k_nat, proj_t, bias_rows)


def _retention_kernel(cdec_ref, q_ref, k_ref, v_ref, g_ref, cos_ref, sin_ref,
                      dec_ref, qdec_ref, kdec_ref, o_ref, state_ref):
    c = pl.program_id(0)

    @pl.when(c == 0)
    def _():
        state_ref[...] = jnp.zeros(state_ref.shape, F32)

    cos = cos_ref[...]
    sin = sin_ref[...]
    half = RET_HEAD_DIM // 2

    def rope(t):
        t1, t2 = t[:half], t[half:]
        return jnp.concatenate([t1 * cos - t2 * sin, t1 * sin + t2 * cos], axis=0)

    for b in range(q_ref.shape[0]):
        for hh in range(RET_HEADS):
            rows = slice(hh * RET_HEAD_DIM, (hh + 1) * RET_HEAD_DIM)
            q = rope(q_ref[b, rows, :].astype(F32))
            k = rope(k_ref[b, rows, :].astype(F32)) * (RET_HEAD_DIM ** -0.5)
            v = v_ref[b, rows, :]
            qb = q.astype(BF16)
            s = lax.dot_general(k.astype(BF16), qb, _TN, preferred_element_type=F32)
            s = (s * dec_ref[hh]).astype(BF16)
            inner = jnp.dot(v, s, preferred_element_type=F32)
            state = state_ref[b, hh]
            cross = jnp.dot(state.astype(BF16), (q * qdec_ref[hh:hh + 1, :]).astype(BF16),
                            preferred_element_type=F32)
            kd = (k * kdec_ref[hh:hh + 1, :]).astype(BF16)
            state_ref[b, hh] = state * cdec_ref[hh] + lax.dot_general(v, kd, _NT, preferred_element_type=F32)
            out = inner + cross
            mu = jnp.mean(out, axis=0, keepdims=True)
            var = jnp.mean(jnp.square(out - mu), axis=0, keepdims=True)
            out = (out - mu) * lax.rsqrt(var + LN_EPS)
            g = g_ref[b, rows, :].astype(F32)
            o_ref[b, rows, :] = (g * jax.nn.sigmoid(g) * out).astype(BF16)


def _retention(proj_t):
    B, _, S = proj_t.shape
    C = RET_CHUNK
    H = RET_HEADS
    half = RET_HEAD_DIM // 2
    inv = ROPE_BASE ** (-jnp.arange(half, dtype=F32) / half)
    ang = inv[:, None] * jnp.arange(S).astype(F32)[None, :]
    cos_t, sin_t = jnp.cos(ang), jnp.sin(ang)
    gammas = 1.0 - jnp.exp(jnp.linspace(math.log(1.0 / 32), math.log(1.0 / 512), H, dtype=F32))
    log_g = jnp.log(gammas)
    idx = jnp.arange(C, dtype=F32)
    diff = idx[None, :] - idx[:, None]
    dec_t = jnp.where(diff[None] >= 0, jnp.exp(jnp.maximum(diff, 0.0)[None] * log_g[:, None, None]), 0.0)
    q_dec = jnp.exp((idx[None, :] + 1.0) * log_g[:, None])
    k_dec = jnp.exp((C - 1.0 - idx[None, :]) * log_g[:, None])
    chunk_dec = jnp.exp(C * log_g)

    grid_spec = pltpu.PrefetchScalarGridSpec(
        num_scalar_prefetch=1,
        grid=(S // C,),
        in_specs=[
            pl.BlockSpec((B, RET_WIDTH, C), lambda c, cd: (0, 1, c)),
            pl.BlockSpec((B, RET_WIDTH, C), lambda c, cd: (0, 2, c)),
            pl.BlockSpec((B, RET_WIDTH, C), lambda c, cd: (0, 3, c)),
            pl.BlockSpec((B, RET_WIDTH, C), lambda c, cd: (0, 4, c)),
            pl.BlockSpec((half, C), lambda c, cd: (0, c)),
            pl.BlockSpec((half, C), lambda c, cd: (0, c)),
            pl.BlockSpec((H, C, C), lambda c, cd: (0, 0, 0)),
            pl.BlockSpec((H, C), lambda c, cd: (0, 0)),
            pl.BlockSpec((H, C), lambda c, cd: (0, 0)),
        ],
        out_specs=pl.BlockSpec((B, RET_WIDTH, C), lambda c, cd: (0, 0, c)),
        scratch_shapes=[pltpu.VMEM((B, H, RET_HEAD_DIM, RET_HEAD_DIM), F32)],
    )
    return pl.pallas_call(
        _retention_kernel,
        grid_spec=grid_spec,
        out_shape=jax.ShapeDtypeStruct((B, RET_WIDTH, S), BF16),
        compiler_params=_params("arbitrary"),
        name="retention",
    )(chunk_dec, proj_t, proj_t, proj_t, proj_t, cos_t, sin_t, dec_t, q_dec, k_dec)


def _layer_norm_rows(y, g, b):
    mu = jnp.mean(y, axis=-1, keepdims=True)
    var = jnp.mean(jnp.square(y - mu), axis=-1, keepdims=True)
    return (y - mu) * lax.rsqrt(var + LN_EPS) * g + b


def _outproj_kernel(attn_ref, ret_ref, x_ref, gain_ref, wa_ref, wr_ref, g1_ref, b1_ref,
                    wrt_ref, brt_ref, before_ref,
                    h_ref, eid_ref, rank_ref, cw_ref, cnt_ref, carry_ref):
    first = jnp.logical_and(pl.program_id(0) == 0, pl.program_id(1) == 0)

    @pl.when(first)
    def _():
        carry_ref[...] = jnp.zeros(carry_ref.shape, F32)

    a = attn_ref[0]
    a = a * lax.rsqrt(jnp.mean(jnp.square(a), axis=0, keepdims=True) + LN_EPS) * gain_ref[...]
    mix = (lax.dot_general(a.astype(BF16), wa_ref[...], _TN, preferred_element_type=F32)
           + lax.dot_general(ret_ref[0], wr_ref[...], _TN, preferred_element_type=F32))
    h = _layer_norm_rows(ALPHA * x_ref[0] + mix, g1_ref[...], b1_ref[...])
    h_ref[...] = h

    R = ROUTER_ROWS
    h_hi = h.astype(BF16)
    h_lo = (h - h_hi.astype(F32)).astype(BF16)
    by_hi = lax.dot_general(wrt_ref[...], h_hi, _NT, preferred_element_type=F32)
    by_lo = lax.dot_general(wrt_ref[0:R, :], h_lo, _NT, preferred_element_type=F32)
    logits = by_hi[0:R] + by_hi[R:2 * R] + by_lo + brt_ref[...]
    T = logits.shape[1]
    gl = logits[0:N_GROUPS]
    gmax = jnp.max(gl, axis=0, keepdims=True)
    grow = lax.broadcasted_iota(jnp.int32, gl.shape, 0)
    gidx = jnp.min(jnp.where(gl == gmax, grow, N_GROUPS), axis=0, keepdims=True)
    g_p = 1.0 / jnp.sum(jnp.exp(gl - gmax), axis=0, keepdims=True)
    el = jnp.zeros((EXPERTS_PER_GROUP, T), F32)
    for g in range(N_GROUPS):
        r0 = ROUTER_EXPERT_ROW0 + g * EXPERTS_PER_GROUP
        el = jnp.where(gidx == g, logits[r0:r0 + EXPERTS_PER_GROUP], el)
    erow = lax.broadcasted_iota(jnp.int32, el.shape, 0)
    e1 = jnp.max(el, axis=0, keepdims=True)
    i1 = jnp.min(jnp.where(el == e1, erow, EXPERTS_PER_GROUP), axis=0, keepdims=True)
    el2 = jnp.where(erow == i1, -jnp.inf, el)
    e2 = jnp.max(el2, axis=0, keepdims=True)
    i2 = jnp.min(jnp.where(el2 == e2, erow, EXPERTS_PER_GROUP), axis=0, keepdims=True)
    r = jnp.exp(e2 - e1)
    w1 = g_p / (1.0 + r)
    w2 = g_p * r / (1.0 + r)
    id1 = gidx * EXPERTS_PER_GROUP + i1
    id2 = gidx * EXPERTS_PER_GROUP + i2
    eid_ref[0:1, :] = id1
    eid_ref[1:2, :] = id2
    cw_ref[0:1, :] = w1
    cw_ref[1:2, :] = w2

    xrow = lax.broadcasted_iota(jnp.int32, (N_EXPERTS, T), 0)
    oh1 = (xrow == id1).astype(F32)
    oh2 = (xrow == id2).astype(F32)
    both = oh1 + oh2
    seen = jnp.dot(both.astype(BF16), before_ref[...], preferred_element_type=F32) + carry_ref[...]
    rank_ref[0:1, :] = jnp.sum(oh1 * seen, axis=0, keepdims=True).astype(jnp.int32)
    rank_ref[1:2, :] = jnp.sum(oh2 * seen, axis=0, keepdims=True).astype(jnp.int32)
    carry = carry_ref[...] + jnp.sum(both, axis=1, keepdims=True)
    carry_ref[...] = carry
    cnt_ref[...] = carry.astype(jnp.int32)


def _outproj(attn_t, ret_t, x, gain, w_out, ln_g, ln_b, w_rg, b_rg, w_re, b_re):
    B, S, D = x.shape
    Tt = OUTPROJ_TOKENS
    n_t = S // Tt
    T = B * S
    wa = w_out[:ATTN_WIDTH].astype(BF16)
    wr = w_out[ATTN_WIDTH:].astype(BF16)
    e0, e1 = ROUTER_EXPERT_ROW0, ROUTER_EXPERT_ROW0 + N_EXPERTS
    wrt = jnp.zeros((ROUTER_ROWS, D), F32)
    wrt = wrt.at[:N_GROUPS].set(w_rg.T).at[e0:e1].set(w_re.T)
    wrt_hi = wrt.astype(BF16)
    wrt_lo = (wrt - wrt_hi.astype(F32)).astype(BF16)
    wrt = jnp.concatenate([wrt_hi, wrt_lo], axis=0)
    brt = jnp.zeros((ROUTER_ROWS, 1), F32)
    brt = brt.at[:N_GROUPS, 0].set(b_rg.astype(F32)).at[e0:e1, 0].set(b_re.astype(F32))
    before = (jnp.arange(Tt)[:, None] < jnp.arange(Tt)[None, :]).astype(BF16)
    const = lambda b, i: (0, 0)
    tok = lambda b, i: (0, b * n_t + i)
    return pl.pallas_call(
        _outproj_kernel,
        grid=(B, n_t),
        in_specs=[
            pl.BlockSpec((1, ATTN_WIDTH, Tt), lambda b, i: (b, 0, i)),
            pl.BlockSpec((1, RET_WIDTH, Tt), lambda b, i: (b, 0, i)),
            pl.BlockSpec((1, Tt, D), lambda b, i: (b, i, 0)),
            pl.BlockSpec((ATTN_WIDTH, 1), const),
            pl.BlockSpec((ATTN_WIDTH, D), const),
            pl.BlockSpec((RET_WIDTH, D), const),
            pl.BlockSpec((1, D), const),
            pl.BlockSpec((1, D), const),
            pl.BlockSpec((2 * ROUTER_ROWS, D), const),
            pl.BlockSpec((ROUTER_ROWS, 1), const),
            pl.BlockSpec((Tt, Tt), const),
        ],
        out_specs=[
            pl.BlockSpec((Tt, D), lambda b, i: (b * n_t + i, 0)),
            pl.BlockSpec((2, Tt), tok),
            pl.BlockSpec((2, Tt), tok),
            pl.BlockSpec((2, Tt), tok),
            pl.BlockSpec((N_EXPERTS, 1), const),
        ],
        out_shape=[
            jax.ShapeDtypeStruct((T, D), F32),
            jax.ShapeDtypeStruct((2, T), jnp.int32),
            jax.ShapeDtypeStruct((2, T), jnp.int32),
            jax.ShapeDtypeStruct((2, T), F32),
            jax.ShapeDtypeStruct((N_EXPERTS, 1), jnp.int32),
        ],
        scratch_shapes=[pltpu.VMEM((N_EXPERTS, 1), F32)],
        compiler_params=_params("arbitrary", "arbitrary"),
        name="outproj_ln_router",
    )(attn_t, ret_t, x, gain.reshape(ATTN_WIDTH, 1), wa, wr, ln_g.reshape(1, D), ln_b.reshape(1, D),
      wrt, brt, before)


def _dispatch_kernel(start_ref, end_ref, pos0_ref, pos1_ref, h_ref, xs_hbm, zero_ref, sem, zsem):
    G = h_ref.shape[0]
    Tm = MOE_ROWS

    @pl.when(pl.program_id(0) == 0)
    def _():
        zero_ref[...] = jnp.zeros(zero_ref.shape, F32)

        def tile_clear(row0):
            row0 = pl.multiple_of(row0, Tm)
            return pltpu.make_async_copy(zero_ref, xs_hbm.at[pl.ds(row0, Tm)], zsem)

        used = end_ref[N_EXPERTS - 1]
        n_rows = xs_hbm.shape[0]
        clears = [(end_ref[e] - Tm, end_ref[e] > start_ref[e]) for e in range(N_EXPERTS)]
        clears += [(jnp.minimum(used + k * Tm, n_rows - Tm), used + k * Tm < n_rows) for k in range(N_EXPERTS)]
        for row0, cond in clears:
            @pl.when(cond)
            def _():
                tile_clear(row0).start()
        for row0, cond in clears:
            @pl.when(cond)
            def _():
                tile_clear(row0).wait()

    def issue(k, carry):
        t0 = pl.multiple_of(k * ROW_DMA_UNROLL, ROW_DMA_UNROLL)
        for u in range(ROW_DMA_UNROLL):
            for pos_ref in (pos0_ref, pos1_ref):
                pltpu.make_async_copy(h_ref.at[pl.ds(t0 + u, 1)], xs_hbm.at[pl.ds(pos_ref[t0 + u], 1)], sem).start()
        return carry

    lax.fori_loop(0, G // ROW_DMA_UNROLL, issue, 0)
    for _ in range(2):
        pltpu.make_async_copy(h_ref, xs_hbm.at[pl.ds(0, G)], sem).wait()


def _dispatch(h, pos0, pos1, starts, ends, n_rows):
    T, D = h.shape
    G = DISPATCH_TOKENS if T % DISPATCH_TOKENS == 0 else ROUTE_TOKENS
    smem_blk = pl.BlockSpec((G,), lambda i, st, en: (i,), memory_space=pltpu.SMEM)
    grid_spec = pltpu.PrefetchScalarGridSpec(
        num_scalar_prefetch=2,
        grid=(T // G,),
        in_specs=[smem_blk, smem_blk, pl.BlockSpec((G, D), lambda i, st, en: (i, 0))],
        out_specs=pl.BlockSpec(memory_space=pl.ANY),
        scratch_shapes=[pltpu.VMEM((MOE_ROWS, D), F32), pltpu.SemaphoreType.DMA(()),
                        pltpu.SemaphoreType.DMA(())],
    )
    return pl.pallas_call(
        _dispatch_kernel,
        grid_spec=grid_spec,
        out_shape=jax.ShapeDtypeStruct((n_rows, D), F32),
        compiler_params=_params("arbitrary"),
        name="moe_dispatch",
    )(starts, ends, pos0, pos1, h)


def _experts_kernel(te_ref, nv_ref, next_ref, slot_ref, xs_ref, wgu_hbm, wdn_hbm, y_ref,
                    wgu_f, wdn_f, wgu_b, wdn_b, sems):
    i = pl.program_id(0)
    live = i < nv_ref[0]
    expert = te_ref[i]
    new_expert = jnp.logical_or(i == 0, expert != te_ref[jnp.maximum(i - 1, 0)])

    def weight_copies(e, slot):
        return (pltpu.make_async_copy(wgu_hbm.at[e], wgu_f.at[slot], sems.at[slot]),
                pltpu.make_async_copy(wdn_hbm.at[e], wdn_f.at[slot], sems.at[slot]))

    @pl.when(jnp.logical_and(live, new_expert))
    def _():
        slot = slot_ref[expert]

        @pl.when(i == 0)
        def _():
            for copy in weight_copies(expert, slot):
                copy.start()

        for copy in weight_copies(expert, slot):
            copy.wait()
        wgu_b[...] = wgu_f[slot].astype(BF16)
        wdn_b[...] = wdn_f[slot].astype(BF16)
        following = next_ref[expert]

        @pl.when(following >= 0)
        def _():
            for copy in weight_copies(following, 1 - slot):
                copy.start()

    @pl.when(live)
    def _():
        gu = jnp.dot(xs_ref[...].astype(BF16), wgu_b[...], preferred_element_type=F32)
        gate, up = gu[:, :D_EXPERT], gu[:, D_EXPERT:]
        hid = gate * jax.nn.sigmoid(gate) * up
        y_ref[...] = jnp.dot(hid.astype(BF16), wdn_b[...], preferred_element_type=F32)

    @pl.when(jnp.logical_not(live))
    def _():
        y_ref[...] = jnp.zeros(y_ref.shape, F32)


def _experts(xs, w_gu, w_dn, tile_expert, n_valid, has_rows):
    n_rows, D = xs.shape
    Tm = MOE_ROWS
    ids = jnp.arange(N_EXPERTS, dtype=jnp.int32)
    later = jnp.logical_and(has_rows[None, :], ids[None, :] > ids[:, None])
    next_expert = jnp.where(jnp.any(later, axis=1), jnp.argmax(later, axis=1), -1).astype(jnp.int32)
    slot = ((jnp.cumsum(has_rows.astype(jnp.int32)) - has_rows.astype(jnp.int32)) % 2).astype(jnp.int32)
    grid_spec = pltpu.PrefetchScalarGridSpec(
        num_scalar_prefetch=4,
        grid=(n_rows // Tm,),
        in_specs=[
            pl.BlockSpec((Tm, D), lambda i, te, nv, nx, sl: (jnp.minimum(i, jnp.maximum(nv[0] - 1, 0)), 0)),
            pl.BlockSpec(memory_space=pl.ANY),
            pl.BlockSpec(memory_space=pl.ANY),
        ],
        out_specs=pl.BlockSpec((Tm, D), lambda i, te, nv, nx, sl: (i, 0)),
        scratch_shapes=[
            pltpu.VMEM((2, D, 2 * D_EXPERT), F32), pltpu.VMEM((2, D_EXPERT, D), F32),
            pltpu.VMEM((D, 2 * D_EXPERT), BF16), pltpu.VMEM((D_EXPERT, D), BF16),
            pltpu.SemaphoreType.DMA((2,)),
        ],
    )
    return pl.pallas_call(
        _experts_kernel,
        grid_spec=grid_spec,
        out_shape=jax.ShapeDtypeStruct((n_rows, D), F32),
        compiler_params=_params("arbitrary"),
        name="moe_experts",
    )(tile_expert, n_valid, next_expert, slot, xs, w_gu, w_dn)


def _combine_kernel(pos0_ref, pos1_ref, next0_ref, next1_ref, y_hbm, h_ref, cw_ref, g_ref, b_ref, o_ref,
                    ybuf_a, ybuf_b, sem_a, sem_b):
    G = ROUTE_TOKENS
    C = ROW_DMA_UNROLL
    i = pl.program_id(0)
    last = pl.num_programs(0) - 1

    def gather(pos_refs, buf, sem, t0):
        for u in range(C):
            for slot, pos_ref in enumerate(pos_refs):
                pltpu.make_async_copy(y_hbm.at[pl.ds(pos_ref[t0 + u], 1)], buf.at[slot, pl.ds(t0 + u, 1)],
                                      sem).start()

    def wait_tile(buf, sem):
        for slot in range(2):
            pltpu.make_async_copy(y_hbm.at[pl.ds(0, G)], buf.at[slot], sem).wait()

    @pl.when(i == 0)
    def _():
        def first_tile(k, carry):
            gather((pos0_ref, pos1_ref), ybuf_a, sem_a, pl.multiple_of(k * C, C))
            return carry
        lax.fori_loop(0, G // C, first_tile, 0)

    def run(buf, sem, spare, spare_sem):
        wait_tile(buf, sem)

        @pl.when(i < last)
        def _():
            def next_tile(k, carry):
                gather((next0_ref, next1_ref), spare, spare_sem, pl.multiple_of(k * C, C))
                return carry
            lax.fori_loop(0, G // C, next_tile, 0)

        for r in range(G // COMBINE_CHUNK):
            rows = slice(r * COMBINE_CHUNK, (r + 1) * COMBINE_CHUNK)
            cw = cw_ref[rows, :]
            ffn = buf[0, rows, :] * cw[:, 0:1] + buf[1, rows, :] * cw[:, 1:2]
            o_ref[rows, :] = _layer_norm_rows(ALPHA * h_ref[rows, :] + ffn, g_ref[...], b_ref[...])

    odd = jnp.bitwise_and(i, 1)

    @pl.when(odd == 0)
    def _():
        run(ybuf_a, sem_a, ybuf_b, sem_b)

    @pl.when(odd == 1)
    def _():
        run(ybuf_b, sem_b, ybuf_a, sem_a)


def _combine(y, h, pos0, pos1, cw_rows, ln_g, ln_b):
    T, D = h.shape
    G = ROUTE_TOKENS
    n = T // G
    smem_blk = pl.BlockSpec((G,), lambda i: (i,), memory_space=pltpu.SMEM)
    smem_next = pl.BlockSpec((G,), lambda i: (jnp.minimum(i + 1, n - 1),), memory_space=pltpu.SMEM)
    return pl.pallas_call(
        _combine_kernel,
        grid=(n,),
        in_specs=[
            smem_blk, smem_blk, smem_next, smem_next,
            pl.BlockSpec(memory_space=pl.ANY),
            pl.BlockSpec((G, D), lambda i: (i, 0)),
            pl.BlockSpec((G, 2), lambda i: (i, 0)),
            pl.BlockSpec((1, D), lambda i: (0, 0)),
            pl.BlockSpec((1, D), lambda i: (0, 0)),
        ],
        out_specs=pl.BlockSpec((G, D), lambda i: (i, 0)),
        scratch_shapes=[pltpu.VMEM((2, G, D), F32), pltpu.VMEM((2, G, D), F32),
                        pltpu.SemaphoreType.DMA(()), pltpu.SemaphoreType.DMA(())],
        out_shape=jax.ShapeDtypeStruct((T, D), F32),
        compiler_params=_params("arbitrary"),
        name="moe_combine_ln",
    )(pos0, pos1, pos0, pos1, y, h, cw_rows, ln_g.reshape(1, D), ln_b.reshape(1, D))


def kernel(x, w_in, attn_out_gain, rel_bias_table, w_out, ln1_g, ln1_b, w_router_group,
           b_router_group, w_router_expert, b_router_expert, w_gate_up, w_down, ln2_g, ln2_b):
    B, S, D = x.shape
    assert D == D_MODEL and S % MOBA_BLOCK == 0 and S % INPROJ_TOKENS == 0
    assert w_in.shape[0] == DEPTH
    T = B * S
    h = x
    for l in range(DEPTH):
        proj_t, k_nat = _inproj(h, w_in[l])
        attn_t = _moba(proj_t, k_nat, rel_bias_table)
        ret_t = _retention(proj_t)
        h1, eid, rank, cw, counts = _outproj(
            attn_t, ret_t, h, attn_out_gain[l], w_out[l], ln1_g[l], ln1_b[l],
            w_router_group[l], b_router_group[l], w_router_expert[l], b_router_expert[l])

        Tm = MOE_ROWS
        n_tiles = (2 * T) // Tm + N_EXPERTS
        padded = ((counts[:, 0] + Tm - 1) // Tm) * Tm
        ends = jnp.cumsum(padded)
        starts = (ends - padded).astype(jnp.int32)
        tile_row0 = jnp.arange(n_tiles, dtype=jnp.int32) * Tm
        tile_expert = jnp.sum((ends[None, :] <= jnp.minimum(tile_row0, ends[-1] - Tm)[:, None]).astype(jnp.int32),
                              axis=1)
        n_valid = (ends[-1:] // Tm).astype(jnp.int32)

        expert_ids = jnp.arange(N_EXPERTS, dtype=jnp.int32)[:, None, None]
        pos = jnp.sum(jnp.where(eid[None] == expert_ids, starts[:, None, None], 0), axis=0) + rank

        xs = _dispatch(h1, pos[0], pos[1], starts, ends.astype(jnp.int32), n_tiles * Tm)
        y = _experts(xs, w_gate_up[l], w_down[l], tile_expert, n_valid, padded > 0)
        h2 = _combine(y, h1, pos[0], pos[1], cw.T, ln2_g[l], ln2_b[l])
        h = h2.reshape(B, S, D)
    return h
```

```python
import functools
import math

import numpy as np
import jax
import jax.numpy as jnp
from jax import lax
from jax.experimental import pallas as pl
from jax.experimental.pallas import tpu as pltpu

F32 = jnp.float32
BF16 = jnp.bfloat16

D_MODEL = 1024
ATTN_HEADS = 8
ATTN_HEAD_DIM = 64
ATTN_WIDTH = ATTN_HEADS * ATTN_HEAD_DIM
RET_HEADS = 4
RET_HEAD_DIM = 128
RET_WIDTH = RET_HEADS * RET_HEAD_DIM
N_IN_SLICES = 7
PROJT_SLICES = N_IN_SLICES - 1
HEAD_PAIR = 2 * ATTN_HEAD_DIM
BF16_SUBLANES = 16
F32_SUBLANES = 8
V_ROWS = ATTN_HEAD_DIM + BF16_SUBLANES
PROJT_V_ROW0 = (PROJT_SLICES - 1) * ATTN_WIDTH
PROJT_WIDTH = PROJT_V_ROW0 + ATTN_HEADS * V_ROWS
MOBA_BLOCK = 256
MOBA_TOPK = 3
RET_CHUNK = 256
REL_BUCKETS = 32
REL_MAX_DIST = 128
ROPE_BASE = 10000.0
N_GROUPS = 4
EXPERTS_PER_GROUP = 8
N_EXPERTS = N_GROUPS * EXPERTS_PER_GROUP
D_EXPERT = D_MODEL // 2
DEPTH = 1
ALPHA = (2.0 * DEPTH) ** 0.25
LN_EPS = 1e-5

LOG2E = math.log2(math.e)
Q_SCALE = ATTN_HEAD_DIM ** -0.5 * LOG2E

MASKED = -1e30

INPROJ_TOKENS = 1024
INPROJ_ROWS = ATTN_WIDTH
OUTPROJ_TOKENS = 1024
MOBA_UNROLL = 2
MOBA_HEADS = 4
MOBA_QTILES = 4
MOE_ROWS = 512
EXPERT_TILES_PER_STEP = 2
DISPATCH_TOKENS = 2048
ROUTE_TOKENS = 1024
COMBINE_CHUNK = 256
ROW_DMA_UNROLL = 32
ROUTER_EXPERT_ROW0 = 8
ROUTER_ROWS = -(-(ROUTER_EXPERT_ROW0 + N_EXPERTS) // BF16_SUBLANES) * BF16_SUBLANES

V7X_VMEM_BYTES = 64 * 1024 * 1024
VMEM_LIMIT = V7X_VMEM_BYTES * 3 // 4

_TN = (((0,), (0,)), ((), ()))
_NT = (((1,), (1,)), ((), ()))


def _params(*sem):
    return pltpu.CompilerParams(dimension_semantics=sem, vmem_limit_bytes=VMEM_LIMIT)


def _inproj_kernel(x_ref, wt_ref, wk_ref, o_ref, k_ref):
    xb = x_ref[0].astype(BF16)
    n_tok = xb.shape[0]
    ones_row = (lax.broadcasted_iota(jnp.int32, (BF16_SUBLANES, n_tok), 0) == 0).astype(BF16)
    for c in range(PROJT_SLICES):
        rows = slice(c * INPROJ_ROWS, (c + 1) * INPROJ_ROWS)
        acc = lax.dot_general(wt_ref[rows, :], xb, _NT, preferred_element_type=F32)
        if c == 0:
            acc = acc * Q_SCALE
        if c < PROJT_SLICES - 1:
            o_ref[0, rows, :] = acc.astype(BF16)
        else:
            for h in range(ATTN_HEADS):
                r0 = PROJT_V_ROW0 + h * V_ROWS
                o_ref[0, r0:r0 + ATTN_HEAD_DIM, :] = acc[h * ATTN_HEAD_DIM:(h + 1) * ATTN_HEAD_DIM].astype(BF16)
                o_ref[0, r0 + ATTN_HEAD_DIM:r0 + V_ROWS, :] = ones_row
    k_ref[0] = jnp.dot(xb, wk_ref[...], preferred_element_type=F32).astype(BF16)


def _inproj(x, w_in):
    B, S, D = x.shape
    W = ATTN_WIDTH
    w_t = jnp.concatenate([w_in[:, :W], w_in[:, 3 * W:], w_in[:, 2 * W:3 * W]], axis=1).T.astype(BF16)
    w_k = w_in[:, W:2 * W].astype(BF16)
    return pl.pallas_call(
        _inproj_kernel,
        grid=(B, S // INPROJ_TOKENS),
        in_specs=[
            pl.BlockSpec((1, INPROJ_TOKENS, D), lambda b, i: (b, i, 0)),
            pl.BlockSpec((PROJT_SLICES * W, D), lambda b, i: (0, 0)),
            pl.BlockSpec((D, W), lambda b, i: (0, 0)),
        ],
        out_specs=[
            pl.BlockSpec((1, PROJT_WIDTH, INPROJ_TOKENS), lambda b, i: (b, 0, i)),
            pl.BlockSpec((1, INPROJ_TOKENS, W), lambda b, i: (b, i, 0)),
        ],
        out_shape=[
            jax.ShapeDtypeStruct((B, PROJT_WIDTH, S), BF16),
            jax.ShapeDtypeStruct((B, S, W), BF16),
        ],
        compiler_params=_params("arbitrary", "arbitrary"),
        name="inproj",
    )(x, w_t, w_k)


def _moba_kernel(far_ref, q_ref, k_ref, v_ref, bias_rows_ref, o_ref,
                 bias_ref, kmean_ref, qz_ref, sel_ref, farsel_ref,
                 s0_ref, smax0_ref, p0_ref, alpha0_ref, s1_ref, smax1_ref, p1_ref, alpha1_ref,
                 m_ref, acc_ref, *, n_blocks):
    L = MOBA_BLOCK
    Dh = ATTN_HEAD_DIM
    HG = MOBA_HEADS
    group = pl.program_id(1)
    qi0 = pl.program_id(2) * MOBA_QTILES
    last_blk = n_blocks - 1
    units = [(a, h) for a in range(MOBA_QTILES) for h in range(HG)]
    s_ref, smax_ref = (s0_ref, s1_ref), (smax0_ref, smax1_ref)
    p_ref, alpha_ref = (p0_ref, p1_ref), (alpha0_ref, alpha1_ref)

    @pl.when(qi0 == 0)
    def _():
        chunk_blocks = min(8, n_blocks)
        chunk = chunk_blocks * L
        acc = jnp.zeros((n_blocks, HG * Dh), F32)
        for c in range(n_blocks // chunk_blocks):
            blk = lax.broadcasted_iota(jnp.int32, (n_blocks, chunk), 0)
            key = lax.broadcasted_iota(jnp.int32, (n_blocks, chunk), 1)
            ind = jnp.where(blk == c * chunk_blocks + key // L, 1.0 / L, 0.0).astype(BF16)
            acc = acc + jnp.dot(ind, k_ref[0, c * chunk:(c + 1) * chunk, :], preferred_element_type=F32)
        for kind in range(2):
            for h in range(HG):
                row = jnp.broadcast_to(bias_rows_ref[kind, h], (L, 2 * L))
                bias_ref[kind, h] = pltpu.roll(row, 0, 1, stride=1, stride_axis=0)[:, 0:L]
        hi = acc.astype(BF16)
        lo = (acc - hi.astype(F32)).astype(BF16)
        for pair in range(HG // 2):
            cols = slice(pair * HEAD_PAIR, (pair + 1) * HEAD_PAIR)
            kmean_ref[pair, 0:n_blocks, :] = hi[:, cols]
            kmean_ref[pair, n_blocks:2 * n_blocks, :] = lo[:, cols]

    blk = lax.broadcasted_iota(jnp.int32, (n_blocks, L), 0)
    for u, (a, h) in enumerate(units):
        qi = qi0 + a
        pair, hd = divmod(h, 2)
        past = blk < qi
        qh = q_ref[0, h * Dh:(h + 1) * Dh, a * L:(a + 1) * L]
        no_q = jnp.zeros_like(qh)
        qz = jnp.concatenate([qh, no_q] if hd == 0 else [no_q, qh], axis=0)
        qz_ref[u] = qz
        gate = jnp.dot(kmean_ref[pair], qz, preferred_element_type=F32)
        gate = gate[0:n_blocks] + gate[n_blocks:2 * n_blocks]
        gate = jnp.where(past, gate, -jnp.inf)
        for _ in range(MOBA_TOPK):
            top = jnp.max(gate, axis=0, keepdims=True)
            first = jnp.min(jnp.where(gate == top, blk, n_blocks), axis=0, keepdims=True)
            gate = jnp.where(blk == first, -jnp.inf, gate)
        knocked_out = gate == -jnp.inf
        sel_ref[u] = jnp.where(knocked_out, jnp.where(past, 0.0, MASKED), MASKED)
        far_row = jnp.where(blk < qi - 1, far_ref[group * HG + h], MASKED)
        farsel_ref[u, 0:n_blocks, :] = jnp.where(knocked_out, far_row, MASKED)
        farsel_ref[u, n_blocks:n_blocks + F32_SUBLANES, :] = jnp.zeros((F32_SUBLANES, L), F32)

    def stage_scores(u, j, tile, slot):
        pair = units[u][1] // 2
        kb = k_ref[0, pl.ds(pl.multiple_of(j * L, L), L), pair * HEAD_PAIR:(pair + 1) * HEAD_PAIR]
        s = jnp.dot(kb, qz_ref[u], preferred_element_type=F32)
        if tile is not None:
            s = s + tile
        s_ref[slot][u] = s
        smax_ref[slot][u] = jnp.max(s, axis=0, keepdims=True)

    def stage_softmax(u, row, slot):
        m_prev = m_ref[u]
        m_new = jnp.maximum(m_prev, smax_ref[slot][u] + row)
        alpha_ref[slot][u] = jnp.exp2(m_prev - m_new)
        p_ref[slot][u] = jnp.exp2(s_ref[slot][u] - (m_new - row)).astype(BF16)
        m_ref[u] = m_new

    def stage_values(u, j, slot):
        h = units[u][1]
        vb = v_ref[0, h * V_ROWS:(h + 1) * V_ROWS, pl.ds(pl.multiple_of(j * L, L), L)]
        pv = jnp.dot(vb, p_ref[slot][u], preferred_element_type=F32)
        acc_ref[u] = alpha_ref[slot][u] * acc_ref[u] + pv

    m_ref[...] = jnp.full(m_ref.shape, MASKED, F32)
    acc_ref[...] = jnp.zeros(acc_ref.shape, F32)

    zero_row = jnp.zeros((1, L), F32)
    own = [qi0 + a for a, _ in units]
    prev = [jnp.maximum(qi0 + a - 1, 0) for a, _ in units]
    for u, (a, h) in enumerate(units):
        stage_scores(u, own[u], bias_ref[0, h], 0)
        stage_scores(u, prev[u], bias_ref[1, h] + sel_ref[u, pl.ds(prev[u], 1), :], 1)
    for u in range(len(units)):
        stage_softmax(u, zero_row, 0)
    everyone = range(len(units))

    def step(t, slot, score_units=everyone, softmax_units=everyone, value_units=everyone):
        far_value = jnp.minimum(t - 2, last_blk)
        j = jnp.minimum(t, last_blk)
        jr = jnp.where(t == 0, n_blocks, jnp.minimum(t - 1, last_blk))
        for u in everyone:
            if u in score_units:
                stage_scores(u, j, None, slot)
            if u in softmax_units:
                stage_softmax(u, farsel_ref[u, pl.ds(jr, 1), :], 1 - slot)
            if u in value_units:
                stage_values(u, jnp.where(t == 0, own[u], jnp.where(t == 1, prev[u], far_value)), slot)

    def run_steps(trips, unroll):
        def unrolled_steps(k, carry):
            for s in range(unroll):
                step(unroll * k + s, s % 2)
            return carry
        lax.fori_loop(0, trips, unrolled_steps, 0)

    QT = MOBA_QTILES

    @pl.when(qi0 < QT)
    def _():
        n_steps = 2 + jnp.maximum(qi0 + QT - 2, 0)
        run_steps((n_steps + 1) // 2, 2)

    @pl.when(qi0 >= QT)
    def _():
        n_main = qi0 - 2
        left = (MOBA_UNROLL - 2) % MOBA_UNROLL
        run_steps((n_main - left) // MOBA_UNROLL, MOBA_UNROLL)
        for k in range(-left, QT + 2):
            with_block = lambda first: [u for u, (a, _) in enumerate(units) if a >= first]
            step(n_main + k, k % 2, with_block(k), with_block(k - 1), with_block(k - 2))
    for u, (a, h) in enumerate(units):
        o_ref[0, h * Dh:(h + 1) * Dh, a * L:(a + 1) * L] = acc_ref[u, 0:Dh, :] / acc_ref[u, Dh:Dh + 1, :]


def _t5_bucket_np(dist):
    n = np.maximum(dist, 0)
    max_exact = REL_BUCKETS // 2
    nf = np.maximum(n, 1).astype(np.float32)
    large = max_exact + (np.log(nf / max_exact) / math.log(REL_MAX_DIST / max_exact)
                         * (REL_BUCKETS - max_exact)).astype(np.int32)
    large = np.minimum(large, REL_BUCKETS - 1)
    return np.where(n < max_exact, n, large)


def _moba(proj_t, k_nat, rel_table):
    B, _, S = proj_t.shape
    L = MOBA_BLOCK
    n_blocks = S // L
    H = ATTN_HEADS
    table_t = rel_table.astype(F32).T
    wrapped = np.concatenate([np.arange(L), np.arange(-L, 0)])
    dist = np.stack([wrapped, wrapped + L])
    onehot = jnp.asarray(_t5_bucket_np(dist))[:, None, None, :] == jnp.arange(REL_BUCKETS)[None, None, :, None]
    bias_rows = jnp.sum(jnp.where(onehot, table_t[None, :, :, None], 0.0), axis=2) * LOG2E
    bias_rows = jnp.where(jnp.asarray(dist >= 0)[:, None, :], bias_rows, MASKED)[:, :, None, :]
    assert int(_t5_bucket_np(np.array(L + 1))) == REL_BUCKETS - 1
    far = table_t[:, REL_BUCKETS - 1] * LOG2E

    HG, QT = MOBA_HEADS, MOBA_QTILES
    U = HG * QT
    q_rows, v_rows = HG * ATTN_HEAD_DIM, HG * V_ROWS
    assert ATTN_HEADS % HG == 0 and HG % 2 == 0 and n_blocks % QT == 0 and PROJT_V_ROW0 % v_rows == 0
    assert MOBA_UNROLL % 2 == 0 and QT % MOBA_UNROLL == 0
    once = dict(pipeline_mode=pl.Buffered(1))
    grid_spec = pltpu.PrefetchScalarGridSpec(
        num_scalar_prefetch=1,
        grid=(B, ATTN_HEADS // HG, n_blocks // QT),
        in_specs=[
            pl.BlockSpec((1, q_rows, QT * L), lambda b, g, i, far: (b, g, i)),
            pl.BlockSpec((1, S, q_rows), lambda b, g, i, far: (b, 0, g), **once),
            pl.BlockSpec((1, v_rows, S), lambda b, g, i, far: (b, PROJT_V_ROW0 // v_rows + g, 0), **once),
            pl.BlockSpec((2, HG, 1, 2 * L), lambda b, g, i, far: (0, g, 0, 0)),
        ],
        out_specs=pl.BlockSpec((1, q_rows, QT * L), lambda b, g, i, far: (b, g, i)),
        scratch_shapes=[
            pltpu.VMEM((2, HG, L, L), F32),
            pltpu.VMEM((HG // 2, 2 * n_blocks, HEAD_PAIR), BF16),
            pltpu.VMEM((U, HEAD_PAIR, L), BF16),
            pltpu.VMEM((U, n_blocks, L), F32),
            pltpu.VMEM((U, n_blocks + F32_SUBLANES, L), F32),
            *([pltpu.VMEM((U, L, L), F32), pltpu.VMEM((U, 1, L), F32),
               pltpu.VMEM((U, L, L), BF16), pltpu.VMEM((U, 1, L), F32)] * 2),
            pltpu.VMEM((U, 1, L), F32),
            pltpu.VMEM((U, V_ROWS, L), F32),
        ],
    )
    return pl.pallas_call(
        functools.partial(_moba_kernel, n_blocks=n_blocks),
        grid_spec=grid_spec,
        out_shape=jax.ShapeDtypeStruct((B, ATTN_WIDTH, S), F32),
        compiler_params=_params("arbitrary", "arbitrary", "arbitrary"),
        name="moba",
    )(far, proj_t, k_nat, proj_t, bias_rows)


def _retention_kernel(cdec_ref, q_ref, k_ref, v_ref, g_ref, cos_ref, sin_ref,
                      dec_ref, qdec_ref, kdec_ref, o_ref, state_ref):
    c = pl.program_id(0)

    @pl.when(c == 0)
    def _():
        state_ref[...] = jnp.zeros(state_ref.shape, F32)

    cos = cos_ref[...]
    sin = sin_ref[...]
    half = RET_HEAD_DIM // 2

    def rope(t):
        t1, t2 = t[:half], t[half:]
        return jnp.concatenate([t1 * cos - t2 * sin, t1 * sin + t2 * cos], axis=0)

    for b in range(q_ref.shape[0]):
        for hh in range(RET_HEADS):
            rows = slice(hh * RET_HEAD_DIM, (hh + 1) * RET_HEAD_DIM)
            q = rope(q_ref[b, rows, :].astype(F32))
            k = rope(k_ref[b, rows, :].astype(F32)) * (RET_HEAD_DIM ** -0.5)
            v = v_ref[b, rows, :]
            qb = q.astype(BF16)
            s = lax.dot_general(k.astype(BF16), qb, _TN, preferred_element_type=F32)
            s = (s * dec_ref[hh]).astype(BF16)
            inner = jnp.dot(v, s, preferred_element_type=F32)
            state = state_ref[b, hh]
            cross = jnp.dot(state.astype(BF16), (q * qdec_ref[hh:hh + 1, :]).astype(BF16),
                            preferred_element_type=F32)
            kd = (k * kdec_ref[hh:hh + 1, :]).astype(BF16)
            state_ref[b, hh] = state * cdec_ref[hh] + lax.dot_general(v, kd, _NT, preferred_element_type=F32)
            out = inner + cross
            mu = jnp.mean(out, axis=0, keepdims=True)
            var = jnp.mean(jnp.square(out - mu), axis=0, keepdims=True)
            out = (out - mu) * lax.rsqrt(var + LN_EPS)
            g = g_ref[b, rows, :].astype(F32)
            o_ref[b, rows, :] = (g * jax.nn.sigmoid(g) * out).astype(BF16)


def _retention(proj_t):
    B, _, S = proj_t.shape
    C = RET_CHUNK
    H = RET_HEADS
    half = RET_HEAD_DIM // 2
    inv = ROPE_BASE ** (-jnp.arange(half, dtype=F32) / half)
    ang = inv[:, None] * jnp.arange(S).astype(F32)[None, :]
    cos_t, sin_t = jnp.cos(ang), jnp.sin(ang)
    gammas = 1.0 - jnp.exp(jnp.linspace(math.log(1.0 / 32), math.log(1.0 / 512), H, dtype=F32))
    log_g = jnp.log(gammas)
    idx = jnp.arange(C, dtype=F32)
    diff = idx[None, :] - idx[:, None]
    dec_t = jnp.where(diff[None] >= 0, jnp.exp(jnp.maximum(diff, 0.0)[None] * log_g[:, None, None]), 0.0)
    q_dec = jnp.exp((idx[None, :] + 1.0) * log_g[:, None])
    k_dec = jnp.exp((C - 1.0 - idx[None, :]) * log_g[:, None])
    chunk_dec = jnp.exp(C * log_g)

    grid_spec = pltpu.PrefetchScalarGridSpec(
        num_scalar_prefetch=1,
        grid=(S // C,),
        in_specs=[
            pl.BlockSpec((B, RET_WIDTH, C), lambda c, cd: (0, 1, c)),
            pl.BlockSpec((B, RET_WIDTH, C), lambda c, cd: (0, 2, c)),
            pl.BlockSpec((B, RET_WIDTH, C), lambda c, cd: (0, 3, c)),
            pl.BlockSpec((B, RET_WIDTH, C), lambda c, cd: (0, 4, c)),
            pl.BlockSpec((half, C), lambda c, cd: (0, c)),
            pl.BlockSpec((half, C), lambda c, cd: (0, c)),
            pl.BlockSpec((H, C, C), lambda c, cd: (0, 0, 0)),
            pl.BlockSpec((H, C), lambda c, cd: (0, 0)),
            pl.BlockSpec((H, C), lambda c, cd: (0, 0)),
        ],
        out_specs=pl.BlockSpec((B, RET_WIDTH, C), lambda c, cd: (0, 0, c)),
        scratch_shapes=[pltpu.VMEM((B, H, RET_HEAD_DIM, RET_HEAD_DIM), F32)],
    )
    return pl.pallas_call(
        _retention_kernel,
        grid_spec=grid_spec,
        out_shape=jax.ShapeDtypeStruct((B, RET_WIDTH, S), BF16),
        compiler_params=_params("arbitrary"),
        name="retention",
    )(chunk_dec, proj_t, proj_t, proj_t, proj_t, cos_t, sin_t, dec_t, q_dec, k_dec)


def _layer_norm_rows(y, g, b):
    mu = jnp.mean(y, axis=-1, keepdims=True)
    var = jnp.mean(jnp.square(y - mu), axis=-1, keepdims=True)
    return (y - mu) * lax.rsqrt(var + LN_EPS) * g + b


def _outproj_kernel(attn_ref, ret_ref, x_ref, gain_ref, wa_ref, wr_ref, g1_ref, b1_ref,
                    wrt_ref, brt_ref, before_ref,
                    h_ref, eid_ref, rank_ref, cw_ref, cnt_ref, carry_ref):
    first = jnp.logical_and(pl.program_id(0) == 0, pl.program_id(1) == 0)

    @pl.when(first)
    def _():
        carry_ref[...] = jnp.zeros(carry_ref.shape, F32)

    a = attn_ref[0]
    a = a * lax.rsqrt(jnp.mean(jnp.square(a), axis=0, keepdims=True) + LN_EPS) * gain_ref[...]
    mix = (lax.dot_general(a.astype(BF16), wa_ref[...], _TN, preferred_element_type=F32)
           + lax.dot_general(ret_ref[0], wr_ref[...], _TN, preferred_element_type=F32))
    h = _layer_norm_rows(ALPHA * x_ref[0] + mix, g1_ref[...], b1_ref[...])
    h_ref[...] = h

    R = ROUTER_ROWS
    h_hi = h.astype(BF16)
    h_lo = (h - h_hi.astype(F32)).astype(BF16)
    by_hi = lax.dot_general(wrt_ref[...], h_hi, _NT, preferred_element_type=F32)
    by_lo = lax.dot_general(wrt_ref[0:R, :], h_lo, _NT, preferred_element_type=F32)
    logits = by_hi[0:R] + by_hi[R:2 * R] + by_lo + brt_ref[...]
    T = logits.shape[1]
    gl = logits[0:N_GROUPS]
    gmax = jnp.max(gl, axis=0, keepdims=True)
    grow = lax.broadcasted_iota(jnp.int32, gl.shape, 0)
    gidx = jnp.min(jnp.where(gl == gmax, grow, N_GROUPS), axis=0, keepdims=True)
    g_p = 1.0 / jnp.sum(jnp.exp(gl - gmax), axis=0, keepdims=True)
    el = jnp.zeros((EXPERTS_PER_GROUP, T), F32)
    for g in range(N_GROUPS):
        r0 = ROUTER_EXPERT_ROW0 + g * EXPERTS_PER_GROUP
        el = jnp.where(gidx == g, logits[r0:r0 + EXPERTS_PER_GROUP], el)
    erow = lax.broadcasted_iota(jnp.int32, el.shape, 0)
    e1 = jnp.max(el, axis=0, keepdims=True)
    i1 = jnp.min(jnp.where(el == e1, erow, EXPERTS_PER_GROUP), axis=0, keepdims=True)
    el2 = jnp.where(erow == i1, -jnp.inf, el)
    e2 = jnp.max(el2, axis=0, keepdims=True)
    i2 = jnp.min(jnp.where(el2 == e2, erow, EXPERTS_PER_GROUP), axis=0, keepdims=True)
    r = jnp.exp(e2 - e1)
    w1 = g_p / (1.0 + r)
    w2 = g_p * r / (1.0 + r)
    id1 = gidx * EXPERTS_PER_GROUP + i1
    id2 = gidx * EXPERTS_PER_GROUP + i2
    eid_ref[0:1, :] = id1
    eid_ref[1:2, :] = id2
    cw_ref[0:1, :] = w1
    cw_ref[1:2, :] = w2

    xrow = lax.broadcasted_iota(jnp.int32, (N_EXPERTS, T), 0)
    oh1 = (xrow == id1).astype(F32)
    oh2 = (xrow == id2).astype(F32)
    both = oh1 + oh2
    seen = jnp.dot(both.astype(BF16), before_ref[...], preferred_element_type=F32) + carry_ref[...]
    rank_ref[0:1, :] = jnp.sum(oh1 * seen, axis=0, keepdims=True).astype(jnp.int32)
    rank_ref[1:2, :] = jnp.sum(oh2 * seen, axis=0, keepdims=True).astype(jnp.int32)
    carry = carry_ref[...] + jnp.sum(both, axis=1, keepdims=True)
    carry_ref[...] = carry
    cnt_ref[...] = carry.astype(jnp.int32)


def _outproj(attn_t, ret_t, x, gain, w_out, ln_g, ln_b, w_rg, b_rg, w_re, b_re):
    B, S, D = x.shape
    Tt = OUTPROJ_TOKENS
    n_t = S // Tt
    T = B * S
    wa = w_out[:ATTN_WIDTH].astype(BF16)
    wr = w_out[ATTN_WIDTH:].astype(BF16)
    e0, e1 = ROUTER_EXPERT_ROW0, ROUTER_EXPERT_ROW0 + N_EXPERTS
    wrt = jnp.zeros((ROUTER_ROWS, D), F32)
    wrt = wrt.at[:N_GROUPS].set(w_rg.T).at[e0:e1].set(w_re.T)
    wrt_hi = wrt.astype(BF16)
    wrt_lo = (wrt - wrt_hi.astype(F32)).astype(BF16)
    wrt = jnp.concatenate([wrt_hi, wrt_lo], axis=0)
    brt = jnp.zeros((ROUTER_ROWS, 1), F32)
    brt = brt.at[:N_GROUPS, 0].set(b_rg.astype(F32)).at[e0:e1, 0].set(b_re.astype(F32))
    before = (jnp.arange(Tt)[:, None] < jnp.arange(Tt)[None, :]).astype(BF16)
    const = lambda b, i: (0, 0)
    tok = lambda b, i: (0, b * n_t + i)
    return pl.pallas_call(
        _outproj_kernel,
        grid=(B, n_t),
        in_specs=[
            pl.BlockSpec((1, ATTN_WIDTH, Tt), lambda b, i: (b, 0, i)),
            pl.BlockSpec((1, RET_WIDTH, Tt), lambda b, i: (b, 0, i)),
            pl.BlockSpec((1, Tt, D), lambda b, i: (b, i, 0)),
            pl.BlockSpec((ATTN_WIDTH, 1), const),
            pl.BlockSpec((ATTN_WIDTH, D), const),
            pl.BlockSpec((RET_WIDTH, D), const),
            pl.BlockSpec((1, D), const),
            pl.BlockSpec((1, D), const),
            pl.BlockSpec((2 * ROUTER_ROWS, D), const),
            pl.BlockSpec((ROUTER_ROWS, 1), const),
            pl.BlockSpec((Tt, Tt), const),
        ],
        out_specs=[
            pl.BlockSpec((Tt, D), lambda b, i: (b * n_t + i, 0)),
            pl.BlockSpec((2, Tt), tok),
            pl.BlockSpec((2, Tt), tok),
            pl.BlockSpec((2, Tt), tok),
            pl.BlockSpec((N_EXPERTS, 1), const),
        ],
        out_shape=[
            jax.ShapeDtypeStruct((T, D), F32),
            jax.ShapeDtypeStruct((2, T), jnp.int32),
            jax.ShapeDtypeStruct((2, T), jnp.int32),
            jax.ShapeDtypeStruct((2, T), F32),
            jax.ShapeDtypeStruct((N_EXPERTS, 1), jnp.int32),
        ],
        scratch_shapes=[pltpu.VMEM((N_EXPERTS, 1), F32)],
        compiler_params=_params("arbitrary", "arbitrary"),
        name="outproj_ln_router",
    )(attn_t, ret_t, x, gain.reshape(ATTN_WIDTH, 1), wa, wr, ln_g.reshape(1, D), ln_b.reshape(1, D),
      wrt, brt, before)


def _dispatch_kernel(start_ref, end_ref, pos0_ref, pos1_ref, h_ref, xs_hbm, zero_ref, sem, zsem):
    G = h_ref.shape[0]
    Tm = MOE_ROWS

    @pl.when(pl.program_id(0) == 0)
    def _():
        zero_ref[...] = jnp.zeros(zero_ref.shape, F32)

        def tile_clear(row0):
            row0 = pl.multiple_of(row0, Tm)
            return pltpu.make_async_copy(zero_ref, xs_hbm.at[pl.ds(row0, Tm)], zsem)

        used = end_ref[N_EXPERTS - 1]
        n_rows = xs_hbm.shape[0]
        clears = [(end_ref[e] - Tm, end_ref[e] > start_ref[e]) for e in range(N_EXPERTS)]
        clears += [(jnp.minimum(used + k * Tm, n_rows - Tm), used + k * Tm < n_rows) for k in range(N_EXPERTS)]
        for row0, cond in clears:
            @pl.when(cond)
            def _():
                tile_clear(row0).start()
        for row0, cond in clears:
            @pl.when(cond)
            def _():
                tile_clear(row0).wait()

    def issue(k, carry):
        t0 = pl.multiple_of(k * ROW_DMA_UNROLL, ROW_DMA_UNROLL)
        for u in range(ROW_DMA_UNROLL):
            for pos_ref in (pos0_ref, pos1_ref):
                pltpu.make_async_copy(h_ref.at[pl.ds(t0 + u, 1)], xs_hbm.at[pl.ds(pos_ref[t0 + u], 1)], sem).start()
        return carry

    lax.fori_loop(0, G // ROW_DMA_UNROLL, issue, 0)
    for _ in range(2):
        pltpu.make_async_copy(h_ref, xs_hbm.at[pl.ds(0, G)], sem).wait()


def _dispatch(h, pos0, pos1, starts, ends, n_rows):
    T, D = h.shape
    G = DISPATCH_TOKENS if T % DISPATCH_TOKENS == 0 else ROUTE_TOKENS
    smem_blk = pl.BlockSpec((G,), lambda i, st, en: (i,), memory_space=pltpu.SMEM)
    grid_spec = pltpu.PrefetchScalarGridSpec(
        num_scalar_prefetch=2,
        grid=(T // G,),
        in_specs=[smem_blk, smem_blk, pl.BlockSpec((G, D), lambda i, st, en: (i, 0))],
        out_specs=pl.BlockSpec(memory_space=pl.ANY),
        scratch_shapes=[pltpu.VMEM((MOE_ROWS, D), F32), pltpu.SemaphoreType.DMA(()),
                        pltpu.SemaphoreType.DMA(())],
    )
    return pl.pallas_call(
        _dispatch_kernel,
        grid_spec=grid_spec,
        out_shape=jax.ShapeDtypeStruct((n_rows, D), F32),
        compiler_params=_params("arbitrary"),
        name="moe_dispatch",
    )(starts, ends, pos0, pos1, h)


def _experts_kernel(te_ref, nv_ref, next_ref, slot_ref, xs_ref, wgu_hbm, wdn_hbm, y_ref,
                    wgu_f, wdn_f, wgu_b, wdn_b, sems):
    Tm = MOE_ROWS

    def weight_copies(e, slot):
        return (pltpu.make_async_copy(wgu_hbm.at[e], wgu_f.at[slot], sems.at[slot]),
                pltpu.make_async_copy(wdn_hbm.at[e], wdn_f.at[slot], sems.at[slot]))

    for part in range(EXPERT_TILES_PER_STEP):
        tile = pl.program_id(0) * EXPERT_TILES_PER_STEP + part
        rows = slice(part * Tm, (part + 1) * Tm)
        live = tile < nv_ref[0]
        expert = te_ref[tile]
        new_expert = jnp.logical_or(tile == 0, expert != te_ref[jnp.maximum(tile - 1, 0)])

        @pl.when(jnp.logical_and(live, new_expert))
        def _():
            slot = slot_ref[expert]

            @pl.when(tile == 0)
            def _():
                for copy in weight_copies(expert, slot):
                    copy.start()

            for copy in weight_copies(expert, slot):
                copy.wait()
            wgu_b[...] = wgu_f[slot].astype(BF16)
            wdn_b[...] = wdn_f[slot].astype(BF16)
            following = next_ref[expert]

            @pl.when(following >= 0)
            def _():
                for copy in weight_copies(following, 1 - slot):
                    copy.start()

        @pl.when(live)
        def _():
            gu = jnp.dot(xs_ref[rows, :].astype(BF16), wgu_b[...], preferred_element_type=F32)
            gate, up = gu[:, :D_EXPERT], gu[:, D_EXPERT:]
            hid = gate * jax.nn.sigmoid(gate) * up
            y_ref[rows, :] = jnp.dot(hid.astype(BF16), wdn_b[...], preferred_element_type=F32)

        @pl.when(jnp.logical_not(live))
        def _():
            y_ref[rows, :] = jnp.zeros((Tm, y_ref.shape[1]), F32)


def _experts(xs, w_gu, w_dn, tile_expert, n_valid, has_rows):
    n_rows, D = xs.shape
    Tm = MOE_ROWS
    ids = jnp.arange(N_EXPERTS, dtype=jnp.int32)
    later = jnp.logical_and(has_rows[None, :], ids[None, :] > ids[:, None])
    next_expert = jnp.where(jnp.any(later, axis=1), jnp.argmax(later, axis=1), -1).astype(jnp.int32)
    slot = ((jnp.cumsum(has_rows.astype(jnp.int32)) - has_rows.astype(jnp.int32)) % 2).astype(jnp.int32)
    per_step = EXPERT_TILES_PER_STEP
    assert (n_rows // Tm) % per_step == 0
    grid_spec = pltpu.PrefetchScalarGridSpec(
        num_scalar_prefetch=4,
        grid=(n_rows // (per_step * Tm),),
        in_specs=[
            pl.BlockSpec((per_step * Tm, D),
                         lambda i, te, nv, nx, sl: (jnp.minimum(i, jnp.maximum(nv[0] - 1, 0) // per_step), 0)),
            pl.BlockSpec(memory_space=pl.ANY),
            pl.BlockSpec(memory_space=pl.ANY),
        ],
        out_specs=pl.BlockSpec((per_step * Tm, D), lambda i, te, nv, nx, sl: (i, 0)),
        scratch_shapes=[
            pltpu.VMEM((2, D, 2 * D_EXPERT), F32), pltpu.VMEM((2, D_EXPERT, D), F32),
            pltpu.VMEM((D, 2 * D_EXPERT), BF16), pltpu.VMEM((D_EXPERT, D), BF16),
            pltpu.SemaphoreType.DMA((2,)),
        ],
    )
    return pl.pallas_call(
        _experts_kernel,
        grid_spec=grid_spec,
        out_shape=jax.ShapeDtypeStruct((n_rows, D), F32),
        compiler_params=_params("arbitrary"),
        name="moe_experts",
    )(tile_expert, n_valid, next_expert, slot, xs, w_gu, w_dn)


def _combine_kernel(pos0_ref, pos1_ref, next0_ref, next1_ref, y_hbm, h_ref, cw_ref, g_ref, b_ref, o_ref,
                    ybuf_a, ybuf_b, sem_a, sem_b):
    G = ROUTE_TOKENS
    C = ROW_DMA_UNROLL
    i = pl.program_id(0)
    last = pl.num_programs(0) - 1

    def gather(pos_refs, buf, sem, t0):
        for u in range(C):
            for slot, pos_ref in enumerate(pos_refs):
                pltpu.make_async_copy(y_hbm.at[pl.ds(pos_ref[t0 + u], 1)], buf.at[slot, pl.ds(t0 + u, 1)],
                                      sem).start()

    def wait_tile(buf, sem):
        for slot in range(2):
            pltpu.make_async_copy(y_hbm.at[pl.ds(0, G)], buf.at[slot], sem).wait()

    @pl.when(i == 0)
    def _():
        def first_tile(k, carry):
            gather((pos0_ref, pos1_ref), ybuf_a, sem_a, pl.multiple_of(k * C, C))
            return carry
        lax.fori_loop(0, G // C, first_tile, 0)

    def run(buf, sem, spare, spare_sem):
        wait_tile(buf, sem)

        @pl.when(i < last)
        def _():
            def next_tile(k, carry):
                gather((next0_ref, next1_ref), spare, spare_sem, pl.multiple_of(k * C, C))
                return carry
            lax.fori_loop(0, G // C, next_tile, 0)

        for r in range(G // COMBINE_CHUNK):
            rows = slice(r * COMBINE_CHUNK, (r + 1) * COMBINE_CHUNK)
            cw = cw_ref[rows, :]
            ffn = buf[0, rows, :] * cw[:, 0:1] + buf[1, rows, :] * cw[:, 1:2]
            o_ref[rows, :] = _layer_norm_rows(ALPHA * h_ref[rows, :] + ffn, g_ref[...], b_ref[...])

    odd = jnp.bitwise_and(i, 1)

    @pl.when(odd == 0)
    def _():
        run(ybuf_a, sem_a, ybuf_b, sem_b)

    @pl.when(odd == 1)
    def _():
        run(ybuf_b, sem_b, ybuf_a, sem_a)


def _combine(y, h, pos0, pos1, cw_rows, ln_g, ln_b):
    T, D = h.shape
    G = ROUTE_TOKENS
    n = T // G
    smem_blk = pl.BlockSpec((G,), lambda i: (i,), memory_space=pltpu.SMEM)
    smem_next = pl.BlockSpec((G,), lambda i: (jnp.minimum(i + 1, n - 1),), memory_space=pltpu.SMEM)
    return pl.pallas_call(
        _combine_kernel,
        grid=(n,),
        in_specs=[
            smem_blk, smem_blk, smem_next, smem_next,
            pl.BlockSpec(memory_space=pl.ANY),
            pl.BlockSpec((G, D), lambda i: (i, 0)),
            pl.BlockSpec((G, 2), lambda i: (i, 0)),
            pl.BlockSpec((1, D), lambda i: (0, 0)),
            pl.BlockSpec((1, D), lambda i: (0, 0)),
        ],
        out_specs=pl.BlockSpec((G, D), lambda i: (i, 0)),
        scratch_shapes=[pltpu.VMEM((2, G, D), F32), pltpu.VMEM((2, G, D), F32),
                        pltpu.SemaphoreType.DMA(()), pltpu.SemaphoreType.DMA(())],
        out_shape=jax.ShapeDtypeStruct((T, D), F32),
        compiler_params=_params("arbitrary"),
        name="moe_combine_ln",
    )(pos0, pos1, pos0, pos1, y, h, cw_rows, ln_g.reshape(1, D), ln_b.reshape(1, D))


def kernel(x, w_in, attn_out_gain, rel_bias_table, w_out, ln1_g, ln1_b, w_router_group,
           b_router_group, w_router_expert, b_router_expert, w_gate_up, w_down, ln2_g, ln2_b):
    B, S, D = x.shape
    assert D == D_MODEL and S % MOBA_BLOCK == 0 and S % INPROJ_TOKENS == 0
    assert w_in.shape[0] == DEPTH
    T = B * S
    h = x
    for l in range(DEPTH):
        proj_t, k_nat = _inproj(h, w_in[l])
        attn_t = _moba(proj_t, k_nat, rel_bias_table)
        ret_t = _retention(proj_t)
        h1, eid, rank, cw, counts = _outproj(
            attn_t, ret_t, h, attn_out_gain[l], w_out[l], ln1_g[l], ln1_b[l],
            w_router_group[l], b_router_group[l], w_router_expert[l], b_router_expert[l])

        Tm = MOE_ROWS
        n_tiles = (2 * T) // Tm + N_EXPERTS
        padded = ((counts[:, 0] + Tm - 1) // Tm) * Tm
        ends = jnp.cumsum(padded)
        starts = (ends - padded).astype(jnp.int32)
        tile_row0 = jnp.arange(n_tiles, dtype=jnp.int32) * Tm
        tile_expert = jnp.sum((ends[None, :] <= jnp.minimum(tile_row0, ends[-1] - Tm)[:, None]).astype(jnp.int32),
                              axis=1)
        n_valid = (ends[-1:] // Tm).astype(jnp.int32)

        expert_ids = jnp.arange(N_EXPERTS, dtype=jnp.int32)[:, None, None]
        pos = jnp.sum(jnp.where(eid[None] == expert_ids, starts[:, None, None], 0), axis=0) + rank

        xs = _dispatch(h1, pos[0], pos[1], starts, ends.astype(jnp.int32), n_tiles * Tm)
        y = _experts(xs, w_gate_up[l], w_down[l], tile_expert, n_valid, padded > 0)
        h2 = _combine(y, h1, pos[0], pos[1], cw.T, ln2_g[l], ln2_b[l])
        h = h2.reshape(B, S, D)
    return h
```

```python
import functools
import math

import numpy as np
import jax
import jax.numpy as jnp
from jax import lax
from jax.experimental import pallas as pl
from jax.experimental.pallas import tpu as pltpu

F32 = jnp.float32
BF16 = jnp.bfloat16

D_MODEL = 1024
ATTN_HEADS = 8
ATTN_HEAD_DIM = 64
ATTN_WIDTH = ATTN_HEADS * ATTN_HEAD_DIM
RET_HEADS = 4
RET_HEAD_DIM = 128
RET_WIDTH = RET_HEADS * RET_HEAD_DIM
N_IN_SLICES = 7
PROJT_SLICES = N_IN_SLICES - 1
HEAD_PAIR = 2 * ATTN_HEAD_DIM
BF16_SUBLANES = 16
F32_SUBLANES = 8
V_ROWS = ATTN_HEAD_DIM + BF16_SUBLANES
PROJT_V_ROW0 = (PROJT_SLICES - 1) * ATTN_WIDTH
PROJT_WIDTH = PROJT_V_ROW0 + ATTN_HEADS * V_ROWS
MOBA_BLOCK = 256
MOBA_TOPK = 3
RET_CHUNK = 256
RET_CHUNKS_PER_STEP = 2
REL_BUCKETS = 32
REL_MAX_DIST = 128
ROPE_BASE = 10000.0
N_GROUPS = 4
EXPERTS_PER_GROUP = 8
N_EXPERTS = N_GROUPS * EXPERTS_PER_GROUP
D_EXPERT = D_MODEL // 2
DEPTH = 1
ALPHA = (2.0 * DEPTH) ** 0.25
LN_EPS = 1e-5

LOG2E = math.log2(math.e)
Q_SCALE = ATTN_HEAD_DIM ** -0.5 * LOG2E

MASKED = -1e30

INPROJ_TOKENS = 1024
INPROJ_ROWS = ATTN_WIDTH
OUTPROJ_TOKENS = 1024
MOBA_UNROLL = 2
MOBA_HEADS = 4
MOBA_QTILES = 4
MOE_ROWS = 512
EXPERT_TILES_PER_STEP = 2
DISPATCH_TOKENS = 2048
ROUTE_TOKENS = 1024
COMBINE_CHUNK = 256
ROW_DMA_UNROLL = 32
ROUTER_EXPERT_ROW0 = 8
ROUTER_ROWS = -(-(ROUTER_EXPERT_ROW0 + N_EXPERTS) // BF16_SUBLANES) * BF16_SUBLANES

V7X_VMEM_BYTES = 64 * 1024 * 1024
VMEM_LIMIT = V7X_VMEM_BYTES * 3 // 4

_TN = (((0,), (0,)), ((), ()))
_NT = (((1,), (1,)), ((), ()))


def _params(*sem):
    return pltpu.CompilerParams(dimension_semantics=sem, vmem_limit_bytes=VMEM_LIMIT)


def _inproj_kernel(x_ref, wt_ref, wk_ref, o_ref, k_ref):
    xb = x_ref[0].astype(BF16)
    n_tok = xb.shape[0]
    ones_row = (lax.broadcasted_iota(jnp.int32, (BF16_SUBLANES, n_tok), 0) == 0).astype(BF16)
    for c in range(PROJT_SLICES):
        rows = slice(c * INPROJ_ROWS, (c + 1) * INPROJ_ROWS)
        acc = lax.dot_general(wt_ref[rows, :], xb, _NT, preferred_element_type=F32)
        if c == 0:
            acc = acc * Q_SCALE
        if c < PROJT_SLICES - 1:
            o_ref[0, rows, :] = acc.astype(BF16)
        else:
            for h in range(ATTN_HEADS):
                r0 = PROJT_V_ROW0 + h * V_ROWS
                o_ref[0, r0:r0 + ATTN_HEAD_DIM, :] = acc[h * ATTN_HEAD_DIM:(h + 1) * ATTN_HEAD_DIM].astype(BF16)
                o_ref[0, r0 + ATTN_HEAD_DIM:r0 + V_ROWS, :] = ones_row
    k_ref[0] = jnp.dot(xb, wk_ref[...], preferred_element_type=F32).astype(BF16)


def _inproj(x, w_in):
    B, S, D = x.shape
    W = ATTN_WIDTH
    w_t = jnp.concatenate([w_in[:, :W], w_in[:, 3 * W:], w_in[:, 2 * W:3 * W]], axis=1).T.astype(BF16)
    w_k = w_in[:, W:2 * W].astype(BF16)
    return pl.pallas_call(
        _inproj_kernel,
        grid=(B, S // INPROJ_TOKENS),
        in_specs=[
            pl.BlockSpec((1, INPROJ_TOKENS, D), lambda b, i: (b, i, 0)),
            pl.BlockSpec((PROJT_SLICES * W, D), lambda b, i: (0, 0)),
            pl.BlockSpec((D, W), lambda b, i: (0, 0)),
        ],
        out_specs=[
            pl.BlockSpec((1, PROJT_WIDTH, INPROJ_TOKENS), lambda b, i: (b, 0, i)),
            pl.BlockSpec((1, INPROJ_TOKENS, W), lambda b, i: (b, i, 0)),
        ],
        out_shape=[
            jax.ShapeDtypeStruct((B, PROJT_WIDTH, S), BF16),
            jax.ShapeDtypeStruct((B, S, W), BF16),
        ],
        compiler_params=_params("arbitrary", "arbitrary"),
        name="inproj",
    )(x, w_t, w_k)


def _moba_kernel(far_ref, q_ref, k_ref, v_ref, bias_rows_ref, o_ref,
                 bias_ref, kmean_ref, qz_ref, sel_ref, farsel_ref,
                 s0_ref, smax0_ref, p0_ref, alpha0_ref, s1_ref, smax1_ref, p1_ref, alpha1_ref,
                 m_ref, acc_ref, *, n_blocks):
    L = MOBA_BLOCK
    Dh = ATTN_HEAD_DIM
    HG = MOBA_HEADS
    group = pl.program_id(1)
    qi0 = pl.program_id(2) * MOBA_QTILES
    last_blk = n_blocks - 1
    units = [(a, h) for a in range(MOBA_QTILES) for h in range(HG)]
    s_ref, smax_ref = (s0_ref, s1_ref), (smax0_ref, smax1_ref)
    p_ref, alpha_ref = (p0_ref, p1_ref), (alpha0_ref, alpha1_ref)

    @pl.when(qi0 == 0)
    def _():
        chunk_blocks = min(8, n_blocks)
        chunk = chunk_blocks * L
        acc = jnp.zeros((n_blocks, HG * Dh), F32)
        for c in range(n_blocks // chunk_blocks):
            blk = lax.broadcasted_iota(jnp.int32, (n_blocks, chunk), 0)
            key = lax.broadcasted_iota(jnp.int32, (n_blocks, chunk), 1)
            ind = jnp.where(blk == c * chunk_blocks + key // L, 1.0 / L, 0.0).astype(BF16)
            acc = acc + jnp.dot(ind, k_ref[0, c * chunk:(c + 1) * chunk, :], preferred_element_type=F32)
        for kind in range(2):
            for h in range(HG):
                row = jnp.broadcast_to(bias_rows_ref[kind, h], (L, 2 * L))
                bias_ref[kind, h] = pltpu.roll(row, 0, 1, stride=1, stride_axis=0)[:, 0:L]
        hi = acc.astype(BF16)
        lo = (acc - hi.astype(F32)).astype(BF16)
        for pair in range(HG // 2):
            cols = slice(pair * HEAD_PAIR, (pair + 1) * HEAD_PAIR)
            kmean_ref[pair, 0:n_blocks, :] = hi[:, cols]
            kmean_ref[pair, n_blocks:2 * n_blocks, :] = lo[:, cols]

    blk = lax.broadcasted_iota(jnp.int32, (n_blocks, L), 0)
    for u, (a, h) in enumerate(units):
        qi = qi0 + a
        pair, hd = divmod(h, 2)
        past = blk < qi
        qh = q_ref[0, h * Dh:(h + 1) * Dh, a * L:(a + 1) * L]
        no_q = jnp.zeros_like(qh)
        qz = jnp.concatenate([qh, no_q] if hd == 0 else [no_q, qh], axis=0)
        qz_ref[u] = qz
        gate = jnp.dot(kmean_ref[pair], qz, preferred_element_type=F32)
        gate = gate[0:n_blocks] + gate[n_blocks:2 * n_blocks]
        gate = jnp.where(past, gate, -jnp.inf)
        for _ in range(MOBA_TOPK):
            top = jnp.max(gate, axis=0, keepdims=True)
            first = jnp.min(jnp.where(gate == top, blk, n_blocks), axis=0, keepdims=True)
            gate = jnp.where(blk == first, -jnp.inf, gate)
        knocked_out = gate == -jnp.inf
        sel_ref[u] = jnp.where(knocked_out, jnp.where(past, 0.0, MASKED), MASKED)
        far_row = jnp.where(blk < qi - 1, far_ref[group * HG + h], MASKED)
        farsel_ref[u, 0:n_blocks, :] = jnp.where(knocked_out, far_row, MASKED)
        farsel_ref[u, n_blocks:n_blocks + F32_SUBLANES, :] = jnp.zeros((F32_SUBLANES, L), F32)

    def stage_scores(u, j, tile, slot):
        pair = units[u][1] // 2
        kb = k_ref[0, pl.ds(pl.multiple_of(j * L, L), L), pair * HEAD_PAIR:(pair + 1) * HEAD_PAIR]
        s = jnp.dot(kb, qz_ref[u], preferred_element_type=F32)
        if tile is not None:
            s = s + tile
        s_ref[slot][u] = s
        smax_ref[slot][u] = jnp.max(s, axis=0, keepdims=True)

    def stage_softmax(u, row, slot):
        m_prev = m_ref[u]
        m_new = jnp.maximum(m_prev, smax_ref[slot][u] + row)
        alpha_ref[slot][u] = jnp.exp2(m_prev - m_new)
        p_ref[slot][u] = jnp.exp2(s_ref[slot][u] - (m_new - row)).astype(BF16)
        m_ref[u] = m_new

    def stage_values(u, j, slot):
        h = units[u][1]
        vb = v_ref[0, h * V_ROWS:(h + 1) * V_ROWS, pl.ds(pl.multiple_of(j * L, L), L)]
        pv = jnp.dot(vb, p_ref[slot][u], preferred_element_type=F32)
        acc_ref[u] = alpha_ref[slot][u] * acc_ref[u] + pv

    m_ref[...] = jnp.full(m_ref.shape, MASKED, F32)
    acc_ref[...] = jnp.zeros(acc_ref.shape, F32)

    zero_row = jnp.zeros((1, L), F32)
    own = [qi0 + a for a, _ in units]
    prev = [jnp.maximum(qi0 + a - 1, 0) for a, _ in units]
    for u, (a, h) in enumerate(units):
        stage_scores(u, own[u], bias_ref[0, h], 0)
        stage_scores(u, prev[u], bias_ref[1, h] + sel_ref[u, pl.ds(prev[u], 1), :], 1)
    for u in range(len(units)):
        stage_softmax(u, zero_row, 0)
    everyone = range(len(units))

    def step(t, slot, score_units=everyone, softmax_units=everyone, value_units=everyone):
        far_value = jnp.minimum(t - 2, last_blk)
        j = jnp.minimum(t, last_blk)
        jr = jnp.where(t == 0, n_blocks, jnp.minimum(t - 1, last_blk))
        for u in everyone:
            if u in score_units:
                stage_scores(u, j, None, slot)
            if u in softmax_units:
                stage_softmax(u, farsel_ref[u, pl.ds(jr, 1), :], 1 - slot)
            if u in value_units:
                stage_values(u, jnp.where(t == 0, own[u], jnp.where(t == 1, prev[u], far_value)), slot)

    def run_steps(trips, unroll):
        def unrolled_steps(k, carry):
            for s in range(unroll):
                step(unroll * k + s, s % 2)
            return carry
        lax.fori_loop(0, trips, unrolled_steps, 0)

    QT = MOBA_QTILES

    @pl.when(qi0 < QT)
    def _():
        n_steps = 2 + jnp.maximum(qi0 + QT - 2, 0)
        run_steps((n_steps + 1) // 2, 2)

    @pl.when(qi0 >= QT)
    def _():
        n_main = qi0 - 2
        left = (MOBA_UNROLL - 2) % MOBA_UNROLL
        run_steps((n_main - left) // MOBA_UNROLL, MOBA_UNROLL)
        for k in range(-left, QT + 2):
            with_block = lambda first: [u for u, (a, _) in enumerate(units) if a >= first]
            step(n_main + k, k % 2, with_block(k), with_block(k - 1), with_block(k - 2))
    for u, (a, h) in enumerate(units):
        o_ref[0, h * Dh:(h + 1) * Dh, a * L:(a + 1) * L] = acc_ref[u, 0:Dh, :] / acc_ref[u, Dh:Dh + 1, :]


def _t5_bucket_np(dist):
    n = np.maximum(dist, 0)
    max_exact = REL_BUCKETS // 2
    nf = np.maximum(n, 1).astype(np.float32)
    large = max_exact + (np.log(nf / max_exact) / math.log(REL_MAX_DIST / max_exact)
                         * (REL_BUCKETS - max_exact)).astype(np.int32)
    large = np.minimum(large, REL_BUCKETS - 1)
    return np.where(n < max_exact, n, large)


def _moba(proj_t, k_nat, rel_table):
    B, _, S = proj_t.shape
    L = MOBA_BLOCK
    n_blocks = S // L
    H = ATTN_HEADS
    table_t = rel_table.astype(F32).T
    wrapped = np.concatenate([np.arange(L), np.arange(-L, 0)])
    dist = np.stack([wrapped, wrapped + L])
    onehot = jnp.asarray(_t5_bucket_np(dist))[:, None, None, :] == jnp.arange(REL_BUCKETS)[None, None, :, None]
    bias_rows = jnp.sum(jnp.where(onehot, table_t[None, :, :, None], 0.0), axis=2) * LOG2E
    bias_rows = jnp.where(jnp.asarray(dist >= 0)[:, None, :], bias_rows, MASKED)[:, :, None, :]
    assert int(_t5_bucket_np(np.array(L + 1))) == REL_BUCKETS - 1
    far = table_t[:, REL_BUCKETS - 1] * LOG2E

    HG, QT = MOBA_HEADS, MOBA_QTILES
    U = HG * QT
    q_rows, v_rows = HG * ATTN_HEAD_DIM, HG * V_ROWS
    assert ATTN_HEADS % HG == 0 and HG % 2 == 0 and n_blocks % QT == 0 and PROJT_V_ROW0 % v_rows == 0
    assert MOBA_UNROLL % 2 == 0 and QT % MOBA_UNROLL == 0
    once = dict(pipeline_mode=pl.Buffered(1))
    grid_spec = pltpu.PrefetchScalarGridSpec(
        num_scalar_prefetch=1,
        grid=(B, ATTN_HEADS // HG, n_blocks // QT),
        in_specs=[
            pl.BlockSpec((1, q_rows, QT * L), lambda b, g, i, far: (b, g, i)),
            pl.BlockSpec((1, S, q_rows), lambda b, g, i, far: (b, 0, g), **once),
            pl.BlockSpec((1, v_rows, S), lambda b, g, i, far: (b, PROJT_V_ROW0 // v_rows + g, 0), **once),
            pl.BlockSpec((2, HG, 1, 2 * L), lambda b, g, i, far: (0, g, 0, 0)),
        ],
        out_specs=pl.BlockSpec((1, q_rows, QT * L), lambda b, g, i, far: (b, g, i)),
        scratch_shapes=[
            pltpu.VMEM((2, HG, L, L), F32),
            pltpu.VMEM((HG // 2, 2 * n_blocks, HEAD_PAIR), BF16),
            pltpu.VMEM((U, HEAD_PAIR, L), BF16),
            pltpu.VMEM((U, n_blocks, L), F32),
            pltpu.VMEM((U, n_blocks + F32_SUBLANES, L), F32),
            *([pltpu.VMEM((U, L, L), F32), pltpu.VMEM((U, 1, L), F32),
               pltpu.VMEM((U, L, L), BF16), pltpu.VMEM((U, 1, L), F32)] * 2),
            pltpu.VMEM((U, 1, L), F32),
            pltpu.VMEM((U, V_ROWS, L), F32),
        ],
    )
    return pl.pallas_call(
        functools.partial(_moba_kernel, n_blocks=n_blocks),
        grid_spec=grid_spec,
        out_shape=jax.ShapeDtypeStruct((B, ATTN_WIDTH, S), F32),
        compiler_params=_params("arbitrary", "arbitrary", "arbitrary"),
        name="moba",
    )(far, proj_t, k_nat, proj_t, bias_rows)


def _retention_kernel(cdec_ref, q_ref, k_ref, v_ref, g_ref, cos_ref, sin_ref,
                      dec_ref, qdec_ref, kdec_ref, o_ref, state_ref):
    c = pl.program_id(0)

    @pl.when(c == 0)
    def _():
        state_ref[...] = jnp.zeros(state_ref.shape, F32)

    half = RET_HEAD_DIM // 2
    C = RET_CHUNK

    for sub in range(RET_CHUNKS_PER_STEP):
        cols = slice(sub * C, (sub + 1) * C)
        cos = cos_ref[:, cols]
        sin = sin_ref[:, cols]

        def rope(t):
            t1, t2 = t[:half], t[half:]
            return jnp.concatenate([t1 * cos - t2 * sin, t1 * sin + t2 * cos], axis=0)

        for b in range(q_ref.shape[0]):
            for hh in range(RET_HEADS):
                rows = slice(hh * RET_HEAD_DIM, (hh + 1) * RET_HEAD_DIM)
                q = rope(q_ref[b, rows, cols].astype(F32))
                k = rope(k_ref[b, rows, cols].astype(F32)) * (RET_HEAD_DIM ** -0.5)
                v = v_ref[b, rows, cols]
                qb = q.astype(BF16)
                s = lax.dot_general(k.astype(BF16), qb, _TN, preferred_element_type=F32)
                s = (s * dec_ref[hh]).astype(BF16)
                inner = jnp.dot(v, s, preferred_element_type=F32)
                state = state_ref[b, hh]
                cross = jnp.dot(state.astype(BF16), (q * qdec_ref[hh:hh + 1, :]).astype(BF16),
                                preferred_element_type=F32)
                kd = (k * kdec_ref[hh:hh + 1, :]).astype(BF16)
                state_ref[b, hh] = state * cdec_ref[hh] + lax.dot_general(v, kd, _NT,
                                                                          preferred_element_type=F32)
                out = inner + cross
                mu = jnp.mean(out, axis=0, keepdims=True)
                var = jnp.mean(jnp.square(out - mu), axis=0, keepdims=True)
                out = (out - mu) * lax.rsqrt(var + LN_EPS)
                g = g_ref[b, rows, cols].astype(F32)
                o_ref[b, rows, cols] = (g * jax.nn.sigmoid(g) * out).astype(BF16)


def _retention(proj_t):
    B, _, S = proj_t.shape
    C = RET_CHUNK
    H = RET_HEADS
    half = RET_HEAD_DIM // 2
    inv = ROPE_BASE ** (-jnp.arange(half, dtype=F32) / half)
    ang = inv[:, None] * jnp.arange(S).astype(F32)[None, :]
    cos_t, sin_t = jnp.cos(ang), jnp.sin(ang)
    gammas = 1.0 - jnp.exp(jnp.linspace(math.log(1.0 / 32), math.log(1.0 / 512), H, dtype=F32))
    log_g = jnp.log(gammas)
    idx = jnp.arange(C, dtype=F32)
    diff = idx[None, :] - idx[:, None]
    dec_t = jnp.where(diff[None] >= 0, jnp.exp(jnp.maximum(diff, 0.0)[None] * log_g[:, None, None]), 0.0)
    q_dec = jnp.exp((idx[None, :] + 1.0) * log_g[:, None])
    k_dec = jnp.exp((C - 1.0 - idx[None, :]) * log_g[:, None])
    chunk_dec = jnp.exp(C * log_g)

    Cs = RET_CHUNKS_PER_STEP * C
    assert S % Cs == 0
    grid_spec = pltpu.PrefetchScalarGridSpec(
        num_scalar_prefetch=1,
        grid=(S // Cs,),
        in_specs=[
            pl.BlockSpec((B, RET_WIDTH, Cs), lambda c, cd: (0, 1, c)),
            pl.BlockSpec((B, RET_WIDTH, Cs), lambda c, cd: (0, 2, c)),
            pl.BlockSpec((B, RET_WIDTH, Cs), lambda c, cd: (0, 3, c)),
            pl.BlockSpec((B, RET_WIDTH, Cs), lambda c, cd: (0, 4, c)),
            pl.BlockSpec((half, Cs), lambda c, cd: (0, c)),
            pl.BlockSpec((half, Cs), lambda c, cd: (0, c)),
            pl.BlockSpec((H, C, C), lambda c, cd: (0, 0, 0)),
            pl.BlockSpec((H, C), lambda c, cd: (0, 0)),
            pl.BlockSpec((H, C), lambda c, cd: (0, 0)),
        ],
        out_specs=pl.BlockSpec((B, RET_WIDTH, Cs), lambda c, cd: (0, 0, c)),
        scratch_shapes=[pltpu.VMEM((B, H, RET_HEAD_DIM, RET_HEAD_DIM), F32)],
    )
    return pl.pallas_call(
        _retention_kernel,
        grid_spec=grid_spec,
        out_shape=jax.ShapeDtypeStruct((B, RET_WIDTH, S), BF16),
        compiler_params=_params("arbitrary"),
        name="retention",
    )(chunk_dec, proj_t, proj_t, proj_t, proj_t, cos_t, sin_t, dec_t, q_dec, k_dec)


def _layer_norm_rows(y, g, b):
    mu = jnp.mean(y, axis=-1, keepdims=True)
    var = jnp.mean(jnp.square(y - mu), axis=-1, keepdims=True)
    return (y - mu) * lax.rsqrt(var + LN_EPS) * g + b


def _outproj_kernel(attn_ref, ret_ref, x_ref, gain_ref, wa_ref, wr_ref, g1_ref, b1_ref,
                    wrt_ref, brt_ref, before_ref,
                    h_ref, eid_ref, rank_ref, cw_ref, cnt_ref, carry_ref):
    first = jnp.logical_and(pl.program_id(0) == 0, pl.program_id(1) == 0)

    @pl.when(first)
    def _():
        carry_ref[...] = jnp.zeros(carry_ref.shape, F32)

    a = attn_ref[0]
    a = a * lax.rsqrt(jnp.mean(jnp.square(a), axis=0, keepdims=True) + LN_EPS) * gain_ref[...]
    mix = (lax.dot_general(a.astype(BF16), wa_ref[...], _TN, preferred_element_type=F32)
           + lax.dot_general(ret_ref[0], wr_ref[...], _TN, preferred_element_type=F32))
    h = _layer_norm_rows(ALPHA * x_ref[0] + mix, g1_ref[...], b1_ref[...])
    h_ref[...] = h

    R = ROUTER_ROWS
    h_hi = h.astype(BF16)
    h_lo = (h - h_hi.astype(F32)).astype(BF16)
    by_hi = lax.dot_general(wrt_ref[...], h_hi, _NT, preferred_element_type=F32)
    by_lo = lax.dot_general(wrt_ref[0:R, :], h_lo, _NT, preferred_element_type=F32)
    logits = by_hi[0:R] + by_hi[R:2 * R] + by_lo + brt_ref[...]
    T = logits.shape[1]
    gl = logits[0:N_GROUPS]
    gmax = jnp.max(gl, axis=0, keepdims=True)
    grow = lax.broadcasted_iota(jnp.int32, gl.shape, 0)
    gidx = jnp.min(jnp.where(gl == gmax, grow, N_GROUPS), axis=0, keepdims=True)
    g_p = 1.0 / jnp.sum(jnp.exp(gl - gmax), axis=0, keepdims=True)
    el = jnp.zeros((EXPERTS_PER_GROUP, T), F32)
    for g in range(N_GROUPS):
        r0 = ROUTER_EXPERT_ROW0 + g * EXPERTS_PER_GROUP
        el = jnp.where(gidx == g, logits[r0:r0 + EXPERTS_PER_GROUP], el)
    erow = lax.broadcasted_iota(jnp.int32, el.shape, 0)
    e1 = jnp.max(el, axis=0, keepdims=True)
    i1 = jnp.min(jnp.where(el == e1, erow, EXPERTS_PER_GROUP), axis=0, keepdims=True)
    el2 = jnp.where(erow == i1, -jnp.inf, el)
    e2 = jnp.max(el2, axis=0, keepdims=True)
    i2 = jnp.min(jnp.where(el2 == e2, erow, EXPERTS_PER_GROUP), axis=0, keepdims=True)
    r = jnp.exp(e2 - e1)
    w1 = g_p / (1.0 + r)
    w2 = g_p * r / (1.0 + r)
    id1 = gidx * EXPERTS_PER_GROUP + i1
    id2 = gidx * EXPERTS_PER_GROUP + i2
    eid_ref[0:1, :] = id1
    eid_ref[1:2, :] = id2
    cw_ref[0:1, :] = w1
    cw_ref[1:2, :] = w2

    xrow = lax.broadcasted_iota(jnp.int32, (N_EXPERTS, T), 0)
    oh1 = (xrow == id1).astype(F32)
    oh2 = (xrow == id2).astype(F32)
    both = oh1 + oh2
    seen = jnp.dot(both.astype(BF16), before_ref[...], preferred_element_type=F32) + carry_ref[...]
    rank_ref[0:1, :] = jnp.sum(oh1 * seen, axis=0, keepdims=True).astype(jnp.int32)
    rank_ref[1:2, :] = jnp.sum(oh2 * seen, axis=0, keepdims=True).astype(jnp.int32)
    carry = carry_ref[...] + jnp.sum(both, axis=1, keepdims=True)
    carry_ref[...] = carry
    cnt_ref[...] = carry.astype(jnp.int32)


def _outproj(attn_t, ret_t, x, gain, w_out, ln_g, ln_b, w_rg, b_rg, w_re, b_re):
    B, S, D = x.shape
    Tt = OUTPROJ_TOKENS
    n_t = S // Tt
    T = B * S
    wa = w_out[:ATTN_WIDTH].astype(BF16)
    wr = w_out[ATTN_WIDTH:].astype(BF16)
    e0, e1 = ROUTER_EXPERT_ROW0, ROUTER_EXPERT_ROW0 + N_EXPERTS
    wrt = jnp.zeros((ROUTER_ROWS, D), F32)
    wrt = wrt.at[:N_GROUPS].set(w_rg.T).at[e0:e1].set(w_re.T)
    wrt_hi = wrt.astype(BF16)
    wrt_lo = (wrt - wrt_hi.astype(F32)).astype(BF16)
    wrt = jnp.concatenate([wrt_hi, wrt_lo], axis=0)
    brt = jnp.zeros((ROUTER_ROWS, 1), F32)
    brt = brt.at[:N_GROUPS, 0].set(b_rg.astype(F32)).at[e0:e1, 0].set(b_re.astype(F32))
    before = (jnp.arange(Tt)[:, None] < jnp.arange(Tt)[None, :]).astype(BF16)
    const = lambda b, i: (0, 0)
    tok = lambda b, i: (0, b * n_t + i)
    return pl.pallas_call(
        _outproj_kernel,
        grid=(B, n_t),
        in_specs=[
            pl.BlockSpec((1, ATTN_WIDTH, Tt), lambda b, i: (b, 0, i)),
            pl.BlockSpec((1, RET_WIDTH, Tt), lambda b, i: (b, 0, i)),
            pl.BlockSpec((1, Tt, D), lambda b, i: (b, i, 0)),
            pl.BlockSpec((ATTN_WIDTH, 1), const),
            pl.BlockSpec((ATTN_WIDTH, D), const),
            pl.BlockSpec((RET_WIDTH, D), const),
            pl.BlockSpec((1, D), const),
            pl.BlockSpec((1, D), const),
            pl.BlockSpec((2 * ROUTER_ROWS, D), const),
            pl.BlockSpec((ROUTER_ROWS, 1), const),
            pl.BlockSpec((Tt, Tt), const),
        ],
        out_specs=[
            pl.BlockSpec((Tt, D), lambda b, i: (b * n_t + i, 0)),
            pl.BlockSpec((2, Tt), tok),
            pl.BlockSpec((2, Tt), tok),
            pl.BlockSpec((2, Tt), tok),
            pl.BlockSpec((N_EXPERTS, 1), const),
        ],
        out_shape=[
            jax.ShapeDtypeStruct((T, D), F32),
            jax.ShapeDtypeStruct((2, T), jnp.int32),
            jax.ShapeDtypeStruct((2, T), jnp.int32),
            jax.ShapeDtypeStruct((2, T), F32),
            jax.ShapeDtypeStruct((N_EXPERTS, 1), jnp.int32),
        ],
        scratch_shapes=[pltpu.VMEM((N_EXPERTS, 1), F32)],
        compiler_params=_params("arbitrary", "arbitrary"),
        name="outproj_ln_router",
    )(attn_t, ret_t, x, gain.reshape(ATTN_WIDTH, 1), wa, wr, ln_g.reshape(1, D), ln_b.reshape(1, D),
      wrt, brt, before)


def _dispatch_kernel(start_ref, end_ref, pos0_ref, pos1_ref, h_ref, xs_hbm, zero_ref, sem, zsem):
    G = h_ref.shape[0]
    Tm = MOE_ROWS

    @pl.when(pl.program_id(0) == 0)
    def _():
        zero_ref[...] = jnp.zeros(zero_ref.shape, F32)

        def tile_clear(row0):
            row0 = pl.multiple_of(row0, Tm)
            return pltpu.make_async_copy(zero_ref, xs_hbm.at[pl.ds(row0, Tm)], zsem)

        used = end_ref[N_EXPERTS - 1]
        n_rows = xs_hbm.shape[0]
        clears = [(end_ref[e] - Tm, end_ref[e] > start_ref[e]) for e in range(N_EXPERTS)]
        clears += [(jnp.minimum(used + k * Tm, n_rows - Tm), used + k * Tm < n_rows) for k in range(N_EXPERTS)]
        for row0, cond in clears:
            @pl.when(cond)
            def _():
                tile_clear(row0).start()
        for row0, cond in clears:
            @pl.when(cond)
            def _():
                tile_clear(row0).wait()

    def issue(k, carry):
        t0 = pl.multiple_of(k * ROW_DMA_UNROLL, ROW_DMA_UNROLL)
        for u in range(ROW_DMA_UNROLL):
            for pos_ref in (pos0_ref, pos1_ref):
                pltpu.make_async_copy(h_ref.at[pl.ds(t0 + u, 1)], xs_hbm.at[pl.ds(pos_ref[t0 + u], 1)], sem).start()
        return carry

    lax.fori_loop(0, G // ROW_DMA_UNROLL, issue, 0)
    for _ in range(2):
        pltpu.make_async_copy(h_ref, xs_hbm.at[pl.ds(0, G)], sem).wait()


def _dispatch(h, pos0, pos1, starts, ends, n_rows):
    T, D = h.shape
    G = DISPATCH_TOKENS if T % DISPATCH_TOKENS == 0 else ROUTE_TOKENS
    smem_blk = pl.BlockSpec((G,), lambda i, st, en: (i,), memory_space=pltpu.SMEM)
    grid_spec = pltpu.PrefetchScalarGridSpec(
        num_scalar_prefetch=2,
        grid=(T // G,),
        in_specs=[smem_blk, smem_blk, pl.BlockSpec((G, D), lambda i, st, en: (i, 0))],
        out_specs=pl.BlockSpec(memory_space=pl.ANY),
        scratch_shapes=[pltpu.VMEM((MOE_ROWS, D), F32), pltpu.SemaphoreType.DMA(()),
                        pltpu.SemaphoreType.DMA(())],
    )
    return pl.pallas_call(
        _dispatch_kernel,
        grid_spec=grid_spec,
        out_shape=jax.ShapeDtypeStruct((n_rows, D), F32),
        compiler_params=_params("arbitrary"),
        name="moe_dispatch",
    )(starts, ends, pos0, pos1, h)


def _experts_kernel(te_ref, nv_ref, next_ref, slot_ref, xs_ref, wgu_hbm, wdn_hbm, y_ref,
                    wgu_f, wdn_f, wgu_b, wdn_b, sems):
    Tm = MOE_ROWS

    def weight_copies(e, slot):
        return (pltpu.make_async_copy(wgu_hbm.at[e], wgu_f.at[slot], sems.at[slot]),
                pltpu.make_async_copy(wdn_hbm.at[e], wdn_f.at[slot], sems.at[slot]))

    for part in range(EXPERT_TILES_PER_STEP):
        tile = pl.program_id(0) * EXPERT_TILES_PER_STEP + part
        rows = slice(part * Tm, (part + 1) * Tm)
        live = tile < nv_ref[0]
        expert = te_ref[tile]
        new_expert = jnp.logical_or(tile == 0, expert != te_ref[jnp.maximum(tile - 1, 0)])

        @pl.when(jnp.logical_and(live, new_expert))
        def _():
            slot = slot_ref[expert]

            @pl.when(tile == 0)
            def _():
                for copy in weight_copies(expert, slot):
                    copy.start()

            for copy in weight_copies(expert, slot):
                copy.wait()
            wgu_b[...] = wgu_f[slot].astype(BF16)
            wdn_b[...] = wdn_f[slot].astype(BF16)
            following = next_ref[expert]

            @pl.when(following >= 0)
            def _():
                for copy in weight_copies(following, 1 - slot):
                    copy.start()

        @pl.when(live)
        def _():
            gu = jnp.dot(xs_ref[rows, :].astype(BF16), wgu_b[...], preferred_element_type=F32)
            gate, up = gu[:, :D_EXPERT], gu[:, D_EXPERT:]
            hid = gate * jax.nn.sigmoid(gate) * up
            y_ref[rows, :] = jnp.dot(hid.astype(BF16), wdn_b[...], preferred_element_type=F32)

        @pl.when(jnp.logical_not(live))
        def _():
            y_ref[rows, :] = jnp.zeros((Tm, y_ref.shape[1]), F32)


def _experts(xs, w_gu, w_dn, tile_expert, n_valid, has_rows):
    n_rows, D = xs.shape
    Tm = MOE_ROWS
    ids = jnp.arange(N_EXPERTS, dtype=jnp.int32)
    later = jnp.logical_and(has_rows[None, :], ids[None, :] > ids[:, None])
    next_expert = jnp.where(jnp.any(later, axis=1), jnp.argmax(later, axis=1), -1).astype(jnp.int32)
    slot = ((jnp.cumsum(has_rows.astype(jnp.int32)) - has_rows.astype(jnp.int32)) % 2).astype(jnp.int32)
    per_step = EXPERT_TILES_PER_STEP
    assert (n_rows // Tm) % per_step == 0
    grid_spec = pltpu.PrefetchScalarGridSpec(
        num_scalar_prefetch=4,
        grid=(n_rows // (per_step * Tm),),
        in_specs=[
            pl.BlockSpec((per_step * Tm, D),
                         lambda i, te, nv, nx, sl: (jnp.minimum(i, jnp.maximum(nv[0] - 1, 0) // per_step), 0)),
            pl.BlockSpec(memory_space=pl.ANY),
            pl.BlockSpec(memory_space=pl.ANY),
        ],
        out_specs=pl.BlockSpec((per_step * Tm, D), lambda i, te, nv, nx, sl: (i, 0)),
        scratch_shapes=[
            pltpu.VMEM((2, D, 2 * D_EXPERT), F32), pltpu.VMEM((2, D_EXPERT, D), F32),
            pltpu.VMEM((D, 2 * D_EXPERT), BF16), pltpu.VMEM((D_EXPERT, D), BF16),
            pltpu.SemaphoreType.DMA((2,)),
        ],
    )
    return pl.pallas_call(
        _experts_kernel,
        grid_spec=grid_spec,
        out_shape=jax.ShapeDtypeStruct((n_rows, D), F32),
        compiler_params=_params("arbitrary"),
        name="moe_experts",
    )(tile_expert, n_valid, next_expert, slot, xs, w_gu, w_dn)


def _combine_kernel(pos0_ref, pos1_ref, next0_ref, next1_ref, y_hbm, h_ref, cw_ref, g_ref, b_ref, o_ref,
                    ybuf_a, ybuf_b, sem_a, sem_b):
    G = ROUTE_TOKENS
    C = ROW_DMA_UNROLL
    i = pl.program_id(0)
    last = pl.num_programs(0) - 1

    def gather(pos_refs, buf, sem, t0):
        for u in range(C):
            for slot, pos_ref in enumerate(pos_refs):
                pltpu.make_async_copy(y_hbm.at[pl.ds(pos_ref[t0 + u], 1)], buf.at[slot, pl.ds(t0 + u, 1)],
                                      sem).start()

    def wait_tile(buf, sem):
        for slot in range(2):
            pltpu.make_async_copy(y_hbm.at[pl.ds(0, G)], buf.at[slot], sem).wait()

    @pl.when(i == 0)
    def _():
        def first_tile(k, carry):
            gather((pos0_ref, pos1_ref), ybuf_a, sem_a, pl.multiple_of(k * C, C))
            return carry
        lax.fori_loop(0, G // C, first_tile, 0)

    def run(buf, sem, spare, spare_sem):
        wait_tile(buf, sem)

        @pl.when(i < last)
        def _():
            def next_tile(k, carry):
                gather((next0_ref, next1_ref), spare, spare_sem, pl.multiple_of(k * C, C))
                return carry
            lax.fori_loop(0, G // C, next_tile, 0)

        for r in range(G // COMBINE_CHUNK):
            rows = slice(r * COMBINE_CHUNK, (r + 1) * COMBINE_CHUNK)
            cw = cw_ref[rows, :]
            ffn = buf[0, rows, :] * cw[:, 0:1] + buf[1, rows, :] * cw[:, 1:2]
            o_ref[rows, :] = _layer_norm_rows(ALPHA * h_ref[rows, :] + ffn, g_ref[...], b_ref[...])

    odd = jnp.bitwise_and(i, 1)

    @pl.when(odd == 0)
    def _():
        run(ybuf_a, sem_a, ybuf_b, sem_b)

    @pl.when(odd == 1)
    def _():
        run(ybuf_b, sem_b, ybuf_a, sem_a)


def _combine(y, h, pos0, pos1, cw_rows, ln_g, ln_b):
    T, D = h.shape
    G = ROUTE_TOKENS
    n = T // G
    smem_blk = pl.BlockSpec((G,), lambda i: (i,), memory_space=pltpu.SMEM)
    smem_next = pl.BlockSpec((G,), lambda i: (jnp.minimum(i + 1, n - 1),), memory_space=pltpu.SMEM)
    return pl.pallas_call(
        _combine_kernel,
        grid=(n,),
        in_specs=[
            smem_blk, smem_blk, smem_next, smem_next,
            pl.BlockSpec(memory_space=pl.ANY),
            pl.BlockSpec((G, D), lambda i: (i, 0)),
            pl.BlockSpec((G, 2), lambda i: (i, 0)),
            pl.BlockSpec((1, D), lambda i: (0, 0)),
            pl.BlockSpec((1, D), lambda i: (0, 0)),
        ],
        out_specs=pl.BlockSpec((G, D), lambda i: (i, 0)),
        scratch_shapes=[pltpu.VMEM((2, G, D), F32), pltpu.VMEM((2, G, D), F32),
                        pltpu.SemaphoreType.DMA(()), pltpu.SemaphoreType.DMA(())],
        out_shape=jax.ShapeDtypeStruct((T, D), F32),
        compiler_params=_params("arbitrary"),
        name="moe_combine_ln",
    )(pos0, pos1, pos0, pos1, y, h, cw_rows, ln_g.reshape(1, D), ln_b.reshape(1, D))


def kernel(x, w_in, attn_out_gain, rel_bias_table, w_out, ln1_g, ln1_b, w_router_group,
           b_router_group, w_router_expert, b_router_expert, w_gate_up, w_down, ln2_g, ln2_b):
    B, S, D = x.shape
    assert D == D_MODEL and S % MOBA_BLOCK == 0 and S % INPROJ_TOKENS == 0
    assert w_in.shape[0] == DEPTH
    T = B * S
    h = x
    for l in range(DEPTH):
        proj_t, k_nat = _inproj(h, w_in[l])
        attn_t = _moba(proj_t, k_nat, rel_bias_table)
        ret_t = _retention(proj_t)
        h1, eid, rank, cw, counts = _outproj(
            attn_t, ret_t, h, attn_out_gain[l], w_out[l], ln1_g[l], ln1_b[l],
            w_router_group[l], b_router_group[l], w_router_expert[l], b_router_expert[l])

        Tm = MOE_ROWS
        n_tiles = (2 * T) // Tm + N_EXPERTS
        padded = ((counts[:, 0] + Tm - 1) // Tm) * Tm
        ends = jnp.cumsum(padded)
        starts = (ends - padded).astype(jnp.int32)
        tile_row0 = jnp.arange(n_tiles, dtype=jnp.int32) * Tm
        tile_expert = jnp.sum((ends[None, :] <= jnp.minimum(tile_row0, ends[-1] - Tm)[:, None]).astype(jnp.int32),
                              axis=1)
        n_valid = (ends[-1:] // Tm).astype(jnp.int32)

        expert_ids = jnp.arange(N_EXPERTS, dtype=jnp.int32)[:, None, None]
        pos = jnp.sum(jnp.where(eid[None] == expert_ids, starts[:, None, None], 0), axis=0) + rank

        xs = _dispatch(h1, pos[0], pos[1], starts, ends.astype(jnp.int32), n_tiles * Tm)
        y = _experts(xs, w_gate_up[l], w_down[l], tile_expert, n_valid, padded > 0)
        h2 = _combine(y, h1, pos[0], pos[1], cw.T, ln2_g[l], ln2_b[l])
        h = h2.reshape(B, S, D)
    return h
```

```python
import functools
import math

import numpy as np
import jax
import jax.numpy as jnp
from jax import lax
from jax.experimental import pallas as pl
from jax.experimental.pallas import tpu as pltpu

F32 = jnp.float32
BF16 = jnp.bfloat16

D_MODEL = 1024
ATTN_HEADS = 8
ATTN_HEAD_DIM = 64
ATTN_WIDTH = ATTN_HEADS * ATTN_HEAD_DIM
RET_HEADS = 4
RET_HEAD_DIM = 128
RET_WIDTH = RET_HEADS * RET_HEAD_DIM
N_IN_SLICES = 7
PROJT_SLICES = N_IN_SLICES - 1
HEAD_PAIR = 2 * ATTN_HEAD_DIM
BF16_SUBLANES = 16
F32_SUBLANES = 8
V_ROWS = ATTN_HEAD_DIM + BF16_SUBLANES
PROJT_V_ROW0 = (PROJT_SLICES - 1) * ATTN_WIDTH
PROJT_WIDTH = PROJT_V_ROW0 + ATTN_HEADS * V_ROWS
MOBA_BLOCK = 256
MOBA_TOPK = 3
RET_CHUNK = 256
REL_BUCKETS = 32
REL_MAX_DIST = 128
ROPE_BASE = 10000.0
N_GROUPS = 4
EXPERTS_PER_GROUP = 8
N_EXPERTS = N_GROUPS * EXPERTS_PER_GROUP
D_EXPERT = D_MODEL // 2
DEPTH = 1
ALPHA = (2.0 * DEPTH) ** 0.25
LN_EPS = 1e-5

LOG2E = math.log2(math.e)
Q_SCALE = ATTN_HEAD_DIM ** -0.5 * LOG2E

MASKED = -1e30

INPROJ_TOKENS = 1024
INPROJ_ROWS = ATTN_WIDTH
OUTPROJ_TOKENS = 1024
MOBA_UNROLL = 2
MOBA_HEADS = 4
MOBA_QTILES = 4
MOE_ROWS = 512
EXPERT_TILES_PER_STEP = 2
DISPATCH_TOKENS = 2048
ROUTE_TOKENS = 1024
COMBINE_TOKENS_PER_ITER = 128
ROW_DMA_UNROLL = 32
ROUTER_EXPERT_ROW0 = 8
ROUTER_ROWS = -(-(ROUTER_EXPERT_ROW0 + N_EXPERTS) // BF16_SUBLANES) * BF16_SUBLANES

V7X_VMEM_BYTES = 64 * 1024 * 1024
VMEM_LIMIT = V7X_VMEM_BYTES * 3 // 4

_TN = (((0,), (0,)), ((), ()))
_NT = (((1,), (1,)), ((), ()))


def _params(*sem):
    return pltpu.CompilerParams(dimension_semantics=sem, vmem_limit_bytes=VMEM_LIMIT)


def _inproj_kernel(x_ref, wt_ref, wk_ref, o_ref, k_ref):
    xb = x_ref[0].astype(BF16)
    n_tok = xb.shape[0]
    ones_row = (lax.broadcasted_iota(jnp.int32, (BF16_SUBLANES, n_tok), 0) == 0).astype(BF16)
    for c in range(PROJT_SLICES):
        rows = slice(c * INPROJ_ROWS, (c + 1) * INPROJ_ROWS)
        acc = lax.dot_general(wt_ref[rows, :], xb, _NT, preferred_element_type=F32)
        if c == 0:
            acc = acc * Q_SCALE
        if c < PROJT_SLICES - 1:
            o_ref[0, rows, :] = acc.astype(BF16)
        else:
            for h in range(ATTN_HEADS):
                r0 = PROJT_V_ROW0 + h * V_ROWS
                o_ref[0, r0:r0 + ATTN_HEAD_DIM, :] = acc[h * ATTN_HEAD_DIM:(h + 1) * ATTN_HEAD_DIM].astype(BF16)
                o_ref[0, r0 + ATTN_HEAD_DIM:r0 + V_ROWS, :] = ones_row
    k_ref[0] = jnp.dot(xb, wk_ref[...], preferred_element_type=F32).astype(BF16)


def _inproj(x, w_in):
    B, S, D = x.shape
    W = ATTN_WIDTH
    w_t = jnp.concatenate([w_in[:, :W], w_in[:, 3 * W:], w_in[:, 2 * W:3 * W]], axis=1).T.astype(BF16)
    w_k = w_in[:, W:2 * W].astype(BF16)
    return pl.pallas_call(
        _inproj_kernel,
        grid=(B, S // INPROJ_TOKENS),
        in_specs=[
            pl.BlockSpec((1, INPROJ_TOKENS, D), lambda b, i: (b, i, 0)),
            pl.BlockSpec((PROJT_SLICES * W, D), lambda b, i: (0, 0)),
            pl.BlockSpec((D, W), lambda b, i: (0, 0)),
        ],
        out_specs=[
            pl.BlockSpec((1, PROJT_WIDTH, INPROJ_TOKENS), lambda b, i: (b, 0, i)),
            pl.BlockSpec((1, INPROJ_TOKENS, W), lambda b, i: (b, i, 0)),
        ],
        out_shape=[
            jax.ShapeDtypeStruct((B, PROJT_WIDTH, S), BF16),
            jax.ShapeDtypeStruct((B, S, W), BF16),
        ],
        compiler_params=_params("arbitrary", "arbitrary"),
        name="inproj",
    )(x, w_t, w_k)


def _moba_kernel(far_ref, q_ref, k_ref, v_ref, bias_rows_ref, o_ref,
                 bias_ref, kmean_ref, qz_ref, sel_ref, farsel_ref,
                 s0_ref, smax0_ref, p0_ref, alpha0_ref, s1_ref, smax1_ref, p1_ref, alpha1_ref,
                 m_ref, acc_ref, *, n_blocks):
    L = MOBA_BLOCK
    Dh = ATTN_HEAD_DIM
    HG = MOBA_HEADS
    group = pl.program_id(1)
    qi0 = pl.program_id(2) * MOBA_QTILES
    last_blk = n_blocks - 1
    units = [(a, h) for a in range(MOBA_QTILES) for h in range(HG)]
    s_ref, smax_ref = (s0_ref, s1_ref), (smax0_ref, smax1_ref)
    p_ref, alpha_ref = (p0_ref, p1_ref), (alpha0_ref, alpha1_ref)

    @pl.when(qi0 == 0)
    def _():
        chunk_blocks = min(8, n_blocks)
        chunk = chunk_blocks * L
        acc = jnp.zeros((n_blocks, HG * Dh), F32)
        for c in range(n_blocks // chunk_blocks):
            blk = lax.broadcasted_iota(jnp.int32, (n_blocks, chunk), 0)
            key = lax.broadcasted_iota(jnp.int32, (n_blocks, chunk), 1)
            ind = jnp.where(blk == c * chunk_blocks + key // L, 1.0 / L, 0.0).astype(BF16)
            acc = acc + jnp.dot(ind, k_ref[0, c * chunk:(c + 1) * chunk, :], preferred_element_type=F32)
        for kind in range(2):
            for h in range(HG):
                row = jnp.broadcast_to(bias_rows_ref[kind, h], (L, 2 * L))
                bias_ref[kind, h] = pltpu.roll(row, 0, 1, stride=1, stride_axis=0)[:, 0:L]
        hi = acc.astype(BF16)
        lo = (acc - hi.astype(F32)).astype(BF16)
        for pair in range(HG // 2):
            cols = slice(pair * HEAD_PAIR, (pair + 1) * HEAD_PAIR)
            kmean_ref[pair, 0:n_blocks, :] = hi[:, cols]
            kmean_ref[pair, n_blocks:2 * n_blocks, :] = lo[:, cols]

    blk = lax.broadcasted_iota(jnp.int32, (n_blocks, L), 0)
    for u, (a, h) in enumerate(units):
        qi = qi0 + a
        pair, hd = divmod(h, 2)
        past = blk < qi
        qh = q_ref[0, h * Dh:(h + 1) * Dh, a * L:(a + 1) * L]
        no_q = jnp.zeros_like(qh)
        qz = jnp.concatenate([qh, no_q] if hd == 0 else [no_q, qh], axis=0)
        qz_ref[u] = qz
        gate = jnp.dot(kmean_ref[pair], qz, preferred_element_type=F32)
        gate = gate[0:n_blocks] + gate[n_blocks:2 * n_blocks]
        gate = jnp.where(past, gate, -jnp.inf)
        for _ in range(MOBA_TOPK):
            top = jnp.max(gate, axis=0, keepdims=True)
            first = jnp.min(jnp.where(gate == top, blk, n_blocks), axis=0, keepdims=True)
            gate = jnp.where(blk == first, -jnp.inf, gate)
        knocked_out = gate == -jnp.inf
        sel_ref[u] = jnp.where(knocked_out, jnp.where(past, 0.0, MASKED), MASKED)
        far_row = jnp.where(blk < qi - 1, far_ref[group * HG + h], MASKED)
        farsel_ref[u, 0:n_blocks, :] = jnp.where(knocked_out, far_row, MASKED)
        farsel_ref[u, n_blocks:n_blocks + F32_SUBLANES, :] = jnp.zeros((F32_SUBLANES, L), F32)

    def stage_scores(u, j, tile, slot):
        pair = units[u][1] // 2
        kb = k_ref[0, pl.ds(pl.multiple_of(j * L, L), L), pair * HEAD_PAIR:(pair + 1) * HEAD_PAIR]
        s = jnp.dot(kb, qz_ref[u], preferred_element_type=F32)
        if tile is not None:
            s = s + tile
        s_ref[slot][u] = s
        smax_ref[slot][u] = jnp.max(s, axis=0, keepdims=True)

    def stage_softmax(u, row, slot):
        m_prev = m_ref[u]
        m_new = jnp.maximum(m_prev, smax_ref[slot][u] + row)
        alpha_ref[slot][u] = jnp.exp2(m_prev - m_new)
        p_ref[slot][u] = jnp.exp2(s_ref[slot][u] - (m_new - row)).astype(BF16)
        m_ref[u] = m_new

    def stage_values(u, j, slot):
        h = units[u][1]
        vb = v_ref[0, h * V_ROWS:(h + 1) * V_ROWS, pl.ds(pl.multiple_of(j * L, L), L)]
        pv = jnp.dot(vb, p_ref[slot][u], preferred_element_type=F32)
        acc_ref[u] = alpha_ref[slot][u] * acc_ref[u] + pv

    m_ref[...] = jnp.full(m_ref.shape, MASKED, F32)
    acc_ref[...] = jnp.zeros(acc_ref.shape, F32)

    zero_row = jnp.zeros((1, L), F32)
    own = [qi0 + a for a, _ in units]
    prev = [jnp.maximum(qi0 + a - 1, 0) for a, _ in units]
    for u, (a, h) in enumerate(units):
        stage_scores(u, own[u], bias_ref[0, h], 0)
        stage_scores(u, prev[u], bias_ref[1, h] + sel_ref[u, pl.ds(prev[u], 1), :], 1)
    for u in range(len(units)):
        stage_softmax(u, zero_row, 0)
    everyone = range(len(units))

    def step(t, slot, score_units=everyone, softmax_units=everyone, value_units=everyone):
        far_value = jnp.minimum(t - 2, last_blk)
        j = jnp.minimum(t, last_blk)
        jr = jnp.where(t == 0, n_blocks, jnp.minimum(t - 1, last_blk))
        for u in everyone:
            if u in score_units:
                stage_scores(u, j, None, slot)
            if u in softmax_units:
                stage_softmax(u, farsel_ref[u, pl.ds(jr, 1), :], 1 - slot)
            if u in value_units:
                stage_values(u, jnp.where(t == 0, own[u], jnp.where(t == 1, prev[u], far_value)), slot)

    def run_steps(trips, unroll):
        def unrolled_steps(k, carry):
            for s in range(unroll):
                step(unroll * k + s, s % 2)
            return carry
        lax.fori_loop(0, trips, unrolled_steps, 0)

    QT = MOBA_QTILES

    @pl.when(qi0 < QT)
    def _():
        n_steps = 2 + jnp.maximum(qi0 + QT - 2, 0)
        run_steps((n_steps + 1) // 2, 2)

    @pl.when(qi0 >= QT)
    def _():
        n_main = qi0 - 2
        left = (MOBA_UNROLL - 2) % MOBA_UNROLL
        run_steps((n_main - left) // MOBA_UNROLL, MOBA_UNROLL)
        for k in range(-left, QT + 2):
            with_block = lambda first: [u for u, (a, _) in enumerate(units) if a >= first]
            step(n_main + k, k % 2, with_block(k), with_block(k - 1), with_block(k - 2))
    for u, (a, h) in enumerate(units):
        o_ref[0, h * Dh:(h + 1) * Dh, a * L:(a + 1) * L] = acc_ref[u, 0:Dh, :] / acc_ref[u, Dh:Dh + 1, :]


def _t5_bucket_np(dist):
    n = np.maximum(dist, 0)
    max_exact = REL_BUCKETS // 2
    nf = np.maximum(n, 1).astype(np.float32)
    large = max_exact + (np.log(nf / max_exact) / math.log(REL_MAX_DIST / max_exact)
                         * (REL_BUCKETS - max_exact)).astype(np.int32)
    large = np.minimum(large, REL_BUCKETS - 1)
    return np.where(n < max_exact, n, large)


def _moba(proj_t, k_nat, rel_table):
    B, _, S = proj_t.shape
    L = MOBA_BLOCK
    n_blocks = S // L
    H = ATTN_HEADS
    table_t = rel_table.astype(F32).T
    wrapped = np.concatenate([np.arange(L), np.arange(-L, 0)])
    dist = np.stack([wrapped, wrapped + L])
    onehot = jnp.asarray(_t5_bucket_np(dist))[:, None, None, :] == jnp.arange(REL_BUCKETS)[None, None, :, None]
    bias_rows = jnp.sum(jnp.where(onehot, table_t[None, :, :, None], 0.0), axis=2) * LOG2E
    bias_rows = jnp.where(jnp.asarray(dist >= 0)[:, None, :], bias_rows, MASKED)[:, :, None, :]
    assert int(_t5_bucket_np(np.array(L + 1))) == REL_BUCKETS - 1
    far = table_t[:, REL_BUCKETS - 1] * LOG2E

    HG, QT = MOBA_HEADS, MOBA_QTILES
    U = HG * QT
    q_rows, v_rows = HG * ATTN_HEAD_DIM, HG * V_ROWS
    assert ATTN_HEADS % HG == 0 and HG % 2 == 0 and n_blocks % QT == 0 and PROJT_V_ROW0 % v_rows == 0
    assert MOBA_UNROLL % 2 == 0 and QT % MOBA_UNROLL == 0
    once = dict(pipeline_mode=pl.Buffered(1))
    grid_spec = pltpu.PrefetchScalarGridSpec(
        num_scalar_prefetch=1,
        grid=(B, ATTN_HEADS // HG, n_blocks // QT),
        in_specs=[
            pl.BlockSpec((1, q_rows, QT * L), lambda b, g, i, far: (b, g, i)),
            pl.BlockSpec((1, S, q_rows), lambda b, g, i, far: (b, 0, g), **once),
            pl.BlockSpec((1, v_rows, S), lambda b, g, i, far: (b, PROJT_V_ROW0 // v_rows + g, 0), **once),
            pl.BlockSpec((2, HG, 1, 2 * L), lambda b, g, i, far: (0, g, 0, 0)),
        ],
        out_specs=pl.BlockSpec((1, q_rows, QT * L), lambda b, g, i, far: (b, g, i)),
        scratch_shapes=[
            pltpu.VMEM((2, HG, L, L), F32),
            pltpu.VMEM((HG // 2, 2 * n_blocks, HEAD_PAIR), BF16),
            pltpu.VMEM((U, HEAD_PAIR, L), BF16),
            pltpu.VMEM((U, n_blocks, L), F32),
            pltpu.VMEM((U, n_blocks + F32_SUBLANES, L), F32),
            *([pltpu.VMEM((U, L, L), F32), pltpu.VMEM((U, 1, L), F32),
               pltpu.VMEM((U, L, L), BF16), pltpu.VMEM((U, 1, L), F32)] * 2),
            pltpu.VMEM((U, 1, L), F32),
            pltpu.VMEM((U, V_ROWS, L), F32),
        ],
    )
    return pl.pallas_call(
        functools.partial(_moba_kernel, n_blocks=n_blocks),
        grid_spec=grid_spec,
        out_shape=jax.ShapeDtypeStruct((B, ATTN_WIDTH, S), F32),
        compiler_params=_params("arbitrary", "arbitrary", "arbitrary"),
        name="moba",
    )(far, proj_t, k_nat, proj_t, bias_rows)


def _retention_kernel(cdec_ref, q_ref, k_ref, v_ref, g_ref, cos_ref, sin_ref,
                      dec_ref, qdec_ref, kdec_ref, o_ref, state_ref):
    c = pl.program_id(0)

    @pl.when(c == 0)
    def _():
        state_ref[...] = jnp.zeros(state_ref.shape, F32)

    cos = cos_ref[...]
    sin = sin_ref[...]
    half = RET_HEAD_DIM // 2

    def rope(t):
        t1, t2 = t[:half], t[half:]
        return jnp.concatenate([t1 * cos - t2 * sin, t1 * sin + t2 * cos], axis=0)

    for b in range(q_ref.shape[0]):
        for hh in range(RET_HEADS):
            rows = slice(hh * RET_HEAD_DIM, (hh + 1) * RET_HEAD_DIM)
            q = rope(q_ref[b, rows, :].astype(F32))
            k = rope(k_ref[b, rows, :].astype(F32)) * (RET_HEAD_DIM ** -0.5)
            v = v_ref[b, rows, :]
            qb = q.astype(BF16)
            s = lax.dot_general(k.astype(BF16), qb, _TN, preferred_element_type=F32)
            s = (s * dec_ref[hh]).astype(BF16)
            inner = jnp.dot(v, s, preferred_element_type=F32)
            state = state_ref[b, hh]
            cross = jnp.dot(state.astype(BF16), (q * qdec_ref[hh:hh + 1, :]).astype(BF16),
                            preferred_element_type=F32)
            kd = (k * kdec_ref[hh:hh + 1, :]).astype(BF16)
            state_ref[b, hh] = state * cdec_ref[hh] + lax.dot_general(v, kd, _NT, preferred_element_type=F32)
            out = inner + cross
            mu = jnp.mean(out, axis=0, keepdims=True)
            var = jnp.mean(jnp.square(out - mu), axis=0, keepdims=True)
            out = (out - mu) * lax.rsqrt(var + LN_EPS)
            g = g_ref[b, rows, :].astype(F32)
            o_ref[b, rows, :] = (g * jax.nn.sigmoid(g) * out).astype(BF16)


def _retention(proj_t):
    B, _, S = proj_t.shape
    C = RET_CHUNK
    H = RET_HEADS
    half = RET_HEAD_DIM // 2
    inv = ROPE_BASE ** (-jnp.arange(half, dtype=F32) / half)
    ang = inv[:, None] * jnp.arange(S).astype(F32)[None, :]
    cos_t, sin_t = jnp.cos(ang), jnp.sin(ang)
    gammas = 1.0 - jnp.exp(jnp.linspace(math.log(1.0 / 32), math.log(1.0 / 512), H, dtype=F32))
    log_g = jnp.log(gammas)
    idx = jnp.arange(C, dtype=F32)
    diff = idx[None, :] - idx[:, None]
    dec_t = jnp.where(diff[None] >= 0, jnp.exp(jnp.maximum(diff, 0.0)[None] * log_g[:, None, None]), 0.0)
    q_dec = jnp.exp((idx[None, :] + 1.0) * log_g[:, None])
    k_dec = jnp.exp((C - 1.0 - idx[None, :]) * log_g[:, None])
    chunk_dec = jnp.exp(C * log_g)

    grid_spec = pltpu.PrefetchScalarGridSpec(
        num_scalar_prefetch=1,
        grid=(S // C,),
        in_specs=[
            pl.BlockSpec((B, RET_WIDTH, C), lambda c, cd: (0, 1, c)),
            pl.BlockSpec((B, RET_WIDTH, C), lambda c, cd: (0, 2, c)),
            pl.BlockSpec((B, RET_WIDTH, C), lambda c, cd: (0, 3, c)),
            pl.BlockSpec((B, RET_WIDTH, C), lambda c, cd: (0, 4, c)),
            pl.BlockSpec((half, C), lambda c, cd: (0, c)),
            pl.BlockSpec((half, C), lambda c, cd: (0, c)),
            pl.BlockSpec((H, C, C), lambda c, cd: (0, 0, 0)),
            pl.BlockSpec((H, C), lambda c, cd: (0, 0)),
            pl.BlockSpec((H, C), lambda c, cd: (0, 0)),
        ],
        out_specs=pl.BlockSpec((B, RET_WIDTH, C), lambda c, cd: (0, 0, c)),
        scratch_shapes=[pltpu.VMEM((B, H, RET_HEAD_DIM, RET_HEAD_DIM), F32)],
    )
    return pl.pallas_call(
        _retention_kernel,
        grid_spec=grid_spec,
        out_shape=jax.ShapeDtypeStruct((B, RET_WIDTH, S), BF16),
        compiler_params=_params("arbitrary"),
        name="retention",
    )(chunk_dec, proj_t, proj_t, proj_t, proj_t, cos_t, sin_t, dec_t, q_dec, k_dec)


def _layer_norm_rows(y, g, b):
    mu = jnp.mean(y, axis=-1, keepdims=True)
    var = jnp.mean(jnp.square(y - mu), axis=-1, keepdims=True)
    return (y - mu) * lax.rsqrt(var + LN_EPS) * g + b


def _outproj_kernel(attn_ref, ret_ref, x_ref, gain_ref, wa_ref, wr_ref, g1_ref, b1_ref,
                    wrt_ref, brt_ref, before_ref,
                    h_ref, eid_ref, rank_ref, cw_ref, cnt_ref, carry_ref):
    first = jnp.logical_and(pl.program_id(0) == 0, pl.program_id(1) == 0)

    @pl.when(first)
    def _():
        carry_ref[...] = jnp.zeros(carry_ref.shape, F32)

    a = attn_ref[0]
    a = a * lax.rsqrt(jnp.mean(jnp.square(a), axis=0, keepdims=True) + LN_EPS) * gain_ref[...]
    mix = (lax.dot_general(a.astype(BF16), wa_ref[...], _TN, preferred_element_type=F32)
           + lax.dot_general(ret_ref[0], wr_ref[...], _TN, preferred_element_type=F32))
    h = _layer_norm_rows(ALPHA * x_ref[0] + mix, g1_ref[...], b1_ref[...])
    h_ref[...] = h

    R = ROUTER_ROWS
    h_hi = h.astype(BF16)
    h_lo = (h - h_hi.astype(F32)).astype(BF16)
    by_hi = lax.dot_general(wrt_ref[...], h_hi, _NT, preferred_element_type=F32)
    by_lo = lax.dot_general(wrt_ref[0:R, :], h_lo, _NT, preferred_element_type=F32)
    logits = by_hi[0:R] + by_hi[R:2 * R] + by_lo + brt_ref[...]
    T = logits.shape[1]
    gl = logits[0:N_GROUPS]
    gmax = jnp.max(gl, axis=0, keepdims=True)
    grow = lax.broadcasted_iota(jnp.int32, gl.shape, 0)
    gidx = jnp.min(jnp.where(gl == gmax, grow, N_GROUPS), axis=0, keepdims=True)
    g_p = 1.0 / jnp.sum(jnp.exp(gl - gmax), axis=0, keepdims=True)
    el = jnp.zeros((EXPERTS_PER_GROUP, T), F32)
    for g in range(N_GROUPS):
        r0 = ROUTER_EXPERT_ROW0 + g * EXPERTS_PER_GROUP
        el = jnp.where(gidx == g, logits[r0:r0 + EXPERTS_PER_GROUP], el)
    erow = lax.broadcasted_iota(jnp.int32, el.shape, 0)
    e1 = jnp.max(el, axis=0, keepdims=True)
    i1 = jnp.min(jnp.where(el == e1, erow, EXPERTS_PER_GROUP), axis=0, keepdims=True)
    el2 = jnp.where(erow == i1, -jnp.inf, el)
    e2 = jnp.max(el2, axis=0, keepdims=True)
    i2 = jnp.min(jnp.where(el2 == e2, erow, EXPERTS_PER_GROUP), axis=0, keepdims=True)
    r = jnp.exp(e2 - e1)
    w1 = g_p / (1.0 + r)
    w2 = g_p * r / (1.0 + r)
    id1 = gidx * EXPERTS_PER_GROUP + i1
    id2 = gidx * EXPERTS_PER_GROUP + i2
    eid_ref[0:1, :] = id1
    eid_ref[1:2, :] = id2
    cw_ref[0:1, :] = w1
    cw_ref[1:2, :] = w2

    xrow = lax.broadcasted_iota(jnp.int32, (N_EXPERTS, T), 0)
    oh1 = (xrow == id1).astype(F32)
    oh2 = (xrow == id2).astype(F32)
    both = oh1 + oh2
    seen = jnp.dot(both.astype(BF16), before_ref[...], preferred_element_type=F32) + carry_ref[...]
    rank_ref[0:1, :] = jnp.sum(oh1 * seen, axis=0, keepdims=True).astype(jnp.int32)
    rank_ref[1:2, :] = jnp.sum(oh2 * seen, axis=0, keepdims=True).astype(jnp.int32)
    carry = carry_ref[...] + jnp.sum(both, axis=1, keepdims=True)
    carry_ref[...] = carry
    cnt_ref[...] = carry.astype(jnp.int32)


def _outproj(attn_t, ret_t, x, gain, w_out, ln_g, ln_b, w_rg, b_rg, w_re, b_re):
    B, S, D = x.shape
    Tt = OUTPROJ_TOKENS
    n_t = S // Tt
    T = B * S
    wa = w_out[:ATTN_WIDTH].astype(BF16)
    wr = w_out[ATTN_WIDTH:].astype(BF16)
    e0, e1 = ROUTER_EXPERT_ROW0, ROUTER_EXPERT_ROW0 + N_EXPERTS
    wrt = jnp.zeros((ROUTER_ROWS, D), F32)
    wrt = wrt.at[:N_GROUPS].set(w_rg.T).at[e0:e1].set(w_re.T)
    wrt_hi = wrt.astype(BF16)
    wrt_lo = (wrt - wrt_hi.astype(F32)).astype(BF16)
    wrt = jnp.concatenate([wrt_hi, wrt_lo], axis=0)
    brt = jnp.zeros((ROUTER_ROWS, 1), F32)
    brt = brt.at[:N_GROUPS, 0].set(b_rg.astype(F32)).at[e0:e1, 0].set(b_re.astype(F32))
    before = (jnp.arange(Tt)[:, None] < jnp.arange(Tt)[None, :]).astype(BF16)
    const = lambda b, i: (0, 0)
    tok = lambda b, i: (0, b * n_t + i)
    return pl.pallas_call(
        _outproj_kernel,
        grid=(B, n_t),
        in_specs=[
            pl.BlockSpec((1, ATTN_WIDTH, Tt), lambda b, i: (b, 0, i)),
            pl.BlockSpec((1, RET_WIDTH, Tt), lambda b, i: (b, 0, i)),
            pl.BlockSpec((1, Tt, D), lambda b, i: (b, i, 0)),
            pl.BlockSpec((ATTN_WIDTH, 1), const),
            pl.BlockSpec((ATTN_WIDTH, D), const),
            pl.BlockSpec((RET_WIDTH, D), const),
            pl.BlockSpec((1, D), const),
            pl.BlockSpec((1, D), const),
            pl.BlockSpec((2 * ROUTER_ROWS, D), const),
            pl.BlockSpec((ROUTER_ROWS, 1), const),
            pl.BlockSpec((Tt, Tt), const),
        ],
        out_specs=[
            pl.BlockSpec((Tt, D), lambda b, i: (b * n_t + i, 0)),
            pl.BlockSpec((2, Tt), tok),
            pl.BlockSpec((2, Tt), tok),
            pl.BlockSpec((2, Tt), tok),
            pl.BlockSpec((N_EXPERTS, 1), const),
        ],
        out_shape=[
            jax.ShapeDtypeStruct((T, D), F32),
            jax.ShapeDtypeStruct((2, T), jnp.int32),
            jax.ShapeDtypeStruct((2, T), jnp.int32),
            jax.ShapeDtypeStruct((2, T), F32),
            jax.ShapeDtypeStruct((N_EXPERTS, 1), jnp.int32),
        ],
        scratch_shapes=[pltpu.VMEM((N_EXPERTS, 1), F32)],
        compiler_params=_params("arbitrary", "arbitrary"),
        name="outproj_ln_router",
    )(attn_t, ret_t, x, gain.reshape(ATTN_WIDTH, 1), wa, wr, ln_g.reshape(1, D), ln_b.reshape(1, D),
      wrt, brt, before)


def _dispatch_kernel(start_ref, end_ref, pos0_ref, pos1_ref, h_ref, xs_hbm, zero_ref, sem, zsem):
    G = h_ref.shape[0]
    Tm = MOE_ROWS

    @pl.when(pl.program_id(0) == 0)
    def _():
        zero_ref[...] = jnp.zeros(zero_ref.shape, F32)

        def tile_clear(row0):
            row0 = pl.multiple_of(row0, Tm)
            return pltpu.make_async_copy(zero_ref, xs_hbm.at[pl.ds(row0, Tm)], zsem)

        used = end_ref[N_EXPERTS - 1]
        n_rows = xs_hbm.shape[0]
        clears = [(end_ref[e] - Tm, end_ref[e] > start_ref[e]) for e in range(N_EXPERTS)]
        clears += [(jnp.minimum(used + k * Tm, n_rows - Tm), used + k * Tm < n_rows) for k in range(N_EXPERTS)]
        for row0, cond in clears:
            @pl.when(cond)
            def _():
                tile_clear(row0).start()
        for row0, cond in clears:
            @pl.when(cond)
            def _():
                tile_clear(row0).wait()

    def issue(k, carry):
        t0 = pl.multiple_of(k * ROW_DMA_UNROLL, ROW_DMA_UNROLL)
        for u in range(ROW_DMA_UNROLL):
            for pos_ref in (pos0_ref, pos1_ref):
                pltpu.make_async_copy(h_ref.at[pl.ds(t0 + u, 1)], xs_hbm.at[pl.ds(pos_ref[t0 + u], 1)], sem).start()
        return carry

    lax.fori_loop(0, G // ROW_DMA_UNROLL, issue, 0)
    for _ in range(2):
        pltpu.make_async_copy(h_ref, xs_hbm.at[pl.ds(0, G)], sem).wait()


def _dispatch(h, pos0, pos1, starts, ends, n_rows):
    T, D = h.shape
    G = DISPATCH_TOKENS if T % DISPATCH_TOKENS == 0 else ROUTE_TOKENS
    smem_blk = pl.BlockSpec((G,), lambda i, st, en: (i,), memory_space=pltpu.SMEM)
    grid_spec = pltpu.PrefetchScalarGridSpec(
        num_scalar_prefetch=2,
        grid=(T // G,),
        in_specs=[smem_blk, smem_blk, pl.BlockSpec((G, D), lambda i, st, en: (i, 0))],
        out_specs=pl.BlockSpec(memory_space=pl.ANY),
        scratch_shapes=[pltpu.VMEM((MOE_ROWS, D), F32), pltpu.SemaphoreType.DMA(()),
                        pltpu.SemaphoreType.DMA(())],
    )
    return pl.pallas_call(
        _dispatch_kernel,
        grid_spec=grid_spec,
        out_shape=jax.ShapeDtypeStruct((n_rows, D), F32),
        compiler_params=_params("arbitrary"),
        name="moe_dispatch",
    )(starts, ends, pos0, pos1, h)


def _experts_kernel(te_ref, nv_ref, next_ref, slot_ref, xs_ref, wgu_hbm, wdn_hbm, y_ref,
                    wgu_f, wdn_f, wgu_b, wdn_b, sems):
    Tm = MOE_ROWS

    def weight_copies(e, slot):
        return (pltpu.make_async_copy(wgu_hbm.at[e], wgu_f.at[slot], sems.at[slot]),
                pltpu.make_async_copy(wdn_hbm.at[e], wdn_f.at[slot], sems.at[slot]))

    for part in range(EXPERT_TILES_PER_STEP):
        tile = pl.program_id(0) * EXPERT_TILES_PER_STEP + part
        rows = slice(part * Tm, (part + 1) * Tm)
        live = tile < nv_ref[0]
        expert = te_ref[tile]
        new_expert = jnp.logical_or(tile == 0, expert != te_ref[jnp.maximum(tile - 1, 0)])

        @pl.when(jnp.logical_and(live, new_expert))
        def _():
            slot = slot_ref[expert]

            @pl.when(tile == 0)
            def _():
                for copy in weight_copies(expert, slot):
                    copy.start()

            for copy in weight_copies(expert, slot):
                copy.wait()
            wgu_b[...] = wgu_f[slot].astype(BF16)
            wdn_b[...] = wdn_f[slot].astype(BF16)
            following = next_ref[expert]

            @pl.when(following >= 0)
            def _():
                for copy in weight_copies(following, 1 - slot):
                    copy.start()

        @pl.when(live)
        def _():
            gu = jnp.dot(xs_ref[rows, :].astype(BF16), wgu_b[...], preferred_element_type=F32)
            gate, up = gu[:, :D_EXPERT], gu[:, D_EXPERT:]
            hid = gate * jax.nn.sigmoid(gate) * up
            y_ref[rows, :] = jnp.dot(hid.astype(BF16), wdn_b[...], preferred_element_type=F32)

        @pl.when(jnp.logical_not(live))
        def _():
            y_ref[rows, :] = jnp.zeros((Tm, y_ref.shape[1]), F32)


def _experts(xs, w_gu, w_dn, tile_expert, n_valid, has_rows):
    n_rows, D = xs.shape
    Tm = MOE_ROWS
    ids = jnp.arange(N_EXPERTS, dtype=jnp.int32)
    later = jnp.logical_and(has_rows[None, :], ids[None, :] > ids[:, None])
    next_expert = jnp.where(jnp.any(later, axis=1), jnp.argmax(later, axis=1), -1).astype(jnp.int32)
    slot = ((jnp.cumsum(has_rows.astype(jnp.int32)) - has_rows.astype(jnp.int32)) % 2).astype(jnp.int32)
    per_step = EXPERT_TILES_PER_STEP
    assert (n_rows // Tm) % per_step == 0
    grid_spec = pltpu.PrefetchScalarGridSpec(
        num_scalar_prefetch=4,
        grid=(n_rows // (per_step * Tm),),
        in_specs=[
            pl.BlockSpec((per_step * Tm, D),
                         lambda i, te, nv, nx, sl: (jnp.minimum(i, jnp.maximum(nv[0] - 1, 0) // per_step), 0)),
            pl.BlockSpec(memory_space=pl.ANY),
            pl.BlockSpec(memory_space=pl.ANY),
        ],
        out_specs=pl.BlockSpec((per_step * Tm, D), lambda i, te, nv, nx, sl: (i, 0)),
        scratch_shapes=[
            pltpu.VMEM((2, D, 2 * D_EXPERT), F32), pltpu.VMEM((2, D_EXPERT, D), F32),
            pltpu.VMEM((D, 2 * D_EXPERT), BF16), pltpu.VMEM((D_EXPERT, D), BF16),
            pltpu.SemaphoreType.DMA((2,)),
        ],
    )
    return pl.pallas_call(
        _experts_kernel,
        grid_spec=grid_spec,
        out_shape=jax.ShapeDtypeStruct((n_rows, D), F32),
        compiler_params=_params("arbitrary"),
        name="moe_experts",
    )(tile_expert, n_valid, next_expert, slot, xs, w_gu, w_dn)


def _combine_kernel(pos0_ref, pos1_ref, next0_ref, next1_ref, y_hbm, h_ref, cw_ref, g_ref, b_ref, o_ref,
                    ybuf_a, ybuf_b, sem_a, sem_b):
    G = ROUTE_TOKENS
    C = COMBINE_TOKENS_PER_ITER
    i = pl.program_id(0)
    last = pl.num_programs(0) - 1

    def gather(pos_refs, buf, sem, t0):
        for u in range(C):
            for slot, pos_ref in enumerate(pos_refs):
                pltpu.make_async_copy(y_hbm.at[pl.ds(pos_ref[t0 + u], 1)], buf.at[slot, pl.ds(t0 + u, 1)],
                                      sem).start()

    def wait_tile(buf, sem):
        for slot in range(2):
            pltpu.make_async_copy(y_hbm.at[pl.ds(0, G)], buf.at[slot], sem).wait()

    @pl.when(i == 0)
    def _():
        def first_tile(k, carry):
            gather((pos0_ref, pos1_ref), ybuf_a, sem_a, pl.multiple_of(k * C, C))
            return carry
        lax.fori_loop(0, G // C, first_tile, 0)

    def run(buf, sem, spare, spare_sem):
        wait_tile(buf, sem)

        def chunk(k, carry):
            t0 = pl.multiple_of(k * C, C)
            rows = pl.ds(t0, C)
            cw = cw_ref[rows, :]
            y0, y1, h = buf[0, rows, :], buf[1, rows, :], h_ref[rows, :]
            gather((next0_ref, next1_ref), spare, spare_sem, t0)
            o_ref[rows, :] = _layer_norm_rows(ALPHA * h + (y0 * cw[:, 0:1] + y1 * cw[:, 1:2]),
                                              g_ref[...], b_ref[...])
            return carry

        lax.fori_loop(0, G // C, chunk, 0)

        @pl.when(i == last)
        def _():
            wait_tile(spare, spare_sem)

    odd = jnp.bitwise_and(i, 1)

    @pl.when(odd == 0)
    def _():
        run(ybuf_a, sem_a, ybuf_b, sem_b)

    @pl.when(odd == 1)
    def _():
        run(ybuf_b, sem_b, ybuf_a, sem_a)


def _combine(y, h, pos0, pos1, cw_rows, ln_g, ln_b):
    T, D = h.shape
    G = ROUTE_TOKENS
    n = T // G
    smem_blk = pl.BlockSpec((G,), lambda i: (i,), memory_space=pltpu.SMEM)
    smem_next = pl.BlockSpec((G,), lambda i: (jnp.minimum(i + 1, n - 1),), memory_space=pltpu.SMEM)
    return pl.pallas_call(
        _combine_kernel,
        grid=(n,),
        in_specs=[
            smem_blk, smem_blk, smem_next, smem_next,
            pl.BlockSpec(memory_space=pl.ANY),
            pl.BlockSpec((G, D), lambda i: (i, 0)),
            pl.BlockSpec((G, 2), lambda i: (i, 0)),
            pl.BlockSpec((1, D), lambda i: (0, 0)),
            pl.BlockSpec((1, D), lambda i: (0, 0)),
        ],
        out_specs=pl.BlockSpec((G, D), lambda i: (i, 0)),
        scratch_shapes=[pltpu.VMEM((2, G, D), F32), pltpu.VMEM((2, G, D), F32),
                        pltpu.SemaphoreType.DMA(()), pltpu.SemaphoreType.DMA(())],
        out_shape=jax.ShapeDtypeStruct((T, D), F32),
        compiler_params=_params("arbitrary"),
        name="moe_combine_ln",
    )(pos0, pos1, pos0, pos1, y, h, cw_rows, ln_g.reshape(1, D), ln_b.reshape(1, D))


def kernel(x, w_in, attn_out_gain, rel_bias_table, w_out, ln1_g, ln1_b, w_router_group,
           b_router_group, w_router_expert, b_router_expert, w_gate_up, w_down, ln2_g, ln2_b):
    B, S, D = x.shape
    assert D == D_MODEL and S % MOBA_BLOCK == 0 and S % INPROJ_TOKENS == 0
    assert w_in.shape[0] == DEPTH
    T = B * S
    h = x
    for l in range(DEPTH):
        proj_t, k_nat = _inproj(h, w_in[l])
        attn_t = _moba(proj_t, k_nat, rel_bias_table)
        ret_t = _retention(proj_t)
        h1, eid, rank, cw, counts = _outproj(
            attn_t, ret_t, h, attn_out_gain[l], w_out[l], ln1_g[l], ln1_b[l],
            w_router_group[l], b_router_group[l], w_router_expert[l], b_router_expert[l])

        Tm = MOE_ROWS
        n_tiles = (2 * T) // Tm + N_EXPERTS
        padded = ((counts[:, 0] + Tm - 1) // Tm) * Tm
        ends = jnp.cumsum(padded)
        starts = (ends - padded).astype(jnp.int32)
        tile_row0 = jnp.arange(n_tiles, dtype=jnp.int32) * Tm
        tile_expert = jnp.sum((ends[None, :] <= jnp.minimum(tile_row0, ends[-1] - Tm)[:, None]).astype(jnp.int32),
                              axis=1)
        n_valid = (ends[-1:] // Tm).astype(jnp.int32)

        expert_ids = jnp.arange(N_EXPERTS, dtype=jnp.int32)[:, None, None]
        pos = jnp.sum(jnp.where(eid[None] == expert_ids, starts[:, None, None], 0), axis=0) + rank

        xs = _dispatch(h1, pos[0], pos[1], starts, ends.astype(jnp.int32), n_tiles * Tm)
        y = _experts(xs, w_gate_up[l], w_down[l], tile_expert, n_valid, padded > 0)
        h2 = _combine(y, h1, pos[0], pos[1], cw.T, ln2_g[l], ln2_b[l])
        h = h2.reshape(B, S, D)
    return h
```
